```python
import jax, jax.numpy as jnp
from jax import lax
import numpy as np

D_MODEL = 2048
BATCH = 16
SEQ = 256
DEPTH = 4
DEC_BATCH = 4
DEC_SEQ = 1024
PAST_LEN = 256

GRID_W = 64
HEAD_DIM = 128
A_HEADS = 8
A_KV_HEADS = 2
A_WINDOW = 128
A_BLOCK = 128
B_HEADS = 8
NA_ROWS_MAX = 8
NA_COLS = 16
NA_QCOLS = 16
NA_KCOLS = 32
C_HEADS = 8
C_DK = D_MODEL // C_HEADS
C_DV = D_MODEL // C_HEADS
C_CHUNK = 128
Q_BLOCK = 128
ROPE_THETA = 10000.0
EPS = 1e-6

N_EVEN = (DEPTH + 1) // 2
N_ODD = DEPTH // 2
A_Q = A_HEADS * HEAD_DIM
A_KV = A_KV_HEADS * HEAD_DIM
B_W = B_HEADS * HEAD_DIM
EVEN_WIDTH = A_Q + B_W
EVEN_IN = A_Q + 2 * A_KV + 3 * B_W + EVEN_WIDTH
C_WIDTH = C_HEADS * C_DV
ODD_IN = 2 * C_HEADS * C_DK + 2 * C_WIDTH
RPB_R = 2 * NA_ROWS_MAX - 1
RPB_C = 2 * NA_COLS - 1

kernel_name = "hybrid_diffusion_prefix_step"


def rmsnorm(x, g):
    xf = x.astype(jnp.float32)
    y = xf * lax.rsqrt(jnp.mean(xf * xf, axis=-1, keepdims=True) + EPS)
    return (y * g.astype(jnp.float32)).astype(x.dtype)


def adaln(cvec, w, b):
    mod = jax.nn.silu(cvec) @ w + b
    return jnp.split(mod, 3, axis=-1)


def _rot(x, ang):
    n = x.shape[-1] // 2
    cos = jnp.cos(ang)[None, :, None, :]
    sin = jnp.sin(ang)[None, :, None, :]
    x1, x2 = x[..., :n], x[..., n:]
    return jnp.concatenate([x1 * cos - x2 * sin, x2 * cos + x1 * sin], axis=-1)


def axial_rope(x):
    T = x.shape[1]
    t = jnp.arange(T)
    half = HEAD_DIM // 2
    nf = half // 2
    inv = ROPE_THETA ** (-jnp.arange(nf, dtype=jnp.float32) / nf)
    ang_r = (t // GRID_W).astype(jnp.float32)[:, None] * inv[None]
    ang_c = (t % GRID_W).astype(jnp.float32)[:, None] * inv[None]
    xf = x.astype(jnp.float32)
    return jnp.concatenate([_rot(xf[..., :half], ang_r), _rot(xf[..., half:], ang_c)], axis=-1).astype(x.dtype)


def context_attention(q, k, v, sink):
    B, T, H, hd = q.shape
    Hkv, Tk = k.shape[1], k.shape[2]
    G = H // Hkv
    nqb = T // Q_BLOCK
    qb = q.reshape(B, nqb, Q_BLOCK, Hkv, G, hd).transpose(1, 0, 2, 3, 4, 5)
    scale = hd ** -0.5

    def block(qi):
        s = jnp.einsum('bqngd,bnkd->bngqk', qi, k).astype(jnp.float32) * scale
        if sink is not None:
            sk = jnp.broadcast_to(sink.reshape(Hkv, G).astype(jnp.float32)[None, :, :, None, None], s.shape[:-1] + (1,))
            s = jnp.concatenate([s, sk], axis=-1)
        p = jax.nn.softmax(s, axis=-1)[..., :Tk].astype(v.dtype)
        return jnp.einsum('bngqk,bnkd->bqngd', p, v).reshape(B, Q_BLOCK, H * hd)

    o = lax.map(block, qb)
    return o.transpose(1, 0, 2, 3).reshape(B, T, H * hd)


def window_attention(q, k, v, ck, cv, sink):
    B, T, H, hd = q.shape
    Hkv = k.shape[2]
    G = H // Hkv
    nb = T // A_BLOCK
    P = ck.shape[2]
    scale = hd ** -0.5

    def band(x):
        xp = jnp.pad(x, ((0, 0), (A_BLOCK, A_BLOCK), (0, 0), (0, 0))).reshape(B, nb + 2, A_BLOCK, Hkv, hd)
        return jnp.concatenate([xp[:, :-2], xp[:, 1:-1], xp[:, 2:]], axis=2)

    kw, vw = band(k), band(v)
    qb = q.reshape(B, nb, A_BLOCK, Hkv, G, hd)
    blk = jnp.arange(nb)[:, None]
    qpos = blk * A_BLOCK + jnp.arange(A_BLOCK)[None]
    kpos = (blk - 1) * A_BLOCK + jnp.arange(3 * A_BLOCK)[None]
    valid = ((jnp.abs(qpos[:, :, None] - kpos[:, None, :]) <= A_WINDOW)
             & (kpos[:, None, :] >= 0) & (kpos[:, None, :] < T))
    s_w = jnp.einsum('bjqngd,bjknd->bjngqk', qb, kw).astype(jnp.float32) * scale
    s_w = jnp.where(valid[None, :, None, None], s_w, -jnp.inf)
    s_c = jnp.einsum('bjqngd,bnkd->bjngqk', qb, ck).astype(jnp.float32) * scale
    s_sink = jnp.broadcast_to(sink.reshape(Hkv, G).astype(jnp.float32)[None, None, :, :, None, None], s_c.shape[:-1] + (1,))
    p = jax.nn.softmax(jnp.concatenate([s_c, s_w, s_sink], axis=-1), axis=-1)
    pc = p[..., :P].astype(v.dtype)
    pw = p[..., P:P + 3 * A_BLOCK].astype(v.dtype)
    o = jnp.einsum('bjngqk,bnkd->bjqngd', pc, cv) + jnp.einsum('bjngqk,bjknd->bjqngd', pw, vw)
    return o.reshape(B, T, H * hd)


def neighbourhood_attention(q, k, v, ck, cv, rpb):
    B, T, H, hd = q.shape
    rows = T // GRID_W
    wr = min(NA_ROWS_MAX, rows)
    ncb = GRID_W // NA_QCOLS
    scale = hd ** -0.5
    r = jnp.arange(rows)
    r0 = jnp.clip(r - wr // 2, 0, rows - wr)
    krow = r0[:, None] + jnp.arange(wr)[None]
    m = jnp.arange(ncb)
    cs = jnp.clip(m * NA_QCOLS - NA_COLS // 2, 0, GRID_W - NA_KCOLS)
    kcol = cs[:, None] + jnp.arange(NA_KCOLS)[None]
    qc = m[:, None] * NA_QCOLS + jnp.arange(NA_QCOLS)[None]
    c0 = jnp.clip(qc - NA_COLS // 2, 0, GRID_W - NA_COLS)
    colmask = (kcol[:, None, :] >= c0[:, :, None]) & (kcol[:, None, :] < c0[:, :, None] + NA_COLS)
    dr_idx = krow - r[:, None] + (NA_ROWS_MAX - 1)
    dc_idx = jnp.clip(kcol[:, None, :] - qc[:, :, None] + (NA_COLS - 1), 0, RPB_C - 1)
    bias = rpb[:, dr_idx[:, None, None, :, None], dc_idx[None, :, :, None, :]].astype(jnp.float32)
    bias = jnp.where(colmask[None, None, :, :, None, :], bias, -jnp.inf)
    bias = bias.reshape(H, rows, ncb, NA_QCOLS, wr * NA_KCOLS).transpose(1, 2, 0, 3, 4)

    def gather(x):
        xg = x.reshape(B, rows, GRID_W, H, hd)
        xn = xg[:, krow[:, None, :, None], kcol[None, :, None, :]]
        return xn.reshape(B, rows, ncb, wr * NA_KCOLS, H, hd)

    kn, vn = gather(k), gather(v)
    qg = q.reshape(B, rows, ncb, NA_QCOLS, H, hd)
    P = ck.shape[2]
    s_n = jnp.einsum('brmqhd,brmkhd->brmhqk', qg, kn).astype(jnp.float32) * scale + bias[None]
    s_c = jnp.einsum('brmqhd,bhkd->brmhqk', qg, ck).astype(jnp.float32) * scale
    p = jax.nn.softmax(jnp.concatenate([s_c, s_n], axis=-1), axis=-1)
    pc = p[..., :P].astype(v.dtype)
    pn = p[..., P:].astype(v.dtype)
    o = jnp.einsum('brmhqk,bhkd->brmqhd', pc, cv) + jnp.einsum('brmhqk,brmkhd->brmqhd', pn, vn)
    return o.reshape(B, T, H * hd)


def _even_split(h, w_in):
    B, T = h.shape[:2]
    proj = h @ w_in
    idx = np.cumsum([A_Q, A_KV, A_KV, B_W, B_W, B_W])
    qa, ka, va, qb, kb, vb, gate = jnp.split(proj, idx, axis=-1)
    qa = qa.reshape(B, T, A_HEADS, HEAD_DIM)
    ka = ka.reshape(B, T, A_KV_HEADS, HEAD_DIM)
    va = va.reshape(B, T, A_KV_HEADS, HEAD_DIM)
    qb = qb.reshape(B, T, B_HEADS, HEAD_DIM)
    kb = kb.reshape(B, T, B_HEADS, HEAD_DIM)
    vb = vb.reshape(B, T, B_HEADS, HEAD_DIM)
    return qa, ka, va, qb, kb, vb, gate


def even_context(h, w_in, w_out, sink):
    qa, ka, va, qb, kb, vb, gate = _even_split(h, w_in)
    ka, va, kb, vb = (x.transpose(0, 2, 1, 3) for x in (ka, va, kb, vb))
    oa = context_attention(qa, ka, va, sink)
    ob = context_attention(qb, kb, vb, None)
    out = (jnp.concatenate([oa, ob], axis=-1) * jax.nn.silu(gate)) @ w_out
    return out, ka, va, kb, vb


def even_latent(h, w_in, w_out, sink, rpb, ck_a, cv_a, ck_b, cv_b):
    qa, ka, va, qb, kb, vb, gate = _even_split(h, w_in)
    qa, ka = axial_rope(qa), axial_rope(ka)
    oa = window_attention(qa, ka, va, ck_a, cv_a, sink)
    ob = neighbourhood_attention(qb, kb, vb, ck_b, cv_b, rpb)
    return (jnp.concatenate([oa, ob], axis=-1) * jax.nn.silu(gate)) @ w_out


def retention_scan(q, k, v, log_g, s0):
    B, T, H, dk = q.shape
    dv = v.shape[-1]
    nc = T // C_CHUNK

    def chunks(x):
        return x.astype(jnp.float32).reshape(B, nc, C_CHUNK, H, x.shape[-1]).transpose(1, 0, 2, 3, 4)

    qc, kc, vc = chunks(q), chunks(k), chunks(v)
    i = jnp.arange(C_CHUNK, dtype=jnp.float32)
    diff = i[:, None] - i[None, :]
    dmat = jnp.where(diff >= 0, jnp.exp(log_g[:, None, None] * jnp.maximum(diff, 0.0)), 0.0)
    q_dec = jnp.exp(log_g[None, :] * (i[:, None] + 1.0))
    k_dec = jnp.exp(log_g[None, :] * (C_CHUNK - 1.0 - i[:, None]))
    chunk_dec = jnp.exp(log_g * C_CHUNK)

    def step(S, inp):
        qi, ki, vi = inp
        s = jnp.einsum('bihd,bjhd->bhij', qi, ki) * dmat[None]
        inner = jnp.einsum('bhij,bjhe->bihe', s, vi)
        cross = jnp.einsum('bihd,bhde->bihe', qi, S) * q_dec[None, :, :, None]
        S_new = S * chunk_dec[None, :, None, None] + jnp.einsum('bjhd,bjhe->bhde', ki * k_dec[None, :, :, None], vi)
        return S_new, inner + cross

    S_fin, outs = lax.scan(step, s0.astype(jnp.float32), (qc, kc, vc))
    return outs.transpose(1, 0, 2, 3, 4).reshape(B, T, H, dv), S_fin


def retention_mixer(h, w_in, w_out, dec_f, dec_b, gn_g, s_f0, s_b0):
    B, T = h.shape[:2]
    q, k, v, gate = jnp.split(h @ w_in, 4, axis=-1)
    q = q.reshape(B, T, C_HEADS, C_DK)
    k = k.reshape(B, T, C_HEADS, C_DK) * (C_DK ** -0.5)
    v = v.reshape(B, T, C_HEADS, C_DV)
    log_f = -jnp.exp(dec_f.astype(jnp.float32))
    log_b = -jnp.exp(dec_b.astype(jnp.float32))
    o_f, S_f = retention_scan(q, k, v, log_f, s_f0)
    o_bf, S_b = retention_scan(jnp.flip(q, 1), jnp.flip(k, 1), jnp.flip(v, 1), log_b, s_b0)
    o = o_f + jnp.flip(o_bf, 1)
    mu = jnp.mean(o, axis=-1, keepdims=True)
    var = jnp.mean((o - mu) ** 2, axis=-1, keepdims=True)
    o = ((o - mu) * lax.rsqrt(var + EPS)).reshape(B, T, C_WIDTH) * gn_g.astype(jnp.float32)
    out = (o.astype(h.dtype) * jax.nn.silu(gate)) @ w_out
    return out, S_f, S_b


def setup_inputs(seed: int = 0) -> dict:
    key = jax.random.key(seed)
    ks = jax.random.split(key, 24)
    f32 = jnp.float32
    nrm = lambda k, shape, s: jax.random.normal(k, shape, f32) * s
    base_dec = jnp.log(-jnp.log1p(-(2.0 ** (-5.0 - jnp.arange(C_HEADS, dtype=f32)))))
    return {
        "x_prompt": nrm(ks[0], (BATCH, SEQ, D_MODEL), 1.0),
        "x_sample": nrm(ks[1], (DEC_BATCH, DEC_SEQ, D_MODEL), 1.0),
        "c": nrm(ks[2], (DEC_BATCH, D_MODEL), 1.0),
        "cache_a_k": nrm(ks[3], (DEC_BATCH, N_EVEN, A_KV_HEADS, PAST_LEN, HEAD_DIM), 1.0),
        "cache_a_v": nrm(ks[4], (DEC_BATCH, N_EVEN, A_KV_HEADS, PAST_LEN, HEAD_DIM), 1.0),
        "cache_b_k": nrm(ks[5], (DEC_BATCH, N_EVEN, B_HEADS, PAST_LEN, HEAD_DIM), 1.0),
        "cache_b_v": nrm(ks[6], (DEC_BATCH, N_EVEN, B_HEADS, PAST_LEN, HEAD_DIM), 1.0),
        "state_ret_f": nrm(ks[7], (DEC_BATCH, N_ODD, C_HEADS, C_DK, C_DV), 0.5),
        "state_ret_b": nrm(ks[8], (DEC_BATCH, N_ODD, C_HEADS, C_DK, C_DV), 0.5),
        "c_ctx": nrm(ks[9], (D_MODEL,), 1.0),
        "w_ada": nrm(ks[10], (DEPTH, D_MODEL, 3 * D_MODEL), 0.5 * D_MODEL ** -0.5),
        "b_ada": nrm(ks[11], (DEPTH, 3 * D_MODEL), 0.01),
        "norm_pre": 1.0 + nrm(ks[12], (DEPTH, D_MODEL), 0.05),
        "norm_post": 1.0 + nrm(ks[13], (DEPTH, D_MODEL), 0.05),
        "w_in_even": nrm(ks[14], (N_EVEN, D_MODEL, EVEN_IN), D_MODEL ** -0.5),
        "w_out_even": nrm(ks[15], (N_EVEN, EVEN_WIDTH, D_MODEL), EVEN_WIDTH ** -0.5),
        "a_sink": nrm(ks[16], (N_EVEN, A_HEADS), 0.5),
        "na_rpb": nrm(ks[17], (N_EVEN, B_HEADS, RPB_R, RPB_C), 0.1),
        "w_in_odd": nrm(ks[18], (N_ODD, D_MODEL, ODD_IN), D_MODEL ** -0.5),
        "w_out_odd": nrm(ks[19], (N_ODD, C_WIDTH, D_MODEL), C_WIDTH ** -0.5),
        "ret_decay_f": base_dec[None] + nrm(ks[20], (N_ODD, C_HEADS), 0.1),
        "ret_decay_b": base_dec[None] + nrm(ks[21], (N_ODD, C_HEADS), 0.1),
        "ret_gn": 1.0 + nrm(ks[22], (N_ODD, C_WIDTH), 0.05),
    }


def reference(x_prompt, x_sample, c, cache_a_k, cache_a_v, cache_b_k, cache_b_v, state_ret_f, state_ret_b,
              c_ctx, w_ada, b_ada, norm_pre, norm_post, w_in_even, w_out_even, a_sink, na_rpb,
              w_in_odd, w_out_odd, ret_decay_f, ret_decay_b, ret_gn):
    xp, xs = x_prompt, x_sample
    zero_state = jnp.zeros((x_prompt.shape[0], C_HEADS, C_DK, C_DV), jnp.float32)
    new_ak, new_av, new_bk, new_bv, new_sf, new_sb = [], [], [], [], [], []
    for l in range(DEPTH):
        i = l // 2
        sh_p, sc_p, g_p = adaln(c_ctx[None, :], w_ada[l], b_ada[l])
        sh_s, sc_s, g_s = adaln(c, w_ada[l], b_ada[l])
        hp = rmsnorm(xp, norm_pre[l]) * (1.0 + sc_p[:, None]) + sh_p[:, None]
        hs = rmsnorm(xs, norm_pre[l]) * (1.0 + sc_s[:, None]) + sh_s[:, None]
        if l % 2 == 0:
            op, ka, va, kb, vb = even_context(hp, w_in_even[i], w_out_even[i], a_sink[i])
            os_ = even_latent(hs, w_in_even[i], w_out_even[i], a_sink[i], na_rpb[i],
                              cache_a_k[:, i], cache_a_v[:, i], cache_b_k[:, i], cache_b_v[:, i])
            new_ak.append(ka)
            new_av.append(va)
            new_bk.append(kb)
            new_bv.append(vb)
        else:
            op, sf, sb = retention_mixer(hp, w_in_odd[i], w_out_odd[i], ret_decay_f[i], ret_decay_b[i],
                                         ret_gn[i], zero_state, zero_state)
            os_, _, _ = retention_mixer(hs, w_in_odd[i], w_out_odd[i], ret_decay_f[i], ret_decay_b[i],
                                        ret_gn[i], state_ret_f[:, i], state_ret_b[:, i])
            new_sf.append(sf)
            new_sb.append(sb)
        xp = xp + g_p[:, None] * rmsnorm(op, norm_post[l])
        xs = xs + g_s[:, None] * rmsnorm(os_, norm_post[l])
    new_a_k = jnp.stack(new_ak, axis=1)
    new_a_v = jnp.stack(new_av, axis=1)
    new_b_k = jnp.stack(new_bk, axis=1)
    new_b_v = jnp.stack(new_bv, axis=1)
    new_ret_f = jnp.stack(new_sf, axis=1)
    new_ret_b = jnp.stack(new_sb, axis=1)
    return (xp, xs, new_a_k, new_a_v, new_b_k, new_b_v, new_ret_f, new_ret_b)
```

```python
import functools

import numpy as np
import jax
import jax.numpy as jnp
from jax import lax
from jax.experimental import pallas as pl
from jax.experimental.pallas import tpu as pltpu

D_MODEL = 2048
BATCH = 16
SEQ = 256
DEPTH = 4
DEC_BATCH = 4
DEC_SEQ = 1024
PAST_LEN = 256
GRID_W = 64
HEAD_DIM = 128
A_HEADS = 8
A_KV_HEADS = 2
A_GROUP = A_HEADS // A_KV_HEADS
A_WINDOW = 128
A_BLOCK = 128
B_HEADS = 8
NA_ROWS = 8
NA_COLS = 16
C_HEADS = 8
C_DK = D_MODEL // C_HEADS
C_DV = D_MODEL // C_HEADS
C_CHUNK = 128
ROPE_THETA = 10000.0
EPS = 1e-6

A_Q = A_HEADS * HEAD_DIM
A_KV = A_KV_HEADS * HEAD_DIM
B_W = B_HEADS * HEAD_DIM
EVEN_WIDTH = A_Q + B_W
EVEN_IN = A_Q + 2 * A_KV + 3 * B_W + EVEN_WIDTH
ODD_IN = 4 * D_MODEL
GRID_ROWS = DEC_SEQ // GRID_W

NP = BATCH * SEQ
NS = DEC_BATCH * DEC_SEQ
NTOK = NP + NS
MOD_ROWS = 8

COL_QA = 0
COL_KA = A_Q
COL_VA = A_Q + A_KV
COL_QB = A_Q + 2 * A_KV
COL_KB = COL_QB + B_W
COL_VB = COL_KB + B_W
COL_GATE = COL_VB + B_W

NA_QROWS = 8
NA_KROWS = 12
NA_Q = NA_QROWS * GRID_W
NA_K = NA_KROWS * GRID_W
NA_KSHIFT = (GRID_ROWS - NA_KROWS) * GRID_W

V7X_VMEM_BYTES = 64 * 1024 * 1024
VMEM_HEADROOM = 8 * 1024 * 1024

F32 = jnp.float32
BF16 = jnp.bfloat16
NEG_INF = float("-inf")


def _vmem_limit(block_bytes, scratch_bytes=0, temp_bytes=0):
    need = 2 * block_bytes + scratch_bytes + temp_bytes + VMEM_HEADROOM
    return int(min(need, V7X_VMEM_BYTES - 4 * 1024 * 1024))


def _params(vmem_bytes, ndims):
    return pltpu.CompilerParams(dimension_semantics=("arbitrary",) * ndims, vmem_limit_bytes=vmem_bytes)


def _silu(x):
    return x / (1.0 + jnp.exp(-x))


def _dot(a, b):
    return jnp.dot(a, b, preferred_element_type=F32)


def _dot_nt(a, b):
    return lax.dot_general(a, b, (((1,), (1,)), ((), ())), preferred_element_type=F32)


def _dot_tn(a, b):
    return lax.dot_general(a, b, (((0,), (0,)), ((), ())), preferred_element_type=F32)


ADA_TN = 1024


def _adaln_kernel(c_ref, w_ref, b_ref, o_ref):
    a = _silu(c_ref[...]).astype(BF16)
    o_ref[...] = _dot(a, w_ref[...].astype(BF16)) + b_ref[...]


def _adaln(cvec, w_ada, b_ada):
    n = 3 * D_MODEL
    blocks = MOD_ROWS * D_MODEL * 4 + D_MODEL * ADA_TN * 4 + ADA_TN * 4 + MOD_ROWS * ADA_TN * 4
    return pl.pallas_call(
        _adaln_kernel,
        out_shape=jax.ShapeDtypeStruct((DEPTH, MOD_ROWS, n), F32),
        grid=(DEPTH, n // ADA_TN),
        in_specs=[
            pl.BlockSpec((MOD_ROWS, D_MODEL), lambda l, j: (0, 0)),
            pl.BlockSpec((None, D_MODEL, ADA_TN), lambda l, j: (l, 0, j)),
            pl.BlockSpec((None, 1, ADA_TN), lambda l, j: (l, 0, j)),
        ],
        out_specs=pl.BlockSpec((None, MOD_ROWS, ADA_TN), lambda l, j: (l, 0, j)),
        compiler_params=_params(_vmem_limit(blocks, temp_bytes=D_MODEL * ADA_TN * 2), 2),
        name="adaln",
    )(cvec, w_ada, b_ada.reshape(DEPTH, 1, n))


def _mod_row(tile, tm):
    first_latent = NP // tm
    return jnp.where(tile < first_latent, 0, 1 + (tile - first_latent) // (DEC_SEQ // tm))


IN_TM = 1024
IN_TN = 512
IN_ROWS = 128


def _inproj_kernel(x_ref, g_ref, sh_ref, sc_ref, w_ref, o_ref, h_ref):
    @pl.when(pl.program_id(1) == 0)
    def _():
        gain = g_ref[...]
        one_sc = 1.0 + sc_ref[...]
        sh = sh_ref[...]

        def body(r, carry):
            sl = pl.ds(pl.multiple_of(r * IN_ROWS, IN_ROWS), IN_ROWS)
            x = x_ref[sl, :]
            ms = jnp.mean(x * x, axis=-1, keepdims=True)
            y = (x * lax.rsqrt(ms + EPS)) * gain
            h_ref[sl, :] = (y * one_sc + sh).astype(BF16)
            return carry

        lax.fori_loop(0, IN_TM // IN_ROWS, body, 0)

    o_ref[...] = _dot(h_ref[...], w_ref[...])


def _inproj(x, gain, mods, layer, w):
    n = w.shape[1]
    blocks = IN_TM * D_MODEL * 4 + 3 * D_MODEL * 4 + D_MODEL * IN_TN * 2 + IN_TM * IN_TN * 4
    mod_spec = lambda which: pl.BlockSpec(
        (None, None, None, 1, D_MODEL), lambda i, j: (layer, _mod_row(i, IN_TM), which, 0, 0))
    return pl.pallas_call(
        _inproj_kernel,
        out_shape=jax.ShapeDtypeStruct((NTOK, n), F32),
        grid=(NTOK // IN_TM, n // IN_TN),
        in_specs=[
            pl.BlockSpec((IN_TM, D_MODEL), lambda i, j: (i, 0)),
            pl.BlockSpec((None, 1, D_MODEL), lambda i, j: (layer, 0, 0)),
            mod_spec(0),
            mod_spec(1),
            pl.BlockSpec((D_MODEL, IN_TN), lambda i, j: (0, j)),
        ],
        out_specs=pl.BlockSpec((IN_TM, IN_TN), lambda i, j: (i, j)),
        scratch_shapes=[pltpu.VMEM((IN_TM, D_MODEL), BF16)],
        compiler_params=_params(
            _vmem_limit(blocks, IN_TM * D_MODEL * 2, 4 * IN_ROWS * D_MODEL * 4), 2),
        name=f"inproj_l{layer}",
    )(x, gain, mods, mods, w)


OUT_TM = 512


def _outproj_kernel(u_ref, w_ref, x_ref, gain_ref, gate_ref, o_ref):
    out = _dot(u_ref[...], w_ref[...])
    ms = jnp.mean(out * out, axis=-1, keepdims=True)
    y = (out * lax.rsqrt(ms + EPS)) * gain_ref[...]
    o_ref[...] = x_ref[...] + gate_ref[...] * y


def _outproj(u, w, x, gain, mods, layer):
    k = u.shape[1]
    blocks = OUT_TM * k * 2 + k * D_MODEL * 2 + 2 * OUT_TM * D_MODEL * 4 + 2 * D_MODEL * 4
    return pl.pallas_call(
        _outproj_kernel,
        out_shape=jax.ShapeDtypeStruct((NTOK, D_MODEL), F32),
        grid=(NTOK // OUT_TM,),
        in_specs=[
            pl.BlockSpec((OUT_TM, k), lambda i: (i, 0)),
            pl.BlockSpec((k, D_MODEL), lambda i: (0, 0)),
            pl.BlockSpec((OUT_TM, D_MODEL), lambda i: (i, 0)),
            pl.BlockSpec((None, 1, D_MODEL), lambda i: (layer, 0, 0)),
            pl.BlockSpec((None, None, None, 1, D_MODEL),
                         lambda i: (layer, _mod_row(i, OUT_TM), 2, 0, 0)),
        ],
        out_specs=pl.BlockSpec((OUT_TM, D_MODEL), lambda i: (i, 0)),
        compiler_params=_params(_vmem_limit(blocks, temp_bytes=3 * OUT_TM * D_MODEL * 4), 1),
        name=f"outproj_l{layer}",
    )(u, w, x, gain, mods)


def _softmax_pv(scores, values, sink=None):
    m = jnp.max(scores[0], axis=-1, keepdims=True)
    for s in scores[1:]:
        m = jnp.maximum(m, jnp.max(s, axis=-1, keepdims=True))
    if sink is not None:
        m = jnp.maximum(m, sink)
    den = None
    acc = None
    for s, v in zip(scores, values):
        p = jnp.exp(s - m)
        ps = jnp.sum(p, axis=-1, keepdims=True)
        pv = _dot(p.astype(BF16), v)
        den = ps if den is None else den + ps
        acc = pv if acc is None else acc + pv
    if sink is not None:
        den = den + jnp.exp(sink - m)
    return acc / den


def _ctx_attn_kernel(sink_ref, p_ref, u_ref, ak_ref, av_ref, bk_ref, bv_ref):
    scale = HEAD_DIM ** -0.5

    def col(c):
        return p_ref[:, c:c + HEAD_DIM]

    def head(cq, k, v, cg, sink, cu):
        q = col(cq).astype(BF16)
        s = _dot_nt(q, k) * scale
        o = _softmax_pv([s], [v], sink)
        u_ref[:, cu:cu + HEAD_DIM] = (o * _silu(col(cg))).astype(BF16)

    for n in range(A_KV_HEADS):
        k32 = col(COL_KA + n * HEAD_DIM)
        v32 = col(COL_VA + n * HEAD_DIM)
        ak_ref[n] = k32
        av_ref[n] = v32
        k = k32.astype(BF16)
        v = v32.astype(BF16)
        for g in range(A_GROUP):
            h = n * A_GROUP + g
            head(COL_QA + h * HEAD_DIM, k, v, COL_GATE + h * HEAD_DIM, sink_ref[h], h * HEAD_DIM)
    for h in range(B_HEADS):
        k32 = col(COL_KB + h * HEAD_DIM)
        v32 = col(COL_VB + h * HEAD_DIM)
        bk_ref[h] = k32
        bv_ref[h] = v32
        head(COL_QB + h * HEAD_DIM, k32.astype(BF16), v32.astype(BF16),
             COL_GATE + A_Q + h * HEAD_DIM, None, A_Q + h * HEAD_DIM)


def _ctx_attn(proj, sink):
    blocks = (SEQ * EVEN_IN * 4 + SEQ * EVEN_WIDTH * 2
              + 2 * (A_KV_HEADS + B_HEADS) * SEQ * HEAD_DIM * 4)
    cache = lambda heads: jax.ShapeDtypeStruct((BATCH, heads, SEQ, HEAD_DIM), F32)
    cache_spec = lambda heads: pl.BlockSpec((None, heads, SEQ, HEAD_DIM), lambda b: (b, 0, 0, 0))
    return pl.pallas_call(
        _ctx_attn_kernel,
        out_shape=(jax.ShapeDtypeStruct((NP, EVEN_WIDTH), BF16),
                   cache(A_KV_HEADS), cache(A_KV_HEADS), cache(B_HEADS), cache(B_HEADS)),
        grid=(BATCH,),
        in_specs=[
            pl.BlockSpec(memory_space=pltpu.SMEM),
            pl.BlockSpec((SEQ, EVEN_IN), lambda b: (b, 0)),
        ],
        out_specs=(pl.BlockSpec((SEQ, EVEN_WIDTH), lambda b: (b, 0)),
                   cache_spec(A_KV_HEADS), cache_spec(A_KV_HEADS),
                   cache_spec(B_HEADS), cache_spec(B_HEADS)),
        compiler_params=_params(_vmem_limit(blocks, temp_bytes=8 * SEQ * SEQ * 4), 1),
        name="ctx_attn",
    )(sink, proj)


WIN_PAD = DEC_SEQ + 2 * A_BLOCK


def _rope_tables():
    t = jnp.arange(DEC_SEQ)
    half = HEAD_DIM // 2
    nf = half // 2
    inv = ROPE_THETA ** (-jnp.arange(nf, dtype=F32) / nf)
    ang_r = (t // GRID_W).astype(F32)[:, None] * inv[None]
    ang_c = (t % GRID_W).astype(F32)[:, None] * inv[None]
    cos = jnp.concatenate([jnp.cos(ang_r)] * 2 + [jnp.cos(ang_c)] * 2, axis=-1)
    sin = jnp.concatenate([-jnp.sin(ang_r), jnp.sin(ang_r), -jnp.sin(ang_c), jnp.sin(ang_c)], axis=-1)
    return cos, sin


def _rope(x, cos, sin):
    quarter = HEAD_DIM // 4
    lane = lax.broadcasted_iota(jnp.int32, x.shape, 1)
    first = (lane & (2 * quarter - 1)) < quarter
    partner = jnp.where(first, pltpu.roll(x, HEAD_DIM - quarter, 1), pltpu.roll(x, quarter, 1))
    return x * cos + partner * sin


def _win_attn_kernel(sink_ref, q_ref, k_ref, v_ref, ck_ref, cv_ref, gate_ref, cos_ref, sin_ref,
                     u_ref, kpad_ref, vpad_ref):
    n = pl.program_id(1)
    j = pl.program_id(2)
    scale = HEAD_DIM ** -0.5

    @pl.when(j == 0)
    def _():
        zeros = jnp.zeros((A_BLOCK, HEAD_DIM), BF16)
        for ref in (kpad_ref, vpad_ref):
            ref[0:A_BLOCK, :] = zeros
            ref[A_BLOCK + DEC_SEQ:WIN_PAD, :] = zeros
        kpad_ref[A_BLOCK:A_BLOCK + DEC_SEQ, :] = _rope(k_ref[...], cos_ref[...], sin_ref[...]).astype(BF16)
        vpad_ref[A_BLOCK:A_BLOCK + DEC_SEQ, :] = v_ref[...].astype(BF16)

    q0 = pl.multiple_of(j * A_BLOCK, A_BLOCK)
    cos_q = cos_ref[pl.ds(q0, A_BLOCK), :]
    sin_q = sin_ref[pl.ds(q0, A_BLOCK), :]
    q = jnp.concatenate(
        [_rope(q_ref[:, g * HEAD_DIM:(g + 1) * HEAD_DIM], cos_q, sin_q).astype(BF16) for g in range(A_GROUP)],
        axis=0)
    band_k = kpad_ref[pl.ds(q0, 3 * A_BLOCK), :]
    band_v = vpad_ref[pl.ds(q0, 3 * A_BLOCK), :]

    rows = A_GROUP * A_BLOCK
    s_c = _dot_nt(q, ck_ref[...].astype(BF16)) * scale
    s_w = _dot_nt(q, band_k) * scale
    row = lax.broadcasted_iota(jnp.int32, (rows, 1), 0)
    qi = row & (A_BLOCK - 1)
    kk = lax.broadcasted_iota(jnp.int32, (rows, 3 * A_BLOCK), 1)
    first_key = jnp.where(j == 0, A_BLOCK, 0)
    last_key = jnp.where(j == DEC_SEQ // A_BLOCK - 1, 2 * A_BLOCK - 1, 3 * A_BLOCK - 1)
    lower = jnp.maximum(qi, first_key)
    upper = jnp.minimum(qi + 2 * A_WINDOW, last_key)
    s_w = jnp.where((kk >= lower) & (kk <= upper), s_w, NEG_INF)

    head = lax.shift_right_logical(row, A_BLOCK.bit_length() - 1)
    sink = jnp.zeros((rows, 1), F32)
    for g in range(A_GROUP):
        sink = jnp.where(head == g, sink_ref[n * A_GROUP + g], sink)

    o = _softmax_pv([s_c, s_w], [cv_ref[...].astype(BF16), band_v], sink)
    for g in range(A_GROUP):
        cols = slice(g * HEAD_DIM, (g + 1) * HEAD_DIM)
        u_ref[:, cols] = (o[g * A_BLOCK:(g + 1) * A_BLOCK] * _silu(gate_ref[:, cols])).astype(BF16)


def _win_attn(proj, sink, cache_k, cache_v, idx, cos, sin):
    nqb = DEC_SEQ // A_BLOCK
    gw = A_GROUP * HEAD_DIM
    row0 = NP // A_BLOCK
    seq0 = NP // DEC_SEQ
    blocks = (2 * A_BLOCK * gw * 4 + 2 * DEC_SEQ * HEAD_DIM * 4 + 2 * PAST_LEN * HEAD_DIM * 4
              + 2 * DEC_SEQ * HEAD_DIM * 4 + A_BLOCK * gw * 2)
    cache_spec = pl.BlockSpec((None, None, None, PAST_LEN, HEAD_DIM), lambda b, n, j: (b, idx, n, 0, 0))
    table_spec = pl.BlockSpec((DEC_SEQ, HEAD_DIM), lambda b, n, j: (0, 0))
    return pl.pallas_call(
        _win_attn_kernel,
        out_shape=jax.ShapeDtypeStruct((NS, A_Q), BF16),
        grid=(DEC_BATCH, A_KV_HEADS, nqb),
        in_specs=[
            pl.BlockSpec(memory_space=pltpu.SMEM),
            pl.BlockSpec((A_BLOCK, gw), lambda b, n, j: (row0 + b * nqb + j, COL_QA // gw + n)),
            pl.BlockSpec((DEC_SEQ, HEAD_DIM), lambda b, n, j: (seq0 + b, COL_KA // HEAD_DIM + n)),
            pl.BlockSpec((DEC_SEQ, HEAD_DIM), lambda b, n, j: (seq0 + b, COL_VA // HEAD_DIM + n)),
            cache_spec,
            cache_spec,
            pl.BlockSpec((A_BLOCK, gw), lambda b, n, j: (row0 + b * nqb + j, COL_GATE // gw + n)),
            table_spec,
            table_spec,
        ],
        out_specs=pl.BlockSpec((A_BLOCK, gw), lambda b, n, j: (b * nqb + j, n)),
        scratch_shapes=[pltpu.VMEM((WIN_PAD, HEAD_DIM), BF16), pltpu.VMEM((WIN_PAD, HEAD_DIM), BF16)],
        compiler_params=_params(
            _vmem_limit(blocks, 2 * WIN_PAD * HEAD_DIM * 2, 8 * A_GROUP * A_BLOCK * 3 * A_BLOCK * 4), 3),
        name="win_attn",
    )(sink, proj, proj, proj, cache_k, cache_v, proj, cos, sin)


def _na_bias(rpb):
    c = np.arange(GRID_W)
    c0 = np.clip(c - NA_COLS // 2, 0, GRID_W - NA_COLS)
    col_ok = (c[None, :] >= c0[:, None]) & (c[None, :] < c0[:, None] + NA_COLS)
    dc = np.clip(c[None, :] - c[:, None] + NA_COLS - 1, 0, 2 * NA_COLS - 2)
    n_dr = 2 * NA_ROWS - 1
    tiles = jnp.where(col_ok[None, None], rpb[:, :, dc], NEG_INF)
    tiles = jnp.concatenate([tiles, jnp.full((B_HEADS, 1, GRID_W, GRID_W), NEG_INF, F32)], axis=1)
    tile_idx = np.full((2, NA_QROWS, NA_KROWS), n_dr, np.int32)
    for g in range(2):
        for rl in range(NA_QROWS):
            r = g * NA_QROWS + rl
            r0 = min(max(r - NA_ROWS // 2, 0), GRID_ROWS - NA_ROWS)
            for kl in range(NA_KROWS):
                kr = g * (GRID_ROWS - NA_KROWS) + kl
                if r0 <= kr < r0 + NA_ROWS:
                    tile_idx[g, rl, kl] = kr - r + NA_ROWS - 1
    bias = tiles[:, tile_idx]
    return bias.transpose(0, 1, 2, 4, 3, 5).reshape(B_HEADS, 2, NA_Q, NA_K)


def _na_attn_kernel(q_ref, k_ref, v_ref, ck_ref, cv_ref, gate_ref, bias_ref, u_ref):
    g = pl.program_id(1)
    scale = HEAD_DIM ** -0.5
    k0 = pl.multiple_of(g * NA_KSHIFT, NA_KSHIFT)
    q = q_ref[...].astype(BF16)
    kwin = k_ref[pl.ds(k0, NA_K), :].astype(BF16)
    vwin = v_ref[pl.ds(k0, NA_K), :].astype(BF16)
    s_c = _dot_nt(q, ck_ref[...].astype(BF16)) * scale
    s_n = _dot_nt(q, kwin) * scale + bias_ref[...]
    o = _softmax_pv([s_c, s_n], [cv_ref[...].astype(BF16), vwin])
    u_ref[...] = (o * _silu(gate_ref[...])).astype(BF16)


def _na_attn(proj, bias, cache_k, cache_v, idx):
    qb0 = NP // NA_Q
    seq0 = NP // DEC_SEQ
    per_seq = DEC_SEQ // NA_Q
    blocks = (2 * NA_Q * HEAD_DIM * 4 + 2 * DEC_SEQ * HEAD_DIM * 4 + 2 * PAST_LEN * HEAD_DIM * 4
              + NA_Q * NA_K * 4 + NA_Q * HEAD_DIM * 2)
    cache_spec = pl.BlockSpec((None, None, None, PAST_LEN, HEAD_DIM), lambda h, g, b: (b, idx, h, 0, 0))
    return pl.pallas_call(
        _na_attn_kernel,
        out_shape=jax.ShapeDtypeStruct((NS, B_W), BF16),
        grid=(B_HEADS, per_seq, DEC_BATCH),
        in_specs=[
            pl.BlockSpec((NA_Q, HEAD_DIM), lambda h, g, b: (qb0 + b * per_seq + g, COL_QB // HEAD_DIM + h)),
            pl.BlockSpec((DEC_SEQ, HEAD_DIM), lambda h, g, b: (seq0 + b, COL_KB // HEAD_DIM + h)),
            pl.BlockSpec((DEC_SEQ, HEAD_DIM), lambda h, g, b: (seq0 + b, COL_VB // HEAD_DIM + h)),
            cache_spec,
            cache_spec,
            pl.BlockSpec((NA_Q, HEAD_DIM),
                         lambda h, g, b: (qb0 + b * per_seq + g, (COL_GATE + A_Q) // HEAD_DIM + h)),
            pl.BlockSpec((None, None, NA_Q, NA_K), lambda h, g, b: (h, g, 0, 0)),
        ],
        out_specs=pl.BlockSpec((NA_Q, HEAD_DIM), lambda h, g, b: (b * per_seq + g, h)),
        compiler_params=_params(_vmem_limit(blocks, temp_bytes=6 * NA_Q * (NA_K + PAST_LEN) * 4), 3),
        name="na_attn",
    )(proj, proj, proj, cache_k, cache_v, proj, bias)


def _ret_kernel(*refs, seq, has_state, emit_state):
    dec_ref, q_ref, k_ref, v_ref, gate_ref, gn_ref = refs[:6]
    pos = 6
    if has_state:
        s0f_ref, s0b_ref = refs[pos:pos + 2]
        pos += 2
    u_ref = refs[pos]
    pos += 1
    if emit_state:
        sf_ref, sb_ref = refs[pos:pos + 2]
        pos += 2
    state_ref, o_ref = refs[pos:pos + 2]

    h = pl.program_id(1)
    nc = seq // C_CHUNK
    kscale = C_DK ** -0.5
    ii = lax.broadcasted_iota(jnp.int32, (C_CHUNK, C_CHUNK), 0).astype(F32)
    jj = lax.broadcasted_iota(jnp.int32, (C_CHUNK, C_CHUNK), 1).astype(F32)
    icol = lax.broadcasted_iota(jnp.int32, (C_CHUNK, 1), 0).astype(F32)

    def scan(direction, first):
        forward = direction == 0
        dec = dec_ref[direction, h]
        log_g = -jnp.exp(jnp.full((C_CHUNK, C_CHUNK), dec, F32))
        log_g_col = -jnp.exp(jnp.full((C_CHUNK, 1), dec, F32))
        log_g_row = -jnp.exp(jnp.full((1, C_DV), dec, F32))
        diff = (ii - jj) if forward else (jj - ii)
        dmat = jnp.where(diff >= 0, jnp.exp(log_g * jnp.maximum(diff, 0.0)), 0.0) * kscale
        if forward:
            q_dec = jnp.exp(log_g_col * (icol + 1.0))
            k_dec = jnp.exp(log_g_col * (C_CHUNK - 1.0 - icol)) * kscale
        else:
            q_dec = jnp.exp(log_g_col * (C_CHUNK - icol))
            k_dec = jnp.exp(log_g_col * icol) * kscale
        chunk_dec = jnp.exp(log_g_row * float(C_CHUNK))

        if has_state:
            state_ref[...] = (s0f_ref if forward else s0b_ref)[...]
        else:
            state_ref[...] = jnp.zeros((C_DK, C_DV), F32)

        def body(t, carry):
            c = t if forward else nc - 1 - t
            sl = pl.ds(pl.multiple_of(c * C_CHUNK, C_CHUNK), C_CHUNK)
            qi = q_ref[sl, :].astype(BF16)
            k32 = k_ref[sl, :]
            vi = v_ref[sl, :].astype(BF16)
            s = _dot_nt(qi, k32.astype(BF16)) * dmat
            inner = _dot(s.astype(BF16), vi)
            state = state_ref[...]
            cross = _dot(qi, state.astype(BF16)) * q_dec
            if first:
                o_ref[sl, :] = inner + cross
            else:
                o_ref[sl, :] = o_ref[sl, :] + (inner + cross)
            state_ref[...] = state * chunk_dec + _dot_tn((k32 * k_dec).astype(BF16), vi)
            return carry

        lax.fori_loop(0, nc, body, 0)

    scan(0, True)
    if emit_state:
        sf_ref[...] = state_ref[...]
    scan(1, False)
    if emit_state:
        sb_ref[...] = state_ref[...]

    gn = gn_ref[...]

    def norm_body(c, carry):
        sl = pl.ds(pl.multiple_of(c * C_CHUNK, C_CHUNK), C_CHUNK)
        o = o_ref[sl, :]
        mu = jnp.mean(o, axis=-1, keepdims=True)
        d = o - mu
        var = jnp.mean(d * d, axis=-1, keepdims=True)
        y = (d * lax.rsqrt(var + EPS)) * gn
        u_ref[sl, :] = (y * _silu(gate_ref[sl, :])).astype(BF16)
        return carry

    lax.fori_loop(0, nc, norm_body, 0)


def _retention(proj, decays, gn, idx, *, latent, state_f=None, state_b=None):
    seq = DEC_SEQ if latent else SEQ
    nb = DEC_BATCH if latent else BATCH
    seq0 = NP // DEC_SEQ if latent else 0
    has_state = latent
    emit_state = not latent
    tok = lambda col0: pl.BlockSpec((seq, C_DK), lambda b, h: (seq0 + b, col0 + h))
    in_specs = [pl.BlockSpec(memory_space=pltpu.SMEM), tok(0), tok(C_HEADS), tok(2 * C_HEADS),
                tok(3 * C_HEADS), pl.BlockSpec((None, 1, C_DV), lambda b, h: (idx, 0, h))]
    args = [decays, proj, proj, proj, proj, gn]
    if has_state:
        spec = pl.BlockSpec((None, None, None, C_DK, C_DV), lambda b, h: (b, idx, h, 0, 0))
        in_specs += [spec, spec]
        args += [state_f, state_b]
    out_shape = [jax.ShapeDtypeStruct((nb * seq, D_MODEL), BF16)]
    out_specs = [pl.BlockSpec((seq, C_DV), lambda b, h: (b, h))]
    if emit_state:
        st = jax.ShapeDtypeStruct((nb, C_HEADS, C_DK, C_DV), F32)
        st_spec = pl.BlockSpec((None, None, C_DK, C_DV), lambda b, h: (b, h, 0, 0))
        out_shape += [st, st]
        out_specs += [st_spec, st_spec]
    blocks = 4 * seq * C_DK * 4 + C_DV * 4 + 2 * C_DK * C_DV * 4 + seq * C_DV * 2
    return pl.pallas_call(
        functools.partial(_ret_kernel, seq=seq, has_state=has_state, emit_state=emit_state),
        out_shape=tuple(out_shape),
        grid=(nb, C_HEADS),
        in_specs=in_specs,
        out_specs=tuple(out_specs),
        scratch_shapes=[pltpu.VMEM((C_DK, C_DV), F32), pltpu.VMEM((seq, C_DV), F32)],
        compiler_params=_params(
            _vmem_limit(blocks, C_DK * C_DV * 4 + seq * C_DV * 4, 16 * C_CHUNK * C_DV * 4), 2),
        name="retention_latent" if latent else "retention_prompt",
    )(*args)


def kernel(x_prompt, x_sample, c, cache_a_k, cache_a_v, cache_b_k, cache_b_v, state_ret_f, state_ret_b,
           c_ctx, w_ada, b_ada, norm_pre, norm_post, w_in_even, w_out_even, a_sink, na_rpb,
           w_in_odd, w_out_odd, ret_decay_f, ret_decay_b, ret_gn):
    x = jnp.concatenate([x_prompt.reshape(NP, D_MODEL), x_sample.reshape(NS, D_MODEL)], axis=0)
    cvec = jnp.concatenate(
        [c_ctx[None, :], c, jnp.zeros((MOD_ROWS - 1 - DEC_BATCH, D_MODEL), F32)], axis=0)
    mods = _adaln(cvec, w_ada, b_ada).reshape(DEPTH, MOD_ROWS, 3, 1, D_MODEL)
    gain_pre = norm_pre.reshape(DEPTH, 1, D_MODEL)
    gain_post = norm_post.reshape(DEPTH, 1, D_MODEL)
    cos, sin = _rope_tables()
    caches = [[], [], [], []]
    states = [[], []]
    for layer in range(DEPTH):
        idx = layer // 2
        if layer % 2 == 0:
            proj = _inproj(x, gain_pre, mods, layer, w_in_even[idx].astype(BF16))
            u_p, ka, va, kb, vb = _ctx_attn(proj, a_sink[idx])
            for dst, val in zip(caches, (ka, va, kb, vb)):
                dst.append(val)
            u_a = _win_attn(proj, a_sink[idx], cache_a_k, cache_a_v, idx, cos, sin)
            u_b = _na_attn(proj, _na_bias(na_rpb[idx]), cache_b_k, cache_b_v, idx)
            u = jnp.concatenate([u_p, jnp.concatenate([u_a, u_b], axis=1)], axis=0)
            w_out = w_out_even[idx]
        else:
            proj = _inproj(x, gain_pre, mods, layer, w_in_odd[idx].astype(BF16))
            decays = jnp.stack([ret_decay_f[idx], ret_decay_b[idx]], axis=0)
            gn = ret_gn.reshape(DEPTH // 2, 1, D_MODEL)
            u_p, s_f, s_b = _retention(proj, decays, gn, idx, latent=False)
            states[0].append(s_f)
            states[1].append(s_b)
            (u_s,) = _retention(proj, decays, gn, idx, latent=True,
                                state_f=state_ret_f, state_b=state_ret_b)
            u = jnp.concatenate([u_p, u_s], axis=0)
            w_out = w_out_odd[idx]
        x = _outproj(u, w_out.astype(BF16), x, gain_post, mods, layer)
    y_prompt = x[:NP].reshape(BATCH, SEQ, D_MODEL)
    y_sample = x[NP:].reshape(DEC_BATCH, DEC_SEQ, D_MODEL)
    return (y_prompt, y_sample) + tuple(jnp.stack(v, axis=1) for v in caches + states)
```

```python
import functools

import numpy as np
import jax
import jax.numpy as jnp
from jax import lax
from jax.experimental import pallas as pl
from jax.experimental.pallas import tpu as pltpu

D_MODEL = 2048
BATCH = 16
SEQ = 256
DEPTH = 4
DEC_BATCH = 4
DEC_SEQ = 1024
PAST_LEN = 256
GRID_W = 64
HEAD_DIM = 128
A_HEADS = 8
A_KV_HEADS = 2
A_GROUP = A_HEADS // A_KV_HEADS
A_WINDOW = 128
A_BLOCK = 128
B_HEADS = 8
NA_ROWS = 8
NA_COLS = 16
C_HEADS = 8
C_DK = D_MODEL // C_HEADS
C_DV = D_MODEL // C_HEADS
C_CHUNK = 128
ROPE_THETA = 10000.0
EPS = 1e-6

N_EVEN = (DEPTH + 1) // 2
N_ODD = DEPTH // 2
A_Q = A_HEADS * HEAD_DIM
A_KV = A_KV_HEADS * HEAD_DIM
B_W = B_HEADS * HEAD_DIM
EVEN_WIDTH = A_Q + B_W
EVEN_IN = A_Q + 2 * A_KV + 3 * B_W + EVEN_WIDTH
ODD_IN = 4 * D_MODEL
GRID_ROWS = DEC_SEQ // GRID_W

NP = BATCH * SEQ
NS = DEC_BATCH * DEC_SEQ
MOD_ROWS = 8

COL_QA = 0
COL_KA = A_Q
COL_VA = A_Q + A_KV
COL_QB = A_Q + 2 * A_KV
COL_KB = COL_QB + B_W
COL_VB = COL_KB + B_W
COL_GATE = COL_VB + B_W

NA_QROWS = 8
NA_KROWS = 12
NA_Q = NA_QROWS * GRID_W
NA_K = NA_KROWS * GRID_W
NA_KSHIFT = (GRID_ROWS - NA_KROWS) * GRID_W

V7X_VMEM_BYTES = 64 * 1024 * 1024
VMEM_HEADROOM = 8 * 1024 * 1024

F32 = jnp.float32
BF16 = jnp.bfloat16
NEG_INF = float("-inf")


def _vmem_limit(block_bytes, scratch_bytes=0, temp_bytes=0):
    need = 2 * block_bytes + scratch_bytes + temp_bytes + VMEM_HEADROOM
    return int(min(need, V7X_VMEM_BYTES - 4 * 1024 * 1024))


def _params(vmem_bytes, ndims):
    return pltpu.CompilerParams(dimension_semantics=("arbitrary",) * ndims, vmem_limit_bytes=vmem_bytes)


def _silu(x):
    return x / (1.0 + jnp.exp(-x))


def _dot(a, b):
    return jnp.dot(a, b, preferred_element_type=F32)


def _dot_nt(a, b):
    return lax.dot_general(a, b, (((1,), (1,)), ((), ())), preferred_element_type=F32)


def _dot_tn(a, b):
    return lax.dot_general(a, b, (((0,), (0,)), ((), ())), preferred_element_type=F32)


def _any_spec():
    return pl.BlockSpec(memory_space=pl.ANY)


ADA_TN = 1024


def _adaln_kernel(c_ref, w_ref, b_ref, o_ref):
    a = _silu(c_ref[...]).astype(BF16)
    o_ref[...] = _dot(a, w_ref[...].astype(BF16)) + b_ref[...]


def _adaln(cvec, w_ada, b_ada):
    n = 3 * D_MODEL
    blocks = MOD_ROWS * D_MODEL * 4 + D_MODEL * ADA_TN * 4 + ADA_TN * 4 + MOD_ROWS * ADA_TN * 4
    return pl.pallas_call(
        _adaln_kernel,
        out_shape=jax.ShapeDtypeStruct((DEPTH, MOD_ROWS, n), F32),
        grid=(DEPTH, n // ADA_TN),
        in_specs=[
            pl.BlockSpec((MOD_ROWS, D_MODEL), lambda l, j: (0, 0)),
            pl.BlockSpec((None, D_MODEL, ADA_TN), lambda l, j: (l, 0, j)),
            pl.BlockSpec((None, 1, ADA_TN), lambda l, j: (l, 0, j)),
        ],
        out_specs=pl.BlockSpec((None, MOD_ROWS, ADA_TN), lambda l, j: (l, 0, j)),
        compiler_params=_params(_vmem_limit(blocks, temp_bytes=D_MODEL * ADA_TN * 2), 2),
        name="adaln",
    )(cvec, w_ada, b_ada.reshape(DEPTH, 1, n))


def _mod_row(tile, tm, latent):
    return 1 + tile // (DEC_SEQ // tm) if latent else 0


IN_TM = 1024
IN_TN = 512
IN_ROWS = 128


def _inproj_kernel(x_ref, g_ref, sh_ref, sc_ref, w_ref, o_ref, h_ref):
    @pl.when(pl.program_id(1) == 0)
    def _():
        gain = g_ref[...]
        one_sc = 1.0 + sc_ref[...]
        sh = sh_ref[...]

        def body(r, carry):
            sl = pl.ds(pl.multiple_of(r * IN_ROWS, IN_ROWS), IN_ROWS)
            x = x_ref[sl, :]
            ms = jnp.mean(x * x, axis=-1, keepdims=True)
            y = (x * lax.rsqrt(ms + EPS)) * gain
            h_ref[sl, :] = (y * one_sc + sh).astype(BF16)
            return carry

        lax.fori_loop(0, IN_TM // IN_ROWS, body, 0)

    o_ref[...] = _dot(h_ref[...], w_ref[...].astype(BF16))


def _inproj(x, gain, mods, layer, w, latent):
    ntok = x.shape[0]
    n = w.shape[2]
    blocks = IN_TM * D_MODEL * 4 + 3 * D_MODEL * 4 + D_MODEL * IN_TN * 4 + IN_TM * IN_TN * 4
    mod_spec = lambda which: pl.BlockSpec(
        (None, None, None, 1, D_MODEL), lambda i, j: (layer, _mod_row(i, IN_TM, latent), which, 0, 0))
    return pl.pallas_call(
        _inproj_kernel,
        out_shape=jax.ShapeDtypeStruct((ntok, n), F32),
        grid=(ntok // IN_TM, n // IN_TN),
        in_specs=[
            pl.BlockSpec((IN_TM, D_MODEL), lambda i, j: (i, 0)),
            pl.BlockSpec((None, 1, D_MODEL), lambda i, j: (layer, 0, 0)),
            mod_spec(0),
            mod_spec(1),
            pl.BlockSpec((None, D_MODEL, IN_TN), lambda i, j: (layer // 2, 0, j)),
        ],
        out_specs=pl.BlockSpec((IN_TM, IN_TN), lambda i, j: (i, j)),
        scratch_shapes=[pltpu.VMEM((IN_TM, D_MODEL), BF16)],
        compiler_params=_params(
            _vmem_limit(blocks, IN_TM * D_MODEL * 2, 4 * IN_ROWS * D_MODEL * 4 + D_MODEL * IN_TN * 2), 2),
        name=f"inproj_l{layer}_{'latent' if latent else 'prompt'}",
    )(x, gain, mods, mods, w)


OUT_TM = 512


def _outproj_kernel(*refs, n_pieces):
    u_refs = refs[:n_pieces]
    w_refs = refs[n_pieces:2 * n_pieces]
    x_ref, gain_ref, gate_ref, o_ref = refs[2 * n_pieces:]
    out = _dot(u_refs[0][...], w_refs[0][...])
    for u_ref, w_ref in zip(u_refs[1:], w_refs[1:]):
        out = out + _dot(u_ref[...], w_ref[...])
    ms = jnp.mean(out * out, axis=-1, keepdims=True)
    y = (out * lax.rsqrt(ms + EPS)) * gain_ref[...]
    o_ref[...] = x_ref[...] + gate_ref[...] * y


def _outproj(us, w, x, gain, mods, layer, latent):
    ntok = x.shape[0]
    widths = [u.shape[1] for u in us]
    kp = widths[0]
    assert all(k == kp for k in widths) and kp * len(us) == w.shape[1]
    blocks = OUT_TM * w.shape[1] * 2 + w.shape[1] * D_MODEL * 2 + 2 * OUT_TM * D_MODEL * 4 + 2 * D_MODEL * 4
    u_specs = [pl.BlockSpec((OUT_TM, kp), lambda i: (i, 0)) for _ in us]
    w_specs = [pl.BlockSpec((None, kp, D_MODEL), lambda i, p=p: (layer // 2, p, 0)) for p in range(len(us))]
    return pl.pallas_call(
        functools.partial(_outproj_kernel, n_pieces=len(us)),
        out_shape=jax.ShapeDtypeStruct((ntok, D_MODEL), F32),
        grid=(ntok // OUT_TM,),
        in_specs=u_specs + w_specs + [
            pl.BlockSpec((OUT_TM, D_MODEL), lambda i: (i, 0)),
            pl.BlockSpec((None, 1, D_MODEL), lambda i: (layer, 0, 0)),
            pl.BlockSpec((None, None, None, 1, D_MODEL),
                         lambda i: (layer, _mod_row(i, OUT_TM, latent), 2, 0, 0)),
        ],
        out_specs=pl.BlockSpec((OUT_TM, D_MODEL), lambda i: (i, 0)),
        compiler_params=_params(_vmem_limit(blocks, temp_bytes=3 * OUT_TM * D_MODEL * 4), 1),
        name=f"outproj_l{layer}_{'latent' if latent else 'prompt'}",
    )(*us, *([w] * len(us)), x, gain, mods)


def _softmax_pv(scores, values, sink=None):
    m = jnp.max(scores[0], axis=-1, keepdims=True)
    for s in scores[1:]:
        m = jnp.maximum(m, jnp.max(s, axis=-1, keepdims=True))
    if sink is not None:
        m = jnp.maximum(m, sink)
    den = None
    acc = None
    for s, v in zip(scores, values):
        p = jnp.exp(s - m)
        ps = jnp.sum(p, axis=-1, keepdims=True)
        pv = _dot(p.astype(BF16), v)
        den = ps if den is None else den + ps
        acc = pv if acc is None else acc + pv
    if sink is not None:
        den = den + jnp.exp(sink - m)
    return acc / den


def _ctx_attn_kernel(sink_ref, p_ref, *refs):
    u_ref, ak_ref, av_ref, bk_ref, bv_ref = refs[-5:]
    scale = HEAD_DIM ** -0.5

    def col(c):
        return p_ref[:, c:c + HEAD_DIM]

    def head(cq, k, v, cg, sink, cu):
        q = col(cq).astype(BF16)
        s = _dot_nt(q, k) * scale
        o = _softmax_pv([s], [v], sink)
        u_ref[:, cu:cu + HEAD_DIM] = (o * _silu(col(cg))).astype(BF16)

    for n in range(A_KV_HEADS):
        k32 = col(COL_KA + n * HEAD_DIM)
        v32 = col(COL_VA + n * HEAD_DIM)
        ak_ref[n] = k32
        av_ref[n] = v32
        k = k32.astype(BF16)
        v = v32.astype(BF16)
        for g in range(A_GROUP):
            h = n * A_GROUP + g
            head(COL_QA + h * HEAD_DIM, k, v, COL_GATE + h * HEAD_DIM, sink_ref[h], h * HEAD_DIM)
    for h in range(B_HEADS):
        k32 = col(COL_KB + h * HEAD_DIM)
        v32 = col(COL_VB + h * HEAD_DIM)
        bk_ref[h] = k32
        bv_ref[h] = v32
        head(COL_QB + h * HEAD_DIM, k32.astype(BF16), v32.astype(BF16),
             COL_GATE + A_Q + h * HEAD_DIM, None, A_Q + h * HEAD_DIM)


def _ctx_attn(proj, sink, idx, prev_caches):
    blocks = (SEQ * EVEN_IN * 4 + SEQ * EVEN_WIDTH * 2
              + 2 * (A_KV_HEADS + B_HEADS) * SEQ * HEAD_DIM * 4)
    heads = (A_KV_HEADS, A_KV_HEADS, B_HEADS, B_HEADS)
    cache = lambda nh: jax.ShapeDtypeStruct((BATCH, N_EVEN, nh, SEQ, HEAD_DIM), F32)
    cache_spec = lambda nh: pl.BlockSpec((None, None, nh, SEQ, HEAD_DIM), lambda b: (b, idx, 0, 0, 0))
    n_prev = len(prev_caches)
    return pl.pallas_call(
        _ctx_attn_kernel,
        out_shape=(jax.ShapeDtypeStruct((NP, EVEN_WIDTH), BF16),) + tuple(cache(nh) for nh in heads),
        grid=(BATCH,),
        in_specs=[
            pl.BlockSpec(memory_space=pltpu.SMEM),
            pl.BlockSpec((SEQ, EVEN_IN), lambda b: (b, 0)),
        ] + [_any_spec() for _ in prev_caches],
        out_specs=(pl.BlockSpec((SEQ, EVEN_WIDTH), lambda b: (b, 0)),) + tuple(cache_spec(nh) for nh in heads),
        input_output_aliases={2 + k: 1 + k for k in range(n_prev)},
        compiler_params=_params(_vmem_limit(blocks, temp_bytes=8 * SEQ * SEQ * 4), 1),
        name=f"ctx_attn_{idx}",
    )(sink, proj, *prev_caches)


WIN_PAD = DEC_SEQ + 2 * A_BLOCK


def _rope_tables():
    t = jnp.arange(DEC_SEQ)
    half = HEAD_DIM // 2
    nf = half // 2
    inv = ROPE_THETA ** (-jnp.arange(nf, dtype=F32) / nf)
    ang_r = (t // GRID_W).astype(F32)[:, None] * inv[None]
    ang_c = (t % GRID_W).astype(F32)[:, None] * inv[None]
    cos = jnp.concatenate([jnp.cos(ang_r)] * 2 + [jnp.cos(ang_c)] * 2, axis=-1)
    sin = jnp.concatenate([-jnp.sin(ang_r), jnp.sin(ang_r), -jnp.sin(ang_c), jnp.sin(ang_c)], axis=-1)
    return cos, sin


def _rope(x, cos, sin):
    quarter = HEAD_DIM // 4
    lane = lax.broadcasted_iota(jnp.int32, x.shape, 1)
    first = (lane & (2 * quarter - 1)) < quarter
    partner = jnp.where(first, pltpu.roll(x, HEAD_DIM - quarter, 1), pltpu.roll(x, quarter, 1))
    return x * cos + partner * sin


def _win_attn_kernel(sink_ref, q_ref, k_ref, v_ref, ck_ref, cv_ref, gate_ref, cos_ref, sin_ref,
                     u_ref, kpad_ref, vpad_ref):
    n = pl.program_id(1)
    j = pl.program_id(2)
    scale = HEAD_DIM ** -0.5

    @pl.when(j == 0)
    def _():
        zeros = jnp.zeros((A_BLOCK, HEAD_DIM), BF16)
        for ref in (kpad_ref, vpad_ref):
            ref[0:A_BLOCK, :] = zeros
            ref[A_BLOCK + DEC_SEQ:WIN_PAD, :] = zeros
        kpad_ref[A_BLOCK:A_BLOCK + DEC_SEQ, :] = _rope(k_ref[...], cos_ref[...], sin_ref[...]).astype(BF16)
        vpad_ref[A_BLOCK:A_BLOCK + DEC_SEQ, :] = v_ref[...].astype(BF16)

    q0 = pl.multiple_of(j * A_BLOCK, A_BLOCK)
    cos_q = cos_ref[pl.ds(q0, A_BLOCK), :]
    sin_q = sin_ref[pl.ds(q0, A_BLOCK), :]
    q = jnp.concatenate(
        [_rope(q_ref[:, g * HEAD_DIM:(g + 1) * HEAD_DIM], cos_q, sin_q).astype(BF16) for g in range(A_GROUP)],
        axis=0)
    band_k = kpad_ref[pl.ds(q0, 3 * A_BLOCK), :]
    band_v = vpad_ref[pl.ds(q0, 3 * A_BLOCK), :]

    rows = A_GROUP * A_BLOCK
    s_c = _dot_nt(q, ck_ref[...].astype(BF16)) * scale
    s_w = _dot_nt(q, band_k) * scale
    row = lax.broadcasted_iota(jnp.int32, (rows, 1), 0)
    qi = row & (A_BLOCK - 1)
    kk = lax.broadcasted_iota(jnp.int32, (rows, 3 * A_BLOCK), 1)
    first_key = jnp.where(j == 0, A_BLOCK, 0)
    last_key = jnp.where(j == DEC_SEQ // A_BLOCK - 1, 2 * A_BLOCK - 1, 3 * A_BLOCK - 1)
    lower = jnp.maximum(qi, first_key)
    upper = jnp.minimum(qi + 2 * A_WINDOW, last_key)
    s_w = jnp.where((kk >= lower) & (kk <= upper), s_w, NEG_INF)

    head = lax.shift_right_logical(row, A_BLOCK.bit_length() - 1)
    sink = jnp.zeros((rows, 1), F32)
    for g in range(A_GROUP):
        sink = jnp.where(head == g, sink_ref[n * A_GROUP + g], sink)

    o = _softmax_pv([s_c, s_w], [cv_ref[...].astype(BF16), band_v], sink)
    for g in range(A_GROUP):
        cols = slice(g * HEAD_DIM, (g + 1) * HEAD_DIM)
        u_ref[:, cols] = (o[g * A_BLOCK:(g + 1) * A_BLOCK] * _silu(gate_ref[:, cols])).astype(BF16)


def _win_attn(proj, sink, cache_k, cache_v, idx, cos, sin):
    nqb = DEC_SEQ // A_BLOCK
    gw = A_GROUP * HEAD_DIM
    blocks = (2 * A_BLOCK * gw * 4 + 2 * DEC_SEQ * HEAD_DIM * 4 + 2 * PAST_LEN * HEAD_DIM * 4
              + 2 * DEC_SEQ * HEAD_DIM * 4 + A_BLOCK * gw * 2)
    cache_spec = pl.BlockSpec((None, None, None, PAST_LEN, HEAD_DIM), lambda b, n, j: (b, idx, n, 0, 0))
    table_spec = pl.BlockSpec((DEC_SEQ, HEAD_DIM), lambda b, n, j: (0, 0))
    return pl.pallas_call(
        _win_attn_kernel,
        out_shape=jax.ShapeDtypeStruct((NS, A_Q), BF16),
        grid=(DEC_BATCH, A_KV_HEADS, nqb),
        in_specs=[
            pl.BlockSpec(memory_space=pltpu.SMEM),
            pl.BlockSpec((A_BLOCK, gw), lambda b, n, j: (b * nqb + j, COL_QA // gw + n)),
            pl.BlockSpec((DEC_SEQ, HEAD_DIM), lambda b, n, j: (b, COL_KA // HEAD_DIM + n)),
            pl.BlockSpec((DEC_SEQ, HEAD_DIM), lambda b, n, j: (b, COL_VA // HEAD_DIM + n)),
            cache_spec,
            cache_spec,
            pl.BlockSpec((A_BLOCK, gw), lambda b, n, j: (b * nqb + j, COL_GATE // gw + n)),
            table_spec,
            table_spec,
        ],
        out_specs=pl.BlockSpec((A_BLOCK, gw), lambda b, n, j: (b * nqb + j, n)),
        scratch_shapes=[pltpu.VMEM((WIN_PAD, HEAD_DIM), BF16), pltpu.VMEM((WIN_PAD, HEAD_DIM), BF16)],
        compiler_params=_params(
            _vmem_limit(blocks, 2 * WIN_PAD * HEAD_DIM * 2, 8 * A_GROUP * A_BLOCK * 3 * A_BLOCK * 4), 3),
        name=f"win_attn_{idx}",
    )(sink, proj, proj, proj, cache_k, cache_v, proj, cos, sin)


def _na_bias(rpb):
    c = np.arange(GRID_W)
    c0 = np.clip(c - NA_COLS // 2, 0, GRID_W - NA_COLS)
    col_ok = (c[None, :] >= c0[:, None]) & (c[None, :] < c0[:, None] + NA_COLS)
    dc = np.clip(c[None, :] - c[:, None] + NA_COLS - 1, 0, 2 * NA_COLS - 2)
    n_dr = 2 * NA_ROWS - 1
    tiles = jnp.where(col_ok[None, None], rpb[:, :, dc], NEG_INF)
    tiles = jnp.concatenate([tiles, jnp.full((B_HEADS, 1, GRID_W, GRID_W), NEG_INF, F32)], axis=1)
    tile_idx = np.full((2, NA_QROWS, NA_KROWS), n_dr, np.int32)
    for g in range(2):
        for rl in range(NA_QROWS):
            r = g * NA_QROWS + rl
            r0 = min(max(r - NA_ROWS // 2, 0), GRID_ROWS - NA_ROWS)
            for kl in range(NA_KROWS):
                kr = g * (GRID_ROWS - NA_KROWS) + kl
                if r0 <= kr < r0 + NA_ROWS:
                    tile_idx[g, rl, kl] = kr - r + NA_ROWS - 1
    bias = tiles[:, tile_idx]
    return bias.transpose(0, 1, 2, 4, 3, 5).reshape(B_HEADS, 2, NA_Q, NA_K)


def _na_attn_kernel(q_ref, k_ref, v_ref, ck_ref, cv_ref, gate_ref, bias_ref, u_ref):
    g = pl.program_id(1)
    scale = HEAD_DIM ** -0.5
    k0 = pl.multiple_of(g * NA_KSHIFT, NA_KSHIFT)
    q = q_ref[...].astype(BF16)
    kwin = k_ref[pl.ds(k0, NA_K), :].astype(BF16)
    vwin = v_ref[pl.ds(k0, NA_K), :].astype(BF16)
    s_c = _dot_nt(q, ck_ref[...].astype(BF16)) * scale
    s_n = _dot_nt(q, kwin) * scale + bias_ref[...]
    o = _softmax_pv([s_c, s_n], [cv_ref[...].astype(BF16), vwin])
    u_ref[...] = (o * _silu(gate_ref[...])).astype(BF16)


def _na_attn(proj, bias, cache_k, cache_v, idx):
    per_seq = DEC_SEQ // NA_Q
    blocks = (2 * NA_Q * HEAD_DIM * 4 + 2 * DEC_SEQ * HEAD_DIM * 4 + 2 * PAST_LEN * HEAD_DIM * 4
              + NA_Q * NA_K * 4 + NA_Q * HEAD_DIM * 2)
    cache_spec = pl.BlockSpec((None, None, None, PAST_LEN, HEAD_DIM), lambda h, g, b: (b, idx, h, 0, 0))
    return pl.pallas_call(
        _na_attn_kernel,
        out_shape=jax.ShapeDtypeStruct((NS, B_W), BF16),
        grid=(B_HEADS, per_seq, DEC_BATCH),
        in_specs=[
            pl.BlockSpec((NA_Q, HEAD_DIM), lambda h, g, b: (b * per_seq + g, COL_QB // HEAD_DIM + h)),
            pl.BlockSpec((DEC_SEQ, HEAD_DIM), lambda h, g, b: (b, COL_KB // HEAD_DIM + h)),
            pl.BlockSpec((DEC_SEQ, HEAD_DIM), lambda h, g, b: (b, COL_VB // HEAD_DIM + h)),
            cache_spec,
            cache_spec,
            pl.BlockSpec((NA_Q, HEAD_DIM), lambda h, g, b: (b * per_seq + g, (COL_GATE + A_Q) // HEAD_DIM + h)),
            pl.BlockSpec((None, None, NA_Q, NA_K), lambda h, g, b: (h, g, 0, 0)),
        ],
        out_specs=pl.BlockSpec((NA_Q, HEAD_DIM), lambda h, g, b: (b * per_seq + g, h)),
        compiler_params=_params(_vmem_limit(blocks, temp_bytes=6 * NA_Q * (NA_K + PAST_LEN) * 4), 3),
        name=f"na_attn_{idx}",
    )(proj, proj, proj, cache_k, cache_v, proj, bias)


def _ret_kernel(*refs, seq, has_state, emit_state, n_prev):
    dec_ref, q_ref, k_ref, v_ref, gate_ref, gn_ref = refs[:6]
    pos = 6
    if has_state:
        s0f_ref, s0b_ref = refs[pos:pos + 2]
        pos += 2
    pos += n_prev
    u_ref = refs[pos]
    pos += 1
    if emit_state:
        sf_ref, sb_ref = refs[pos:pos + 2]
        pos += 2
    state_ref, o_ref = refs[pos:pos + 2]

    h = pl.program_id(1)
    nc = seq // C_CHUNK
    kscale = C_DK ** -0.5
    ii = lax.broadcasted_iota(jnp.int32, (C_CHUNK, C_CHUNK), 0).astype(F32)
    jj = lax.broadcasted_iota(jnp.int32, (C_CHUNK, C_CHUNK), 1).astype(F32)
    icol = lax.broadcasted_iota(jnp.int32, (C_CHUNK, 1), 0).astype(F32)

    def scan(direction, first):
        forward = direction == 0
        dec = dec_ref[direction, h]
        log_g = -jnp.exp(jnp.full((C_CHUNK, C_CHUNK), dec, F32))
        log_g_col = -jnp.exp(jnp.full((C_CHUNK, 1), dec, F32))
        log_g_row = -jnp.exp(jnp.full((1, C_DV), dec, F32))
        diff = (ii - jj) if forward else (jj - ii)
        dmat = jnp.where(diff >= 0, jnp.exp(log_g * jnp.maximum(diff, 0.0)), 0.0) * kscale
        if forward:
            q_dec = jnp.exp(log_g_col * (icol + 1.0))
            k_dec = jnp.exp(log_g_col * (C_CHUNK - 1.0 - icol)) * kscale
        else:
            q_dec = jnp.exp(log_g_col * (C_CHUNK - icol))
            k_dec = jnp.exp(log_g_col * icol) * kscale
        chunk_dec = jnp.exp(log_g_row * float(C_CHUNK))

        if has_state:
            state_ref[...] = (s0f_ref if forward else s0b_ref)[...]
        else:
            state_ref[...] = jnp.zeros((C_DK, C_DV), F32)

        def body(t, carry):
            c = t if forward else nc - 1 - t
            sl = pl.ds(pl.multiple_of(c * C_CHUNK, C_CHUNK), C_CHUNK)
            qi = q_ref[sl, :].astype(BF16)
            k32 = k_ref[sl, :]
            vi = v_ref[sl, :].astype(BF16)
            s = _dot_nt(qi, k32.astype(BF16)) * dmat
            inner = _dot(s.astype(BF16), vi)
            state = state_ref[...]
            cross = _dot(qi, state.astype(BF16)) * q_dec
            if first:
                o_ref[sl, :] = inner + cross
            else:
                o_ref[sl, :] = o_ref[sl, :] + (inner + cross)
            state_ref[...] = state * chunk_dec + _dot_tn((k32 * k_dec).astype(BF16), vi)
            return carry

        lax.fori_loop(0, nc, body, 0)

    scan(0, True)
    if emit_state:
        sf_ref[...] = state_ref[...]
    scan(1, False)
    if emit_state:
        sb_ref[...] = state_ref[...]

    gn = gn_ref[...]

    def norm_body(c, carry):
        sl = pl.ds(pl.multiple_of(c * C_CHUNK, C_CHUNK), C_CHUNK)
        o = o_ref[sl, :]
        mu = jnp.mean(o, axis=-1, keepdims=True)
        d = o - mu
        var = jnp.mean(d * d, axis=-1, keepdims=True)
        y = (d * lax.rsqrt(var + EPS)) * gn
        u_ref[sl, :] = (y * _silu(gate_ref[sl, :])).astype(BF16)
        return carry

    lax.fori_loop(0, nc, norm_body, 0)


def _retention(proj, decays, gn, idx, *, latent, state_f=None, state_b=None, prev_states=()):
    seq = DEC_SEQ if latent else SEQ
    nb = DEC_BATCH if latent else BATCH
    has_state = latent
    emit_state = not latent
    tok = lambda col0: pl.BlockSpec((seq, C_DK), lambda b, h: (b, col0 + h))
    in_specs = [pl.BlockSpec(memory_space=pltpu.SMEM), tok(0), tok(C_HEADS), tok(2 * C_HEADS),
                tok(3 * C_HEADS), pl.BlockSpec((None, 1, C_DV), lambda b, h: (idx, 0, h))]
    args = [decays, proj, proj, proj, proj, gn]
    if has_state:
        spec = pl.BlockSpec((None, None, None, C_DK, C_DV), lambda b, h: (b, idx, h, 0, 0))
        in_specs += [spec, spec]
        args += [state_f, state_b]
    aliases = {len(args) + k: 1 + k for k in range(len(prev_states))}
    in_specs += [_any_spec() for _ in prev_states]
    args += list(prev_states)
    out_shape = [jax.ShapeDtypeStruct((nb * seq, D_MODEL), BF16)]
    out_specs = [pl.BlockSpec((seq, C_DV), lambda b, h: (b, h))]
    if emit_state:
        st = jax.ShapeDtypeStruct((nb, N_ODD, C_HEADS, C_DK, C_DV), F32)
        st_spec = pl.BlockSpec((None, None, None, C_DK, C_DV), lambda b, h: (b, idx, h, 0, 0))
        out_shape += [st, st]
        out_specs += [st_spec, st_spec]
    blocks = 4 * seq * C_DK * 4 + C_DV * 4 + 2 * C_DK * C_DV * 4 + seq * C_DV * 2
    return pl.pallas_call(
        functools.partial(_ret_kernel, seq=seq, has_state=has_state, emit_state=emit_state,
                          n_prev=len(prev_states)),
        out_shape=tuple(out_shape),
        grid=(nb, C_HEADS),
        in_specs=in_specs,
        out_specs=tuple(out_specs),
        scratch_shapes=[pltpu.VMEM((C_DK, C_DV), F32), pltpu.VMEM((seq, C_DV), F32)],
        input_output_aliases=aliases,
        compiler_params=_params(
            _vmem_limit(blocks, C_DK * C_DV * 4 + seq * C_DV * 4, 16 * C_CHUNK * C_DV * 4), 2),
        name=f"retention_{'latent' if latent else 'prompt'}_{idx}",
    )(*args)


def kernel(x_prompt, x_sample, c, cache_a_k, cache_a_v, cache_b_k, cache_b_v, state_ret_f, state_ret_b,
           c_ctx, w_ada, b_ada, norm_pre, norm_post, w_in_even, w_out_even, a_sink, na_rpb,
           w_in_odd, w_out_odd, ret_decay_f, ret_decay_b, ret_gn):
    xp = x_prompt.reshape(NP, D_MODEL)
    xs = x_sample.reshape(NS, D_MODEL)
    cvec = jnp.concatenate(
        [c_ctx[None, :], c, jnp.zeros((MOD_ROWS - 1 - DEC_BATCH, D_MODEL), F32)], axis=0)
    mods = _adaln(cvec, w_ada, b_ada).reshape(DEPTH, MOD_ROWS, 3, 1, D_MODEL)
    gain_pre = norm_pre.reshape(DEPTH, 1, D_MODEL)
    gain_post = norm_post.reshape(DEPTH, 1, D_MODEL)
    gn = ret_gn.reshape(N_ODD, 1, D_MODEL)
    w_out_even_bf = w_out_even.astype(BF16)
    w_out_odd_bf = w_out_odd.astype(BF16)
    cos, sin = _rope_tables()
    caches = ()
    states = ()
    for layer in range(DEPTH):
        idx = layer // 2
        if layer % 2 == 0:
            proj_p = _inproj(xp, gain_pre, mods, layer, w_in_even, False)
            proj_s = _inproj(xs, gain_pre, mods, layer, w_in_even, True)
            u_p, *caches = _ctx_attn(proj_p, a_sink[idx], idx, caches)
            u_a = _win_attn(proj_s, a_sink[idx], cache_a_k, cache_a_v, idx, cos, sin)
            u_b = _na_attn(proj_s, _na_bias(na_rpb[idx]), cache_b_k, cache_b_v, idx)
            us_p, us_s, w_out = [u_p], [u_a, u_b], w_out_even_bf
        else:
            proj_p = _inproj(xp, gain_pre, mods, layer, w_in_odd, False)
            proj_s = _inproj(xs, gain_pre, mods, layer, w_in_odd, True)
            decays = jnp.stack([ret_decay_f[idx], ret_decay_b[idx]], axis=0)
            u_p, *states = _retention(proj_p, decays, gn, idx, latent=False, prev_states=states)
            (u_s,) = _retention(proj_s, decays, gn, idx, latent=True,
                                state_f=state_ret_f, state_b=state_ret_b)
            us_p, us_s = [u_p], [u_s]
            w_out = w_out_odd_bf
        xp = _outproj(us_p, w_out, xp, gain_post, mods, layer, False)
        xs = _outproj(us_s, w_out, xs, gain_post, mods, layer, True)
    return (xp.reshape(BATCH, SEQ, D_MODEL), xs.reshape(DEC_BATCH, DEC_SEQ, D_MODEL), *caches, *states)
```

```python
import functools

import numpy as np
import jax
import jax.numpy as jnp
from jax import lax
from jax.experimental import pallas as pl
from jax.experimental.pallas import tpu as pltpu

D_MODEL = 2048
BATCH = 16
SEQ = 256
DEPTH = 4
DEC_BATCH = 4
DEC_SEQ = 1024
PAST_LEN = 256
GRID_W = 64
HEAD_DIM = 128
A_HEADS = 8
A_KV_HEADS = 2
A_GROUP = A_HEADS // A_KV_HEADS
A_WINDOW = 128
A_BLOCK = 128
B_HEADS = 8
NA_ROWS = 8
NA_COLS = 16
C_HEADS = 8
C_DK = D_MODEL // C_HEADS
C_DV = D_MODEL // C_HEADS
C_CHUNK = 128
ROPE_THETA = 10000.0
EPS = 1e-6

N_EVEN = (DEPTH + 1) // 2
N_ODD = DEPTH // 2
A_Q = A_HEADS * HEAD_DIM
A_KV = A_KV_HEADS * HEAD_DIM
B_W = B_HEADS * HEAD_DIM
EVEN_WIDTH = A_Q + B_W
EVEN_IN = A_Q + 2 * A_KV + 3 * B_W + EVEN_WIDTH
ODD_IN = 4 * D_MODEL
GRID_ROWS = DEC_SEQ // GRID_W

NP = BATCH * SEQ
NS = DEC_BATCH * DEC_SEQ
MOD_ROWS = 8

COL_QA = 0
COL_KA = A_Q
COL_VA = A_Q + A_KV
COL_QB = A_Q + 2 * A_KV
COL_KB = COL_QB + B_W
COL_VB = COL_KB + B_W
COL_GATE = COL_VB + B_W

NA_QROWS = 8
NA_KROWS = 12
NA_Q = NA_QROWS * GRID_W
NA_K = NA_KROWS * GRID_W
NA_KSHIFT = (GRID_ROWS - NA_KROWS) * GRID_W
NA_SUB = 128

V7X_VMEM_BYTES = 64 * 1024 * 1024
VMEM_HEADROOM = 8 * 1024 * 1024

F32 = jnp.float32
BF16 = jnp.bfloat16
NEG_INF = float("-inf")


def _vmem_limit(block_bytes, scratch_bytes=0, temp_bytes=0):
    need = 2 * block_bytes + scratch_bytes + temp_bytes + VMEM_HEADROOM
    return int(min(need, V7X_VMEM_BYTES - 4 * 1024 * 1024))


def _params(vmem_bytes, ndims):
    return pltpu.CompilerParams(dimension_semantics=("arbitrary",) * ndims, vmem_limit_bytes=vmem_bytes)


def _silu(x):
    return x / (1.0 + jnp.exp(-x))


def _dot(a, b):
    return jnp.dot(a, b, preferred_element_type=F32)


def _dot_nt(a, b):
    return lax.dot_general(a, b, (((1,), (1,)), ((), ())), preferred_element_type=F32)


def _dot_tn(a, b):
    return lax.dot_general(a, b, (((0,), (0,)), ((), ())), preferred_element_type=F32)


def _any_spec():
    return pl.BlockSpec(memory_space=pl.ANY)


ADA_TN = 1024


def _adaln_kernel(c_ref, w_ref, b_ref, o_ref):
    a = _silu(c_ref[...]).astype(BF16)
    o_ref[...] = _dot(a, w_ref[...].astype(BF16)) + b_ref[...]


def _adaln(cvec, w_ada, b_ada):
    n = 3 * D_MODEL
    blocks = MOD_ROWS * D_MODEL * 4 + D_MODEL * ADA_TN * 4 + ADA_TN * 4 + MOD_ROWS * ADA_TN * 4
    return pl.pallas_call(
        _adaln_kernel,
        out_shape=jax.ShapeDtypeStruct((DEPTH, MOD_ROWS, n), F32),
        grid=(DEPTH, n // ADA_TN),
        in_specs=[
            pl.BlockSpec((MOD_ROWS, D_MODEL), lambda l, j: (0, 0)),
            pl.BlockSpec((None, D_MODEL, ADA_TN), lambda l, j: (l, 0, j)),
            pl.BlockSpec((None, 1, ADA_TN), lambda l, j: (l, 0, j)),
        ],
        out_specs=pl.BlockSpec((None, MOD_ROWS, ADA_TN), lambda l, j: (l, 0, j)),
        compiler_params=_params(_vmem_limit(blocks, temp_bytes=D_MODEL * ADA_TN * 2), 2),
        name="adaln",
    )(cvec, w_ada, b_ada.reshape(DEPTH, 1, n))


def _mod_row(tile, tm, latent):
    return 1 + tile // (DEC_SEQ // tm) if latent else 0


IN_TM = 1024
IN_TN = 512
IN_ROWS = 128


def _inproj_kernel(x_ref, g_ref, sh_ref, sc_ref, w_ref, o_ref, h_ref):
    @pl.when(pl.program_id(1) == 0)
    def _():
        gain = g_ref[...]
        one_sc = 1.0 + sc_ref[...]
        sh = sh_ref[...]

        def body(r, carry):
            sl = pl.ds(pl.multiple_of(r * IN_ROWS, IN_ROWS), IN_ROWS)
            x = x_ref[sl, :]
            ms = jnp.mean(x * x, axis=-1, keepdims=True)
            y = (x * lax.rsqrt(ms + EPS)) * gain
            h_ref[sl, :] = (y * one_sc + sh).astype(BF16)
            return carry

        lax.fori_loop(0, IN_TM // IN_ROWS, body, 0)

    o_ref[...] = _dot(h_ref[...], w_ref[...].astype(BF16))


def _inproj(x, gain, mods, layer, w, latent):
    ntok = x.shape[0]
    n = w.shape[2]
    blocks = IN_TM * D_MODEL * 4 + 3 * D_MODEL * 4 + D_MODEL * IN_TN * 4 + IN_TM * IN_TN * 4
    mod_spec = lambda which: pl.BlockSpec(
        (None, None, None, 1, D_MODEL), lambda i, j: (layer, _mod_row(i, IN_TM, latent), which, 0, 0))
    return pl.pallas_call(
        _inproj_kernel,
        out_shape=jax.ShapeDtypeStruct((ntok, n), F32),
        grid=(ntok // IN_TM, n // IN_TN),
        in_specs=[
            pl.BlockSpec((IN_TM, D_MODEL), lambda i, j: (i, 0)),
            pl.BlockSpec((None, 1, D_MODEL), lambda i, j: (layer, 0, 0)),
            mod_spec(0),
            mod_spec(1),
            pl.BlockSpec((None, D_MODEL, IN_TN), lambda i, j: (layer // 2, 0, j)),
        ],
        out_specs=pl.BlockSpec((IN_TM, IN_TN), lambda i, j: (i, j)),
        scratch_shapes=[pltpu.VMEM((IN_TM, D_MODEL), BF16)],
        compiler_params=_params(
            _vmem_limit(blocks, IN_TM * D_MODEL * 2, 4 * IN_ROWS * D_MODEL * 4 + D_MODEL * IN_TN * 2), 2),
        name=f"inproj_l{layer}_{'latent' if latent else 'prompt'}",
    )(x, gain, mods, mods, w)


OUT_TM = 512


def _outproj_kernel(*refs, n_pieces):
    u_refs = refs[:n_pieces]
    w_refs = refs[n_pieces:2 * n_pieces]
    x_ref, gain_ref, gate_ref, o_ref = refs[2 * n_pieces:]
    out = _dot(u_refs[0][...], w_refs[0][...])
    for u_ref, w_ref in zip(u_refs[1:], w_refs[1:]):
        out = out + _dot(u_ref[...], w_ref[...])
    ms = jnp.mean(out * out, axis=-1, keepdims=True)
    y = (out * lax.rsqrt(ms + EPS)) * gain_ref[...]
    o_ref[...] = x_ref[...] + gate_ref[...] * y


def _outproj(us, w, x, gain, mods, layer, latent):
    ntok = x.shape[0]
    widths = [u.shape[1] for u in us]
    kp = widths[0]
    assert all(k == kp for k in widths) and kp * len(us) == w.shape[1]
    blocks = OUT_TM * w.shape[1] * 2 + w.shape[1] * D_MODEL * 2 + 2 * OUT_TM * D_MODEL * 4 + 2 * D_MODEL * 4
    u_specs = [pl.BlockSpec((OUT_TM, kp), lambda i: (i, 0)) for _ in us]
    w_specs = [pl.BlockSpec((None, kp, D_MODEL), lambda i, p=p: (layer // 2, p, 0)) for p in range(len(us))]
    return pl.pallas_call(
        functools.partial(_outproj_kernel, n_pieces=len(us)),
        out_shape=jax.ShapeDtypeStruct((ntok, D_MODEL), F32),
        grid=(ntok // OUT_TM,),
        in_specs=u_specs + w_specs + [
            pl.BlockSpec((OUT_TM, D_MODEL), lambda i: (i, 0)),
            pl.BlockSpec((None, 1, D_MODEL), lambda i: (layer, 0, 0)),
            pl.BlockSpec((None, None, None, 1, D_MODEL),
                         lambda i: (layer, _mod_row(i, OUT_TM, latent), 2, 0, 0)),
        ],
        out_specs=pl.BlockSpec((OUT_TM, D_MODEL), lambda i: (i, 0)),
        compiler_params=_params(_vmem_limit(blocks, temp_bytes=3 * OUT_TM * D_MODEL * 4), 1),
        name=f"outproj_l{layer}_{'latent' if latent else 'prompt'}",
    )(*us, *([w] * len(us)), x, gain, mods)


def _softmax_pv(scores, values, sink=None):
    m = jnp.max(scores[0], axis=-1, keepdims=True)
    for s in scores[1:]:
        m = jnp.maximum(m, jnp.max(s, axis=-1, keepdims=True))
    if sink is not None:
        m = jnp.maximum(m, sink)
    den = None
    acc = None
    for s, v in zip(scores, values):
        p = jnp.exp(s - m)
        ps = jnp.sum(p, axis=-1, keepdims=True)
        pv = _dot(p.astype(BF16), v)
        den = ps if den is None else den + ps
        acc = pv if acc is None else acc + pv
    if sink is not None:
        den = den + jnp.exp(sink - m)
    return acc / den


def _ctx_attn_kernel(sink_ref, p_ref, *refs):
    u_ref, ak_ref, av_ref, bk_ref, bv_ref = refs[-5:]
    scale = HEAD_DIM ** -0.5

    def col(c):
        return p_ref[:, c:c + HEAD_DIM]

    def head(cq, k, v, cg, sink, cu):
        q = col(cq).astype(BF16)
        s = _dot_nt(q, k) * scale
        o = _softmax_pv([s], [v], sink)
        u_ref[:, cu:cu + HEAD_DIM] = (o * _silu(col(cg))).astype(BF16)

    for n in range(A_KV_HEADS):
        k32 = col(COL_KA + n * HEAD_DIM)
        v32 = col(COL_VA + n * HEAD_DIM)
        ak_ref[n] = k32
        av_ref[n] = v32
        k = k32.astype(BF16)
        v = v32.astype(BF16)
        for g in range(A_GROUP):
            h = n * A_GROUP + g
            head(COL_QA + h * HEAD_DIM, k, v, COL_GATE + h * HEAD_DIM, sink_ref[h], h * HEAD_DIM)
    for h in range(B_HEADS):
        k32 = col(COL_KB + h * HEAD_DIM)
        v32 = col(COL_VB + h * HEAD_DIM)
        bk_ref[h] = k32
        bv_ref[h] = v32
        head(COL_QB + h * HEAD_DIM, k32.astype(BF16), v32.astype(BF16),
             COL_GATE + A_Q + h * HEAD_DIM, None, A_Q + h * HEAD_DIM)


def _ctx_attn(proj, sink, idx, prev_caches):
    blocks = (SEQ * EVEN_IN * 4 + SEQ * EVEN_WIDTH * 2
              + 2 * (A_KV_HEADS + B_HEADS) * SEQ * HEAD_DIM * 4)
    heads = (A_KV_HEADS, A_KV_HEADS, B_HEADS, B_HEADS)
    cache = lambda nh: jax.ShapeDtypeStruct((BATCH, N_EVEN, nh, SEQ, HEAD_DIM), F32)
    cache_spec = lambda nh: pl.BlockSpec((None, None, nh, SEQ, HEAD_DIM), lambda b: (b, idx, 0, 0, 0))
    n_prev = len(prev_caches)
    return pl.pallas_call(
        _ctx_attn_kernel,
        out_shape=(jax.ShapeDtypeStruct((NP, EVEN_WIDTH), BF16),) + tuple(cache(nh) for nh in heads),
        grid=(BATCH,),
        in_specs=[
            pl.BlockSpec(memory_space=pltpu.SMEM),
            pl.BlockSpec((SEQ, EVEN_IN), lambda b: (b, 0)),
        ] + [_any_spec() for _ in prev_caches],
        out_specs=(pl.BlockSpec((SEQ, EVEN_WIDTH), lambda b: (b, 0)),) + tuple(cache_spec(nh) for nh in heads),
        input_output_aliases={2 + k: 1 + k for k in range(n_prev)},
        compiler_params=_params(_vmem_limit(blocks, temp_bytes=8 * SEQ * SEQ * 4), 1),
        name=f"ctx_attn_{idx}",
    )(sink, proj, *prev_caches)


WIN_PAD = DEC_SEQ + 2 * A_BLOCK


def _rope_tables():
    t = jnp.arange(DEC_SEQ)
    half = HEAD_DIM // 2
    nf = half // 2
    inv = ROPE_THETA ** (-jnp.arange(nf, dtype=F32) / nf)
    ang_r = (t // GRID_W).astype(F32)[:, None] * inv[None]
    ang_c = (t % GRID_W).astype(F32)[:, None] * inv[None]
    cos = jnp.concatenate([jnp.cos(ang_r)] * 2 + [jnp.cos(ang_c)] * 2, axis=-1)
    sin = jnp.concatenate([-jnp.sin(ang_r), jnp.sin(ang_r), -jnp.sin(ang_c), jnp.sin(ang_c)], axis=-1)
    return cos, sin


def _rope(x, cos, sin):
    quarter = HEAD_DIM // 4
    lane = lax.broadcasted_iota(jnp.int32, x.shape, 1)
    first = (lane & (2 * quarter - 1)) < quarter
    partner = jnp.where(first, pltpu.roll(x, HEAD_DIM - quarter, 1), pltpu.roll(x, quarter, 1))
    return x * cos + partner * sin


def _win_attn_kernel(sink_ref, q_ref, k_ref, v_ref, ck_ref, cv_ref, gate_ref, cos_ref, sin_ref,
                     u_ref, kpad_ref, vpad_ref):
    n = pl.program_id(1)
    j = pl.program_id(2)
    scale = HEAD_DIM ** -0.5

    @pl.when(j == 0)
    def _():
        zeros = jnp.zeros((A_BLOCK, HEAD_DIM), BF16)
        for ref in (kpad_ref, vpad_ref):
            ref[0:A_BLOCK, :] = zeros
            ref[A_BLOCK + DEC_SEQ:WIN_PAD, :] = zeros
        kpad_ref[A_BLOCK:A_BLOCK + DEC_SEQ, :] = _rope(k_ref[...], cos_ref[...], sin_ref[...]).astype(BF16)
        vpad_ref[A_BLOCK:A_BLOCK + DEC_SEQ, :] = v_ref[...].astype(BF16)

    q0 = pl.multiple_of(j * A_BLOCK, A_BLOCK)
    cos_q = cos_ref[pl.ds(q0, A_BLOCK), :]
    sin_q = sin_ref[pl.ds(q0, A_BLOCK), :]
    q = jnp.concatenate(
        [_rope(q_ref[:, g * HEAD_DIM:(g + 1) * HEAD_DIM], cos_q, sin_q).astype(BF16) for g in range(A_GROUP)],
        axis=0)
    band_k = kpad_ref[pl.ds(q0, 3 * A_BLOCK), :]
    band_v = vpad_ref[pl.ds(q0, 3 * A_BLOCK), :]

    rows = A_GROUP * A_BLOCK
    s_c = _dot_nt(q, ck_ref[...].astype(BF16)) * scale
    s_w = _dot_nt(q, band_k) * scale
    row = lax.broadcasted_iota(jnp.int32, (rows, 1), 0)
    qi = row & (A_BLOCK - 1)
    kk = lax.broadcasted_iota(jnp.int32, (rows, 3 * A_BLOCK), 1)
    first_key = jnp.where(j == 0, A_BLOCK, 0)
    last_key = jnp.where(j == DEC_SEQ // A_BLOCK - 1, 2 * A_BLOCK - 1, 3 * A_BLOCK - 1)
    lower = jnp.maximum(qi, first_key)
    upper = jnp.minimum(qi + 2 * A_WINDOW, last_key)
    s_w = jnp.where((kk >= lower) & (kk <= upper), s_w, NEG_INF)

    head = lax.shift_right_logical(row, A_BLOCK.bit_length() - 1)
    sink = jnp.zeros((rows, 1), F32)
    for g in range(A_GROUP):
        sink = jnp.where(head == g, sink_ref[n * A_GROUP + g], sink)

    o = _softmax_pv([s_c, s_w], [cv_ref[...].astype(BF16), band_v], sink)
    for g in range(A_GROUP):
        cols = slice(g * HEAD_DIM, (g + 1) * HEAD_DIM)
        u_ref[:, cols] = (o[g * A_BLOCK:(g + 1) * A_BLOCK] * _silu(gate_ref[:, cols])).astype(BF16)


def _win_attn(proj, sink, cache_k, cache_v, idx, cos, sin):
    nqb = DEC_SEQ // A_BLOCK
    gw = A_GROUP * HEAD_DIM
    blocks = (2 * A_BLOCK * gw * 4 + 2 * DEC_SEQ * HEAD_DIM * 4 + 2 * PAST_LEN * HEAD_DIM * 4
              + 2 * DEC_SEQ * HEAD_DIM * 4 + A_BLOCK * gw * 2)
    cache_spec = pl.BlockSpec((None, None, None, PAST_LEN, HEAD_DIM), lambda b, n, j: (b, idx, n, 0, 0))
    table_spec = pl.BlockSpec((DEC_SEQ, HEAD_DIM), lambda b, n, j: (0, 0))
    return pl.pallas_call(
        _win_attn_kernel,
        out_shape=jax.ShapeDtypeStruct((NS, A_Q), BF16),
        grid=(DEC_BATCH, A_KV_HEADS, nqb),
        in_specs=[
            pl.BlockSpec(memory_space=pltpu.SMEM),
            pl.BlockSpec((A_BLOCK, gw), lambda b, n, j: (b * nqb + j, COL_QA // gw + n)),
            pl.BlockSpec((DEC_SEQ, HEAD_DIM), lambda b, n, j: (b, COL_KA // HEAD_DIM + n)),
            pl.BlockSpec((DEC_SEQ, HEAD_DIM), lambda b, n, j: (b, COL_VA // HEAD_DIM + n)),
            cache_spec,
            cache_spec,
            pl.BlockSpec((A_BLOCK, gw), lambda b, n, j: (b * nqb + j, COL_GATE // gw + n)),
            table_spec,
            table_spec,
        ],
        out_specs=pl.BlockSpec((A_BLOCK, gw), lambda b, n, j: (b * nqb + j, n)),
        scratch_shapes=[pltpu.VMEM((WIN_PAD, HEAD_DIM), BF16), pltpu.VMEM((WIN_PAD, HEAD_DIM), BF16)],
        compiler_params=_params(
            _vmem_limit(blocks, 2 * WIN_PAD * HEAD_DIM * 2, 8 * A_GROUP * A_BLOCK * 3 * A_BLOCK * 4), 3),
        name=f"win_attn_{idx}",
    )(sink, proj, proj, proj, cache_k, cache_v, proj, cos, sin)


RPB_PAD = (16, 128)


def _na_row_offsets(g):
    offsets = []
    for rl in range(NA_QROWS):
        r = g * NA_QROWS + rl
        r0 = min(max(r - NA_ROWS // 2, 0), GRID_ROWS - NA_ROWS)
        row = []
        for kl in range(NA_KROWS):
            kr = g * (GRID_ROWS - NA_KROWS) + kl
            row.append(kr - r + NA_ROWS - 1 if r0 <= kr < r0 + NA_ROWS else None)
        offsets.append(row)
    return offsets


def _fill_na_bias(rpb_ref, bias_ref, g):
    shape = (GRID_W, 2 * GRID_W)
    c = lax.broadcasted_iota(jnp.int32, shape, 0)
    lane = lax.broadcasted_iota(jnp.int32, shape, 1)
    kc = lane & (GRID_W - 1)
    c0 = jnp.clip(c - NA_COLS // 2, 0, GRID_W - NA_COLS)
    col_ok = (kc >= c0) & (kc < c0 + NA_COLS)
    low = lane < GRID_W
    offsets = _na_row_offsets(g)
    used = sorted({d for row in offsets for d in row if d is not None})
    lo, hi = {}, {}
    for d in used:
        row = jnp.broadcast_to(rpb_ref[d:d + 1, :], shape)
        lo[d] = pltpu.roll(row, 2 * GRID_W - (NA_COLS - 1), 1, stride=1, stride_axis=0)
        hi[d] = pltpu.roll(row, GRID_W - (NA_COLS - 1), 1, stride=1, stride_axis=0)
    neg = jnp.full(shape, NEG_INF, F32)
    for rl in range(NA_QROWS):
        for p in range(NA_KROWS // 2):
            da, db = offsets[rl][2 * p], offsets[rl][2 * p + 1]
            a = neg if da is None else lo[da]
            b = neg if db is None else hi[db]
            piece = jnp.where(col_ok, jnp.where(low, a, b), NEG_INF)
            bias_ref[rl * GRID_W:(rl + 1) * GRID_W, 2 * p * GRID_W:2 * (p + 1) * GRID_W] = piece


def _na_attn_kernel(q_ref, k_ref, v_ref, ck_ref, cv_ref, gate_ref, rpb_ref, u_ref, bias_ref):
    g = pl.program_id(1)
    scale = HEAD_DIM ** -0.5

    @pl.when(pl.program_id(2) == 0)
    def _():
        for group in range(DEC_SEQ // NA_Q):
            @pl.when(g == group)
            def _():
                _fill_na_bias(rpb_ref, bias_ref, group)

    k0 = pl.multiple_of(g * NA_KSHIFT, NA_KSHIFT)
    kwin = k_ref[pl.ds(k0, NA_K), :].astype(BF16)
    vwin = v_ref[pl.ds(k0, NA_K), :].astype(BF16)
    ck = ck_ref[...].astype(BF16)
    cv = cv_ref[...].astype(BF16)
    for r in range(NA_Q // NA_SUB):
        rows = slice(r * NA_SUB, (r + 1) * NA_SUB)
        q = q_ref[rows, :].astype(BF16)
        s_c = _dot_nt(q, ck) * scale
        s_n = _dot_nt(q, kwin) * scale + bias_ref[rows, :]
        o = _softmax_pv([s_c, s_n], [cv, vwin])
        u_ref[rows, :] = (o * _silu(gate_ref[rows, :])).astype(BF16)


def _na_attn(proj, rpb, cache_k, cache_v, idx):
    per_seq = DEC_SEQ // NA_Q
    blocks = (2 * NA_Q * HEAD_DIM * 4 + 2 * DEC_SEQ * HEAD_DIM * 4 + 2 * PAST_LEN * HEAD_DIM * 4
              + RPB_PAD[0] * RPB_PAD[1] * 4 + NA_Q * HEAD_DIM * 2)
    cache_spec = pl.BlockSpec((None, None, None, PAST_LEN, HEAD_DIM), lambda h, g, b: (b, idx, h, 0, 0))
    return pl.pallas_call(
        _na_attn_kernel,
        out_shape=jax.ShapeDtypeStruct((NS, B_W), BF16),
        grid=(B_HEADS, per_seq, DEC_BATCH),
        in_specs=[
            pl.BlockSpec((NA_Q, HEAD_DIM), lambda h, g, b: (b * per_seq + g, COL_QB // HEAD_DIM + h)),
            pl.BlockSpec((DEC_SEQ, HEAD_DIM), lambda h, g, b: (b, COL_KB // HEAD_DIM + h)),
            pl.BlockSpec((DEC_SEQ, HEAD_DIM), lambda h, g, b: (b, COL_VB // HEAD_DIM + h)),
            cache_spec,
            cache_spec,
            pl.BlockSpec((NA_Q, HEAD_DIM), lambda h, g, b: (b * per_seq + g, (COL_GATE + A_Q) // HEAD_DIM + h)),
            pl.BlockSpec((None, None) + RPB_PAD, lambda h, g, b: (idx, h, 0, 0)),
        ],
        out_specs=pl.BlockSpec((NA_Q, HEAD_DIM), lambda h, g, b: (b * per_seq + g, h)),
        scratch_shapes=[pltpu.VMEM((NA_Q, NA_K), F32)],
        compiler_params=_params(
            _vmem_limit(blocks, NA_Q * NA_K * 4, 6 * NA_Q * (NA_K + PAST_LEN) * 4), 3),
        name=f"na_attn_{idx}",
    )(proj, proj, proj, cache_k, cache_v, proj, rpb)


def _ret_kernel(*refs, seq, has_state, emit_state, n_prev):
    dec_ref, q_ref, k_ref, v_ref, gate_ref, gn_ref = refs[:6]
    pos = 6
    if has_state:
        s0f_ref, s0b_ref = refs[pos:pos + 2]
        pos += 2
    pos += n_prev
    u_ref = refs[pos]
    pos += 1
    if emit_state:
        sf_ref, sb_ref = refs[pos:pos + 2]
        pos += 2
    stf_ref, stb_ref, o_ref = refs[pos:pos + 3]

    h = pl.program_id(1)
    nc = seq // C_CHUNK
    assert nc % 2 == 0
    kscale = C_DK ** -0.5
    ii = lax.broadcasted_iota(jnp.int32, (C_CHUNK, C_CHUNK), 0).astype(F32)
    jj = lax.broadcasted_iota(jnp.int32, (C_CHUNK, C_CHUNK), 1).astype(F32)
    icol = lax.broadcasted_iota(jnp.int32, (C_CHUNK, 1), 0).astype(F32)

    def decay_tables(direction):
        forward = direction == 0
        dec = dec_ref[direction, h]
        log_g = -jnp.exp(jnp.full((C_CHUNK, C_CHUNK), dec, F32))
        log_g_col = -jnp.exp(jnp.full((C_CHUNK, 1), dec, F32))
        log_g_row = -jnp.exp(jnp.full((1, C_DV), dec, F32))
        diff = (ii - jj) if forward else (jj - ii)
        dmat = jnp.where(diff >= 0, jnp.exp(log_g * jnp.maximum(diff, 0.0)), 0.0)
        if forward:
            q_dec = jnp.exp(log_g_col * (icol + 1.0))
            k_dec = jnp.exp(log_g_col * (C_CHUNK - 1.0 - icol)) * kscale
        else:
            q_dec = jnp.exp(log_g_col * (C_CHUNK - icol))
            k_dec = jnp.exp(log_g_col * icol) * kscale
        return dmat, q_dec, k_dec, jnp.exp(log_g_row * float(C_CHUNK))

    dmat_f, q_dec_f, k_dec_f, chunk_dec_f = decay_tables(0)
    dmat_b, q_dec_b, k_dec_b, chunk_dec_b = decay_tables(1)
    dmat = (dmat_f + dmat_b) * kscale

    if has_state:
        stf_ref[...] = s0f_ref[...]
        stb_ref[...] = s0b_ref[...]
    else:
        stf_ref[...] = jnp.zeros((C_DK, C_DV), F32)
        stb_ref[...] = jnp.zeros((C_DK, C_DV), F32)

    gn = gn_ref[...]

    def chunk(c):
        return pl.ds(pl.multiple_of(c * C_CHUNK, C_CHUNK), C_CHUNK)

    def accumulate(sl, value, first):
        o_ref[sl, :] = value if first else o_ref[sl, :] + value

    def finalize(sl):
        o = o_ref[sl, :]
        mu = jnp.mean(o, axis=-1, keepdims=True)
        d = o - mu
        var = jnp.mean(d * d, axis=-1, keepdims=True)
        y = (d * lax.rsqrt(var + EPS)) * gn
        u_ref[sl, :] = (y * _silu(gate_ref[sl, :])).astype(BF16)

    def step(t, first):
        sl = chunk(t)
        q = q_ref[sl, :].astype(BF16)
        k32 = k_ref[sl, :]
        v = v_ref[sl, :].astype(BF16)
        s = _dot_nt(q, k32.astype(BF16)) * dmat
        state = stf_ref[...]
        accumulate(sl, _dot(s.astype(BF16), v) + _dot(q, state.astype(BF16)) * q_dec_f, first)
        stf_ref[...] = state * chunk_dec_f + _dot_tn((k32 * k_dec_f).astype(BF16), v)

        slb = chunk(nc - 1 - t)
        qb = q_ref[slb, :].astype(BF16)
        vb = v_ref[slb, :].astype(BF16)
        state_b = stb_ref[...]
        accumulate(slb, _dot(qb, state_b.astype(BF16)) * q_dec_b, first)
        stb_ref[...] = state_b * chunk_dec_b + _dot_tn((k_ref[slb, :] * k_dec_b).astype(BF16), vb)
        if not first:
            finalize(sl)
            finalize(slb)

    def first_half(t, carry):
        step(t, True)
        return carry

    def second_half(t, carry):
        step(t, False)
        return carry

    lax.fori_loop(0, nc // 2, first_half, 0)
    lax.fori_loop(nc // 2, nc, second_half, 0)
    if emit_state:
        sf_ref[...] = stf_ref[...]
        sb_ref[...] = stb_ref[...]


def _retention(proj, decays, gn, idx, *, latent, state_f=None, state_b=None, prev_states=()):
    seq = DEC_SEQ if latent else SEQ
    nb = DEC_BATCH if latent else BATCH
    has_state = latent
    emit_state = not latent
    tok = lambda col0: pl.BlockSpec((seq, C_DK), lambda b, h: (b, col0 + h))
    in_specs = [pl.BlockSpec(memory_space=pltpu.SMEM), tok(0), tok(C_HEADS), tok(2 * C_HEADS),
                tok(3 * C_HEADS), pl.BlockSpec((None, 1, C_DV), lambda b, h: (idx, 0, h))]
    args = [decays, proj, proj, proj, proj, gn]
    if has_state:
        spec = pl.BlockSpec((None, None, None, C_DK, C_DV), lambda b, h: (b, idx, h, 0, 0))
        in_specs += [spec, spec]
        args += [state_f, state_b]
    aliases = {len(args) + k: 1 + k for k in range(len(prev_states))}
    in_specs += [_any_spec() for _ in prev_states]
    args += list(prev_states)
    out_shape = [jax.ShapeDtypeStruct((nb * seq, D_MODEL), BF16)]
    out_specs = [pl.BlockSpec((seq, C_DV), lambda b, h: (b, h))]
    if emit_state:
        st = jax.ShapeDtypeStruct((nb, N_ODD, C_HEADS, C_DK, C_DV), F32)
        st_spec = pl.BlockSpec((None, None, None, C_DK, C_DV), lambda b, h: (b, idx, h, 0, 0))
        out_shape += [st, st]
        out_specs += [st_spec, st_spec]
    blocks = 4 * seq * C_DK * 4 + C_DV * 4 + 2 * C_DK * C_DV * 4 + seq * C_DV * 2
    return pl.pallas_call(
        functools.partial(_ret_kernel, seq=seq, has_state=has_state, emit_state=emit_state,
                          n_prev=len(prev_states)),
        out_shape=tuple(out_shape),
        grid=(nb, C_HEADS),
        in_specs=in_specs,
        out_specs=tuple(out_specs),
        scratch_shapes=[pltpu.VMEM((C_DK, C_DV), F32), pltpu.VMEM((C_DK, C_DV), F32),
                        pltpu.VMEM((seq, C_DV), F32)],
        input_output_aliases=aliases,
        compiler_params=_params(
            _vmem_limit(blocks, 2 * C_DK * C_DV * 4 + seq * C_DV * 4, 32 * C_CHUNK * C_DV * 4), 2),
        name=f"retention_{'latent' if latent else 'prompt'}_{idx}",
    )(*args)


def kernel(x_prompt, x_sample, c, cache_a_k, cache_a_v, cache_b_k, cache_b_v, state_ret_f, state_ret_b,
           c_ctx, w_ada, b_ada, norm_pre, norm_post, w_in_even, w_out_even, a_sink, na_rpb,
           w_in_odd, w_out_odd, ret_decay_f, ret_decay_b, ret_gn):
    xp = x_prompt.reshape(NP, D_MODEL)
    xs = x_sample.reshape(NS, D_MODEL)
    cvec = jnp.concatenate(
        [c_ctx[None, :], c, jnp.zeros((MOD_ROWS - 1 - DEC_BATCH, D_MODEL), F32)], axis=0)
    mods = _adaln(cvec, w_ada, b_ada).reshape(DEPTH, MOD_ROWS, 3, 1, D_MODEL)
    gain_pre = norm_pre.reshape(DEPTH, 1, D_MODEL)
    gain_post = norm_post.reshape(DEPTH, 1, D_MODEL)
    gn = ret_gn.reshape(N_ODD, 1, D_MODEL)
    w_out_even_bf = w_out_even.astype(BF16)
    w_out_odd_bf = w_out_odd.astype(BF16)
    cos, sin = _rope_tables()
    rpb = jnp.pad(na_rpb, ((0, 0), (0, 0), (0, RPB_PAD[0] - na_rpb.shape[2]),
                           (0, RPB_PAD[1] - na_rpb.shape[3])))
    caches = ()
    states = ()
    for layer in range(DEPTH):
        idx = layer // 2
        if layer % 2 == 0:
            proj_p = _inproj(xp, gain_pre, mods, layer, w_in_even, False)
            proj_s = _inproj(xs, gain_pre, mods, layer, w_in_even, True)
            u_p, *caches = _ctx_attn(proj_p, a_sink[idx], idx, caches)
            u_a = _win_attn(proj_s, a_sink[idx], cache_a_k, cache_a_v, idx, cos, sin)
            u_b = _na_attn(proj_s, rpb, cache_b_k, cache_b_v, idx)
            us_p, us_s, w_out = [u_p], [u_a, u_b], w_out_even_bf
        else:
            proj_p = _inproj(xp, gain_pre, mods, layer, w_in_odd, False)
            proj_s = _inproj(xs, gain_pre, mods, layer, w_in_odd, True)
            decays = jnp.stack([ret_decay_f[idx], ret_decay_b[idx]], axis=0)
            u_p, *states = _retention(proj_p, decays, gn, idx, latent=False, prev_states=states)
            (u_s,) = _retention(proj_s, decays, gn, idx, latent=True,
                                state_f=state_ret_f, state_b=state_ret_b)
            us_p, us_s = [u_p], [u_s]
            w_out = w_out_odd_bf
        xp = _outproj(us_p, w_out, xp, gain_post, mods, layer, False)
        xs = _outproj(us_s, w_out, xs, gain_post, mods, layer, True)
    return (xp.reshape(BATCH, SEQ, D_MODEL), xs.reshape(DEC_BATCH, DEC_SEQ, D_MODEL), *caches, *states)
```

```python
import functools

import numpy as np
import jax
import jax.numpy as jnp
from jax import lax
from jax.experimental import pallas as pl
from jax.experimental.pallas import tpu as pltpu

D_MODEL = 2048
BATCH = 16
SEQ = 256
DEPTH = 4
DEC_BATCH = 4
DEC_SEQ = 1024
PAST_LEN = 256
GRID_W = 64
HEAD_DIM = 128
A_HEADS = 8
A_KV_HEADS = 2
A_GROUP = A_HEADS // A_KV_HEADS
A_WINDOW = 128
A_BLOCK = 128
B_HEADS = 8
NA_ROWS = 8
NA_COLS = 16
C_HEADS = 8
C_DK = D_MODEL // C_HEADS
C_DV = D_MODEL // C_HEADS
C_CHUNK = 128
ROPE_THETA = 10000.0
EPS = 1e-6

N_EVEN = (DEPTH + 1) // 2
N_ODD = DEPTH // 2
A_Q = A_HEADS * HEAD_DIM
A_KV = A_KV_HEADS * HEAD_DIM
B_W = B_HEADS * HEAD_DIM
EVEN_WIDTH = A_Q + B_W
EVEN_IN = A_Q + 2 * A_KV + 3 * B_W + EVEN_WIDTH
ODD_IN = 4 * D_MODEL
GRID_ROWS = DEC_SEQ // GRID_W

NP = BATCH * SEQ
NS = DEC_BATCH * DEC_SEQ
MOD_ROWS = 8

COL_QA = 0
COL_KA = A_Q
COL_VA = A_Q + A_KV
COL_QB = A_Q + 2 * A_KV
COL_KB = COL_QB + B_W
COL_VB = COL_KB + B_W
COL_GATE = COL_VB + B_W

NA_QROWS = 8
NA_KROWS = 12
NA_Q = NA_QROWS * GRID_W
NA_K = NA_KROWS * GRID_W
NA_KSHIFT = (GRID_ROWS - NA_KROWS) * GRID_W
NA_SUB = 128

V7X_VMEM_BYTES = 64 * 1024 * 1024
VMEM_HEADROOM = 6 * 1024 * 1024

F32 = jnp.float32
BF16 = jnp.bfloat16
NEG_INF = float("-inf")


def _vmem_limit(block_bytes, scratch_bytes=0, temp_bytes=0):
    usable = V7X_VMEM_BYTES - 4 * 1024 * 1024
    assert 2 * block_bytes + scratch_bytes + temp_bytes + VMEM_HEADROOM <= usable
    return usable


def _params(vmem_bytes, ndims):
    return pltpu.CompilerParams(dimension_semantics=("arbitrary",) * ndims, vmem_limit_bytes=vmem_bytes)


def _silu(x):
    return x / (1.0 + jnp.exp(-x))


def _dot(a, b):
    return jnp.dot(a, b, preferred_element_type=F32)


def _dot_nt(a, b):
    return lax.dot_general(a, b, (((1,), (1,)), ((), ())), preferred_element_type=F32)


def _dot_tn(a, b):
    return lax.dot_general(a, b, (((0,), (0,)), ((), ())), preferred_element_type=F32)


def _any_spec():
    return pl.BlockSpec(memory_space=pl.ANY)


ADA_TN = 1024


def _adaln_kernel(c_ref, w_ref, b_ref, o_ref):
    a = _silu(c_ref[...]).astype(BF16)
    o_ref[...] = _dot(a, w_ref[...].astype(BF16)) + b_ref[...]


def _adaln(cvec, w_ada, b_ada):
    n = 3 * D_MODEL
    blocks = MOD_ROWS * D_MODEL * 4 + D_MODEL * ADA_TN * 4 + ADA_TN * 4 + MOD_ROWS * ADA_TN * 4
    return pl.pallas_call(
        _adaln_kernel,
        out_shape=jax.ShapeDtypeStruct((DEPTH, MOD_ROWS, n), F32),
        grid=(DEPTH, n // ADA_TN),
        in_specs=[
            pl.BlockSpec((MOD_ROWS, D_MODEL), lambda l, j: (0, 0)),
            pl.BlockSpec((None, D_MODEL, ADA_TN), lambda l, j: (l, 0, j)),
            pl.BlockSpec((None, 1, ADA_TN), lambda l, j: (l, 0, j)),
        ],
        out_specs=pl.BlockSpec((None, MOD_ROWS, ADA_TN), lambda l, j: (l, 0, j)),
        compiler_params=_params(_vmem_limit(blocks, temp_bytes=D_MODEL * ADA_TN * 2), 2),
        name="adaln",
    )(cvec, w_ada, b_ada.reshape(DEPTH, 1, n))


def _mod_row(tile, tm, latent):
    return 1 + tile // (DEC_SEQ // tm) if latent else 0


NORM_TM = 512
NORM_ROWS = 128
IN_TN = 512


def _prenorm_kernel(x_ref, g_ref, sh_ref, sc_ref, h_ref):
    gain = g_ref[...]
    one_sc = 1.0 + sc_ref[...]
    sh = sh_ref[...]

    def body(r, carry):
        sl = pl.ds(pl.multiple_of(r * NORM_ROWS, NORM_ROWS), NORM_ROWS)
        x = x_ref[sl, :]
        ms = jnp.mean(x * x, axis=-1, keepdims=True)
        y = (x * lax.rsqrt(ms + EPS)) * gain
        h_ref[sl, :] = (y * one_sc + sh).astype(BF16)
        return carry

    lax.fori_loop(0, NORM_TM // NORM_ROWS, body, 0)


def _prenorm(x, gain, mods, layer, latent):
    ntok = x.shape[0]
    blocks = NORM_TM * D_MODEL * 4 + 3 * D_MODEL * 4 + NORM_TM * D_MODEL * 2
    mod_spec = lambda which: pl.BlockSpec(
        (None, None, None, 1, D_MODEL), lambda i: (layer, _mod_row(i, NORM_TM, latent), which, 0, 0))
    return pl.pallas_call(
        _prenorm_kernel,
        out_shape=jax.ShapeDtypeStruct((ntok, D_MODEL), BF16),
        grid=(ntok // NORM_TM,),
        in_specs=[
            pl.BlockSpec((NORM_TM, D_MODEL), lambda i: (i, 0)),
            pl.BlockSpec((None, 1, D_MODEL), lambda i: (layer, 0, 0)),
            mod_spec(0),
            mod_spec(1),
        ],
        out_specs=pl.BlockSpec((NORM_TM, D_MODEL), lambda i: (i, 0)),
        compiler_params=_params(_vmem_limit(blocks, temp_bytes=4 * NORM_ROWS * D_MODEL * 4), 1),
        name=f"prenorm_l{layer}_{'latent' if latent else 'prompt'}",
    )(x, gain, mods, mods)


def _inproj_kernel(h_ref, w_ref, o_ref):
    o_ref[...] = _dot(h_ref[...], w_ref[...].astype(BF16))


def _inproj(h, layer, w, latent):
    ntok = h.shape[0]
    n = w.shape[2]
    blocks = D_MODEL * IN_TN * 4 + ntok * IN_TN * 4
    return pl.pallas_call(
        _inproj_kernel,
        out_shape=jax.ShapeDtypeStruct((ntok, n), F32),
        grid=(n // IN_TN,),
        in_specs=[
            pl.BlockSpec((ntok, D_MODEL), lambda j: (0, 0), pipeline_mode=pl.Buffered(1)),
            pl.BlockSpec((None, D_MODEL, IN_TN), lambda j: (layer // 2, 0, j)),
        ],
        out_specs=pl.BlockSpec((ntok, IN_TN), lambda j: (0, j)),
        compiler_params=_params(_vmem_limit(blocks, ntok * D_MODEL * 2, D_MODEL * IN_TN * 2), 1),
        name=f"inproj_l{layer}_{'latent' if latent else 'prompt'}",
    )(h, w)


OUT_TM = 512


def _outproj_kernel(*refs, n_pieces):
    u_refs = refs[:n_pieces]
    w_refs = refs[n_pieces:2 * n_pieces]
    x_ref, gain_ref, gate_ref, o_ref, wbf_ref = refs[2 * n_pieces:]

    @pl.when(pl.program_id(0) == 0)
    def _():
        for p, w_ref in enumerate(w_refs):
            wbf_ref[p] = w_ref[...].astype(BF16)

    out = _dot(u_refs[0][...], wbf_ref[0])
    for p in range(1, n_pieces):
        out = out + _dot(u_refs[p][...], wbf_ref[p])
    ms = jnp.mean(out * out, axis=-1, keepdims=True)
    y = (out * lax.rsqrt(ms + EPS)) * gain_ref[...]
    o_ref[...] = x_ref[...] + gate_ref[...] * y


def _outproj(us, w, x, gain, mods, layer, latent):
    ntok = x.shape[0]
    widths = [u.shape[1] for u in us]
    kp = widths[0]
    k = w.shape[1]
    assert all(kw == kp for kw in widths) and kp * len(us) == k
    blocks = OUT_TM * k * 2 + 2 * OUT_TM * D_MODEL * 4 + 2 * D_MODEL * 4
    u_specs = [pl.BlockSpec((OUT_TM, kp), lambda i: (i, 0)) for _ in us]
    w_specs = [pl.BlockSpec((None, kp, D_MODEL), lambda i, p=p: (layer // 2, p, 0),
                            pipeline_mode=pl.Buffered(1)) for p in range(len(us))]
    return pl.pallas_call(
        functools.partial(_outproj_kernel, n_pieces=len(us)),
        out_shape=jax.ShapeDtypeStruct((ntok, D_MODEL), F32),
        grid=(ntok // OUT_TM,),
        in_specs=u_specs + w_specs + [
            pl.BlockSpec((OUT_TM, D_MODEL), lambda i: (i, 0)),
            pl.BlockSpec((None, 1, D_MODEL), lambda i: (layer, 0, 0)),
            pl.BlockSpec((None, None, None, 1, D_MODEL),
                         lambda i: (layer, _mod_row(i, OUT_TM, latent), 2, 0, 0)),
        ],
        out_specs=pl.BlockSpec((OUT_TM, D_MODEL), lambda i: (i, 0)),
        scratch_shapes=[pltpu.VMEM((len(us), kp, D_MODEL), BF16)],
        compiler_params=_params(
            _vmem_limit(blocks, k * D_MODEL * (4 + 2), 2 * OUT_TM * D_MODEL * 4), 1),
        name=f"outproj_l{layer}_{'latent' if latent else 'prompt'}",
    )(*us, *([w] * len(us)), x, gain, mods)


def _softmax_pv(scores, values, sink=None):
    m = jnp.max(scores[0], axis=-1, keepdims=True)
    for s in scores[1:]:
        m = jnp.maximum(m, jnp.max(s, axis=-1, keepdims=True))
    if sink is not None:
        m = jnp.maximum(m, sink)
    den = None
    acc = None
    for s, v in zip(scores, values):
        p = jnp.exp(s - m)
        ps = jnp.sum(p, axis=-1, keepdims=True)
        pv = _dot(p.astype(BF16), v)
        den = ps if den is None else den + ps
        acc = pv if acc is None else acc + pv
    if sink is not None:
        den = den + jnp.exp(sink - m)
    return acc / den


def _ctx_attn_kernel(sink_ref, p_ref, *refs):
    u_ref, ak_ref, av_ref, bk_ref, bv_ref = refs[-5:]
    scale = HEAD_DIM ** -0.5

    def col(c):
        return p_ref[:, c:c + HEAD_DIM]

    def head(cq, k, v, cg, sink, cu):
        q = col(cq).astype(BF16)
        s = _dot_nt(q, k) * scale
        o = _softmax_pv([s], [v], sink)
        u_ref[:, cu:cu + HEAD_DIM] = (o * _silu(col(cg))).astype(BF16)

    for n in range(A_KV_HEADS):
        k32 = col(COL_KA + n * HEAD_DIM)
        v32 = col(COL_VA + n * HEAD_DIM)
        ak_ref[n] = k32
        av_ref[n] = v32
        k = k32.astype(BF16)
        v = v32.astype(BF16)
        for g in range(A_GROUP):
            h = n * A_GROUP + g
            head(COL_QA + h * HEAD_DIM, k, v, COL_GATE + h * HEAD_DIM, sink_ref[h], h * HEAD_DIM)
    for h in range(B_HEADS):
        k32 = col(COL_KB + h * HEAD_DIM)
        v32 = col(COL_VB + h * HEAD_DIM)
        bk_ref[h] = k32
        bv_ref[h] = v32
        head(COL_QB + h * HEAD_DIM, k32.astype(BF16), v32.astype(BF16),
             COL_GATE + A_Q + h * HEAD_DIM, None, A_Q + h * HEAD_DIM)


def _ctx_attn(proj, sink, idx, prev_caches):
    blocks = (SEQ * EVEN_IN * 4 + SEQ * EVEN_WIDTH * 2
              + 2 * (A_KV_HEADS + B_HEADS) * SEQ * HEAD_DIM * 4)
    heads = (A_KV_HEADS, A_KV_HEADS, B_HEADS, B_HEADS)
    cache = lambda nh: jax.ShapeDtypeStruct((BATCH, N_EVEN, nh, SEQ, HEAD_DIM), F32)
    cache_spec = lambda nh: pl.BlockSpec((None, None, nh, SEQ, HEAD_DIM), lambda b: (b, idx, 0, 0, 0))
    n_prev = len(prev_caches)
    return pl.pallas_call(
        _ctx_attn_kernel,
        out_shape=(jax.ShapeDtypeStruct((NP, EVEN_WIDTH), BF16),) + tuple(cache(nh) for nh in heads),
        grid=(BATCH,),
        in_specs=[
            pl.BlockSpec(memory_space=pltpu.SMEM),
            pl.BlockSpec((SEQ, EVEN_IN), lambda b: (b, 0)),
        ] + [_any_spec() for _ in prev_caches],
        out_specs=(pl.BlockSpec((SEQ, EVEN_WIDTH), lambda b: (b, 0)),) + tuple(cache_spec(nh) for nh in heads),
        input_output_aliases={2 + k: 1 + k for k in range(n_prev)},
        compiler_params=_params(_vmem_limit(blocks, temp_bytes=8 * SEQ * SEQ * 4), 1),
        name=f"ctx_attn_{idx}",
    )(sink, proj, *prev_caches)


WIN_PAD = DEC_SEQ + 2 * A_BLOCK


def _rope_tables():
    t = jnp.arange(DEC_SEQ)
    half = HEAD_DIM // 2
    nf = half // 2
    inv = ROPE_THETA ** (-jnp.arange(nf, dtype=F32) / nf)
    ang_r = (t // GRID_W).astype(F32)[:, None] * inv[None]
    ang_c = (t % GRID_W).astype(F32)[:, None] * inv[None]
    cos = jnp.concatenate([jnp.cos(ang_r)] * 2 + [jnp.cos(ang_c)] * 2, axis=-1)
    sin = jnp.concatenate([-jnp.sin(ang_r), jnp.sin(ang_r), -jnp.sin(ang_c), jnp.sin(ang_c)], axis=-1)
    return cos, sin


def _rope(x, cos, sin):
    quarter = HEAD_DIM // 4
    lane = lax.broadcasted_iota(jnp.int32, x.shape, 1)
    first = (lane & (2 * quarter - 1)) < quarter
    partner = jnp.where(first, pltpu.roll(x, HEAD_DIM - quarter, 1), pltpu.roll(x, quarter, 1))
    return x * cos + partner * sin


def _win_attn_kernel(sink_ref, q_ref, k_ref, v_ref, ck_ref, cv_ref, gate_ref, cos_ref, sin_ref,
                     u_ref, kpad_ref, vpad_ref):
    n = pl.program_id(1)
    j = pl.program_id(2)
    scale = HEAD_DIM ** -0.5

    @pl.when(j == 0)
    def _():
        zeros = jnp.zeros((A_BLOCK, HEAD_DIM), BF16)
        for ref in (kpad_ref, vpad_ref):
            ref[0:A_BLOCK, :] = zeros
            ref[A_BLOCK + DEC_SEQ:WIN_PAD, :] = zeros
        kpad_ref[A_BLOCK:A_BLOCK + DEC_SEQ, :] = _rope(k_ref[...], cos_ref[...], sin_ref[...]).astype(BF16)
        vpad_ref[A_BLOCK:A_BLOCK + DEC_SEQ, :] = v_ref[...].astype(BF16)

    q0 = pl.multiple_of(j * A_BLOCK, A_BLOCK)
    cos_q = cos_ref[pl.ds(q0, A_BLOCK), :]
    sin_q = sin_ref[pl.ds(q0, A_BLOCK), :]
    q = jnp.concatenate(
        [_rope(q_ref[:, g * HEAD_DIM:(g + 1) * HEAD_DIM], cos_q, sin_q).astype(BF16) for g in range(A_GROUP)],
        axis=0)
    band_k = kpad_ref[pl.ds(q0, 3 * A_BLOCK), :]
    band_v = vpad_ref[pl.ds(q0, 3 * A_BLOCK), :]

    rows = A_GROUP * A_BLOCK
    s_c = _dot_nt(q, ck_ref[...].astype(BF16)) * scale
    s_w = _dot_nt(q, band_k) * scale
    row = lax.broadcasted_iota(jnp.int32, (rows, 1), 0)
    qi = row & (A_BLOCK - 1)
    kk = lax.broadcasted_iota(jnp.int32, (rows, 3 * A_BLOCK), 1)
    first_key = jnp.where(j == 0, A_BLOCK, 0)
    last_key = jnp.where(j == DEC_SEQ // A_BLOCK - 1, 2 * A_BLOCK - 1, 3 * A_BLOCK - 1)
    lower = jnp.maximum(qi, first_key)
    upper = jnp.minimum(qi + 2 * A_WINDOW, last_key)
    s_w = jnp.where((kk >= lower) & (kk <= upper), s_w, NEG_INF)

    head = lax.shift_right_logical(row, A_BLOCK.bit_length() - 1)
    sink = jnp.zeros((rows, 1), F32)
    for g in range(A_GROUP):
        sink = jnp.where(head == g, sink_ref[n * A_GROUP + g], sink)

    o = _softmax_pv([s_c, s_w], [cv_ref[...].astype(BF16), band_v], sink)
    for g in range(A_GROUP):
        cols = slice(g * HEAD_DIM, (g + 1) * HEAD_DIM)
        u_ref[:, cols] = (o[g * A_BLOCK:(g + 1) * A_BLOCK] * _silu(gate_ref[:, cols])).astype(BF16)


def _win_attn(proj, sink, cache_k, cache_v, idx, cos, sin):
    nqb = DEC_SEQ // A_BLOCK
    gw = A_GROUP * HEAD_DIM
    blocks = (2 * A_BLOCK * gw * 4 + 2 * DEC_SEQ * HEAD_DIM * 4 + 2 * PAST_LEN * HEAD_DIM * 4
              + 2 * DEC_SEQ * HEAD_DIM * 4 + A_BLOCK * gw * 2)
    cache_spec = pl.BlockSpec((None, None, None, PAST_LEN, HEAD_DIM), lambda b, n, j: (b, idx, n, 0, 0))
    table_spec = pl.BlockSpec((DEC_SEQ, HEAD_DIM), lambda b, n, j: (0, 0))
    return pl.pallas_call(
        _win_attn_kernel,
        out_shape=jax.ShapeDtypeStruct((NS, A_Q), BF16),
        grid=(DEC_BATCH, A_KV_HEADS, nqb),
        in_specs=[
            pl.BlockSpec(memory_space=pltpu.SMEM),
            pl.BlockSpec((A_BLOCK, gw), lambda b, n, j: (b * nqb + j, COL_QA // gw + n)),
            pl.BlockSpec((DEC_SEQ, HEAD_DIM), lambda b, n, j: (b, COL_KA // HEAD_DIM + n)),
            pl.BlockSpec((DEC_SEQ, HEAD_DIM), lambda b, n, j: (b, COL_VA // HEAD_DIM + n)),
            cache_spec,
            cache_spec,
            pl.BlockSpec((A_BLOCK, gw), lambda b, n, j: (b * nqb + j, COL_GATE // gw + n)),
            table_spec,
            table_spec,
        ],
        out_specs=pl.BlockSpec((A_BLOCK, gw), lambda b, n, j: (b * nqb + j, n)),
        scratch_shapes=[pltpu.VMEM((WIN_PAD, HEAD_DIM), BF16), pltpu.VMEM((WIN_PAD, HEAD_DIM), BF16)],
        compiler_params=_params(
            _vmem_limit(blocks, 2 * WIN_PAD * HEAD_DIM * 2, 8 * A_GROUP * A_BLOCK * 3 * A_BLOCK * 4), 3),
        name=f"win_attn_{idx}",
    )(sink, proj, proj, proj, cache_k, cache_v, proj, cos, sin)


RPB_PAD = (16, 128)


def _na_row_offsets(g):
    offsets = []
    for rl in range(NA_QROWS):
        r = g * NA_QROWS + rl
        r0 = min(max(r - NA_ROWS // 2, 0), GRID_ROWS - NA_ROWS)
        row = []
        for kl in range(NA_KROWS):
            kr = g * (GRID_ROWS - NA_KROWS) + kl
            row.append(kr - r + NA_ROWS - 1 if r0 <= kr < r0 + NA_ROWS else None)
        offsets.append(row)
    return offsets


def _fill_na_bias(rpb_ref, bias_ref, g):
    shape = (GRID_W, 2 * GRID_W)
    c = lax.broadcasted_iota(jnp.int32, shape, 0)
    lane = lax.broadcasted_iota(jnp.int32, shape, 1)
    kc = lane & (GRID_W - 1)
    c0 = jnp.clip(c - NA_COLS // 2, 0, GRID_W - NA_COLS)
    col_ok = (kc >= c0) & (kc < c0 + NA_COLS)
    low = lane < GRID_W
    offsets = _na_row_offsets(g)
    used = sorted({d for row in offsets for d in row if d is not None})
    lo, hi = {}, {}
    for d in used:
        row = jnp.broadcast_to(rpb_ref[d:d + 1, :], shape)
        lo[d] = pltpu.roll(row, 2 * GRID_W - (NA_COLS - 1), 1, stride=1, stride_axis=0)
        hi[d] = pltpu.roll(row, GRID_W - (NA_COLS - 1), 1, stride=1, stride_axis=0)
    neg = jnp.full(shape, NEG_INF, F32)
    for rl in range(NA_QROWS):
        for p in range(NA_KROWS // 2):
            da, db = offsets[rl][2 * p], offsets[rl][2 * p + 1]
            a = neg if da is None else lo[da]
            b = neg if db is None else hi[db]
            piece = jnp.where(col_ok, jnp.where(low, a, b), NEG_INF)
            bias_ref[rl * GRID_W:(rl + 1) * GRID_W, 2 * p * GRID_W:2 * (p + 1) * GRID_W] = piece


def _na_attn_kernel(q_ref, k_ref, v_ref, ck_ref, cv_ref, gate_ref, rpb_ref, u_ref, bias_ref):
    g = pl.program_id(1)
    scale = HEAD_DIM ** -0.5

    @pl.when(pl.program_id(2) == 0)
    def _():
        for group in range(DEC_SEQ // NA_Q):
            @pl.when(g == group)
            def _():
                _fill_na_bias(rpb_ref, bias_ref, group)

    k0 = pl.multiple_of(g * NA_KSHIFT, NA_KSHIFT)
    kwin = k_ref[pl.ds(k0, NA_K), :].astype(BF16)
    vwin = v_ref[pl.ds(k0, NA_K), :].astype(BF16)
    ck = ck_ref[...].astype(BF16)
    cv = cv_ref[...].astype(BF16)
    for r in range(NA_Q // NA_SUB):
        rows = slice(r * NA_SUB, (r + 1) * NA_SUB)
        q = q_ref[rows, :].astype(BF16)
        s_c = _dot_nt(q, ck) * scale
        s_n = _dot_nt(q, kwin) * scale + bias_ref[rows, :]
        o = _softmax_pv([s_c, s_n], [cv, vwin])
        u_ref[rows, :] = (o * _silu(gate_ref[rows, :])).astype(BF16)


def _na_attn(proj, rpb, cache_k, cache_v, idx):
    per_seq = DEC_SEQ // NA_Q
    blocks = (2 * NA_Q * HEAD_DIM * 4 + 2 * DEC_SEQ * HEAD_DIM * 4 + 2 * PAST_LEN * HEAD_DIM * 4
              + RPB_PAD[0] * RPB_PAD[1] * 4 + NA_Q * HEAD_DIM * 2)
    cache_spec = pl.BlockSpec((None, None, None, PAST_LEN, HEAD_DIM), lambda h, g, b: (b, idx, h, 0, 0))
    return pl.pallas_call(
        _na_attn_kernel,
        out_shape=jax.ShapeDtypeStruct((NS, B_W), BF16),
        grid=(B_HEADS, per_seq, DEC_BATCH),
        in_specs=[
            pl.BlockSpec((NA_Q, HEAD_DIM), lambda h, g, b: (b * per_seq + g, COL_QB // HEAD_DIM + h)),
            pl.BlockSpec((DEC_SEQ, HEAD_DIM), lambda h, g, b: (b, COL_KB // HEAD_DIM + h)),
            pl.BlockSpec((DEC_SEQ, HEAD_DIM), lambda h, g, b: (b, COL_VB // HEAD_DIM + h)),
            cache_spec,
            cache_spec,
            pl.BlockSpec((NA_Q, HEAD_DIM), lambda h, g, b: (b * per_seq + g, (COL_GATE + A_Q) // HEAD_DIM + h)),
            pl.BlockSpec((None, None) + RPB_PAD, lambda h, g, b: (idx, h, 0, 0)),
        ],
        out_specs=pl.BlockSpec((NA_Q, HEAD_DIM), lambda h, g, b: (b * per_seq + g, h)),
        scratch_shapes=[pltpu.VMEM((NA_Q, NA_K), F32)],
        compiler_params=_params(
            _vmem_limit(blocks, NA_Q * NA_K * 4, 6 * NA_Q * (NA_K + PAST_LEN) * 4), 3),
        name=f"na_attn_{idx}",
    )(proj, proj, proj, cache_k, cache_v, proj, rpb)


def _ret_kernel(*refs, seq, has_state, emit_state, n_prev):
    dec_ref, q_ref, k_ref, v_ref, gate_ref, gn_ref = refs[:6]
    pos = 6
    if has_state:
        s0f_ref, s0b_ref = refs[pos:pos + 2]
        pos += 2
    pos += n_prev
    u_ref = refs[pos]
    pos += 1
    if emit_state:
        sf_ref, sb_ref = refs[pos:pos + 2]
        pos += 2
    stf_ref, stb_ref, o_ref = refs[pos:pos + 3]

    h = pl.program_id(1)
    nc = seq // C_CHUNK
    assert nc % 2 == 0
    kscale = C_DK ** -0.5
    ii = lax.broadcasted_iota(jnp.int32, (C_CHUNK, C_CHUNK), 0).astype(F32)
    jj = lax.broadcasted_iota(jnp.int32, (C_CHUNK, C_CHUNK), 1).astype(F32)
    icol = lax.broadcasted_iota(jnp.int32, (C_CHUNK, 1), 0).astype(F32)

    def decay_tables(direction):
        forward = direction == 0
        dec = dec_ref[direction, h]
        log_g = -jnp.exp(jnp.full((C_CHUNK, C_CHUNK), dec, F32))
        log_g_col = -jnp.exp(jnp.full((C_CHUNK, 1), dec, F32))
        log_g_row = -jnp.exp(jnp.full((1, C_DV), dec, F32))
        diff = (ii - jj) if forward else (jj - ii)
        dmat = jnp.where(diff >= 0, jnp.exp(log_g * jnp.maximum(diff, 0.0)), 0.0)
        if forward:
            q_dec = jnp.exp(log_g_col * (icol + 1.0))
            k_dec = jnp.exp(log_g_col * (C_CHUNK - 1.0 - icol)) * kscale
        else:
            q_dec = jnp.exp(log_g_col * (C_CHUNK - icol))
            k_dec = jnp.exp(log_g_col * icol) * kscale
        return dmat, q_dec, k_dec, jnp.exp(log_g_row * float(C_CHUNK))

    dmat_f, q_dec_f, k_dec_f, chunk_dec_f = decay_tables(0)
    dmat_b, q_dec_b, k_dec_b, chunk_dec_b = decay_tables(1)
    dmat = (dmat_f + dmat_b) * kscale

    if has_state:
        stf_ref[...] = s0f_ref[...]
        stb_ref[...] = s0b_ref[...]
    else:
        stf_ref[...] = jnp.zeros((C_DK, C_DV), F32)
        stb_ref[...] = jnp.zeros((C_DK, C_DV), F32)

    gn = gn_ref[...]

    def chunk(c):
        return pl.ds(pl.multiple_of(c * C_CHUNK, C_CHUNK), C_CHUNK)

    def accumulate(sl, value, first):
        o_ref[sl, :] = value if first else o_ref[sl, :] + value

    def finalize(sl):
        o = o_ref[sl, :]
        mu = jnp.mean(o, axis=-1, keepdims=True)
        d = o - mu
        var = jnp.mean(d * d, axis=-1, keepdims=True)
        y = (d * lax.rsqrt(var + EPS)) * gn
        u_ref[sl, :] = (y * _silu(gate_ref[sl, :])).astype(BF16)

    def step(t, first):
        sl = chunk(t)
        q = q_ref[sl, :].astype(BF16)
        k32 = k_ref[sl, :]
        v = v_ref[sl, :].astype(BF16)
        s = _dot_nt(q, k32.astype(BF16)) * dmat
        state = stf_ref[...]
        accumulate(sl, _dot(s.astype(BF16), v) + _dot(q, state.astype(BF16)) * q_dec_f, first)
        stf_ref[...] = state * chunk_dec_f + _dot_tn((k32 * k_dec_f).astype(BF16), v)

        slb = chunk(nc - 1 - t)
        qb = q_ref[slb, :].astype(BF16)
        vb = v_ref[slb, :].astype(BF16)
        state_b = stb_ref[...]
        accumulate(slb, _dot(qb, state_b.astype(BF16)) * q_dec_b, first)
        stb_ref[...] = state_b * chunk_dec_b + _dot_tn((k_ref[slb, :] * k_dec_b).astype(BF16), vb)
        if not first:
            finalize(sl)
            finalize(slb)

    def first_half(t, carry):
        step(t, True)
        return carry

    def second_half(t, carry):
        step(t, False)
        return carry

    lax.fori_loop(0, nc // 2, first_half, 0)
    lax.fori_loop(nc // 2, nc, second_half, 0)
    if emit_state:
        sf_ref[...] = stf_ref[...]
        sb_ref[...] = stb_ref[...]


def _retention(proj, decays, gn, idx, *, latent, state_f=None, state_b=None, prev_states=()):
    seq = DEC_SEQ if latent else SEQ
    nb = DEC_BATCH if latent else BATCH
    has_state = latent
    emit_state = not latent
    tok = lambda col0: pl.BlockSpec((seq, C_DK), lambda b, h: (b, col0 + h))
    in_specs = [pl.BlockSpec(memory_space=pltpu.SMEM), tok(0), tok(C_HEADS), tok(2 * C_HEADS),
                tok(3 * C_HEADS), pl.BlockSpec((None, 1, C_DV), lambda b, h: (idx, 0, h))]
    args = [decays, proj, proj, proj, proj, gn]
    if has_state:
        spec = pl.BlockSpec((None, None, None, C_DK, C_DV), lambda b, h: (b, idx, h, 0, 0))
        in_specs += [spec, spec]
        args += [state_f, state_b]
    aliases = {len(args) + k: 1 + k for k in range(len(prev_states))}
    in_specs += [_any_spec() for _ in prev_states]
    args += list(prev_states)
    out_shape = [jax.ShapeDtypeStruct((nb * seq, D_MODEL), BF16)]
    out_specs = [pl.BlockSpec((seq, C_DV), lambda b, h: (b, h))]
    if emit_state:
        st = jax.ShapeDtypeStruct((nb, N_ODD, C_HEADS, C_DK, C_DV), F32)
        st_spec = pl.BlockSpec((None, None, None, C_DK, C_DV), lambda b, h: (b, idx, h, 0, 0))
        out_shape += [st, st]
        out_specs += [st_spec, st_spec]
    blocks = 4 * seq * C_DK * 4 + C_DV * 4 + 2 * C_DK * C_DV * 4 + seq * C_DV * 2
    return pl.pallas_call(
        functools.partial(_ret_kernel, seq=seq, has_state=has_state, emit_state=emit_state,
                          n_prev=len(prev_states)),
        out_shape=tuple(out_shape),
        grid=(nb, C_HEADS),
        in_specs=in_specs,
        out_specs=tuple(out_specs),
        scratch_shapes=[pltpu.VMEM((C_DK, C_DV), F32), pltpu.VMEM((C_DK, C_DV), F32),
                        pltpu.VMEM((seq, C_DV), F32)],
        input_output_aliases=aliases,
        compiler_params=_params(
            _vmem_limit(blocks, 2 * C_DK * C_DV * 4 + seq * C_DV * 4, 32 * C_CHUNK * C_DV * 4), 2),
        name=f"retention_{'latent' if latent else 'prompt'}_{idx}",
    )(*args)


def kernel(x_prompt, x_sample, c, cache_a_k, cache_a_v, cache_b_k, cache_b_v, state_ret_f, state_ret_b,
           c_ctx, w_ada, b_ada, norm_pre, norm_post, w_in_even, w_out_even, a_sink, na_rpb,
           w_in_odd, w_out_odd, ret_decay_f, ret_decay_b, ret_gn):
    xp = x_prompt.reshape(NP, D_MODEL)
    xs = x_sample.reshape(NS, D_MODEL)
    cvec = jnp.concatenate(
        [c_ctx[None, :], c, jnp.zeros((MOD_ROWS - 1 - DEC_BATCH, D_MODEL), F32)], axis=0)
    mods = _adaln(cvec, w_ada, b_ada).reshape(DEPTH, MOD_ROWS, 3, 1, D_MODEL)
    gain_pre = norm_pre.reshape(DEPTH, 1, D_MODEL)
    gain_post = norm_post.reshape(DEPTH, 1, D_MODEL)
    gn = ret_gn.reshape(N_ODD, 1, D_MODEL)
    cos, sin = _rope_tables()
    rpb = jnp.pad(na_rpb, ((0, 0), (0, 0), (0, RPB_PAD[0] - na_rpb.shape[2]),
                           (0, RPB_PAD[1] - na_rpb.shape[3])))
    caches = ()
    states = ()
    for layer in range(DEPTH):
        idx = layer // 2
        if layer % 2 == 0:
            proj_p = _inproj(_prenorm(xp, gain_pre, mods, layer, False), layer, w_in_even, False)
            proj_s = _inproj(_prenorm(xs, gain_pre, mods, layer, True), layer, w_in_even, True)
            u_p, *caches = _ctx_attn(proj_p, a_sink[idx], idx, caches)
            u_a = _win_attn(proj_s, a_sink[idx], cache_a_k, cache_a_v, idx, cos, sin)
            u_b = _na_attn(proj_s, rpb, cache_b_k, cache_b_v, idx)
            us_p, us_s, w_out = [u_p], [u_a, u_b], w_out_even
        else:
            proj_p = _inproj(_prenorm(xp, gain_pre, mods, layer, False), layer, w_in_odd, False)
            proj_s = _inproj(_prenorm(xs, gain_pre, mods, layer, True), layer, w_in_odd, True)
            decays = jnp.stack([ret_decay_f[idx], ret_decay_b[idx]], axis=0)
            u_p, *states = _retention(proj_p, decays, gn, idx, latent=False, prev_states=states)
            (u_s,) = _retention(proj_s, decays, gn, idx, latent=True,
                                state_f=state_ret_f, state_b=state_ret_b)
            us_p, us_s = [u_p], [u_s]
            w_out = w_out_odd
        xp = _outproj(us_p, w_out, xp, gain_post, mods, layer, False)
        xs = _outproj(us_s, w_out, xs, gain_post, mods, layer, True)
    return (xp.reshape(BATCH, SEQ, D_MODEL), xs.reshape(DEC_BATCH, DEC_SEQ, D_MODEL), *caches, *states)
```

```python
import functools

import numpy as np
import jax
import jax.numpy as jnp
from jax import lax
from jax.experimental import pallas as pl
from jax.experimental.pallas import tpu as pltpu

D_MODEL = 2048
BATCH = 16
SEQ = 256
DEPTH = 4
DEC_BATCH = 4
DEC_SEQ = 1024
PAST_LEN = 256
GRID_W = 64
HEAD_DIM = 128
A_HEADS = 8
A_KV_HEADS = 2
A_GROUP = A_HEADS // A_KV_HEADS
A_WINDOW = 128
A_BLOCK = 128
B_HEADS = 8
NA_ROWS = 8
NA_COLS = 16
C_HEADS = 8
C_DK = D_MODEL // C_HEADS
C_DV = D_MODEL // C_HEADS
C_CHUNK = 128
ROPE_THETA = 10000.0
EPS = 1e-6

N_EVEN = (DEPTH + 1) // 2
N_ODD = DEPTH // 2
A_Q = A_HEADS * HEAD_DIM
A_KV = A_KV_HEADS * HEAD_DIM
B_W = B_HEADS * HEAD_DIM
EVEN_WIDTH = A_Q + B_W
EVEN_IN = A_Q + 2 * A_KV + 3 * B_W + EVEN_WIDTH
ODD_IN = 4 * D_MODEL
GRID_ROWS = DEC_SEQ // GRID_W

NP = BATCH * SEQ
NS = DEC_BATCH * DEC_SEQ
MOD_ROWS = 8

COL_QA = 0
COL_KA = A_Q
COL_VA = A_Q + A_KV
COL_QB = A_Q + 2 * A_KV
COL_KB = COL_QB + B_W
COL_VB = COL_KB + B_W
COL_GATE = COL_VB + B_W

NA_QROWS = 8
NA_KROWS = 12
NA_Q = NA_QROWS * GRID_W
NA_K = NA_KROWS * GRID_W
NA_KSHIFT = (GRID_ROWS - NA_KROWS) * GRID_W
NA_SUB = 128

V7X_VMEM_BYTES = 64 * 1024 * 1024
VMEM_HEADROOM = 6 * 1024 * 1024

F32 = jnp.float32
BF16 = jnp.bfloat16
NEG_INF = float("-inf")


def _vmem_limit(block_bytes, scratch_bytes=0, temp_bytes=0):
    usable = V7X_VMEM_BYTES - 4 * 1024 * 1024
    assert 2 * block_bytes + scratch_bytes + temp_bytes + VMEM_HEADROOM <= usable
    return usable


def _params(vmem_bytes, ndims):
    return pltpu.CompilerParams(dimension_semantics=("arbitrary",) * ndims, vmem_limit_bytes=vmem_bytes)


def _silu(x):
    return x / (1.0 + jnp.exp(-x))


def _dot(a, b):
    return jnp.dot(a, b, preferred_element_type=F32)


def _dot_nt(a, b):
    return lax.dot_general(a, b, (((1,), (1,)), ((), ())), preferred_element_type=F32)


def _dot_tn(a, b):
    return lax.dot_general(a, b, (((0,), (0,)), ((), ())), preferred_element_type=F32)


def _any_spec():
    return pl.BlockSpec(memory_space=pl.ANY)


ADA_TN = 1024


def _adaln_kernel(c_ref, w_ref, b_ref, o_ref):
    a = _silu(c_ref[...]).astype(BF16)
    o_ref[...] = _dot(a, w_ref[...].astype(BF16)) + b_ref[...]


def _adaln(cvec, w_ada, b_ada):
    n = 3 * D_MODEL
    blocks = MOD_ROWS * D_MODEL * 4 + D_MODEL * ADA_TN * 4 + ADA_TN * 4 + MOD_ROWS * ADA_TN * 4
    return pl.pallas_call(
        _adaln_kernel,
        out_shape=jax.ShapeDtypeStruct((DEPTH, MOD_ROWS, n), F32),
        grid=(DEPTH, n // ADA_TN),
        in_specs=[
            pl.BlockSpec((MOD_ROWS, D_MODEL), lambda l, j: (0, 0)),
            pl.BlockSpec((None, D_MODEL, ADA_TN), lambda l, j: (l, 0, j)),
            pl.BlockSpec((None, 1, ADA_TN), lambda l, j: (l, 0, j)),
        ],
        out_specs=pl.BlockSpec((None, MOD_ROWS, ADA_TN), lambda l, j: (l, 0, j)),
        compiler_params=_params(_vmem_limit(blocks, temp_bytes=D_MODEL * ADA_TN * 2), 2),
        name="adaln",
    )(cvec, w_ada, b_ada.reshape(DEPTH, 1, n))


def _mod_row(tile, tm, latent):
    return 1 + tile // (DEC_SEQ // tm) if latent else 0


NORM_TM = 512
NORM_ROWS = 32
IN_TN = 512


def _prenorm_kernel(x_ref, g_ref, sh_ref, sc_ref, h_ref):
    gain = g_ref[...]
    one_sc = 1.0 + sc_ref[...]
    sh = sh_ref[...]

    def body(r, carry):
        sl = pl.ds(pl.multiple_of(r * NORM_ROWS, NORM_ROWS), NORM_ROWS)
        x = x_ref[sl, :]
        ms = jnp.mean(x * x, axis=-1, keepdims=True)
        y = (x * lax.rsqrt(ms + EPS)) * gain
        h_ref[sl, :] = (y * one_sc + sh).astype(BF16)
        return carry

    lax.fori_loop(0, NORM_TM // NORM_ROWS, body, 0, unroll=True)


def _prenorm(x, gain, mods, layer, latent):
    ntok = x.shape[0]
    blocks = NORM_TM * D_MODEL * 4 + 3 * D_MODEL * 4 + NORM_TM * D_MODEL * 2
    mod_spec = lambda which: pl.BlockSpec(
        (None, None, None, 1, D_MODEL), lambda i: (layer, _mod_row(i, NORM_TM, latent), which, 0, 0))
    return pl.pallas_call(
        _prenorm_kernel,
        out_shape=jax.ShapeDtypeStruct((ntok, D_MODEL), BF16),
        grid=(ntok // NORM_TM,),
        in_specs=[
            pl.BlockSpec((NORM_TM, D_MODEL), lambda i: (i, 0)),
            pl.BlockSpec((None, 1, D_MODEL), lambda i: (layer, 0, 0)),
            mod_spec(0),
            mod_spec(1),
        ],
        out_specs=pl.BlockSpec((NORM_TM, D_MODEL), lambda i: (i, 0)),
        compiler_params=_params(_vmem_limit(blocks, temp_bytes=4 * NORM_ROWS * D_MODEL * 4), 1),
        name=f"prenorm_l{layer}_{'latent' if latent else 'prompt'}",
    )(x, gain, mods, mods)


def _inproj_kernel(h_ref, w_ref, o_ref):
    o_ref[...] = _dot(h_ref[...], w_ref[...].astype(BF16))


def _inproj(h, layer, w, latent):
    ntok = h.shape[0]
    n = w.shape[2]
    blocks = D_MODEL * IN_TN * 4 + ntok * IN_TN * 4
    return pl.pallas_call(
        _inproj_kernel,
        out_shape=jax.ShapeDtypeStruct((ntok, n), F32),
        grid=(n // IN_TN,),
        in_specs=[
            pl.BlockSpec((ntok, D_MODEL), lambda j: (0, 0), pipeline_mode=pl.Buffered(1)),
            pl.BlockSpec((None, D_MODEL, IN_TN), lambda j: (layer // 2, 0, j)),
        ],
        out_specs=pl.BlockSpec((ntok, IN_TN), lambda j: (0, j)),
        compiler_params=_params(_vmem_limit(blocks, ntok * D_MODEL * 2, D_MODEL * IN_TN * 2), 1),
        name=f"inproj_l{layer}_{'latent' if latent else 'prompt'}",
    )(h, w)


OUT_TM = 512


def _outproj_kernel(*refs, n_pieces):
    u_refs = refs[:n_pieces]
    w_refs = refs[n_pieces:2 * n_pieces]
    x_ref, gain_ref, gate_ref, o_ref, wbf_ref = refs[2 * n_pieces:]

    @pl.when(pl.program_id(0) == 0)
    def _():
        for p, w_ref in enumerate(w_refs):
            wbf_ref[p] = w_ref[...].astype(BF16)

    out = _dot(u_refs[0][...], wbf_ref[0])
    for p in range(1, n_pieces):
        out = out + _dot(u_refs[p][...], wbf_ref[p])
    ms = jnp.mean(out * out, axis=-1, keepdims=True)
    y = (out * lax.rsqrt(ms + EPS)) * gain_ref[...]
    o_ref[...] = x_ref[...] + gate_ref[...] * y


def _outproj(us, w, x, gain, mods, layer, latent):
    ntok = x.shape[0]
    widths = [u.shape[1] for u in us]
    kp = widths[0]
    k = w.shape[1]
    assert all(kw == kp for kw in widths) and kp * len(us) == k
    blocks = OUT_TM * k * 2 + 2 * OUT_TM * D_MODEL * 4 + 2 * D_MODEL * 4
    u_specs = [pl.BlockSpec((OUT_TM, kp), lambda i: (i, 0)) for _ in us]
    w_specs = [pl.BlockSpec((None, kp, D_MODEL), lambda i, p=p: (layer // 2, p, 0),
                            pipeline_mode=pl.Buffered(1)) for p in range(len(us))]
    return pl.pallas_call(
        functools.partial(_outproj_kernel, n_pieces=len(us)),
        out_shape=jax.ShapeDtypeStruct((ntok, D_MODEL), F32),
        grid=(ntok // OUT_TM,),
        in_specs=u_specs + w_specs + [
            pl.BlockSpec((OUT_TM, D_MODEL), lambda i: (i, 0)),
            pl.BlockSpec((None, 1, D_MODEL), lambda i: (layer, 0, 0)),
            pl.BlockSpec((None, None, None, 1, D_MODEL),
                         lambda i: (layer, _mod_row(i, OUT_TM, latent), 2, 0, 0)),
        ],
        out_specs=pl.BlockSpec((OUT_TM, D_MODEL), lambda i: (i, 0)),
        scratch_shapes=[pltpu.VMEM((len(us), kp, D_MODEL), BF16)],
        compiler_params=_params(
            _vmem_limit(blocks, k * D_MODEL * (4 + 2), 2 * OUT_TM * D_MODEL * 4), 1),
        name=f"outproj_l{layer}_{'latent' if latent else 'prompt'}",
    )(*us, *([w] * len(us)), x, gain, mods)


LOG2_E = 1.4426950408889634
Q_SCALE_LOG2 = HEAD_DIM ** -0.5 * LOG2_E


def _softmax_pv(scores, values, sink=None):
    m = jnp.max(scores[0], axis=-1, keepdims=True)
    for s in scores[1:]:
        m = jnp.maximum(m, jnp.max(s, axis=-1, keepdims=True))
    if sink is not None:
        m = jnp.maximum(m, sink)
    den = None
    acc = None
    for s, v in zip(scores, values):
        p = jnp.exp2(s - m)
        ps = jnp.sum(p, axis=-1, keepdims=True)
        pv = _dot(p.astype(BF16), v)
        den = ps if den is None else den + ps
        acc = pv if acc is None else acc + pv
    if sink is not None:
        den = den + jnp.exp2(sink - m)
    return acc / den


def _ctx_attn_kernel(sink_ref, p_ref, *refs):
    u_ref, ak_ref, av_ref, bk_ref, bv_ref = refs[-5:]
    def col(c):
        return p_ref[:, c:c + HEAD_DIM]

    def head(cq, k, v, cg, sink, cu):
        q = (col(cq) * Q_SCALE_LOG2).astype(BF16)
        o = _softmax_pv([_dot_nt(q, k)], [v], None if sink is None else sink * LOG2_E)
        u_ref[:, cu:cu + HEAD_DIM] = (o * _silu(col(cg))).astype(BF16)

    for n in range(A_KV_HEADS):
        k32 = col(COL_KA + n * HEAD_DIM)
        v32 = col(COL_VA + n * HEAD_DIM)
        ak_ref[n] = k32
        av_ref[n] = v32
        k = k32.astype(BF16)
        v = v32.astype(BF16)
        for g in range(A_GROUP):
            h = n * A_GROUP + g
            head(COL_QA + h * HEAD_DIM, k, v, COL_GATE + h * HEAD_DIM, sink_ref[h], h * HEAD_DIM)
    for h in range(B_HEADS):
        k32 = col(COL_KB + h * HEAD_DIM)
        v32 = col(COL_VB + h * HEAD_DIM)
        bk_ref[h] = k32
        bv_ref[h] = v32
        head(COL_QB + h * HEAD_DIM, k32.astype(BF16), v32.astype(BF16),
             COL_GATE + A_Q + h * HEAD_DIM, None, A_Q + h * HEAD_DIM)


def _ctx_attn(proj, sink, idx, prev_caches):
    blocks = (SEQ * EVEN_IN * 4 + SEQ * EVEN_WIDTH * 2
              + 2 * (A_KV_HEADS + B_HEADS) * SEQ * HEAD_DIM * 4)
    heads = (A_KV_HEADS, A_KV_HEADS, B_HEADS, B_HEADS)
    cache = lambda nh: jax.ShapeDtypeStruct((BATCH, N_EVEN, nh, SEQ, HEAD_DIM), F32)
    cache_spec = lambda nh: pl.BlockSpec((None, None, nh, SEQ, HEAD_DIM), lambda b: (b, idx, 0, 0, 0))
    n_prev = len(prev_caches)
    return pl.pallas_call(
        _ctx_attn_kernel,
        out_shape=(jax.ShapeDtypeStruct((NP, EVEN_WIDTH), BF16),) + tuple(cache(nh) for nh in heads),
        grid=(BATCH,),
        in_specs=[
            pl.BlockSpec(memory_space=pltpu.SMEM),
            pl.BlockSpec((SEQ, EVEN_IN), lambda b: (b, 0)),
        ] + [_any_spec() for _ in prev_caches],
        out_specs=(pl.BlockSpec((SEQ, EVEN_WIDTH), lambda b: (b, 0)),) + tuple(cache_spec(nh) for nh in heads),
        input_output_aliases={2 + k: 1 + k for k in range(n_prev)},
        compiler_params=_params(_vmem_limit(blocks, temp_bytes=8 * SEQ * SEQ * 4), 1),
        name=f"ctx_attn_{idx}",
    )(sink, proj, *prev_caches)


WIN_PAD = DEC_SEQ + 2 * A_BLOCK


def _rope_tables():
    t = jnp.arange(DEC_SEQ)
    half = HEAD_DIM // 2
    nf = half // 2
    inv = ROPE_THETA ** (-jnp.arange(nf, dtype=F32) / nf)
    ang_r = (t // GRID_W).astype(F32)[:, None] * inv[None]
    ang_c = (t % GRID_W).astype(F32)[:, None] * inv[None]
    cos = jnp.concatenate([jnp.cos(ang_r)] * 2 + [jnp.cos(ang_c)] * 2, axis=-1)
    sin = jnp.concatenate([-jnp.sin(ang_r), jnp.sin(ang_r), -jnp.sin(ang_c), jnp.sin(ang_c)], axis=-1)
    return cos, sin


def _rope(x, cos, sin):
    quarter = HEAD_DIM // 4
    lane = lax.broadcasted_iota(jnp.int32, x.shape, 1)
    first = (lane & (2 * quarter - 1)) < quarter
    partner = jnp.where(first, pltpu.roll(x, HEAD_DIM - quarter, 1), pltpu.roll(x, quarter, 1))
    return x * cos + partner * sin


def _win_attn_kernel(sink_ref, q_ref, k_ref, v_ref, ck_ref, cv_ref, gate_ref, cos_ref, sin_ref,
                     u_ref, kpad_ref, vpad_ref):
    n = pl.program_id(1)
    j = pl.program_id(2)

    @pl.when(j == 0)
    def _():
        zeros = jnp.zeros((A_BLOCK, HEAD_DIM), BF16)
        for ref in (kpad_ref, vpad_ref):
            ref[0:A_BLOCK, :] = zeros
            ref[A_BLOCK + DEC_SEQ:WIN_PAD, :] = zeros
        kpad_ref[A_BLOCK:A_BLOCK + DEC_SEQ, :] = _rope(k_ref[...], cos_ref[...], sin_ref[...]).astype(BF16)
        vpad_ref[A_BLOCK:A_BLOCK + DEC_SEQ, :] = v_ref[...].astype(BF16)

    q0 = pl.multiple_of(j * A_BLOCK, A_BLOCK)
    cos_q = cos_ref[pl.ds(q0, A_BLOCK), :]
    sin_q = sin_ref[pl.ds(q0, A_BLOCK), :]
    q = jnp.concatenate(
        [(_rope(q_ref[:, g * HEAD_DIM:(g + 1) * HEAD_DIM], cos_q, sin_q) * Q_SCALE_LOG2).astype(BF16)
         for g in range(A_GROUP)],
        axis=0)
    band_k = kpad_ref[pl.ds(q0, 3 * A_BLOCK), :]
    band_v = vpad_ref[pl.ds(q0, 3 * A_BLOCK), :]

    rows = A_GROUP * A_BLOCK
    s_c = _dot_nt(q, ck_ref[...].astype(BF16))
    s_w = _dot_nt(q, band_k)
    row = lax.broadcasted_iota(jnp.int32, (rows, 1), 0)
    qi = row & (A_BLOCK - 1)
    kk = lax.broadcasted_iota(jnp.int32, (rows, 3 * A_BLOCK), 1)
    first_key = jnp.where(j == 0, A_BLOCK, 0)
    last_key = jnp.where(j == DEC_SEQ // A_BLOCK - 1, 2 * A_BLOCK - 1, 3 * A_BLOCK - 1)
    lower = jnp.maximum(qi, first_key)
    upper = jnp.minimum(qi + 2 * A_WINDOW, last_key)
    s_w = jnp.where((kk >= lower) & (kk <= upper), s_w, NEG_INF)

    head = lax.shift_right_logical(row, A_BLOCK.bit_length() - 1)
    sink = jnp.zeros((rows, 1), F32)
    for g in range(A_GROUP):
        sink = jnp.where(head == g, sink_ref[n * A_GROUP + g] * LOG2_E, sink)

    o = _softmax_pv([s_c, s_w], [cv_ref[...].astype(BF16), band_v], sink)
    for g in range(A_GROUP):
        cols = slice(g * HEAD_DIM, (g + 1) * HEAD_DIM)
        u_ref[:, cols] = (o[g * A_BLOCK:(g + 1) * A_BLOCK] * _silu(gate_ref[:, cols])).astype(BF16)


def _win_attn(proj, sink, cache_k, cache_v, idx, cos, sin):
    nqb = DEC_SEQ // A_BLOCK
    gw = A_GROUP * HEAD_DIM
    blocks = (2 * A_BLOCK * gw * 4 + 2 * DEC_SEQ * HEAD_DIM * 4 + 2 * PAST_LEN * HEAD_DIM * 4
              + 2 * DEC_SEQ * HEAD_DIM * 4 + A_BLOCK * gw * 2)
    cache_spec = pl.BlockSpec((None, None, None, PAST_LEN, HEAD_DIM), lambda b, n, j: (b, idx, n, 0, 0))
    table_spec = pl.BlockSpec((DEC_SEQ, HEAD_DIM), lambda b, n, j: (0, 0))
    return pl.pallas_call(
        _win_attn_kernel,
        out_shape=jax.ShapeDtypeStruct((NS, A_Q), BF16),
        grid=(DEC_BATCH, A_KV_HEADS, nqb),
        in_specs=[
            pl.BlockSpec(memory_space=pltpu.SMEM),
            pl.BlockSpec((A_BLOCK, gw), lambda b, n, j: (b * nqb + j, COL_QA // gw + n)),
            pl.BlockSpec((DEC_SEQ, HEAD_DIM), lambda b, n, j: (b, COL_KA // HEAD_DIM + n)),
            pl.BlockSpec((DEC_SEQ, HEAD_DIM), lambda b, n, j: (b, COL_VA // HEAD_DIM + n)),
            cache_spec,
            cache_spec,
            pl.BlockSpec((A_BLOCK, gw), lambda b, n, j: (b * nqb + j, COL_GATE // gw + n)),
            table_spec,
            table_spec,
        ],
        out_specs=pl.BlockSpec((A_BLOCK, gw), lambda b, n, j: (b * nqb + j, n)),
        scratch_shapes=[pltpu.VMEM((WIN_PAD, HEAD_DIM), BF16), pltpu.VMEM((WIN_PAD, HEAD_DIM), BF16)],
        compiler_params=_params(
            _vmem_limit(blocks, 2 * WIN_PAD * HEAD_DIM * 2, 8 * A_GROUP * A_BLOCK * 3 * A_BLOCK * 4), 3),
        name=f"win_attn_{idx}",
    )(sink, proj, proj, proj, cache_k, cache_v, proj, cos, sin)


RPB_PAD = (16, 128)


def _na_row_offsets(g):
    offsets = []
    for rl in range(NA_QROWS):
        r = g * NA_QROWS + rl
        r0 = min(max(r - NA_ROWS // 2, 0), GRID_ROWS - NA_ROWS)
        row = []
        for kl in range(NA_KROWS):
            kr = g * (GRID_ROWS - NA_KROWS) + kl
            row.append(kr - r + NA_ROWS - 1 if r0 <= kr < r0 + NA_ROWS else None)
        offsets.append(row)
    return offsets


def _fill_na_bias(rpb_ref, bias_ref, g):
    shape = (GRID_W, 2 * GRID_W)
    c = lax.broadcasted_iota(jnp.int32, shape, 0)
    lane = lax.broadcasted_iota(jnp.int32, shape, 1)
    kc = lane & (GRID_W - 1)
    c0 = jnp.clip(c - NA_COLS // 2, 0, GRID_W - NA_COLS)
    col_ok = (kc >= c0) & (kc < c0 + NA_COLS)
    low = lane < GRID_W
    offsets = _na_row_offsets(g)
    used = sorted({d for row in offsets for d in row if d is not None})
    lo, hi = {}, {}
    for d in used:
        row = jnp.broadcast_to(rpb_ref[d:d + 1, :] * LOG2_E, shape)
        lo[d] = pltpu.roll(row, 2 * GRID_W - (NA_COLS - 1), 1, stride=1, stride_axis=0)
        hi[d] = pltpu.roll(row, GRID_W - (NA_COLS - 1), 1, stride=1, stride_axis=0)
    neg = jnp.full(shape, NEG_INF, F32)
    for rl in range(NA_QROWS):
        for p in range(NA_KROWS // 2):
            da, db = offsets[rl][2 * p], offsets[rl][2 * p + 1]
            a = neg if da is None else lo[da]
            b = neg if db is None else hi[db]
            piece = jnp.where(col_ok, jnp.where(low, a, b), NEG_INF)
            bias_ref[rl * GRID_W:(rl + 1) * GRID_W, 2 * p * GRID_W:2 * (p + 1) * GRID_W] = piece


def _na_attn_kernel(q_ref, k_ref, v_ref, ck_ref, cv_ref, gate_ref, rpb_ref, u_ref, bias_ref):
    g = pl.program_id(1)

    @pl.when(pl.program_id(2) == 0)
    def _():
        for group in range(DEC_SEQ // NA_Q):
            @pl.when(g == group)
            def _():
                _fill_na_bias(rpb_ref, bias_ref, group)

    k0 = pl.multiple_of(g * NA_KSHIFT, NA_KSHIFT)
    kwin = k_ref[pl.ds(k0, NA_K), :].astype(BF16)
    vwin = v_ref[pl.ds(k0, NA_K), :].astype(BF16)
    ck = ck_ref[...].astype(BF16)
    cv = cv_ref[...].astype(BF16)
    for r in range(NA_Q // NA_SUB):
        rows = slice(r * NA_SUB, (r + 1) * NA_SUB)
        q = (q_ref[rows, :] * Q_SCALE_LOG2).astype(BF16)
        s_c = _dot_nt(q, ck)
        s_n = _dot_nt(q, kwin) + bias_ref[rows, :]
        o = _softmax_pv([s_c, s_n], [cv, vwin])
        u_ref[rows, :] = (o * _silu(gate_ref[rows, :])).astype(BF16)


def _na_attn(proj, rpb, cache_k, cache_v, idx):
    per_seq = DEC_SEQ // NA_Q
    blocks = (2 * NA_Q * HEAD_DIM * 4 + 2 * DEC_SEQ * HEAD_DIM * 4 + 2 * PAST_LEN * HEAD_DIM * 4
              + RPB_PAD[0] * RPB_PAD[1] * 4 + NA_Q * HEAD_DIM * 2)
    cache_spec = pl.BlockSpec((None, None, None, PAST_LEN, HEAD_DIM), lambda h, g, b: (b, idx, h, 0, 0))
    return pl.pallas_call(
        _na_attn_kernel,
        out_shape=jax.ShapeDtypeStruct((NS, B_W), BF16),
        grid=(B_HEADS, per_seq, DEC_BATCH),
        in_specs=[
            pl.BlockSpec((NA_Q, HEAD_DIM), lambda h, g, b: (b * per_seq + g, COL_QB // HEAD_DIM + h)),
            pl.BlockSpec((DEC_SEQ, HEAD_DIM), lambda h, g, b: (b, COL_KB // HEAD_DIM + h)),
            pl.BlockSpec((DEC_SEQ, HEAD_DIM), lambda h, g, b: (b, COL_VB // HEAD_DIM + h)),
            cache_spec,
            cache_spec,
            pl.BlockSpec((NA_Q, HEAD_DIM), lambda h, g, b: (b * per_seq + g, (COL_GATE + A_Q) // HEAD_DIM + h)),
            pl.BlockSpec((None, None) + RPB_PAD, lambda h, g, b: (idx, h, 0, 0)),
        ],
        out_specs=pl.BlockSpec((NA_Q, HEAD_DIM), lambda h, g, b: (b * per_seq + g, h)),
        scratch_shapes=[pltpu.VMEM((NA_Q, NA_K), F32)],
        compiler_params=_params(
            _vmem_limit(blocks, NA_Q * NA_K * 4, 6 * NA_Q * (NA_K + PAST_LEN) * 4), 3),
        name=f"na_attn_{idx}",
    )(proj, proj, proj, cache_k, cache_v, proj, rpb)


def _ret_kernel(*refs, seq, heads, has_state, emit_state, n_prev):
    dec_ref, q_ref, k_ref, v_ref, gate_ref, gn_ref = refs[:6]
    pos = 6
    if has_state:
        s0f_ref, s0b_ref = refs[pos:pos + 2]
        pos += 2
    pos += n_prev
    u_ref = refs[pos]
    pos += 1
    if emit_state:
        sf_ref, sb_ref = refs[pos:pos + 2]
        pos += 2
    stf_ref, stb_ref, o_ref = refs[pos:pos + 3]

    nc = seq // C_CHUNK
    assert nc % 2 == 0
    kscale = C_DK ** -0.5
    ii = lax.broadcasted_iota(jnp.int32, (C_CHUNK, C_CHUNK), 0).astype(F32)
    jj = lax.broadcasted_iota(jnp.int32, (C_CHUNK, C_CHUNK), 1).astype(F32)
    icol = lax.broadcasted_iota(jnp.int32, (C_CHUNK, 1), 0).astype(F32)

    def decay_tables(direction, head):
        forward = direction == 0
        dec = dec_ref[direction, head]
        log_g = -jnp.exp(jnp.full((C_CHUNK, C_CHUNK), dec, F32))
        log_g_col = -jnp.exp(jnp.full((C_CHUNK, 1), dec, F32))
        log_g_row = -jnp.exp(jnp.full((1, C_DV), dec, F32))
        diff = (ii - jj) if forward else (jj - ii)
        dmat = jnp.where(diff >= 0, jnp.exp(log_g * jnp.maximum(diff, 0.0)), 0.0)
        if forward:
            q_dec = jnp.exp(log_g_col * (icol + 1.0))
            k_dec = jnp.exp(log_g_col * (C_CHUNK - 1.0 - icol)) * kscale
        else:
            q_dec = jnp.exp(log_g_col * (C_CHUNK - icol))
            k_dec = jnp.exp(log_g_col * icol) * kscale
        return dmat, q_dec, k_dec, jnp.exp(log_g_row * float(C_CHUNK))

    def one_head(hh):
        cols = slice(hh * C_DK, (hh + 1) * C_DK)
        head = pl.program_id(1) * heads + hh
        dmat_f, q_dec_f, k_dec_f, chunk_dec_f = decay_tables(0, head)
        dmat_b, q_dec_b, k_dec_b, chunk_dec_b = decay_tables(1, head)
        dmat = (dmat_f + dmat_b) * kscale
        gn = gn_ref[:, cols]

        if has_state:
            stf_ref[hh] = s0f_ref[hh]
            stb_ref[hh] = s0b_ref[hh]
        else:
            stf_ref[hh] = jnp.zeros((C_DK, C_DV), F32)
            stb_ref[hh] = jnp.zeros((C_DK, C_DV), F32)

        def accumulate(rows, value, first):
            o_ref[rows, cols] = value if first else o_ref[rows, cols] + value

        def finalize(rows):
            o = o_ref[rows, cols]
            mu = jnp.mean(o, axis=-1, keepdims=True)
            d = o - mu
            var = jnp.mean(d * d, axis=-1, keepdims=True)
            y = (d * lax.rsqrt(var + EPS)) * gn
            u_ref[rows, cols] = (y * _silu(gate_ref[rows, cols])).astype(BF16)

        for t in range(nc):
            first = t < nc // 2
            rows = slice(t * C_CHUNK, (t + 1) * C_CHUNK)
            q = q_ref[rows, cols].astype(BF16)
            k32 = k_ref[rows, cols]
            v = v_ref[rows, cols].astype(BF16)
            s = _dot_nt(q, k32.astype(BF16)) * dmat
            state = stf_ref[hh]
            accumulate(rows, _dot(s.astype(BF16), v) + _dot(q, state.astype(BF16)) * q_dec_f, first)
            stf_ref[hh] = state * chunk_dec_f + _dot_tn((k32 * k_dec_f).astype(BF16), v)

            rows_b = slice((nc - 1 - t) * C_CHUNK, (nc - t) * C_CHUNK)
            qb = q_ref[rows_b, cols].astype(BF16)
            vb = v_ref[rows_b, cols].astype(BF16)
            state_b = stb_ref[hh]
            accumulate(rows_b, _dot(qb, state_b.astype(BF16)) * q_dec_b, first)
            stb_ref[hh] = state_b * chunk_dec_b + _dot_tn((k_ref[rows_b, cols] * k_dec_b).astype(BF16), vb)
            if not first:
                finalize(rows)
                finalize(rows_b)

        if emit_state:
            sf_ref[hh] = stf_ref[hh]
            sb_ref[hh] = stb_ref[hh]

    for hh in range(heads):
        one_head(hh)


RET_HEADS_PROMPT = 4
RET_HEADS_LATENT = 2


def _retention(proj, decays, gn, idx, *, latent, state_f=None, state_b=None, prev_states=()):
    seq = DEC_SEQ if latent else SEQ
    nb = DEC_BATCH if latent else BATCH
    heads = RET_HEADS_LATENT if latent else RET_HEADS_PROMPT
    width = heads * C_DK
    groups = C_HEADS // heads
    has_state = latent
    emit_state = not latent
    tok = lambda kind: pl.BlockSpec((seq, width), lambda b, h: (b, kind * groups + h))
    in_specs = [pl.BlockSpec(memory_space=pltpu.SMEM), tok(0), tok(1), tok(2), tok(3),
                pl.BlockSpec((None, 1, width), lambda b, h: (idx, 0, h))]
    args = [decays, proj, proj, proj, proj, gn]
    state_spec = pl.BlockSpec((None, None, heads, C_DK, C_DV), lambda b, h: (b, idx, h, 0, 0))
    if has_state:
        in_specs += [state_spec, state_spec]
        args += [state_f, state_b]
    aliases = {len(args) + k: 1 + k for k in range(len(prev_states))}
    in_specs += [_any_spec() for _ in prev_states]
    args += list(prev_states)
    out_shape = [jax.ShapeDtypeStruct((nb * seq, D_MODEL), BF16)]
    out_specs = [pl.BlockSpec((seq, width), lambda b, h: (b, h))]
    if emit_state:
        st = jax.ShapeDtypeStruct((nb, N_ODD, C_HEADS, C_DK, C_DV), F32)
        out_shape += [st, st]
        out_specs += [state_spec, state_spec]
    blocks = 4 * seq * width * 4 + width * 4 + 2 * heads * C_DK * C_DV * 4 + seq * width * 2
    scratch = 2 * heads * C_DK * C_DV * 4 + seq * width * 4
    return pl.pallas_call(
        functools.partial(_ret_kernel, seq=seq, heads=heads, has_state=has_state, emit_state=emit_state,
                          n_prev=len(prev_states)),
        out_shape=tuple(out_shape),
        grid=(nb, groups),
        in_specs=in_specs,
        out_specs=tuple(out_specs),
        scratch_shapes=[pltpu.VMEM((heads, C_DK, C_DV), F32), pltpu.VMEM((heads, C_DK, C_DV), F32),
                        pltpu.VMEM((seq, width), F32)],
        input_output_aliases=aliases,
        compiler_params=_params(_vmem_limit(blocks, scratch, 32 * C_CHUNK * C_DV * 4), 2),
        name=f"retention_{'latent' if latent else 'prompt'}_{idx}",
    )(*args)


def kernel(x_prompt, x_sample, c, cache_a_k, cache_a_v, cache_b_k, cache_b_v, state_ret_f, state_ret_b,
           c_ctx, w_ada, b_ada, norm_pre, norm_post, w_in_even, w_out_even, a_sink, na_rpb,
           w_in_odd, w_out_odd, ret_decay_f, ret_decay_b, ret_gn):
    xp = x_prompt.reshape(NP, D_MODEL)
    xs = x_sample.reshape(NS, D_MODEL)
    cvec = jnp.concatenate(
        [c_ctx[None, :], c, jnp.zeros((MOD_ROWS - 1 - DEC_BATCH, D_MODEL), F32)], axis=0)
    mods = _adaln(cvec, w_ada, b_ada).reshape(DEPTH, MOD_ROWS, 3, 1, D_MODEL)
    gain_pre = norm_pre.reshape(DEPTH, 1, D_MODEL)
    gain_post = norm_post.reshape(DEPTH, 1, D_MODEL)
    gn = ret_gn.reshape(N_ODD, 1, D_MODEL)
    cos, sin = _rope_tables()
    rpb = jnp.pad(na_rpb, ((0, 0), (0, 0), (0, RPB_PAD[0] - na_rpb.shape[2]),
                           (0, RPB_PAD[1] - na_rpb.shape[3])))
    caches = ()
    states = ()
    for layer in range(DEPTH):
        idx = layer // 2
        if layer % 2 == 0:
            proj_p = _inproj(_prenorm(xp, gain_pre, mods, layer, False), layer, w_in_even, False)
            proj_s = _inproj(_prenorm(xs, gain_pre, mods, layer, True), layer, w_in_even, True)
            u_p, *caches = _ctx_attn(proj_p, a_sink[idx], idx, caches)
            u_a = _win_attn(proj_s, a_sink[idx], cache_a_k, cache_a_v, idx, cos, sin)
            u_b = _na_attn(proj_s, rpb, cache_b_k, cache_b_v, idx)
            us_p, us_s, w_out = [u_p], [u_a, u_b], w_out_even
        else:
            proj_p = _inproj(_prenorm(xp, gain_pre, mods, layer, False), layer, w_in_odd, False)
            proj_s = _inproj(_prenorm(xs, gain_pre, mods, layer, True), layer, w_in_odd, True)
            decays = jnp.stack([ret_decay_f[idx], ret_decay_b[idx]], axis=0)
            u_p, *states = _retention(proj_p, decays, gn, idx, latent=False, prev_states=states)
            (u_s,) = _retention(proj_s, decays, gn, idx, latent=True,
                                state_f=state_ret_f, state_b=state_ret_b)
            us_p, us_s = [u_p], [u_s]
            w_out = w_out_odd
        xp = _outproj(us_p, w_out, xp, gain_post, mods, layer, False)
        xs = _outproj(us_s, w_out, xs, gain_post, mods, layer, True)
    return (xp.reshape(BATCH, SEQ, D_MODEL), xs.reshape(DEC_BATCH, DEC_SEQ, D_MODEL), *caches, *states)
```

```python
import functools

import numpy as np
import jax
import jax.numpy as jnp
from jax import lax
from jax.experimental import pallas as pl
from jax.experimental.pallas import tpu as pltpu

D_MODEL = 2048
BATCH = 16
SEQ = 256
DEPTH = 4
DEC_BATCH = 4
DEC_SEQ = 1024
PAST_LEN = 256
GRID_W = 64
HEAD_DIM = 128
A_HEADS = 8
A_KV_HEADS = 2
A_GROUP = A_HEADS // A_KV_HEADS
A_WINDOW = 128
A_BLOCK = 128
B_HEADS = 8
NA_ROWS = 8
NA_COLS = 16
C_HEADS = 8
C_DK = D_MODEL // C_HEADS
C_DV = D_MODEL // C_HEADS
C_CHUNK = 128
ROPE_THETA = 10000.0
EPS = 1e-6

N_EVEN = (DEPTH + 1) // 2
N_ODD = DEPTH // 2
A_Q = A_HEADS * HEAD_DIM
A_KV = A_KV_HEADS * HEAD_DIM
B_W = B_HEADS * HEAD_DIM
EVEN_WIDTH = A_Q + B_W
EVEN_IN = A_Q + 2 * A_KV + 3 * B_W + EVEN_WIDTH
ODD_IN = 4 * D_MODEL
GRID_ROWS = DEC_SEQ // GRID_W

NP = BATCH * SEQ
NS = DEC_BATCH * DEC_SEQ
MOD_ROWS = 8

COL_QA = 0
COL_KA = A_Q
COL_VA = A_Q + A_KV
COL_QB = A_Q + 2 * A_KV
COL_KB = COL_QB + B_W
COL_VB = COL_KB + B_W
COL_GATE = COL_VB + B_W

NA_QROWS = 8
NA_KROWS = 12
NA_Q = NA_QROWS * GRID_W
NA_K = NA_KROWS * GRID_W
NA_KSHIFT = (GRID_ROWS - NA_KROWS) * GRID_W
NA_SUB = 128

V7X_VMEM_BYTES = 64 * 1024 * 1024
VMEM_HEADROOM = 4 * 1024 * 1024

F32 = jnp.float32
BF16 = jnp.bfloat16
NEG_INF = float("-inf")


def _vmem_limit(block_bytes, scratch_bytes=0, temp_bytes=0):
    usable = V7X_VMEM_BYTES - 4 * 1024 * 1024
    assert 2 * block_bytes + scratch_bytes + temp_bytes + VMEM_HEADROOM <= usable
    return usable


def _params(vmem_bytes, ndims):
    return pltpu.CompilerParams(dimension_semantics=("arbitrary",) * ndims, vmem_limit_bytes=vmem_bytes)


def _silu(x):
    return x / (1.0 + jnp.exp(-x))


def _dot(a, b):
    return jnp.dot(a, b, preferred_element_type=F32)


def _dot_nt(a, b):
    return lax.dot_general(a, b, (((1,), (1,)), ((), ())), preferred_element_type=F32)


def _dot_tn(a, b):
    return lax.dot_general(a, b, (((0,), (0,)), ((), ())), preferred_element_type=F32)


def _any_spec():
    return pl.BlockSpec(memory_space=pl.ANY)


ADA_TN = 1024


def _adaln_kernel(c_ref, w_ref, b_ref, o_ref):
    a = _silu(c_ref[...]).astype(BF16)
    o_ref[...] = _dot(a, w_ref[...].astype(BF16)) + b_ref[...]


def _adaln(cvec, w_ada, b_ada):
    n = 3 * D_MODEL
    blocks = MOD_ROWS * D_MODEL * 4 + D_MODEL * ADA_TN * 4 + ADA_TN * 4 + MOD_ROWS * ADA_TN * 4
    return pl.pallas_call(
        _adaln_kernel,
        out_shape=jax.ShapeDtypeStruct((DEPTH, MOD_ROWS, n), F32),
        grid=(DEPTH, n // ADA_TN),
        in_specs=[
            pl.BlockSpec((MOD_ROWS, D_MODEL), lambda l, j: (0, 0)),
            pl.BlockSpec((None, D_MODEL, ADA_TN), lambda l, j: (l, 0, j)),
            pl.BlockSpec((None, 1, ADA_TN), lambda l, j: (l, 0, j)),
        ],
        out_specs=pl.BlockSpec((None, MOD_ROWS, ADA_TN), lambda l, j: (l, 0, j)),
        compiler_params=_params(_vmem_limit(blocks, temp_bytes=D_MODEL * ADA_TN * 2), 2),
        name="adaln",
    )(cvec, w_ada, b_ada.reshape(DEPTH, 1, n))


def _mod_row(tile, tm, latent):
    return 1 + tile // (DEC_SEQ // tm) if latent else 0


NORM_TM = 512
NORM_ROWS = 32
IN_TN = 512


def _prenorm_kernel(x_ref, g_ref, sh_ref, sc_ref, h_ref):
    gain = g_ref[...]
    one_sc = 1.0 + sc_ref[...]
    sh = sh_ref[...]

    def body(r, carry):
        sl = pl.ds(pl.multiple_of(r * NORM_ROWS, NORM_ROWS), NORM_ROWS)
        x = x_ref[sl, :]
        ms = jnp.mean(x * x, axis=-1, keepdims=True)
        y = (x * lax.rsqrt(ms + EPS)) * gain
        h_ref[sl, :] = (y * one_sc + sh).astype(BF16)
        return carry

    lax.fori_loop(0, NORM_TM // NORM_ROWS, body, 0, unroll=True)


def _prenorm(x, gain, mods, layer, latent):
    ntok = x.shape[0]
    blocks = NORM_TM * D_MODEL * 4 + 3 * D_MODEL * 4 + NORM_TM * D_MODEL * 2
    mod_spec = lambda which: pl.BlockSpec(
        (None, None, None, 1, D_MODEL), lambda i: (layer, _mod_row(i, NORM_TM, latent), which, 0, 0))
    return pl.pallas_call(
        _prenorm_kernel,
        out_shape=jax.ShapeDtypeStruct((ntok, D_MODEL), BF16),
        grid=(ntok // NORM_TM,),
        in_specs=[
            pl.BlockSpec((NORM_TM, D_MODEL), lambda i: (i, 0)),
            pl.BlockSpec((None, 1, D_MODEL), lambda i: (layer, 0, 0)),
            mod_spec(0),
            mod_spec(1),
        ],
        out_specs=pl.BlockSpec((NORM_TM, D_MODEL), lambda i: (i, 0)),
        compiler_params=_params(_vmem_limit(blocks, temp_bytes=4 * NORM_ROWS * D_MODEL * 4), 1),
        name=f"prenorm_l{layer}_{'latent' if latent else 'prompt'}",
    )(x, gain, mods, mods)


def _inproj_kernel(h_ref, w_ref, o_ref):
    o_ref[...] = _dot(h_ref[...], w_ref[...].astype(BF16))


def _inproj(h, layer, w, latent):
    ntok = h.shape[0]
    n = w.shape[2]
    blocks = D_MODEL * IN_TN * 4 + ntok * IN_TN * 4
    return pl.pallas_call(
        _inproj_kernel,
        out_shape=jax.ShapeDtypeStruct((ntok, n), F32),
        grid=(n // IN_TN,),
        in_specs=[
            pl.BlockSpec((ntok, D_MODEL), lambda j: (0, 0), pipeline_mode=pl.Buffered(1)),
            pl.BlockSpec((None, D_MODEL, IN_TN), lambda j: (layer // 2, 0, j)),
        ],
        out_specs=pl.BlockSpec((ntok, IN_TN), lambda j: (0, j)),
        compiler_params=_params(_vmem_limit(blocks, ntok * D_MODEL * 2, D_MODEL * IN_TN * 2), 1),
        name=f"inproj_l{layer}_{'latent' if latent else 'prompt'}",
    )(h, w)


OUT_TM = 512
OUT_SUB = 256


def _outproj_kernel(*refs, n_pieces, emit_next):
    u_refs = refs[:n_pieces]
    w_refs = refs[n_pieces:2 * n_pieces]
    pos = 2 * n_pieces
    x_ref, gain_ref, gate_ref = refs[pos:pos + 3]
    pos += 3
    if emit_next:
        ngain_ref, nshift_ref, nscale_ref = refs[pos:pos + 3]
        pos += 3
    o_ref = refs[pos]
    pos += 1
    if emit_next:
        h_ref = refs[pos]
        pos += 1
    wbf_ref = refs[pos]

    @pl.when(pl.program_id(0) == 0)
    def _():
        for p, w_ref in enumerate(w_refs):
            wbf_ref[p] = w_ref[...].astype(BF16)

    for r in range(OUT_TM // OUT_SUB):
        rows = slice(r * OUT_SUB, (r + 1) * OUT_SUB)
        out = _dot(u_refs[0][rows, :], wbf_ref[0])
        for p in range(1, n_pieces):
            out = out + _dot(u_refs[p][rows, :], wbf_ref[p])
        ms = jnp.mean(out * out, axis=-1, keepdims=True)
        y = (out * lax.rsqrt(ms + EPS)) * gain_ref[...]
        x_new = x_ref[rows, :] + gate_ref[...] * y
        o_ref[rows, :] = x_new
        if emit_next:
            ms = jnp.mean(x_new * x_new, axis=-1, keepdims=True)
            y = (x_new * lax.rsqrt(ms + EPS)) * ngain_ref[...]
            h_ref[rows, :] = (y * (1.0 + nscale_ref[...]) + nshift_ref[...]).astype(BF16)


def _outproj(us, w, x, gain_post, gain_pre, mods, layer, latent):
    ntok = x.shape[0]
    widths = [u.shape[1] for u in us]
    kp = widths[0]
    k = w.shape[1]
    assert all(kw == kp for kw in widths) and kp * len(us) == k
    emit_next = layer + 1 < DEPTH
    blocks = OUT_TM * k * 2 + 2 * OUT_TM * D_MODEL * 4 + 5 * D_MODEL * 4 + emit_next * OUT_TM * D_MODEL * 2
    tile_spec = pl.BlockSpec((OUT_TM, D_MODEL), lambda i: (i, 0))
    gain_spec = lambda l: pl.BlockSpec((None, 1, D_MODEL), lambda i: (l, 0, 0))
    mod_spec = lambda l, which: pl.BlockSpec(
        (None, None, None, 1, D_MODEL), lambda i: (l, _mod_row(i, OUT_TM, latent), which, 0, 0))
    u_specs = [pl.BlockSpec((OUT_TM, kp), lambda i: (i, 0)) for _ in us]
    w_specs = [pl.BlockSpec((None, kp, D_MODEL), lambda i, p=p: (layer // 2, p, 0),
                            pipeline_mode=pl.Buffered(1)) for p in range(len(us))]
    in_specs = u_specs + w_specs + [tile_spec, gain_spec(layer), mod_spec(layer, 2)]
    args = [*us, *([w] * len(us)), x, gain_post, mods]
    out_shape = [jax.ShapeDtypeStruct((ntok, D_MODEL), F32)]
    out_specs = [tile_spec]
    if emit_next:
        in_specs += [gain_spec(layer + 1), mod_spec(layer + 1, 0), mod_spec(layer + 1, 1)]
        args += [gain_pre, mods, mods]
        out_shape.append(jax.ShapeDtypeStruct((ntok, D_MODEL), BF16))
        out_specs.append(tile_spec)
    return pl.pallas_call(
        functools.partial(_outproj_kernel, n_pieces=len(us), emit_next=emit_next),
        out_shape=tuple(out_shape),
        grid=(ntok // OUT_TM,),
        in_specs=in_specs,
        out_specs=tuple(out_specs),
        scratch_shapes=[pltpu.VMEM((len(us), kp, D_MODEL), BF16)],
        compiler_params=_params(
            _vmem_limit(blocks, k * D_MODEL * (4 + 2), 3 * OUT_TM * D_MODEL * 2), 1),
        name=f"outproj_l{layer}_{'latent' if latent else 'prompt'}",
    )(*args)


LOG2_E = 1.4426950408889634
Q_SCALE_LOG2 = HEAD_DIM ** -0.5 * LOG2_E


def _softmax_pv(scores, values, sink=None):
    m = jnp.max(scores[0], axis=-1, keepdims=True)
    for s in scores[1:]:
        m = jnp.maximum(m, jnp.max(s, axis=-1, keepdims=True))
    if sink is not None:
        m = jnp.maximum(m, sink)
    den = None
    acc = None
    for s, v in zip(scores, values):
        p = jnp.exp2(s - m)
        ps = jnp.sum(p, axis=-1, keepdims=True)
        pv = _dot(p.astype(BF16), v)
        den = ps if den is None else den + ps
        acc = pv if acc is None else acc + pv
    if sink is not None:
        den = den + jnp.exp2(sink - m)
    return acc / den


def _ctx_attn_kernel(sink_ref, p_ref, *refs):
    u_ref, ak_ref, av_ref, bk_ref, bv_ref = refs[-5:]
    def col(c):
        return p_ref[:, c:c + HEAD_DIM]

    def head(cq, k, v, cg, sink, cu):
        q = (col(cq) * Q_SCALE_LOG2).astype(BF16)
        o = _softmax_pv([_dot_nt(q, k)], [v], None if sink is None else sink * LOG2_E)
        u_ref[:, cu:cu + HEAD_DIM] = (o * _silu(col(cg))).astype(BF16)

    for n in range(A_KV_HEADS):
        k32 = col(COL_KA + n * HEAD_DIM)
        v32 = col(COL_VA + n * HEAD_DIM)
        ak_ref[n] = k32
        av_ref[n] = v32
        k = k32.astype(BF16)
        v = v32.astype(BF16)
        for g in range(A_GROUP):
            h = n * A_GROUP + g
            head(COL_QA + h * HEAD_DIM, k, v, COL_GATE + h * HEAD_DIM, sink_ref[h], h * HEAD_DIM)
    for h in range(B_HEADS):
        k32 = col(COL_KB + h * HEAD_DIM)
        v32 = col(COL_VB + h * HEAD_DIM)
        bk_ref[h] = k32
        bv_ref[h] = v32
        head(COL_QB + h * HEAD_DIM, k32.astype(BF16), v32.astype(BF16),
             COL_GATE + A_Q + h * HEAD_DIM, None, A_Q + h * HEAD_DIM)


def _ctx_attn(proj, sink, idx, prev_caches):
    blocks = (SEQ * EVEN_IN * 4 + SEQ * EVEN_WIDTH * 2
              + 2 * (A_KV_HEADS + B_HEADS) * SEQ * HEAD_DIM * 4)
    heads = (A_KV_HEADS, A_KV_HEADS, B_HEADS, B_HEADS)
    cache = lambda nh: jax.ShapeDtypeStruct((BATCH, N_EVEN, nh, SEQ, HEAD_DIM), F32)
    cache_spec = lambda nh: pl.BlockSpec((None, None, nh, SEQ, HEAD_DIM), lambda b: (b, idx, 0, 0, 0))
    n_prev = len(prev_caches)
    return pl.pallas_call(
        _ctx_attn_kernel,
        out_shape=(jax.ShapeDtypeStruct((NP, EVEN_WIDTH), BF16),) + tuple(cache(nh) for nh in heads),
        grid=(BATCH,),
        in_specs=[
            pl.BlockSpec(memory_space=pltpu.SMEM),
            pl.BlockSpec((SEQ, EVEN_IN), lambda b: (b, 0)),
        ] + [_any_spec() for _ in prev_caches],
        out_specs=(pl.BlockSpec((SEQ, EVEN_WIDTH), lambda b: (b, 0)),) + tuple(cache_spec(nh) for nh in heads),
        input_output_aliases={2 + k: 1 + k for k in range(n_prev)},
        compiler_params=_params(_vmem_limit(blocks, temp_bytes=8 * SEQ * SEQ * 4), 1),
        name=f"ctx_attn_{idx}",
    )(sink, proj, *prev_caches)


WIN_PAD = DEC_SEQ + 2 * A_BLOCK


def _rope_tables():
    t = jnp.arange(DEC_SEQ)
    half = HEAD_DIM // 2
    nf = half // 2
    inv = ROPE_THETA ** (-jnp.arange(nf, dtype=F32) / nf)
    ang_r = (t // GRID_W).astype(F32)[:, None] * inv[None]
    ang_c = (t % GRID_W).astype(F32)[:, None] * inv[None]
    cos = jnp.concatenate([jnp.cos(ang_r)] * 2 + [jnp.cos(ang_c)] * 2, axis=-1)
    sin = jnp.concatenate([-jnp.sin(ang_r), jnp.sin(ang_r), -jnp.sin(ang_c), jnp.sin(ang_c)], axis=-1)
    return cos, sin


def _rope(x, cos, sin):
    quarter = HEAD_DIM // 4
    lane = lax.broadcasted_iota(jnp.int32, x.shape, 1)
    first = (lane & (2 * quarter - 1)) < quarter
    partner = jnp.where(first, pltpu.roll(x, HEAD_DIM - quarter, 1), pltpu.roll(x, quarter, 1))
    return x * cos + partner * sin


def _win_attn_kernel(sink_ref, q_ref, k_ref, v_ref, ck_ref, cv_ref, gate_ref, cos_ref, sin_ref,
                     u_ref, kpad_ref, vpad_ref):
    n = pl.program_id(1)
    j = pl.program_id(2)

    @pl.when(j == 0)
    def _():
        zeros = jnp.zeros((A_BLOCK, HEAD_DIM), BF16)
        for ref in (kpad_ref, vpad_ref):
            ref[0:A_BLOCK, :] = zeros
            ref[A_BLOCK + DEC_SEQ:WIN_PAD, :] = zeros
        kpad_ref[A_BLOCK:A_BLOCK + DEC_SEQ, :] = _rope(k_ref[...], cos_ref[...], sin_ref[...]).astype(BF16)
        vpad_ref[A_BLOCK:A_BLOCK + DEC_SEQ, :] = v_ref[...].astype(BF16)

    q0 = pl.multiple_of(j * A_BLOCK, A_BLOCK)
    cos_q = cos_ref[pl.ds(q0, A_BLOCK), :]
    sin_q = sin_ref[pl.ds(q0, A_BLOCK), :]
    q = jnp.concatenate(
        [(_rope(q_ref[:, g * HEAD_DIM:(g + 1) * HEAD_DIM], cos_q, sin_q) * Q_SCALE_LOG2).astype(BF16)
         for g in range(A_GROUP)],
        axis=0)
    band_k = kpad_ref[pl.ds(q0, 3 * A_BLOCK), :]
    band_v = vpad_ref[pl.ds(q0, 3 * A_BLOCK), :]

    rows = A_GROUP * A_BLOCK
    s_c = _dot_nt(q, ck_ref[...].astype(BF16))
    s_w = _dot_nt(q, band_k)
    row = lax.broadcasted_iota(jnp.int32, (rows, 1), 0)
    qi = row & (A_BLOCK - 1)
    kk = lax.broadcasted_iota(jnp.int32, (rows, 3 * A_BLOCK), 1)
    first_key = jnp.where(j == 0, A_BLOCK, 0)
    last_key = jnp.where(j == DEC_SEQ // A_BLOCK - 1, 2 * A_BLOCK - 1, 3 * A_BLOCK - 1)
    lower = jnp.maximum(qi, first_key)
    upper = jnp.minimum(qi + 2 * A_WINDOW, last_key)
    s_w = jnp.where((kk >= lower) & (kk <= upper), s_w, NEG_INF)

    head = lax.shift_right_logical(row, A_BLOCK.bit_length() - 1)
    sink = jnp.zeros((rows, 1), F32)
    for g in range(A_GROUP):
        sink = jnp.where(head == g, sink_ref[n * A_GROUP + g] * LOG2_E, sink)

    o = _softmax_pv([s_c, s_w], [cv_ref[...].astype(BF16), band_v], sink)
    for g in range(A_GROUP):
        cols = slice(g * HEAD_DIM, (g + 1) * HEAD_DIM)
        u_ref[:, cols] = (o[g * A_BLOCK:(g + 1) * A_BLOCK] * _silu(gate_ref[:, cols])).astype(BF16)


def _win_attn(proj, sink, cache_k, cache_v, idx, cos, sin):
    nqb = DEC_SEQ // A_BLOCK
    gw = A_GROUP * HEAD_DIM
    blocks = (2 * A_BLOCK * gw * 4 + 2 * DEC_SEQ * HEAD_DIM * 4 + 2 * PAST_LEN * HEAD_DIM * 4
              + 2 * DEC_SEQ * HEAD_DIM * 4 + A_BLOCK * gw * 2)
    cache_spec = pl.BlockSpec((None, None, None, PAST_LEN, HEAD_DIM), lambda b, n, j: (b, idx, n, 0, 0))
    table_spec = pl.BlockSpec((DEC_SEQ, HEAD_DIM), lambda b, n, j: (0, 0))
    return pl.pallas_call(
        _win_attn_kernel,
        out_shape=jax.ShapeDtypeStruct((NS, A_Q), BF16),
        grid=(DEC_BATCH, A_KV_HEADS, nqb),
        in_specs=[
            pl.BlockSpec(memory_space=pltpu.SMEM),
            pl.BlockSpec((A_BLOCK, gw), lambda b, n, j: (b * nqb + j, COL_QA // gw + n)),
            pl.BlockSpec((DEC_SEQ, HEAD_DIM), lambda b, n, j: (b, COL_KA // HEAD_DIM + n)),
            pl.BlockSpec((DEC_SEQ, HEAD_DIM), lambda b, n, j: (b, COL_VA // HEAD_DIM + n)),
            cache_spec,
            cache_spec,
            pl.BlockSpec((A_BLOCK, gw), lambda b, n, j: (b * nqb + j, COL_GATE // gw + n)),
            table_spec,
            table_spec,
        ],
        out_specs=pl.BlockSpec((A_BLOCK, gw), lambda b, n, j: (b * nqb + j, n)),
        scratch_shapes=[pltpu.VMEM((WIN_PAD, HEAD_DIM), BF16), pltpu.VMEM((WIN_PAD, HEAD_DIM), BF16)],
        compiler_params=_params(
            _vmem_limit(blocks, 2 * WIN_PAD * HEAD_DIM * 2, 8 * A_GROUP * A_BLOCK * 3 * A_BLOCK * 4), 3),
        name=f"win_attn_{idx}",
    )(sink, proj, proj, proj, cache_k, cache_v, proj, cos, sin)


RPB_PAD = (16, 128)


def _na_row_offsets(g):
    offsets = []
    for rl in range(NA_QROWS):
        r = g * NA_QROWS + rl
        r0 = min(max(r - NA_ROWS // 2, 0), GRID_ROWS - NA_ROWS)
        row = []
        for kl in range(NA_KROWS):
            kr = g * (GRID_ROWS - NA_KROWS) + kl
            row.append(kr - r + NA_ROWS - 1 if r0 <= kr < r0 + NA_ROWS else None)
        offsets.append(row)
    return offsets


def _fill_na_bias(rpb_ref, bias_ref, g):
    shape = (GRID_W, 2 * GRID_W)
    c = lax.broadcasted_iota(jnp.int32, shape, 0)
    lane = lax.broadcasted_iota(jnp.int32, shape, 1)
    kc = lane & (GRID_W - 1)
    c0 = jnp.clip(c - NA_COLS // 2, 0, GRID_W - NA_COLS)
    col_ok = (kc >= c0) & (kc < c0 + NA_COLS)
    low = lane < GRID_W
    offsets = _na_row_offsets(g)
    used = sorted({d for row in offsets for d in row if d is not None})
    lo, hi = {}, {}
    for d in used:
        row = jnp.broadcast_to(rpb_ref[d:d + 1, :] * LOG2_E, shape)
        lo[d] = pltpu.roll(row, 2 * GRID_W - (NA_COLS - 1), 1, stride=1, stride_axis=0)
        hi[d] = pltpu.roll(row, GRID_W - (NA_COLS - 1), 1, stride=1, stride_axis=0)
    neg = jnp.full(shape, NEG_INF, F32)
    for rl in range(NA_QROWS):
        for p in range(NA_KROWS // 2):
            da, db = offsets[rl][2 * p], offsets[rl][2 * p + 1]
            a = neg if da is None else lo[da]
            b = neg if db is None else hi[db]
            piece = jnp.where(col_ok, jnp.where(low, a, b), NEG_INF)
            bias_ref[rl * GRID_W:(rl + 1) * GRID_W, 2 * p * GRID_W:2 * (p + 1) * GRID_W] = piece


def _na_attn_kernel(q_ref, k_ref, v_ref, ck_ref, cv_ref, gate_ref, rpb_ref, u_ref, bias_ref):
    g = pl.program_id(1)

    @pl.when(pl.program_id(2) == 0)
    def _():
        for group in range(DEC_SEQ // NA_Q):
            @pl.when(g == group)
            def _():
                _fill_na_bias(rpb_ref, bias_ref, group)

    k0 = pl.multiple_of(g * NA_KSHIFT, NA_KSHIFT)
    kwin = k_ref[pl.ds(k0, NA_K), :].astype(BF16)
    vwin = v_ref[pl.ds(k0, NA_K), :].astype(BF16)
    ck = ck_ref[...].astype(BF16)
    cv = cv_ref[...].astype(BF16)
    for r in range(NA_Q // NA_SUB):
        rows = slice(r * NA_SUB, (r + 1) * NA_SUB)
        q = (q_ref[rows, :] * Q_SCALE_LOG2).astype(BF16)
        s_c = _dot_nt(q, ck)
        s_n = _dot_nt(q, kwin) + bias_ref[rows, :]
        o = _softmax_pv([s_c, s_n], [cv, vwin])
        u_ref[rows, :] = (o * _silu(gate_ref[rows, :])).astype(BF16)


def _na_attn(proj, rpb, cache_k, cache_v, idx):
    per_seq = DEC_SEQ // NA_Q
    blocks = (2 * NA_Q * HEAD_DIM * 4 + 2 * DEC_SEQ * HEAD_DIM * 4 + 2 * PAST_LEN * HEAD_DIM * 4
              + RPB_PAD[0] * RPB_PAD[1] * 4 + NA_Q * HEAD_DIM * 2)
    cache_spec = pl.BlockSpec((None, None, None, PAST_LEN, HEAD_DIM), lambda h, g, b: (b, idx, h, 0, 0))
    return pl.pallas_call(
        _na_attn_kernel,
        out_shape=jax.ShapeDtypeStruct((NS, B_W), BF16),
        grid=(B_HEADS, per_seq, DEC_BATCH),
        in_specs=[
            pl.BlockSpec((NA_Q, HEAD_DIM), lambda h, g, b: (b * per_seq + g, COL_QB // HEAD_DIM + h)),
            pl.BlockSpec((DEC_SEQ, HEAD_DIM), lambda h, g, b: (b, COL_KB // HEAD_DIM + h)),
            pl.BlockSpec((DEC_SEQ, HEAD_DIM), lambda h, g, b: (b, COL_VB // HEAD_DIM + h)),
            cache_spec,
            cache_spec,
            pl.BlockSpec((NA_Q, HEAD_DIM), lambda h, g, b: (b * per_seq + g, (COL_GATE + A_Q) // HEAD_DIM + h)),
            pl.BlockSpec((None, None) + RPB_PAD, lambda h, g, b: (idx, h, 0, 0)),
        ],
        out_specs=pl.BlockSpec((NA_Q, HEAD_DIM), lambda h, g, b: (b * per_seq + g, h)),
        scratch_shapes=[pltpu.VMEM((NA_Q, NA_K), F32)],
        compiler_params=_params(
            _vmem_limit(blocks, NA_Q * NA_K * 4, 6 * NA_Q * (NA_K + PAST_LEN) * 4), 3),
        name=f"na_attn_{idx}",
    )(proj, proj, proj, cache_k, cache_v, proj, rpb)


def _ret_kernel(*refs, seq, heads, has_state, emit_state, n_prev):
    dec_ref, q_ref, k_ref, v_ref, gate_ref, gn_ref = refs[:6]
    pos = 6
    if has_state:
        s0f_ref, s0b_ref = refs[pos:pos + 2]
        pos += 2
    pos += n_prev
    u_ref = refs[pos]
    pos += 1
    if emit_state:
        sf_ref, sb_ref = refs[pos:pos + 2]
        pos += 2
    stf_ref, stb_ref, o_ref = refs[pos:pos + 3]

    nc = seq // C_CHUNK
    assert nc % 2 == 0
    kscale = C_DK ** -0.5
    ii = lax.broadcasted_iota(jnp.int32, (C_CHUNK, C_CHUNK), 0).astype(F32)
    jj = lax.broadcasted_iota(jnp.int32, (C_CHUNK, C_CHUNK), 1).astype(F32)
    icol = lax.broadcasted_iota(jnp.int32, (C_CHUNK, 1), 0).astype(F32)

    def decay_tables(direction, head):
        forward = direction == 0
        dec = dec_ref[direction, head]
        log_g = -jnp.exp(jnp.full((C_CHUNK, C_CHUNK), dec, F32))
        log_g_col = -jnp.exp(jnp.full((C_CHUNK, 1), dec, F32))
        log_g_row = -jnp.exp(jnp.full((1, C_DV), dec, F32))
        diff = (ii - jj) if forward else (jj - ii)
        dmat = jnp.where(diff >= 0, jnp.exp(log_g * jnp.maximum(diff, 0.0)), 0.0)
        if forward:
            q_dec = jnp.exp(log_g_col * (icol + 1.0))
            k_dec = jnp.exp(log_g_col * (C_CHUNK - 1.0 - icol)) * kscale
        else:
            q_dec = jnp.exp(log_g_col * (C_CHUNK - icol))
            k_dec = jnp.exp(log_g_col * icol) * kscale
        return dmat, q_dec, k_dec, jnp.exp(log_g_row * float(C_CHUNK))

    def one_head(hh):
        cols = slice(hh * C_DK, (hh + 1) * C_DK)
        head = pl.program_id(1) * heads + hh
        dmat_f, q_dec_f, k_dec_f, chunk_dec_f = decay_tables(0, head)
        dmat_b, q_dec_b, k_dec_b, chunk_dec_b = decay_tables(1, head)
        dmat = (dmat_f + dmat_b) * kscale
        gn = gn_ref[:, cols]

        if has_state:
            stf_ref[hh] = s0f_ref[hh]
            stb_ref[hh] = s0b_ref[hh]
        else:
            stf_ref[hh] = jnp.zeros((C_DK, C_DV), F32)
            stb_ref[hh] = jnp.zeros((C_DK, C_DV), F32)

        def accumulate(rows, value, first):
            o_ref[rows, cols] = value if first else o_ref[rows, cols] + value

        def finalize(rows):
            o = o_ref[rows, cols]
            mu = jnp.mean(o, axis=-1, keepdims=True)
            d = o - mu
            var = jnp.mean(d * d, axis=-1, keepdims=True)
            y = (d * lax.rsqrt(var + EPS)) * gn
            u_ref[rows, cols] = (y * _silu(gate_ref[rows, cols])).astype(BF16)

        for t in range(nc):
            first = t < nc // 2
            rows = slice(t * C_CHUNK, (t + 1) * C_CHUNK)
            q = q_ref[rows, cols].astype(BF16)
            k32 = k_ref[rows, cols]
            v = v_ref[rows, cols].astype(BF16)
            s = _dot_nt(q, k32.astype(BF16)) * dmat
            state = stf_ref[hh]
            accumulate(rows, _dot(s.astype(BF16), v) + _dot(q, state.astype(BF16)) * q_dec_f, first)
            stf_ref[hh] = state * chunk_dec_f + _dot_tn((k32 * k_dec_f).astype(BF16), v)

            rows_b = slice((nc - 1 - t) * C_CHUNK, (nc - t) * C_CHUNK)
            qb = q_ref[rows_b, cols].astype(BF16)
            vb = v_ref[rows_b, cols].astype(BF16)
            state_b = stb_ref[hh]
            accumulate(rows_b, _dot(qb, state_b.astype(BF16)) * q_dec_b, first)
            stb_ref[hh] = state_b * chunk_dec_b + _dot_tn((k_ref[rows_b, cols] * k_dec_b).astype(BF16), vb)
            if not first:
                finalize(rows)
                finalize(rows_b)

        if emit_state:
            sf_ref[hh] = stf_ref[hh]
            sb_ref[hh] = stb_ref[hh]

    for hh in range(heads):
        one_head(hh)


RET_HEADS_PROMPT = 4
RET_HEADS_LATENT = 2


def _retention(proj, decays, gn, idx, *, latent, state_f=None, state_b=None, prev_states=()):
    seq = DEC_SEQ if latent else SEQ
    nb = DEC_BATCH if latent else BATCH
    heads = RET_HEADS_LATENT if latent else RET_HEADS_PROMPT
    width = heads * C_DK
    groups = C_HEADS // heads
    has_state = latent
    emit_state = not latent
    tok = lambda kind: pl.BlockSpec((seq, width), lambda b, h: (b, kind * groups + h))
    in_specs = [pl.BlockSpec(memory_space=pltpu.SMEM), tok(0), tok(1), tok(2), tok(3),
                pl.BlockSpec((None, 1, width), lambda b, h: (idx, 0, h))]
    args = [decays, proj, proj, proj, proj, gn]
    state_spec = pl.BlockSpec((None, None, heads, C_DK, C_DV), lambda b, h: (b, idx, h, 0, 0))
    if has_state:
        in_specs += [state_spec, state_spec]
        args += [state_f, state_b]
    aliases = {len(args) + k: 1 + k for k in range(len(prev_states))}
    in_specs += [_any_spec() for _ in prev_states]
    args += list(prev_states)
    out_shape = [jax.ShapeDtypeStruct((nb * seq, D_MODEL), BF16)]
    out_specs = [pl.BlockSpec((seq, width), lambda b, h: (b, h))]
    if emit_state:
        st = jax.ShapeDtypeStruct((nb, N_ODD, C_HEADS, C_DK, C_DV), F32)
        out_shape += [st, st]
        out_specs += [state_spec, state_spec]
    blocks = 4 * seq * width * 4 + width * 4 + 2 * heads * C_DK * C_DV * 4 + seq * width * 2
    scratch = 2 * heads * C_DK * C_DV * 4 + seq * width * 4
    return pl.pallas_call(
        functools.partial(_ret_kernel, seq=seq, heads=heads, has_state=has_state, emit_state=emit_state,
                          n_prev=len(prev_states)),
        out_shape=tuple(out_shape),
        grid=(nb, groups),
        in_specs=in_specs,
        out_specs=tuple(out_specs),
        scratch_shapes=[pltpu.VMEM((heads, C_DK, C_DV), F32), pltpu.VMEM((heads, C_DK, C_DV), F32),
                        pltpu.VMEM((seq, width), F32)],
        input_output_aliases=aliases,
        compiler_params=_params(_vmem_limit(blocks, scratch, 32 * C_CHUNK * C_DV * 4), 2),
        name=f"retention_{'latent' if latent else 'prompt'}_{idx}",
    )(*args)


def kernel(x_prompt, x_sample, c, cache_a_k, cache_a_v, cache_b_k, cache_b_v, state_ret_f, state_ret_b,
           c_ctx, w_ada, b_ada, norm_pre, norm_post, w_in_even, w_out_even, a_sink, na_rpb,
           w_in_odd, w_out_odd, ret_decay_f, ret_decay_b, ret_gn):
    xp = x_prompt.reshape(NP, D_MODEL)
    xs = x_sample.reshape(NS, D_MODEL)
    cvec = jnp.concatenate(
        [c_ctx[None, :], c, jnp.zeros((MOD_ROWS - 1 - DEC_BATCH, D_MODEL), F32)], axis=0)
    mods = _adaln(cvec, w_ada, b_ada).reshape(DEPTH, MOD_ROWS, 3, 1, D_MODEL)
    gain_pre = norm_pre.reshape(DEPTH, 1, D_MODEL)
    gain_post = norm_post.reshape(DEPTH, 1, D_MODEL)
    gn = ret_gn.reshape(N_ODD, 1, D_MODEL)
    cos, sin = _rope_tables()
    rpb = jnp.pad(na_rpb, ((0, 0), (0, 0), (0, RPB_PAD[0] - na_rpb.shape[2]),
                           (0, RPB_PAD[1] - na_rpb.shape[3])))
    caches = ()
    states = ()
    hp = _prenorm(xp, gain_pre, mods, 0, False)
    hs = _prenorm(xs, gain_pre, mods, 0, True)
    for layer in range(DEPTH):
        idx = layer // 2
        if layer % 2 == 0:
            proj_p = _inproj(hp, layer, w_in_even, False)
            proj_s = _inproj(hs, layer, w_in_even, True)
            u_p, *caches = _ctx_attn(proj_p, a_sink[idx], idx, caches)
            u_a = _win_attn(proj_s, a_sink[idx], cache_a_k, cache_a_v, idx, cos, sin)
            u_b = _na_attn(proj_s, rpb, cache_b_k, cache_b_v, idx)
            us_p, us_s, w_out = [u_p], [u_a, u_b], w_out_even
        else:
            proj_p = _inproj(hp, layer, w_in_odd, False)
            proj_s = _inproj(hs, layer, w_in_odd, True)
            decays = jnp.stack([ret_decay_f[idx], ret_decay_b[idx]], axis=0)
            u_p, *states = _retention(proj_p, decays, gn, idx, latent=False, prev_states=states)
            (u_s,) = _retention(proj_s, decays, gn, idx, latent=True,
                                state_f=state_ret_f, state_b=state_ret_b)
            us_p, us_s = [u_p], [u_s]
            w_out = w_out_odd
        xp, *hp = _outproj(us_p, w_out, xp, gain_post, gain_pre, mods, layer, False)
        xs, *hs = _outproj(us_s, w_out, xs, gain_post, gain_pre, mods, layer, True)
        hp, hs = (hp[0], hs[0]) if hp else (None, None)
    return (xp.reshape(BATCH, SEQ, D_MODEL), xs.reshape(DEC_BATCH, DEC_SEQ, D_MODEL), *caches, *states)
```

```python
import functools

import numpy as np
import jax
import jax.numpy as jnp
from jax import lax
from jax.experimental import pallas as pl
from jax.experimental.pallas import tpu as pltpu

D_MODEL = 2048
BATCH = 16
SEQ = 256
DEPTH = 4
DEC_BATCH = 4
DEC_SEQ = 1024
PAST_LEN = 256
GRID_W = 64
HEAD_DIM = 128
A_HEADS = 8
A_KV_HEADS = 2
A_GROUP = A_HEADS // A_KV_HEADS
A_WINDOW = 128
A_BLOCK = 128
B_HEADS = 8
NA_ROWS = 8
NA_COLS = 16
C_HEADS = 8
C_DK = D_MODEL // C_HEADS
C_DV = D_MODEL // C_HEADS
C_CHUNK = 128
ROPE_THETA = 10000.0
EPS = 1e-6

N_EVEN = (DEPTH + 1) // 2
N_ODD = DEPTH // 2
A_Q = A_HEADS * HEAD_DIM
A_KV = A_KV_HEADS * HEAD_DIM
B_W = B_HEADS * HEAD_DIM
EVEN_WIDTH = A_Q + B_W
EVEN_IN = A_Q + 2 * A_KV + 3 * B_W + EVEN_WIDTH
ODD_IN = 4 * D_MODEL
GRID_ROWS = DEC_SEQ // GRID_W

NP = BATCH * SEQ
NS = DEC_BATCH * DEC_SEQ
MOD_ROWS = 8

COL_QA = 0
COL_KA = A_Q
COL_VA = A_Q + A_KV
COL_QB = A_Q + 2 * A_KV
COL_KB = COL_QB + B_W
COL_VB = COL_KB + B_W
COL_GATE = COL_VB + B_W

NA_QROWS = 8
NA_KROWS = 12
NA_Q = NA_QROWS * GRID_W
NA_K = NA_KROWS * GRID_W
NA_KSHIFT = (GRID_ROWS - NA_KROWS) * GRID_W
NA_REQS = 4

V7X_VMEM_BYTES = 64 * 1024 * 1024
VMEM_HEADROOM = 4 * 1024 * 1024

F32 = jnp.float32
BF16 = jnp.bfloat16
NEG_INF = float("-inf")


def _vmem_limit(block_bytes, scratch_bytes=0, temp_bytes=0):
    usable = V7X_VMEM_BYTES - 4 * 1024 * 1024
    assert 2 * block_bytes + scratch_bytes + temp_bytes + VMEM_HEADROOM <= usable
    return usable


def _params(vmem_bytes, ndims):
    return pltpu.CompilerParams(dimension_semantics=("arbitrary",) * ndims, vmem_limit_bytes=vmem_bytes)


def _silu(x):
    return x / (1.0 + jnp.exp(-x))


def _dot(a, b):
    return jnp.dot(a, b, preferred_element_type=F32)


def _dot_nt(a, b):
    return lax.dot_general(a, b, (((1,), (1,)), ((), ())), preferred_element_type=F32)


def _dot_tn(a, b):
    return lax.dot_general(a, b, (((0,), (0,)), ((), ())), preferred_element_type=F32)


def _any_spec():
    return pl.BlockSpec(memory_space=pl.ANY)


ADA_TN = 1024


def _adaln_kernel(c_ref, w_ref, b_ref, o_ref):
    a = _silu(c_ref[...]).astype(BF16)
    o_ref[...] = _dot(a, w_ref[...].astype(BF16)) + b_ref[...]


def _adaln(cvec, w_ada, b_ada):
    n = 3 * D_MODEL
    blocks = MOD_ROWS * D_MODEL * 4 + D_MODEL * ADA_TN * 4 + ADA_TN * 4 + MOD_ROWS * ADA_TN * 4
    return pl.pallas_call(
        _adaln_kernel,
        out_shape=jax.ShapeDtypeStruct((DEPTH, MOD_ROWS, n), F32),
        grid=(DEPTH, n // ADA_TN),
        in_specs=[
            pl.BlockSpec((MOD_ROWS, D_MODEL), lambda l, j: (0, 0)),
            pl.BlockSpec((None, D_MODEL, ADA_TN), lambda l, j: (l, 0, j)),
            pl.BlockSpec((None, 1, ADA_TN), lambda l, j: (l, 0, j)),
        ],
        out_specs=pl.BlockSpec((None, MOD_ROWS, ADA_TN), lambda l, j: (l, 0, j)),
        compiler_params=_params(_vmem_limit(blocks, temp_bytes=D_MODEL * ADA_TN * 2), 2),
        name="adaln",
    )(cvec, w_ada, b_ada.reshape(DEPTH, 1, n))


def _mod_row(tile, tm, latent):
    return 1 + tile // (DEC_SEQ // tm) if latent else 0


NORM_TM = 512
NORM_ROWS = 32
IN_TN = 512


def _prenorm_kernel(x_ref, g_ref, sh_ref, sc_ref, h_ref):
    gain = g_ref[...]
    one_sc = 1.0 + sc_ref[...]
    sh = sh_ref[...]

    def body(r, carry):
        sl = pl.ds(pl.multiple_of(r * NORM_ROWS, NORM_ROWS), NORM_ROWS)
        x = x_ref[sl, :]
        ms = jnp.mean(x * x, axis=-1, keepdims=True)
        y = (x * lax.rsqrt(ms + EPS)) * gain
        h_ref[sl, :] = (y * one_sc + sh).astype(BF16)
        return carry

    lax.fori_loop(0, NORM_TM // NORM_ROWS, body, 0, unroll=True)


def _prenorm(x, gain, mods, layer, latent):
    ntok = x.shape[0]
    blocks = NORM_TM * D_MODEL * 4 + 3 * D_MODEL * 4 + NORM_TM * D_MODEL * 2
    mod_spec = lambda which: pl.BlockSpec(
        (None, None, None, 1, D_MODEL), lambda i: (layer, _mod_row(i, NORM_TM, latent), which, 0, 0))
    return pl.pallas_call(
        _prenorm_kernel,
        out_shape=jax.ShapeDtypeStruct((ntok, D_MODEL), BF16),
        grid=(ntok // NORM_TM,),
        in_specs=[
            pl.BlockSpec((NORM_TM, D_MODEL), lambda i: (i, 0)),
            pl.BlockSpec((None, 1, D_MODEL), lambda i: (layer, 0, 0)),
            mod_spec(0),
            mod_spec(1),
        ],
        out_specs=pl.BlockSpec((NORM_TM, D_MODEL), lambda i: (i, 0)),
        compiler_params=_params(_vmem_limit(blocks, temp_bytes=4 * NORM_ROWS * D_MODEL * 4), 1),
        name=f"prenorm_l{layer}_{'latent' if latent else 'prompt'}",
    )(x, gain, mods, mods)


def _inproj_kernel(h_ref, w_ref, o_ref):
    o_ref[...] = _dot(h_ref[...], w_ref[...].astype(BF16))


def _inproj(h, layer, w, latent):
    ntok = h.shape[0]
    n = w.shape[2]
    blocks = D_MODEL * IN_TN * 4 + ntok * IN_TN * 4
    return pl.pallas_call(
        _inproj_kernel,
        out_shape=jax.ShapeDtypeStruct((ntok, n), F32),
        grid=(n // IN_TN,),
        in_specs=[
            pl.BlockSpec((ntok, D_MODEL), lambda j: (0, 0), pipeline_mode=pl.Buffered(1)),
            pl.BlockSpec((None, D_MODEL, IN_TN), lambda j: (layer // 2, 0, j)),
        ],
        out_specs=pl.BlockSpec((ntok, IN_TN), lambda j: (0, j)),
        compiler_params=_params(_vmem_limit(blocks, ntok * D_MODEL * 2, D_MODEL * IN_TN * 2), 1),
        name=f"inproj_l{layer}_{'latent' if latent else 'prompt'}",
    )(h, w)


OUT_TM = 512
OUT_SUB = 256


def _outproj_kernel(*refs, n_pieces, emit_next):
    u_refs = refs[:n_pieces]
    w_refs = refs[n_pieces:2 * n_pieces]
    pos = 2 * n_pieces
    x_ref, gain_ref, gate_ref = refs[pos:pos + 3]
    pos += 3
    if emit_next:
        ngain_ref, nshift_ref, nscale_ref = refs[pos:pos + 3]
        pos += 3
    o_ref = refs[pos]
    pos += 1
    if emit_next:
        h_ref = refs[pos]
        pos += 1
    wbf_ref = refs[pos]

    @pl.when(pl.program_id(0) == 0)
    def _():
        for p, w_ref in enumerate(w_refs):
            wbf_ref[p] = w_ref[...].astype(BF16)

    for r in range(OUT_TM // OUT_SUB):
        rows = slice(r * OUT_SUB, (r + 1) * OUT_SUB)
        out = _dot(u_refs[0][rows, :], wbf_ref[0])
        for p in range(1, n_pieces):
            out = out + _dot(u_refs[p][rows, :], wbf_ref[p])
        ms = jnp.mean(out * out, axis=-1, keepdims=True)
        y = (out * lax.rsqrt(ms + EPS)) * gain_ref[...]
        x_new = x_ref[rows, :] + gate_ref[...] * y
        o_ref[rows, :] = x_new
        if emit_next:
            ms = jnp.mean(x_new * x_new, axis=-1, keepdims=True)
            y = (x_new * lax.rsqrt(ms + EPS)) * ngain_ref[...]
            h_ref[rows, :] = (y * (1.0 + nscale_ref[...]) + nshift_ref[...]).astype(BF16)


def _outproj(us, w, x, gain_post, gain_pre, mods, layer, latent):
    ntok = x.shape[0]
    widths = [u.shape[1] for u in us]
    kp = widths[0]
    k = w.shape[1]
    assert all(kw == kp for kw in widths) and kp * len(us) == k
    emit_next = layer + 1 < DEPTH
    blocks = OUT_TM * k * 2 + 2 * OUT_TM * D_MODEL * 4 + 5 * D_MODEL * 4 + emit_next * OUT_TM * D_MODEL * 2
    tile_spec = pl.BlockSpec((OUT_TM, D_MODEL), lambda i: (i, 0))
    gain_spec = lambda l: pl.BlockSpec((None, 1, D_MODEL), lambda i: (l, 0, 0))
    mod_spec = lambda l, which: pl.BlockSpec(
        (None, None, None, 1, D_MODEL), lambda i: (l, _mod_row(i, OUT_TM, latent), which, 0, 0))
    u_specs = [pl.BlockSpec((OUT_TM, kp), lambda i: (i, 0)) for _ in us]
    w_specs = [pl.BlockSpec((None, kp, D_MODEL), lambda i, p=p: (layer // 2, p, 0),
                            pipeline_mode=pl.Buffered(1)) for p in range(len(us))]
    in_specs = u_specs + w_specs + [tile_spec, gain_spec(layer), mod_spec(layer, 2)]
    args = [*us, *([w] * len(us)), x, gain_post, mods]
    out_shape = [jax.ShapeDtypeStruct((ntok, D_MODEL), F32)]
    out_specs = [tile_spec]
    if emit_next:
        in_specs += [gain_spec(layer + 1), mod_spec(layer + 1, 0), mod_spec(layer + 1, 1)]
        args += [gain_pre, mods, mods]
        out_shape.append(jax.ShapeDtypeStruct((ntok, D_MODEL), BF16))
        out_specs.append(tile_spec)
    return pl.pallas_call(
        functools.partial(_outproj_kernel, n_pieces=len(us), emit_next=emit_next),
        out_shape=tuple(out_shape),
        grid=(ntok // OUT_TM,),
        in_specs=in_specs,
        out_specs=tuple(out_specs),
        scratch_shapes=[pltpu.VMEM((len(us), kp, D_MODEL), BF16)],
        compiler_params=_params(
            _vmem_limit(blocks, k * D_MODEL * (4 + 2), 3 * OUT_TM * D_MODEL * 2), 1),
        name=f"outproj_l{layer}_{'latent' if latent else 'prompt'}",
    )(*args)


LOG2_E = 1.4426950408889634
Q_SCALE_LOG2 = HEAD_DIM ** -0.5 * LOG2_E


def _softmax_pv(scores, values, sink=None):
    m = jnp.max(scores[0], axis=-1, keepdims=True)
    for s in scores[1:]:
        m = jnp.maximum(m, jnp.max(s, axis=-1, keepdims=True))
    if sink is not None:
        m = jnp.maximum(m, sink)
    den = None
    acc = None
    for s, v in zip(scores, values):
        p = jnp.exp2(s - m)
        ps = jnp.sum(p, axis=-1, keepdims=True)
        pv = _dot(p.astype(BF16), v)
        den = ps if den is None else den + ps
        acc = pv if acc is None else acc + pv
    if sink is not None:
        den = den + jnp.exp2(sink - m)
    return acc / den


def _ctx_attn_kernel(sink_ref, p_ref, *refs):
    u_ref, ak_ref, av_ref, bk_ref, bv_ref = refs[-5:]
    def col(c):
        return p_ref[:, c:c + HEAD_DIM]

    def head(cq, k, v, cg, sink, cu):
        q = (col(cq) * Q_SCALE_LOG2).astype(BF16)
        o = _softmax_pv([_dot_nt(q, k)], [v], None if sink is None else sink * LOG2_E)
        u_ref[:, cu:cu + HEAD_DIM] = (o * _silu(col(cg))).astype(BF16)

    for n in range(A_KV_HEADS):
        k32 = col(COL_KA + n * HEAD_DIM)
        v32 = col(COL_VA + n * HEAD_DIM)
        ak_ref[n] = k32
        av_ref[n] = v32
        k = k32.astype(BF16)
        v = v32.astype(BF16)
        for g in range(A_GROUP):
            h = n * A_GROUP + g
            head(COL_QA + h * HEAD_DIM, k, v, COL_GATE + h * HEAD_DIM, sink_ref[h], h * HEAD_DIM)
    for h in range(B_HEADS):
        k32 = col(COL_KB + h * HEAD_DIM)
        v32 = col(COL_VB + h * HEAD_DIM)
        bk_ref[h] = k32
        bv_ref[h] = v32
        head(COL_QB + h * HEAD_DIM, k32.astype(BF16), v32.astype(BF16),
             COL_GATE + A_Q + h * HEAD_DIM, None, A_Q + h * HEAD_DIM)


def _ctx_attn(proj, sink, idx, prev_caches):
    blocks = (SEQ * EVEN_IN * 4 + SEQ * EVEN_WIDTH * 2
              + 2 * (A_KV_HEADS + B_HEADS) * SEQ * HEAD_DIM * 4)
    heads = (A_KV_HEADS, A_KV_HEADS, B_HEADS, B_HEADS)
    cache = lambda nh: jax.ShapeDtypeStruct((BATCH, N_EVEN, nh, SEQ, HEAD_DIM), F32)
    cache_spec = lambda nh: pl.BlockSpec((None, None, nh, SEQ, HEAD_DIM), lambda b: (b, idx, 0, 0, 0))
    n_prev = len(prev_caches)
    return pl.pallas_call(
        _ctx_attn_kernel,
        out_shape=(jax.ShapeDtypeStruct((NP, EVEN_WIDTH), BF16),) + tuple(cache(nh) for nh in heads),
        grid=(BATCH,),
        in_specs=[
            pl.BlockSpec(memory_space=pltpu.SMEM),
            pl.BlockSpec((SEQ, EVEN_IN), lambda b: (b, 0)),
        ] + [_any_spec() for _ in prev_caches],
        out_specs=(pl.BlockSpec((SEQ, EVEN_WIDTH), lambda b: (b, 0)),) + tuple(cache_spec(nh) for nh in heads),
        input_output_aliases={2 + k: 1 + k for k in range(n_prev)},
        compiler_params=_params(_vmem_limit(blocks, temp_bytes=8 * SEQ * SEQ * 4), 1),
        name=f"ctx_attn_{idx}",
    )(sink, proj, *prev_caches)


WIN_PAD = DEC_SEQ + 2 * A_BLOCK


def _rope_tables():
    t = jnp.arange(DEC_SEQ)
    half = HEAD_DIM // 2
    nf = half // 2
    inv = ROPE_THETA ** (-jnp.arange(nf, dtype=F32) / nf)
    ang_r = (t // GRID_W).astype(F32)[:, None] * inv[None]
    ang_c = (t % GRID_W).astype(F32)[:, None] * inv[None]
    cos = jnp.concatenate([jnp.cos(ang_r)] * 2 + [jnp.cos(ang_c)] * 2, axis=-1)
    sin = jnp.concatenate([-jnp.sin(ang_r), jnp.sin(ang_r), -jnp.sin(ang_c), jnp.sin(ang_c)], axis=-1)
    return cos, sin


def _rope(x, cos, sin):
    quarter = HEAD_DIM // 4
    lane = lax.broadcasted_iota(jnp.int32, x.shape, 1)
    first = (lane & (2 * quarter - 1)) < quarter
    partner = jnp.where(first, pltpu.roll(x, HEAD_DIM - quarter, 1), pltpu.roll(x, quarter, 1))
    return x * cos + partner * sin


def _win_attn_kernel(sink_ref, q_ref, k_ref, v_ref, ck_ref, cv_ref, gate_ref, cos_ref, sin_ref,
                     u_ref, kpad_ref, vpad_ref):
    n = pl.program_id(1)
    nqb = DEC_SEQ // A_BLOCK
    rows = A_GROUP * A_BLOCK

    zeros = jnp.zeros((A_BLOCK, HEAD_DIM), BF16)
    for ref in (kpad_ref, vpad_ref):
        ref[0:A_BLOCK, :] = zeros
        ref[A_BLOCK + DEC_SEQ:WIN_PAD, :] = zeros
    kpad_ref[A_BLOCK:A_BLOCK + DEC_SEQ, :] = _rope(k_ref[...], cos_ref[...], sin_ref[...]).astype(BF16)
    vpad_ref[A_BLOCK:A_BLOCK + DEC_SEQ, :] = v_ref[...].astype(BF16)
    ck = ck_ref[...].astype(BF16)
    cv = cv_ref[...].astype(BF16)

    row = lax.broadcasted_iota(jnp.int32, (rows, 1), 0)
    qi = row & (A_BLOCK - 1)
    kk = lax.broadcasted_iota(jnp.int32, (rows, 3 * A_BLOCK), 1)
    head = lax.shift_right_logical(row, A_BLOCK.bit_length() - 1)
    sink = jnp.zeros((rows, 1), F32)
    for g in range(A_GROUP):
        sink = jnp.where(head == g, sink_ref[n * A_GROUP + g] * LOG2_E, sink)

    for j in range(nqb):
        blk = slice(j * A_BLOCK, (j + 1) * A_BLOCK)
        q = jnp.concatenate(
            [(_rope(q_ref[blk, g * HEAD_DIM:(g + 1) * HEAD_DIM], cos_ref[blk, :], sin_ref[blk, :])
              * Q_SCALE_LOG2).astype(BF16) for g in range(A_GROUP)],
            axis=0)
        band = slice(j * A_BLOCK, (j + 3) * A_BLOCK)
        s_c = _dot_nt(q, ck)
        s_w = _dot_nt(q, kpad_ref[band, :])
        lower = jnp.maximum(qi, A_BLOCK if j == 0 else 0)
        upper = jnp.minimum(qi + 2 * A_WINDOW, (2 if j == nqb - 1 else 3) * A_BLOCK - 1)
        s_w = jnp.where((kk >= lower) & (kk <= upper), s_w, NEG_INF)
        o = _softmax_pv([s_c, s_w], [cv, vpad_ref[band, :]], sink)
        for g in range(A_GROUP):
            cols = slice(g * HEAD_DIM, (g + 1) * HEAD_DIM)
            u_ref[blk, cols] = (o[g * A_BLOCK:(g + 1) * A_BLOCK] * _silu(gate_ref[blk, cols])).astype(BF16)


def _win_attn(proj, sink, cache_k, cache_v, idx, cos, sin):
    gw = A_GROUP * HEAD_DIM
    blocks = (2 * DEC_SEQ * gw * 4 + 2 * DEC_SEQ * HEAD_DIM * 4 + 2 * PAST_LEN * HEAD_DIM * 4
              + 2 * DEC_SEQ * HEAD_DIM * 4 + DEC_SEQ * gw * 2)
    cache_spec = pl.BlockSpec((None, None, None, PAST_LEN, HEAD_DIM), lambda b, n: (b, idx, n, 0, 0))
    table_spec = pl.BlockSpec((DEC_SEQ, HEAD_DIM), lambda b, n: (0, 0))
    return pl.pallas_call(
        _win_attn_kernel,
        out_shape=jax.ShapeDtypeStruct((NS, A_Q), BF16),
        grid=(DEC_BATCH, A_KV_HEADS),
        in_specs=[
            pl.BlockSpec(memory_space=pltpu.SMEM),
            pl.BlockSpec((DEC_SEQ, gw), lambda b, n: (b, COL_QA // gw + n)),
            pl.BlockSpec((DEC_SEQ, HEAD_DIM), lambda b, n: (b, COL_KA // HEAD_DIM + n)),
            pl.BlockSpec((DEC_SEQ, HEAD_DIM), lambda b, n: (b, COL_VA // HEAD_DIM + n)),
            cache_spec,
            cache_spec,
            pl.BlockSpec((DEC_SEQ, gw), lambda b, n: (b, COL_GATE // gw + n)),
            table_spec,
            table_spec,
        ],
        out_specs=pl.BlockSpec((DEC_SEQ, gw), lambda b, n: (b, n)),
        scratch_shapes=[pltpu.VMEM((WIN_PAD, HEAD_DIM), BF16), pltpu.VMEM((WIN_PAD, HEAD_DIM), BF16)],
        compiler_params=_params(
            _vmem_limit(blocks, 2 * WIN_PAD * HEAD_DIM * 2,
                        (DEC_SEQ // A_BLOCK) * 3 * A_GROUP * A_BLOCK * (PAST_LEN + 3 * A_BLOCK) * 4), 2),
        name=f"win_attn_{idx}",
    )(sink, proj, proj, proj, cache_k, cache_v, proj, cos, sin)


RPB_PAD = (16, 128)


def _na_row_offsets(g):
    offsets = []
    for rl in range(NA_QROWS):
        r = g * NA_QROWS + rl
        r0 = min(max(r - NA_ROWS // 2, 0), GRID_ROWS - NA_ROWS)
        row = []
        for kl in range(NA_KROWS):
            kr = g * (GRID_ROWS - NA_KROWS) + kl
            row.append(kr - r + NA_ROWS - 1 if r0 <= kr < r0 + NA_ROWS else None)
        offsets.append(row)
    return offsets


def _fill_na_bias(rpb_ref, bias_ref, g):
    shape = (GRID_W, 2 * GRID_W)
    c = lax.broadcasted_iota(jnp.int32, shape, 0)
    lane = lax.broadcasted_iota(jnp.int32, shape, 1)
    kc = lane & (GRID_W - 1)
    c0 = jnp.clip(c - NA_COLS // 2, 0, GRID_W - NA_COLS)
    col_ok = (kc >= c0) & (kc < c0 + NA_COLS)
    low = lane < GRID_W
    offsets = _na_row_offsets(g)
    used = sorted({d for row in offsets for d in row if d is not None})
    lo, hi = {}, {}
    for d in used:
        row = jnp.broadcast_to(rpb_ref[d:d + 1, :] * LOG2_E, shape)
        lo[d] = pltpu.roll(row, 2 * GRID_W - (NA_COLS - 1), 1, stride=1, stride_axis=0)
        hi[d] = pltpu.roll(row, GRID_W - (NA_COLS - 1), 1, stride=1, stride_axis=0)
    neg = jnp.full(shape, NEG_INF, F32)
    for rl in range(NA_QROWS):
        for p in range(NA_KROWS // 2):
            da, db = offsets[rl][2 * p], offsets[rl][2 * p + 1]
            a = neg if da is None else lo[da]
            b = neg if db is None else hi[db]
            piece = jnp.where(col_ok, jnp.where(low, a, b), NEG_INF)
            bias_ref[rl * GRID_W:(rl + 1) * GRID_W, 2 * p * GRID_W:2 * (p + 1) * GRID_W] = piece


def _na_attn_kernel(q_ref, k_ref, v_ref, ck_ref, cv_ref, gate_ref, rpb_ref, u_ref, bias_ref):
    g = pl.program_id(1)

    @pl.when(pl.program_id(2) == 0)
    def _():
        for group in range(DEC_SEQ // NA_Q):
            @pl.when(g == group)
            def _():
                _fill_na_bias(rpb_ref, bias_ref, group)

    k0 = pl.multiple_of(g * NA_KSHIFT, NA_KSHIFT)
    for i in range(NA_REQS):
        q = (q_ref[i] * Q_SCALE_LOG2).astype(BF16)
        s_c = _dot_nt(q, ck_ref[i].astype(BF16))
        s_n = _dot_nt(q, k_ref[i, pl.ds(k0, NA_K), :].astype(BF16)) + bias_ref[...]
        o = _softmax_pv([s_c, s_n], [cv_ref[i].astype(BF16), v_ref[i, pl.ds(k0, NA_K), :].astype(BF16)])
        u_ref[i] = (o * _silu(gate_ref[i])).astype(BF16)


def _na_attn(proj, rpb, cache_k, cache_v, idx):
    per_seq = DEC_SEQ // NA_Q
    proj = proj.reshape(DEC_BATCH, DEC_SEQ, EVEN_IN)
    blocks = NA_REQS * (2 * NA_Q * HEAD_DIM * 4 + 2 * DEC_SEQ * HEAD_DIM * 4 + 2 * PAST_LEN * HEAD_DIM * 4
                        + NA_Q * HEAD_DIM * 2) + RPB_PAD[0] * RPB_PAD[1] * 4
    tok_spec = lambda rows, col0: pl.BlockSpec(
        (NA_REQS, rows, HEAD_DIM), lambda h, g, b: (b, g if rows == NA_Q else 0, col0 // HEAD_DIM + h))
    cache_spec = pl.BlockSpec((NA_REQS, None, None, PAST_LEN, HEAD_DIM), lambda h, g, b: (b, idx, h, 0, 0))
    out = pl.pallas_call(
        _na_attn_kernel,
        out_shape=jax.ShapeDtypeStruct((DEC_BATCH, DEC_SEQ, B_W), BF16),
        grid=(B_HEADS, per_seq, DEC_BATCH // NA_REQS),
        in_specs=[
            tok_spec(NA_Q, COL_QB),
            tok_spec(DEC_SEQ, COL_KB),
            tok_spec(DEC_SEQ, COL_VB),
            cache_spec,
            cache_spec,
            tok_spec(NA_Q, COL_GATE + A_Q),
            pl.BlockSpec((None, None) + RPB_PAD, lambda h, g, b: (idx, h, 0, 0)),
        ],
        out_specs=pl.BlockSpec((NA_REQS, NA_Q, HEAD_DIM), lambda h, g, b: (b, g, h)),
        scratch_shapes=[pltpu.VMEM((NA_Q, NA_K), F32)],
        compiler_params=_params(
            _vmem_limit(blocks, NA_Q * NA_K * 4, NA_REQS * 3 * NA_Q * (NA_K + PAST_LEN) * 4), 3),
        name=f"na_attn_{idx}",
    )(proj, proj, proj, cache_k, cache_v, proj, rpb)
    return out.reshape(NS, B_W)


def _ret_kernel(*refs, seq, heads, has_state, emit_state, n_prev):
    dec_ref, q_ref, k_ref, v_ref, gate_ref, gn_ref = refs[:6]
    pos = 6
    if has_state:
        s0f_ref, s0b_ref = refs[pos:pos + 2]
        pos += 2
    pos += n_prev
    u_ref = refs[pos]
    pos += 1
    if emit_state:
        sf_ref, sb_ref = refs[pos:pos + 2]
        pos += 2
    stf_ref, stb_ref, o_ref = refs[pos:pos + 3]

    nc = seq // C_CHUNK
    assert nc % 2 == 0
    kscale = C_DK ** -0.5
    ii = lax.broadcasted_iota(jnp.int32, (C_CHUNK, C_CHUNK), 0).astype(F32)
    jj = lax.broadcasted_iota(jnp.int32, (C_CHUNK, C_CHUNK), 1).astype(F32)
    icol = lax.broadcasted_iota(jnp.int32, (C_CHUNK, 1), 0).astype(F32)

    def decay_tables(direction, head):
        forward = direction == 0
        dec = dec_ref[direction, head]
        log_g = -jnp.exp(jnp.full((C_CHUNK, C_CHUNK), dec, F32))
        log_g_col = -jnp.exp(jnp.full((C_CHUNK, 1), dec, F32))
        log_g_row = -jnp.exp(jnp.full((1, C_DV), dec, F32))
        diff = (ii - jj) if forward else (jj - ii)
        dmat = jnp.where(diff >= 0, jnp.exp(log_g * jnp.maximum(diff, 0.0)), 0.0)
        if forward:
            q_dec = jnp.exp(log_g_col * (icol + 1.0))
            k_dec = jnp.exp(log_g_col * (C_CHUNK - 1.0 - icol)) * kscale
        else:
            q_dec = jnp.exp(log_g_col * (C_CHUNK - icol))
            k_dec = jnp.exp(log_g_col * icol) * kscale
        return dmat, q_dec, k_dec, jnp.exp(log_g_row * float(C_CHUNK))

    def one_head(hh):
        cols = slice(hh * C_DK, (hh + 1) * C_DK)
        head = pl.program_id(1) * heads + hh
        dmat_f, q_dec_f, k_dec_f, chunk_dec_f = decay_tables(0, head)
        dmat_b, q_dec_b, k_dec_b, chunk_dec_b = decay_tables(1, head)
        dmat = (dmat_f + dmat_b) * kscale
        gn = gn_ref[:, cols]

        if has_state:
            stf_ref[hh] = s0f_ref[hh]
            stb_ref[hh] = s0b_ref[hh]
        else:
            stf_ref[hh] = jnp.zeros((C_DK, C_DV), F32)
            stb_ref[hh] = jnp.zeros((C_DK, C_DV), F32)

        def accumulate(rows, value, first):
            o_ref[rows, cols] = value if first else o_ref[rows, cols] + value

        def finalize(rows):
            o = o_ref[rows, cols]
            mu = jnp.mean(o, axis=-1, keepdims=True)
            d = o - mu
            var = jnp.mean(d * d, axis=-1, keepdims=True)
            y = (d * lax.rsqrt(var + EPS)) * gn
            u_ref[rows, cols] = (y * _silu(gate_ref[rows, cols])).astype(BF16)

        for t in range(nc):
            first = t < nc // 2
            rows = slice(t * C_CHUNK, (t + 1) * C_CHUNK)
            q = q_ref[rows, cols].astype(BF16)
            k32 = k_ref[rows, cols]
            v = v_ref[rows, cols].astype(BF16)
            s = _dot_nt(q, k32.astype(BF16)) * dmat
            state = stf_ref[hh]
            accumulate(rows, _dot(s.astype(BF16), v) + _dot(q, state.astype(BF16)) * q_dec_f, first)
            stf_ref[hh] = state * chunk_dec_f + _dot_tn((k32 * k_dec_f).astype(BF16), v)

            rows_b = slice((nc - 1 - t) * C_CHUNK, (nc - t) * C_CHUNK)
            qb = q_ref[rows_b, cols].astype(BF16)
            vb = v_ref[rows_b, cols].astype(BF16)
            state_b = stb_ref[hh]
            accumulate(rows_b, _dot(qb, state_b.astype(BF16)) * q_dec_b, first)
            stb_ref[hh] = state_b * chunk_dec_b + _dot_tn((k_ref[rows_b, cols] * k_dec_b).astype(BF16), vb)
            if not first:
                finalize(rows)
                finalize(rows_b)

        if emit_state:
            sf_ref[hh] = stf_ref[hh]
            sb_ref[hh] = stb_ref[hh]

    for hh in range(heads):
        one_head(hh)


RET_HEADS_PROMPT = 4
RET_HEADS_LATENT = 2


def _retention(proj, decays, gn, idx, *, latent, state_f=None, state_b=None, prev_states=()):
    seq = DEC_SEQ if latent else SEQ
    nb = DEC_BATCH if latent else BATCH
    heads = RET_HEADS_LATENT if latent else RET_HEADS_PROMPT
    width = heads * C_DK
    groups = C_HEADS // heads
    has_state = latent
    emit_state = not latent
    tok = lambda kind: pl.BlockSpec((seq, width), lambda b, h: (b, kind * groups + h))
    in_specs = [pl.BlockSpec(memory_space=pltpu.SMEM), tok(0), tok(1), tok(2), tok(3),
                pl.BlockSpec((None, 1, width), lambda b, h: (idx, 0, h))]
    args = [decays, proj, proj, proj, proj, gn]
    state_spec = pl.BlockSpec((None, None, heads, C_DK, C_DV), lambda b, h: (b, idx, h, 0, 0))
    if has_state:
        in_specs += [state_spec, state_spec]
        args += [state_f, state_b]
    aliases = {len(args) + k: 1 + k for k in range(len(prev_states))}
    in_specs += [_any_spec() for _ in prev_states]
    args += list(prev_states)
    out_shape = [jax.ShapeDtypeStruct((nb * seq, D_MODEL), BF16)]
    out_specs = [pl.BlockSpec((seq, width), lambda b, h: (b, h))]
    if emit_state:
        st = jax.ShapeDtypeStruct((nb, N_ODD, C_HEADS, C_DK, C_DV), F32)
        out_shape += [st, st]
        out_specs += [state_spec, state_spec]
    blocks = 4 * seq * width * 4 + width * 4 + 2 * heads * C_DK * C_DV * 4 + seq * width * 2
    scratch = 2 * heads * C_DK * C_DV * 4 + seq * width * 4
    return pl.pallas_call(
        functools.partial(_ret_kernel, seq=seq, heads=heads, has_state=has_state, emit_state=emit_state,
                          n_prev=len(prev_states)),
        out_shape=tuple(out_shape),
        grid=(nb, groups),
        in_specs=in_specs,
        out_specs=tuple(out_specs),
        scratch_shapes=[pltpu.VMEM((heads, C_DK, C_DV), F32), pltpu.VMEM((heads, C_DK, C_DV), F32),
                        pltpu.VMEM((seq, width), F32)],
        input_output_aliases=aliases,
        compiler_params=_params(_vmem_limit(blocks, scratch, 32 * C_CHUNK * C_DV * 4), 2),
        name=f"retention_{'latent' if latent else 'prompt'}_{idx}",
    )(*args)


def kernel(x_prompt, x_sample, c, cache_a_k, cache_a_v, cache_b_k, cache_b_v, state_ret_f, state_ret_b,
           c_ctx, w_ada, b_ada, norm_pre, norm_post, w_in_even, w_out_even, a_sink, na_rpb,
           w_in_odd, w_out_odd, ret_decay_f, ret_decay_b, ret_gn):
    xp = x_prompt.reshape(NP, D_MODEL)
    xs = x_sample.reshape(NS, D_MODEL)
    cvec = jnp.concatenate(
        [c_ctx[None, :], c, jnp.zeros((MOD_ROWS - 1 - DEC_BATCH, D_MODEL), F32)], axis=0)
    mods = _adaln(cvec, w_ada, b_ada).reshape(DEPTH, MOD_ROWS, 3, 1, D_MODEL)
    gain_pre = norm_pre.reshape(DEPTH, 1, D_MODEL)
    gain_post = norm_post.reshape(DEPTH, 1, D_MODEL)
    gn = ret_gn.reshape(N_ODD, 1, D_MODEL)
    cos, sin = _rope_tables()
    rpb = jnp.pad(na_rpb, ((0, 0), (0, 0), (0, RPB_PAD[0] - na_rpb.shape[2]),
                           (0, RPB_PAD[1] - na_rpb.shape[3])))
    caches = ()
    states = ()
    hp = _prenorm(xp, gain_pre, mods, 0, False)
    hs = _prenorm(xs, gain_pre, mods, 0, True)
    for layer in range(DEPTH):
        idx = layer // 2
        if layer % 2 == 0:
            proj_p = _inproj(hp, layer, w_in_even, False)
            proj_s = _inproj(hs, layer, w_in_even, True)
            u_p, *caches = _ctx_attn(proj_p, a_sink[idx], idx, caches)
            u_a = _win_attn(proj_s, a_sink[idx], cache_a_k, cache_a_v, idx, cos, sin)
            u_b = _na_attn(proj_s, rpb, cache_b_k, cache_b_v, idx)
            us_p, us_s, w_out = [u_p], [u_a, u_b], w_out_even
        else:
            proj_p = _inproj(hp, layer, w_in_odd, False)
            proj_s = _inproj(hs, layer, w_in_odd, True)
            decays = jnp.stack([ret_decay_f[idx], ret_decay_b[idx]], axis=0)
            u_p, *states = _retention(proj_p, decays, gn, idx, latent=False, prev_states=states)
            (u_s,) = _retention(proj_s, decays, gn, idx, latent=True,
                                state_f=state_ret_f, state_b=state_ret_b)
            us_p, us_s = [u_p], [u_s]
            w_out = w_out_odd
        xp, *hp = _outproj(us_p, w_out, xp, gain_post, gain_pre, mods, layer, False)
        xs, *hs = _outproj(us_s, w_out, xs, gain_post, gain_pre, mods, layer, True)
        hp, hs = (hp[0], hs[0]) if hp else (None, None)
    return (xp.reshape(BATCH, SEQ, D_MODEL), xs.reshape(DEC_BATCH, DEC_SEQ, D_MODEL), *caches, *states)
```

```python
import functools

import numpy as np
import jax
import jax.numpy as jnp
from jax import lax
from jax.experimental import pallas as pl
from jax.experimental.pallas import tpu as pltpu

D_MODEL = 2048
BATCH = 16
SEQ = 256
DEPTH = 4
DEC_BATCH = 4
DEC_SEQ = 1024
PAST_LEN = 256
GRID_W = 64
HEAD_DIM = 128
A_HEADS = 8
A_KV_HEADS = 2
A_GROUP = A_HEADS // A_KV_HEADS
A_WINDOW = 128
A_BLOCK = 128
B_HEADS = 8
NA_ROWS = 8
NA_COLS = 16
C_HEADS = 8
C_DK = D_MODEL // C_HEADS
C_DV = D_MODEL // C_HEADS
C_CHUNK = 128
ROPE_THETA = 10000.0
EPS = 1e-6

N_EVEN = (DEPTH + 1) // 2
N_ODD = DEPTH // 2
A_Q = A_HEADS * HEAD_DIM
A_KV = A_KV_HEADS * HEAD_DIM
B_W = B_HEADS * HEAD_DIM
EVEN_WIDTH = A_Q + B_W
EVEN_IN = A_Q + 2 * A_KV + 3 * B_W + EVEN_WIDTH
ODD_IN = 4 * D_MODEL
GRID_ROWS = DEC_SEQ // GRID_W

NP = BATCH * SEQ
NS = DEC_BATCH * DEC_SEQ
MOD_ROWS = 8

COL_QA = 0
COL_KA = A_Q
COL_VA = A_Q + A_KV
COL_QB = A_Q + 2 * A_KV
COL_KB = COL_QB + B_W
COL_VB = COL_KB + B_W
COL_GATE = COL_VB + B_W

NA_QROWS = 8
NA_KROWS = 12
NA_Q = NA_QROWS * GRID_W
NA_K = NA_KROWS * GRID_W
NA_KSHIFT = (GRID_ROWS - NA_KROWS) * GRID_W
NA_REQS = 4

LANES = 128
V7X_VMEM_BYTES = 64 * 1024 * 1024
VMEM_HEADROOM = 4 * 1024 * 1024

F32 = jnp.float32
BF16 = jnp.bfloat16
NEG_INF = float("-inf")


def _vmem_limit(block_bytes, scratch_bytes=0, temp_bytes=0):
    usable = V7X_VMEM_BYTES - 4 * 1024 * 1024
    assert 2 * block_bytes + scratch_bytes + temp_bytes + VMEM_HEADROOM <= usable
    return usable


def _params(vmem_bytes, ndims):
    return pltpu.CompilerParams(dimension_semantics=("arbitrary",) * ndims, vmem_limit_bytes=vmem_bytes)


def _silu(x):
    half = 0.5 * x
    return half + half * jnp.tanh(half)


def _dot(a, b):
    return jnp.dot(a, b, preferred_element_type=F32)


def _dot_nt(a, b):
    return lax.dot_general(a, b, (((1,), (1,)), ((), ())), preferred_element_type=F32)


def _dot_tn(a, b):
    return lax.dot_general(a, b, (((0,), (0,)), ((), ())), preferred_element_type=F32)


def _any_spec():
    return pl.BlockSpec(memory_space=pl.ANY)


ADA_TN = 1024


def _adaln_kernel(c_ref, w_ref, b_ref, o_ref):
    a = _silu(c_ref[...]).astype(BF16)
    o_ref[...] = _dot(a, w_ref[...].astype(BF16)) + b_ref[...]


def _adaln(cvec, w_ada, b_ada):
    n = 3 * D_MODEL
    blocks = MOD_ROWS * D_MODEL * 4 + D_MODEL * ADA_TN * 4 + ADA_TN * 4 + MOD_ROWS * ADA_TN * 4
    return pl.pallas_call(
        _adaln_kernel,
        out_shape=jax.ShapeDtypeStruct((DEPTH, MOD_ROWS, n), F32),
        grid=(DEPTH, n // ADA_TN),
        in_specs=[
            pl.BlockSpec((MOD_ROWS, D_MODEL), lambda l, j: (0, 0)),
            pl.BlockSpec((None, D_MODEL, ADA_TN), lambda l, j: (l, 0, j)),
            pl.BlockSpec((None, 1, ADA_TN), lambda l, j: (l, 0, j)),
        ],
        out_specs=pl.BlockSpec((None, MOD_ROWS, ADA_TN), lambda l, j: (l, 0, j)),
        compiler_params=_params(_vmem_limit(blocks, temp_bytes=D_MODEL * ADA_TN * 2), 2),
        name="adaln",
    )(cvec, w_ada, b_ada.reshape(DEPTH, 1, n))


def _mod_row(tile, tm, latent):
    return 1 + tile // (DEC_SEQ // tm) if latent else 0


NORM_TM = 512
NORM_ROWS = 32
IN_TN = 512


def _prenorm_kernel(x_ref, g_ref, sh_ref, sc_ref, h_ref):
    gain = g_ref[...]
    one_sc = 1.0 + sc_ref[...]
    sh = sh_ref[...]

    def body(r, carry):
        sl = pl.ds(pl.multiple_of(r * NORM_ROWS, NORM_ROWS), NORM_ROWS)
        x = x_ref[sl, :]
        ms = jnp.mean(x * x, axis=-1, keepdims=True)
        y = (x * lax.rsqrt(ms + EPS)) * gain
        h_ref[sl, :] = (y * one_sc + sh).astype(BF16)
        return carry

    lax.fori_loop(0, NORM_TM // NORM_ROWS, body, 0, unroll=True)


def _prenorm(x, gain, mods, layer, latent):
    ntok = x.shape[0]
    blocks = NORM_TM * D_MODEL * 4 + 3 * D_MODEL * 4 + NORM_TM * D_MODEL * 2
    mod_spec = lambda which: pl.BlockSpec(
        (None, None, None, 1, D_MODEL), lambda i: (layer, _mod_row(i, NORM_TM, latent), which, 0, 0))
    return pl.pallas_call(
        _prenorm_kernel,
        out_shape=jax.ShapeDtypeStruct((ntok, D_MODEL), BF16),
        grid=(ntok // NORM_TM,),
        in_specs=[
            pl.BlockSpec((NORM_TM, D_MODEL), lambda i: (i, 0)),
            pl.BlockSpec((None, 1, D_MODEL), lambda i: (layer, 0, 0)),
            mod_spec(0),
            mod_spec(1),
        ],
        out_specs=pl.BlockSpec((NORM_TM, D_MODEL), lambda i: (i, 0)),
        compiler_params=_params(_vmem_limit(blocks, temp_bytes=4 * NORM_ROWS * D_MODEL * 4), 1),
        name=f"prenorm_l{layer}_{'latent' if latent else 'prompt'}",
    )(x, gain, mods, mods)


def _inproj_kernel(h_ref, w_ref, o_ref):
    o_ref[...] = _dot(h_ref[...], w_ref[...].astype(BF16))


def _inproj(h, layer, w, latent):
    ntok = h.shape[0]
    n = w.shape[2]
    blocks = D_MODEL * IN_TN * 4 + ntok * IN_TN * 4
    return pl.pallas_call(
        _inproj_kernel,
        out_shape=jax.ShapeDtypeStruct((ntok, n), F32),
        grid=(n // IN_TN,),
        in_specs=[
            pl.BlockSpec((ntok, D_MODEL), lambda j: (0, 0), pipeline_mode=pl.Buffered(1)),
            pl.BlockSpec((None, D_MODEL, IN_TN), lambda j: (layer // 2, 0, j)),
        ],
        out_specs=pl.BlockSpec((ntok, IN_TN), lambda j: (0, j)),
        compiler_params=_params(_vmem_limit(blocks, ntok * D_MODEL * 2, D_MODEL * IN_TN * 2), 1),
        name=f"inproj_l{layer}_{'latent' if latent else 'prompt'}",
    )(h, w)


OUT_TM = 512
OUT_SUB = 256


def _outproj_kernel(*refs, n_pieces, emit_next):
    u_refs = refs[:n_pieces]
    w_refs = refs[n_pieces:2 * n_pieces]
    pos = 2 * n_pieces
    x_ref, gain_ref, gate_ref = refs[pos:pos + 3]
    pos += 3
    if emit_next:
        ngain_ref, nshift_ref, nscale_ref = refs[pos:pos + 3]
        pos += 3
    o_ref = refs[pos]
    pos += 1
    if emit_next:
        h_ref = refs[pos]
        pos += 1
    wbf_ref = refs[pos]

    @pl.when(pl.program_id(0) == 0)
    def _():
        for p, w_ref in enumerate(w_refs):
            wbf_ref[p] = w_ref[...].astype(BF16)

    gated_gain = gate_ref[...] * gain_ref[...]
    if emit_next:
        next_gain = ngain_ref[...] * (1.0 + nscale_ref[...])
        next_shift = nshift_ref[...]
    for r in range(OUT_TM // OUT_SUB):
        rows = slice(r * OUT_SUB, (r + 1) * OUT_SUB)
        out = _dot(u_refs[0][rows, :], wbf_ref[0])
        for p in range(1, n_pieces):
            out = out + _dot(u_refs[p][rows, :], wbf_ref[p])
        ms = jnp.mean(out * out, axis=-1, keepdims=True)
        x_new = x_ref[rows, :] + (out * lax.rsqrt(ms + EPS)) * gated_gain
        o_ref[rows, :] = x_new
        if emit_next:
            ms = jnp.mean(x_new * x_new, axis=-1, keepdims=True)
            h_ref[rows, :] = ((x_new * lax.rsqrt(ms + EPS)) * next_gain + next_shift).astype(BF16)


def _outproj(us, w, x, gain_post, gain_pre, mods, layer, latent):
    ntok = x.shape[0]
    widths = [u.shape[1] for u in us]
    kp = widths[0]
    k = w.shape[1]
    assert all(kw == kp for kw in widths) and kp * len(us) == k
    emit_next = layer + 1 < DEPTH
    blocks = OUT_TM * k * 2 + 2 * OUT_TM * D_MODEL * 4 + 5 * D_MODEL * 4 + emit_next * OUT_TM * D_MODEL * 2
    tile_spec = pl.BlockSpec((OUT_TM, D_MODEL), lambda i: (i, 0))
    gain_spec = lambda l: pl.BlockSpec((None, 1, D_MODEL), lambda i: (l, 0, 0))
    mod_spec = lambda l, which: pl.BlockSpec(
        (None, None, None, 1, D_MODEL), lambda i: (l, _mod_row(i, OUT_TM, latent), which, 0, 0))
    u_specs = [pl.BlockSpec((OUT_TM, kp), lambda i: (i, 0)) for _ in us]
    w_specs = [pl.BlockSpec((None, kp, D_MODEL), lambda i, p=p: (layer // 2, p, 0),
                            pipeline_mode=pl.Buffered(1)) for p in range(len(us))]
    in_specs = u_specs + w_specs + [tile_spec, gain_spec(layer), mod_spec(layer, 2)]
    args = [*us, *([w] * len(us)), x, gain_post, mods]
    out_shape = [jax.ShapeDtypeStruct((ntok, D_MODEL), F32)]
    out_specs = [tile_spec]
    if emit_next:
        in_specs += [gain_spec(layer + 1), mod_spec(layer + 1, 0), mod_spec(layer + 1, 1)]
        args += [gain_pre, mods, mods]
        out_shape.append(jax.ShapeDtypeStruct((ntok, D_MODEL), BF16))
        out_specs.append(tile_spec)
    return pl.pallas_call(
        functools.partial(_outproj_kernel, n_pieces=len(us), emit_next=emit_next),
        out_shape=tuple(out_shape),
        grid=(ntok // OUT_TM,),
        in_specs=in_specs,
        out_specs=tuple(out_specs),
        scratch_shapes=[pltpu.VMEM((len(us), kp, D_MODEL), BF16)],
        compiler_params=_params(
            _vmem_limit(blocks, k * D_MODEL * (4 + 2), 3 * OUT_TM * D_MODEL * 2), 1),
        name=f"outproj_l{layer}_{'latent' if latent else 'prompt'}",
    )(*args)


LOG2_E = 1.4426950408889634
Q_SCALE_LOG2 = HEAD_DIM ** -0.5 * LOG2_E


def _softmax_pv(scores, values, sink=None):
    def fold(blocks, op):
        tiles = [b[:, t:t + LANES] for b in blocks for t in range(0, b.shape[-1], LANES)]
        out = tiles[0]
        for t in tiles[1:]:
            out = op(out, t)
        return out

    m = jnp.max(fold(scores, jnp.maximum), axis=-1, keepdims=True)
    if sink is not None:
        m = jnp.maximum(m, sink)
    probs = [jnp.exp2(s - m) for s in scores]
    den = jnp.sum(fold(probs, jnp.add), axis=-1, keepdims=True)
    if sink is not None:
        den = den + jnp.exp2(sink - m)
    acc = _dot(probs[0].astype(BF16), values[0])
    for p, v in zip(probs[1:], values[1:]):
        acc = acc + _dot(p.astype(BF16), v)
    return acc / den


def _ctx_attn_kernel(sink_ref, p_ref, *refs):
    u_ref, ak_ref, av_ref, bk_ref, bv_ref = refs[-5:]
    def col(c):
        return p_ref[:, c:c + HEAD_DIM]

    def head(cq, k, v, cg, sink, cu):
        q = (col(cq) * Q_SCALE_LOG2).astype(BF16)
        o = _softmax_pv([_dot_nt(q, k)], [v], None if sink is None else sink * LOG2_E)
        u_ref[:, cu:cu + HEAD_DIM] = (o * _silu(col(cg))).astype(BF16)

    for n in range(A_KV_HEADS):
        k32 = col(COL_KA + n * HEAD_DIM)
        v32 = col(COL_VA + n * HEAD_DIM)
        ak_ref[n] = k32
        av_ref[n] = v32
        k = k32.astype(BF16)
        v = v32.astype(BF16)
        for g in range(A_GROUP):
            h = n * A_GROUP + g
            head(COL_QA + h * HEAD_DIM, k, v, COL_GATE + h * HEAD_DIM, sink_ref[h], h * HEAD_DIM)
    for h in range(B_HEADS):
        k32 = col(COL_KB + h * HEAD_DIM)
        v32 = col(COL_VB + h * HEAD_DIM)
        bk_ref[h] = k32
        bv_ref[h] = v32
        head(COL_QB + h * HEAD_DIM, k32.astype(BF16), v32.astype(BF16),
             COL_GATE + A_Q + h * HEAD_DIM, None, A_Q + h * HEAD_DIM)


def _ctx_attn(proj, sink, idx, prev_caches):
    blocks = (SEQ * EVEN_IN * 4 + SEQ * EVEN_WIDTH * 2
              + 2 * (A_KV_HEADS + B_HEADS) * SEQ * HEAD_DIM * 4)
    heads = (A_KV_HEADS, A_KV_HEADS, B_HEADS, B_HEADS)
    cache = lambda nh: jax.ShapeDtypeStruct((BATCH, N_EVEN, nh, SEQ, HEAD_DIM), F32)
    cache_spec = lambda nh: pl.BlockSpec((None, None, nh, SEQ, HEAD_DIM), lambda b: (b, idx, 0, 0, 0))
    n_prev = len(prev_caches)
    return pl.pallas_call(
        _ctx_attn_kernel,
        out_shape=(jax.ShapeDtypeStruct((NP, EVEN_WIDTH), BF16),) + tuple(cache(nh) for nh in heads),
        grid=(BATCH,),
        in_specs=[
            pl.BlockSpec(memory_space=pltpu.SMEM),
            pl.BlockSpec((SEQ, EVEN_IN), lambda b: (b, 0)),
        ] + [_any_spec() for _ in prev_caches],
        out_specs=(pl.BlockSpec((SEQ, EVEN_WIDTH), lambda b: (b, 0)),) + tuple(cache_spec(nh) for nh in heads),
        input_output_aliases={2 + k: 1 + k for k in range(n_prev)},
        compiler_params=_params(_vmem_limit(blocks, temp_bytes=8 * SEQ * SEQ * 4), 1),
        name=f"ctx_attn_{idx}",
    )(sink, proj, *prev_caches)


WIN_PAD = DEC_SEQ + 2 * A_BLOCK


def _rope_tables():
    t = jnp.arange(DEC_SEQ)
    half = HEAD_DIM // 2
    nf = half // 2
    inv = ROPE_THETA ** (-jnp.arange(nf, dtype=F32) / nf)
    ang_r = (t // GRID_W).astype(F32)[:, None] * inv[None]
    ang_c = (t % GRID_W).astype(F32)[:, None] * inv[None]
    cos = jnp.concatenate([jnp.cos(ang_r)] * 2 + [jnp.cos(ang_c)] * 2, axis=-1)
    sin = jnp.concatenate([-jnp.sin(ang_r), jnp.sin(ang_r), -jnp.sin(ang_c), jnp.sin(ang_c)], axis=-1)
    return cos, sin


def _rope(x, cos, sin):
    quarter = HEAD_DIM // 4
    lane = lax.broadcasted_iota(jnp.int32, x.shape, 1)
    first = (lane & (2 * quarter - 1)) < quarter
    partner = jnp.where(first, pltpu.roll(x, HEAD_DIM - quarter, 1), pltpu.roll(x, quarter, 1))
    return x * cos + partner * sin


def _win_attn_kernel(sink_ref, q_ref, k_ref, v_ref, ck_ref, cv_ref, gate_ref, cos_ref, sin_ref,
                     u_ref, kpad_ref, vpad_ref):
    n = pl.program_id(1)
    nqb = DEC_SEQ // A_BLOCK
    rows = A_GROUP * A_BLOCK

    zeros = jnp.zeros((A_BLOCK, HEAD_DIM), BF16)
    for ref in (kpad_ref, vpad_ref):
        ref[0:A_BLOCK, :] = zeros
        ref[A_BLOCK + DEC_SEQ:WIN_PAD, :] = zeros
    kpad_ref[A_BLOCK:A_BLOCK + DEC_SEQ, :] = _rope(k_ref[...], cos_ref[...], sin_ref[...]).astype(BF16)
    vpad_ref[A_BLOCK:A_BLOCK + DEC_SEQ, :] = v_ref[...].astype(BF16)
    ck = ck_ref[...].astype(BF16)
    cv = cv_ref[...].astype(BF16)

    row = lax.broadcasted_iota(jnp.int32, (rows, 1), 0)
    qi = row & (A_BLOCK - 1)
    kk = lax.broadcasted_iota(jnp.int32, (rows, 3 * A_BLOCK), 1)
    head = lax.shift_right_logical(row, A_BLOCK.bit_length() - 1)
    sink = jnp.zeros((rows, 1), F32)
    for g in range(A_GROUP):
        sink = jnp.where(head == g, sink_ref[n * A_GROUP + g] * LOG2_E, sink)

    for j in range(nqb):
        blk = slice(j * A_BLOCK, (j + 1) * A_BLOCK)
        q = jnp.concatenate(
            [(_rope(q_ref[blk, g * HEAD_DIM:(g + 1) * HEAD_DIM], cos_ref[blk, :], sin_ref[blk, :])
              * Q_SCALE_LOG2).astype(BF16) for g in range(A_GROUP)],
            axis=0)
        band = slice(j * A_BLOCK, (j + 3) * A_BLOCK)
        s_c = _dot_nt(q, ck)
        s_w = _dot_nt(q, kpad_ref[band, :])
        lower = jnp.maximum(qi, A_BLOCK if j == 0 else 0)
        upper = jnp.minimum(qi + 2 * A_WINDOW, (2 if j == nqb - 1 else 3) * A_BLOCK - 1)
        s_w = jnp.where((kk >= lower) & (kk <= upper), s_w, NEG_INF)
        o = _softmax_pv([s_c, s_w], [cv, vpad_ref[band, :]], sink)
        for g in range(A_GROUP):
            cols = slice(g * HEAD_DIM, (g + 1) * HEAD_DIM)
            u_ref[blk, cols] = (o[g * A_BLOCK:(g + 1) * A_BLOCK] * _silu(gate_ref[blk, cols])).astype(BF16)


def _win_attn(proj, sink, cache_k, cache_v, idx, cos, sin):
    gw = A_GROUP * HEAD_DIM
    blocks = (2 * DEC_SEQ * gw * 4 + 2 * DEC_SEQ * HEAD_DIM * 4 + 2 * PAST_LEN * HEAD_DIM * 4
              + 2 * DEC_SEQ * HEAD_DIM * 4 + DEC_SEQ * gw * 2)
    cache_spec = pl.BlockSpec((None, None, None, PAST_LEN, HEAD_DIM), lambda b, n: (b, idx, n, 0, 0))
    table_spec = pl.BlockSpec((DEC_SEQ, HEAD_DIM), lambda b, n: (0, 0))
    return pl.pallas_call(
        _win_attn_kernel,
        out_shape=jax.ShapeDtypeStruct((NS, A_Q), BF16),
        grid=(DEC_BATCH, A_KV_HEADS),
        in_specs=[
            pl.BlockSpec(memory_space=pltpu.SMEM),
            pl.BlockSpec((DEC_SEQ, gw), lambda b, n: (b, COL_QA // gw + n)),
            pl.BlockSpec((DEC_SEQ, HEAD_DIM), lambda b, n: (b, COL_KA // HEAD_DIM + n)),
            pl.BlockSpec((DEC_SEQ, HEAD_DIM), lambda b, n: (b, COL_VA // HEAD_DIM + n)),
            cache_spec,
            cache_spec,
            pl.BlockSpec((DEC_SEQ, gw), lambda b, n: (b, COL_GATE // gw + n)),
            table_spec,
            table_spec,
        ],
        out_specs=pl.BlockSpec((DEC_SEQ, gw), lambda b, n: (b, n)),
        scratch_shapes=[pltpu.VMEM((WIN_PAD, HEAD_DIM), BF16), pltpu.VMEM((WIN_PAD, HEAD_DIM), BF16)],
        compiler_params=_params(
            _vmem_limit(blocks, 2 * WIN_PAD * HEAD_DIM * 2,
                        (DEC_SEQ // A_BLOCK) * 3 * A_GROUP * A_BLOCK * (PAST_LEN + 3 * A_BLOCK) * 4), 2),
        name=f"win_attn_{idx}",
    )(sink, proj, proj, proj, cache_k, cache_v, proj, cos, sin)


RPB_PAD = (16, 128)


def _na_row_offsets(g):
    offsets = []
    for rl in range(NA_QROWS):
        r = g * NA_QROWS + rl
        r0 = min(max(r - NA_ROWS // 2, 0), GRID_ROWS - NA_ROWS)
        row = []
        for kl in range(NA_KROWS):
            kr = g * (GRID_ROWS - NA_KROWS) + kl
            row.append(kr - r + NA_ROWS - 1 if r0 <= kr < r0 + NA_ROWS else None)
        offsets.append(row)
    return offsets


def _fill_na_bias(rpb_ref, bias_ref, g):
    shape = (GRID_W, 2 * GRID_W)
    c = lax.broadcasted_iota(jnp.int32, shape, 0)
    lane = lax.broadcasted_iota(jnp.int32, shape, 1)
    kc = lane & (GRID_W - 1)
    c0 = jnp.clip(c - NA_COLS // 2, 0, GRID_W - NA_COLS)
    col_ok = (kc >= c0) & (kc < c0 + NA_COLS)
    low = lane < GRID_W
    offsets = _na_row_offsets(g)
    used = sorted({d for row in offsets for d in row if d is not None})
    lo, hi = {}, {}
    for d in used:
        row = jnp.broadcast_to(rpb_ref[d:d + 1, :] * LOG2_E, shape)
        lo[d] = pltpu.roll(row, 2 * GRID_W - (NA_COLS - 1), 1, stride=1, stride_axis=0)
        hi[d] = pltpu.roll(row, GRID_W - (NA_COLS - 1), 1, stride=1, stride_axis=0)
    neg = jnp.full(shape, NEG_INF, F32)
    for rl in range(NA_QROWS):
        for p in range(NA_KROWS // 2):
            da, db = offsets[rl][2 * p], offsets[rl][2 * p + 1]
            a = neg if da is None else lo[da]
            b = neg if db is None else hi[db]
            piece = jnp.where(col_ok, jnp.where(low, a, b), NEG_INF)
            bias_ref[rl * GRID_W:(rl + 1) * GRID_W, 2 * p * GRID_W:2 * (p + 1) * GRID_W] = piece


def _na_attn_kernel(q_ref, k_ref, v_ref, ck_ref, cv_ref, gate_ref, rpb_ref, u_ref, bias_ref):
    g = pl.program_id(1)

    @pl.when(pl.program_id(2) == 0)
    def _():
        for group in range(DEC_SEQ // NA_Q):
            @pl.when(g == group)
            def _():
                _fill_na_bias(rpb_ref, bias_ref, group)

    k0 = pl.multiple_of(g * NA_KSHIFT, NA_KSHIFT)
    for i in range(NA_REQS):
        q = (q_ref[i] * Q_SCALE_LOG2).astype(BF16)
        s_c = _dot_nt(q, ck_ref[i].astype(BF16))
        s_n = _dot_nt(q, k_ref[i, pl.ds(k0, NA_K), :].astype(BF16)) + bias_ref[...]
        o = _softmax_pv([s_c, s_n], [cv_ref[i].astype(BF16), v_ref[i, pl.ds(k0, NA_K), :].astype(BF16)])
        u_ref[i] = (o * _silu(gate_ref[i])).astype(BF16)


def _na_attn(proj, rpb, cache_k, cache_v, idx):
    per_seq = DEC_SEQ // NA_Q
    proj = proj.reshape(DEC_BATCH, DEC_SEQ, EVEN_IN)
    blocks = NA_REQS * (2 * NA_Q * HEAD_DIM * 4 + 2 * DEC_SEQ * HEAD_DIM * 4 + 2 * PAST_LEN * HEAD_DIM * 4
                        + NA_Q * HEAD_DIM * 2) + RPB_PAD[0] * RPB_PAD[1] * 4
    tok_spec = lambda rows, col0: pl.BlockSpec(
        (NA_REQS, rows, HEAD_DIM), lambda h, g, b: (b, g if rows == NA_Q else 0, col0 // HEAD_DIM + h))
    cache_spec = pl.BlockSpec((NA_REQS, None, None, PAST_LEN, HEAD_DIM), lambda h, g, b: (b, idx, h, 0, 0))
    out = pl.pallas_call(
        _na_attn_kernel,
        out_shape=jax.ShapeDtypeStruct((DEC_BATCH, DEC_SEQ, B_W), BF16),
        grid=(B_HEADS, per_seq, DEC_BATCH // NA_REQS),
        in_specs=[
            tok_spec(NA_Q, COL_QB),
            tok_spec(DEC_SEQ, COL_KB),
            tok_spec(DEC_SEQ, COL_VB),
            cache_spec,
            cache_spec,
            tok_spec(NA_Q, COL_GATE + A_Q),
            pl.BlockSpec((None, None) + RPB_PAD, lambda h, g, b: (idx, h, 0, 0)),
        ],
        out_specs=pl.BlockSpec((NA_REQS, NA_Q, HEAD_DIM), lambda h, g, b: (b, g, h)),
        scratch_shapes=[pltpu.VMEM((NA_Q, NA_K), F32)],
        compiler_params=_params(
            _vmem_limit(blocks, NA_Q * NA_K * 4, NA_REQS * 3 * NA_Q * (NA_K + PAST_LEN) * 4), 3),
        name=f"na_attn_{idx}",
    )(proj, proj, proj, cache_k, cache_v, proj, rpb)
    return out.reshape(NS, B_W)


def _ret_kernel(*refs, seq, heads, has_state, emit_state, n_prev):
    dec_ref, q_ref, k_ref, v_ref, gate_ref, gn_ref = refs[:6]
    pos = 6
    if has_state:
        s0f_ref, s0b_ref = refs[pos:pos + 2]
        pos += 2
    pos += n_prev
    u_ref = refs[pos]
    pos += 1
    if emit_state:
        sf_ref, sb_ref = refs[pos:pos + 2]
        pos += 2
    stf_ref, stb_ref, o_ref = refs[pos:pos + 3]

    nc = seq // C_CHUNK
    assert nc % 2 == 0
    kscale = C_DK ** -0.5
    ii = lax.broadcasted_iota(jnp.int32, (C_CHUNK, C_CHUNK), 0).astype(F32)
    jj = lax.broadcasted_iota(jnp.int32, (C_CHUNK, C_CHUNK), 1).astype(F32)
    icol = lax.broadcasted_iota(jnp.int32, (C_CHUNK, 1), 0).astype(F32)

    def decay_tables(direction, head):
        forward = direction == 0
        dec = dec_ref[direction, head]
        log_g = -jnp.exp(jnp.full((C_CHUNK, C_CHUNK), dec, F32))
        log_g_col = -jnp.exp(jnp.full((C_CHUNK, 1), dec, F32))
        log_g_row = -jnp.exp(jnp.full((1, C_DV), dec, F32))
        diff = (ii - jj) if forward else (jj - ii)
        dmat = jnp.where(diff >= 0, jnp.exp(log_g * jnp.maximum(diff, 0.0)), 0.0)
        if forward:
            q_dec = jnp.exp(log_g_col * (icol + 1.0))
            k_dec = jnp.exp(log_g_col * (C_CHUNK - 1.0 - icol)) * kscale
        else:
            q_dec = jnp.exp(log_g_col * (C_CHUNK - icol))
            k_dec = jnp.exp(log_g_col * icol) * kscale
        return dmat, q_dec, k_dec, jnp.exp(log_g_row * float(C_CHUNK))

    def one_head(hh):
        cols = slice(hh * C_DK, (hh + 1) * C_DK)
        head = pl.program_id(1) * heads + hh
        dmat_f, q_dec_f, k_dec_f, chunk_dec_f = decay_tables(0, head)
        dmat_b, q_dec_b, k_dec_b, chunk_dec_b = decay_tables(1, head)
        dmat = (dmat_f + dmat_b) * kscale
        gn = gn_ref[:, cols]

        if has_state:
            stf_ref[hh] = s0f_ref[hh]
            stb_ref[hh] = s0b_ref[hh]
        else:
            stf_ref[hh] = jnp.zeros((C_DK, C_DV), F32)
            stb_ref[hh] = jnp.zeros((C_DK, C_DV), F32)

        def accumulate(rows, value, first):
            o_ref[rows, cols] = value if first else o_ref[rows, cols] + value

        def finalize(rows):
            o = o_ref[rows, cols]
            mu = jnp.mean(o, axis=-1, keepdims=True)
            d = o - mu
            var = jnp.mean(d * d, axis=-1, keepdims=True)
            y = (d * lax.rsqrt(var + EPS)) * gn
            u_ref[rows, cols] = (y * _silu(gate_ref[rows, cols])).astype(BF16)

        for t in range(nc):
            first = t < nc // 2
            rows = slice(t * C_CHUNK, (t + 1) * C_CHUNK)
            q = q_ref[rows, cols].astype(BF16)
            k32 = k_ref[rows, cols]
            v = v_ref[rows, cols].astype(BF16)
            s = _dot_nt(q, k32.astype(BF16)) * dmat
            state = stf_ref[hh]
            accumulate(rows, _dot(s.astype(BF16), v) + _dot(q, state.astype(BF16)) * q_dec_f, first)
            stf_ref[hh] = state * chunk_dec_f + _dot_tn((k32 * k_dec_f).astype(BF16), v)

            rows_b = slice((nc - 1 - t) * C_CHUNK, (nc - t) * C_CHUNK)
            qb = q_ref[rows_b, cols].astype(BF16)
            vb = v_ref[rows_b, cols].astype(BF16)
            state_b = stb_ref[hh]
            accumulate(rows_b, _dot(qb, state_b.astype(BF16)) * q_dec_b, first)
            stb_ref[hh] = state_b * chunk_dec_b + _dot_tn((k_ref[rows_b, cols] * k_dec_b).astype(BF16), vb)
            if not first:
                finalize(rows)
                finalize(rows_b)

        if emit_state:
            sf_ref[hh] = stf_ref[hh]
            sb_ref[hh] = stb_ref[hh]

    for hh in range(heads):
        one_head(hh)


RET_HEADS_PROMPT = 8
RET_HEADS_LATENT = 4


def _retention(proj, decays, gn, idx, *, latent, state_f=None, state_b=None, prev_states=()):
    seq = DEC_SEQ if latent else SEQ
    nb = DEC_BATCH if latent else BATCH
    heads = RET_HEADS_LATENT if latent else RET_HEADS_PROMPT
    width = heads * C_DK
    groups = C_HEADS // heads
    has_state = latent
    emit_state = not latent
    tok = lambda kind: pl.BlockSpec((seq, width), lambda b, h: (b, kind * groups + h))
    in_specs = [pl.BlockSpec(memory_space=pltpu.SMEM), tok(0), tok(1), tok(2), tok(3),
                pl.BlockSpec((None, 1, width), lambda b, h: (idx, 0, h))]
    args = [decays, proj, proj, proj, proj, gn]
    state_spec = pl.BlockSpec((None, None, heads, C_DK, C_DV), lambda b, h: (b, idx, h, 0, 0))
    if has_state:
        in_specs += [state_spec, state_spec]
        args += [state_f, state_b]
    aliases = {len(args) + k: 1 + k for k in range(len(prev_states))}
    in_specs += [_any_spec() for _ in prev_states]
    args += list(prev_states)
    out_shape = [jax.ShapeDtypeStruct((nb * seq, D_MODEL), BF16)]
    out_specs = [pl.BlockSpec((seq, width), lambda b, h: (b, h))]
    if emit_state:
        st = jax.ShapeDtypeStruct((nb, N_ODD, C_HEADS, C_DK, C_DV), F32)
        out_shape += [st, st]
        out_specs += [state_spec, state_spec]
    blocks = 4 * seq * width * 4 + width * 4 + 2 * heads * C_DK * C_DV * 4 + seq * width * 2
    scratch = 2 * heads * C_DK * C_DV * 4 + seq * width * 4
    return pl.pallas_call(
        functools.partial(_ret_kernel, seq=seq, heads=heads, has_state=has_state, emit_state=emit_state,
                          n_prev=len(prev_states)),
        out_shape=tuple(out_shape),
        grid=(nb, groups),
        in_specs=in_specs,
        out_specs=tuple(out_specs),
        scratch_shapes=[pltpu.VMEM((heads, C_DK, C_DV), F32), pltpu.VMEM((heads, C_DK, C_DV), F32),
                        pltpu.VMEM((seq, width), F32)],
        input_output_aliases=aliases,
        compiler_params=_params(_vmem_limit(blocks, scratch, 32 * C_CHUNK * C_DV * 4), 2),
        name=f"retention_{'latent' if latent else 'prompt'}_{idx}",
    )(*args)


def kernel(x_prompt, x_sample, c, cache_a_k, cache_a_v, cache_b_k, cache_b_v, state_ret_f, state_ret_b,
           c_ctx, w_ada, b_ada, norm_pre, norm_post, w_in_even, w_out_even, a_sink, na_rpb,
           w_in_odd, w_out_odd, ret_decay_f, ret_decay_b, ret_gn):
    xp = x_prompt.reshape(NP, D_MODEL)
    xs = x_sample.reshape(NS, D_MODEL)
    cvec = jnp.concatenate(
        [c_ctx[None, :], c, jnp.zeros((MOD_ROWS - 1 - DEC_BATCH, D_MODEL), F32)], axis=0)
    mods = _adaln(cvec, w_ada, b_ada).reshape(DEPTH, MOD_ROWS, 3, 1, D_MODEL)
    gain_pre = norm_pre.reshape(DEPTH, 1, D_MODEL)
    gain_post = norm_post.reshape(DEPTH, 1, D_MODEL)
    gn = ret_gn.reshape(N_ODD, 1, D_MODEL)
    cos, sin = _rope_tables()
    rpb = jnp.pad(na_rpb, ((0, 0), (0, 0), (0, RPB_PAD[0] - na_rpb.shape[2]),
                           (0, RPB_PAD[1] - na_rpb.shape[3])))
    caches = ()
    states = ()
    hp = _prenorm(xp, gain_pre, mods, 0, False)
    hs = _prenorm(xs, gain_pre, mods, 0, True)
    for layer in range(DEPTH):
        idx = layer // 2
        if layer % 2 == 0:
            proj_p = _inproj(hp, layer, w_in_even, False)
            proj_s = _inproj(hs, layer, w_in_even, True)
            u_p, *caches = _ctx_attn(proj_p, a_sink[idx], idx, caches)
            u_a = _win_attn(proj_s, a_sink[idx], cache_a_k, cache_a_v, idx, cos, sin)
            u_b = _na_attn(proj_s, rpb, cache_b_k, cache_b_v, idx)
            us_p, us_s, w_out = [u_p], [u_a, u_b], w_out_even
        else:
            proj_p = _inproj(hp, layer, w_in_odd, False)
            proj_s = _inproj(hs, layer, w_in_odd, True)
            decays = jnp.stack([ret_decay_f[idx], ret_decay_b[idx]], axis=0)
            u_p, *states = _retention(proj_p, decays, gn, idx, latent=False, prev_states=states)
            (u_s,) = _retention(proj_s, decays, gn, idx, latent=True,
                                state_f=state_ret_f, state_b=state_ret_b)
            us_p, us_s = [u_p], [u_s]
            w_out = w_out_odd
        xp, *hp = _outproj(us_p, w_out, xp, gain_post, gain_pre, mods, layer, False)
        xs, *hs = _outproj(us_s, w_out, xs, gain_post, gain_pre, mods, layer, True)
        hp, hs = (hp[0], hs[0]) if hp else (None, None)
    return (xp.reshape(BATCH, SEQ, D_MODEL), xs.reshape(DEC_BATCH, DEC_SEQ, D_MODEL), *caches, *states)
```

```python
import functools

import numpy as np
import jax
import jax.numpy as jnp
from jax import lax
from jax.experimental import pallas as pl
from jax.experimental.pallas import tpu as pltpu

D_MODEL = 2048
BATCH = 16
SEQ = 256
DEPTH = 4
DEC_BATCH = 4
DEC_SEQ = 1024
PAST_LEN = 256
GRID_W = 64
HEAD_DIM = 128
A_HEADS = 8
A_KV_HEADS = 2
A_GROUP = A_HEADS // A_KV_HEADS
A_WINDOW = 128
A_BLOCK = 128
B_HEADS = 8
NA_ROWS = 8
NA_COLS = 16
C_HEADS = 8
C_DK = D_MODEL // C_HEADS
C_DV = D_MODEL // C_HEADS
C_CHUNK = 128
ROPE_THETA = 10000.0
EPS = 1e-6

N_EVEN = (DEPTH + 1) // 2
N_ODD = DEPTH // 2
A_Q = A_HEADS * HEAD_DIM
A_KV = A_KV_HEADS * HEAD_DIM
B_W = B_HEADS * HEAD_DIM
EVEN_WIDTH = A_Q + B_W
EVEN_IN = A_Q + 2 * A_KV + 3 * B_W + EVEN_WIDTH
ODD_IN = 4 * D_MODEL
GRID_ROWS = DEC_SEQ // GRID_W

NP = BATCH * SEQ
NS = DEC_BATCH * DEC_SEQ
MOD_ROWS = 8

COL_QA = 0
COL_KA = A_Q
COL_VA = A_Q + A_KV
COL_QB = A_Q + 2 * A_KV
COL_KB = COL_QB + B_W
COL_VB = COL_KB + B_W
COL_GATE = COL_VB + B_W

NA_QROWS = 8
NA_KROWS = 12
NA_Q = NA_QROWS * GRID_W
NA_K = NA_KROWS * GRID_W
NA_KSHIFT = (GRID_ROWS - NA_KROWS) * GRID_W
NA_REQS = 4

LANES = 128
V7X_VMEM_BYTES = 64 * 1024 * 1024
VMEM_HEADROOM = 4 * 1024 * 1024

F32 = jnp.float32
BF16 = jnp.bfloat16
NEG_INF = float("-inf")


def _vmem_limit(block_bytes, scratch_bytes=0, temp_bytes=0):
    usable = V7X_VMEM_BYTES - 4 * 1024 * 1024
    assert 2 * block_bytes + scratch_bytes + temp_bytes + VMEM_HEADROOM <= usable
    return usable


def _params(vmem_bytes, ndims):
    return pltpu.CompilerParams(dimension_semantics=("arbitrary",) * ndims, vmem_limit_bytes=vmem_bytes)


def _silu(x):
    half = 0.5 * x
    return half + half * jnp.tanh(half)


def _dot(a, b):
    return jnp.dot(a, b, preferred_element_type=F32)


def _dot_nt(a, b):
    return lax.dot_general(a, b, (((1,), (1,)), ((), ())), preferred_element_type=F32)


def _dot_tn(a, b):
    return lax.dot_general(a, b, (((0,), (0,)), ((), ())), preferred_element_type=F32)


def _any_spec():
    return pl.BlockSpec(memory_space=pl.ANY)


ADA_TN = 1024


def _adaln_kernel(c_ref, w_ref, b_ref, o_ref):
    a = _silu(c_ref[...]).astype(BF16)
    o_ref[...] = _dot(a, w_ref[...].astype(BF16)) + b_ref[...]


def _adaln(cvec, w_ada, b_ada):
    n = 3 * D_MODEL
    blocks = MOD_ROWS * D_MODEL * 4 + D_MODEL * ADA_TN * 4 + ADA_TN * 4 + MOD_ROWS * ADA_TN * 4
    return pl.pallas_call(
        _adaln_kernel,
        out_shape=jax.ShapeDtypeStruct((DEPTH, MOD_ROWS, n), F32),
        grid=(DEPTH, n // ADA_TN),
        in_specs=[
            pl.BlockSpec((MOD_ROWS, D_MODEL), lambda l, j: (0, 0)),
            pl.BlockSpec((None, D_MODEL, ADA_TN), lambda l, j: (l, 0, j)),
            pl.BlockSpec((None, 1, ADA_TN), lambda l, j: (l, 0, j)),
        ],
        out_specs=pl.BlockSpec((None, MOD_ROWS, ADA_TN), lambda l, j: (l, 0, j)),
        compiler_params=_params(_vmem_limit(blocks, temp_bytes=D_MODEL * ADA_TN * 2), 2),
        name="adaln",
    )(cvec, w_ada, b_ada.reshape(DEPTH, 1, n))


def _mod_row(tile, tm, latent):
    return 1 + tile // (DEC_SEQ // tm) if latent else 0


NORM_TM = 512
NORM_ROWS = 32
IN_TN = 512


def _prenorm_kernel(x_ref, g_ref, sh_ref, sc_ref, h_ref):
    gain = g_ref[...]
    one_sc = 1.0 + sc_ref[...]
    sh = sh_ref[...]

    def body(r, carry):
        sl = pl.ds(pl.multiple_of(r * NORM_ROWS, NORM_ROWS), NORM_ROWS)
        x = x_ref[sl, :]
        ms = jnp.mean(x * x, axis=-1, keepdims=True)
        y = (x * lax.rsqrt(ms + EPS)) * gain
        h_ref[sl, :] = (y * one_sc + sh).astype(BF16)
        return carry

    lax.fori_loop(0, NORM_TM // NORM_ROWS, body, 0, unroll=True)


def _prenorm(x, gain, mods, layer, latent):
    ntok = x.shape[0]
    blocks = NORM_TM * D_MODEL * 4 + 3 * D_MODEL * 4 + NORM_TM * D_MODEL * 2
    mod_spec = lambda which: pl.BlockSpec(
        (None, None, None, 1, D_MODEL), lambda i: (layer, _mod_row(i, NORM_TM, latent), which, 0, 0))
    return pl.pallas_call(
        _prenorm_kernel,
        out_shape=jax.ShapeDtypeStruct((ntok, D_MODEL), BF16),
        grid=(ntok // NORM_TM,),
        in_specs=[
            pl.BlockSpec((NORM_TM, D_MODEL), lambda i: (i, 0)),
            pl.BlockSpec((None, 1, D_MODEL), lambda i: (layer, 0, 0)),
            mod_spec(0),
            mod_spec(1),
        ],
        out_specs=pl.BlockSpec((NORM_TM, D_MODEL), lambda i: (i, 0)),
        compiler_params=_params(_vmem_limit(blocks, temp_bytes=4 * NORM_ROWS * D_MODEL * 4), 1),
        name=f"prenorm_l{layer}_{'latent' if latent else 'prompt'}",
    )(x, gain, mods, mods)


def _inproj_kernel(h_ref, w_ref, o_ref):
    o_ref[...] = _dot(h_ref[...], w_ref[...].astype(BF16))


def _inproj(h, layer, w, latent):
    ntok = h.shape[0]
    n = w.shape[2]
    blocks = D_MODEL * IN_TN * 4 + ntok * IN_TN * 4
    return pl.pallas_call(
        _inproj_kernel,
        out_shape=jax.ShapeDtypeStruct((ntok, n), F32),
        grid=(n // IN_TN,),
        in_specs=[
            pl.BlockSpec((ntok, D_MODEL), lambda j: (0, 0), pipeline_mode=pl.Buffered(1)),
            pl.BlockSpec((None, D_MODEL, IN_TN), lambda j: (layer // 2, 0, j)),
        ],
        out_specs=pl.BlockSpec((ntok, IN_TN), lambda j: (0, j)),
        compiler_params=_params(_vmem_limit(blocks, ntok * D_MODEL * 2, D_MODEL * IN_TN * 2), 1),
        name=f"inproj_l{layer}_{'latent' if latent else 'prompt'}",
    )(h, w)


OUT_TM = 512
OUT_SUB = 256


def _outproj_kernel(*refs, n_pieces, emit_next):
    u_refs = refs[:n_pieces]
    w_refs = refs[n_pieces:2 * n_pieces]
    pos = 2 * n_pieces
    x_ref, gain_ref, gate_ref = refs[pos:pos + 3]
    pos += 3
    if emit_next:
        ngain_ref, nshift_ref, nscale_ref = refs[pos:pos + 3]
        pos += 3
    o_ref = refs[pos]
    pos += 1
    if emit_next:
        h_ref = refs[pos]
        pos += 1
    wbf_ref = refs[pos]

    @pl.when(pl.program_id(0) == 0)
    def _():
        for p, w_ref in enumerate(w_refs):
            wbf_ref[p] = w_ref[...].astype(BF16)

    gated_gain = gate_ref[...] * gain_ref[...]
    if emit_next:
        next_gain = ngain_ref[...] * (1.0 + nscale_ref[...])
        next_shift = nshift_ref[...]
    for r in range(OUT_TM // OUT_SUB):
        rows = slice(r * OUT_SUB, (r + 1) * OUT_SUB)
        out = _dot(u_refs[0][rows, :], wbf_ref[0])
        for p in range(1, n_pieces):
            out = out + _dot(u_refs[p][rows, :], wbf_ref[p])
        ms = jnp.mean(out * out, axis=-1, keepdims=True)
        x_new = x_ref[rows, :] + (out * lax.rsqrt(ms + EPS)) * gated_gain
        o_ref[rows, :] = x_new
        if emit_next:
            ms = jnp.mean(x_new * x_new, axis=-1, keepdims=True)
            h_ref[rows, :] = ((x_new * lax.rsqrt(ms + EPS)) * next_gain + next_shift).astype(BF16)


def _outproj(us, w, x, gain_post, gain_pre, mods, layer, latent):
    ntok = x.shape[0]
    widths = [u.shape[1] for u in us]
    kp = widths[0]
    k = w.shape[1]
    assert all(kw == kp for kw in widths) and kp * len(us) == k
    emit_next = layer + 1 < DEPTH
    blocks = OUT_TM * k * 2 + 2 * OUT_TM * D_MODEL * 4 + 5 * D_MODEL * 4 + emit_next * OUT_TM * D_MODEL * 2
    tile_spec = pl.BlockSpec((OUT_TM, D_MODEL), lambda i: (i, 0))
    gain_spec = lambda l: pl.BlockSpec((None, 1, D_MODEL), lambda i: (l, 0, 0))
    mod_spec = lambda l, which: pl.BlockSpec(
        (None, None, None, 1, D_MODEL), lambda i: (l, _mod_row(i, OUT_TM, latent), which, 0, 0))
    u_specs = [pl.BlockSpec((OUT_TM, kp), lambda i: (i, 0)) for _ in us]
    w_specs = [pl.BlockSpec((None, kp, D_MODEL), lambda i, p=p: (layer // 2, p, 0),
                            pipeline_mode=pl.Buffered(1)) for p in range(len(us))]
    in_specs = u_specs + w_specs + [tile_spec, gain_spec(layer), mod_spec(layer, 2)]
    args = [*us, *([w] * len(us)), x, gain_post, mods]
    out_shape = [jax.ShapeDtypeStruct((ntok, D_MODEL), F32)]
    out_specs = [tile_spec]
    if emit_next:
        in_specs += [gain_spec(layer + 1), mod_spec(layer + 1, 0), mod_spec(layer + 1, 1)]
        args += [gain_pre, mods, mods]
        out_shape.append(jax.ShapeDtypeStruct((ntok, D_MODEL), BF16))
        out_specs.append(tile_spec)
    return pl.pallas_call(
        functools.partial(_outproj_kernel, n_pieces=len(us), emit_next=emit_next),
        out_shape=tuple(out_shape),
        grid=(ntok // OUT_TM,),
        in_specs=in_specs,
        out_specs=tuple(out_specs),
        scratch_shapes=[pltpu.VMEM((len(us), kp, D_MODEL), BF16)],
        compiler_params=_params(
            _vmem_limit(blocks, k * D_MODEL * (4 + 2), 3 * OUT_TM * D_MODEL * 2), 1),
        name=f"outproj_l{layer}_{'latent' if latent else 'prompt'}",
    )(*args)


LOG2_E = 1.4426950408889634
Q_SCALE_LOG2 = HEAD_DIM ** -0.5 * LOG2_E


def _softmax_pv(scores, values, sink=None):
    def fold(blocks, op):
        tiles = [b[:, t:t + LANES] for b in blocks for t in range(0, b.shape[-1], LANES)]
        out = tiles[0]
        for t in tiles[1:]:
            out = op(out, t)
        return out

    m = jnp.max(fold(scores, jnp.maximum), axis=-1, keepdims=True)
    if sink is not None:
        m = jnp.maximum(m, sink)
    probs = [jnp.exp2(s - m) for s in scores]
    den = jnp.sum(fold(probs, jnp.add), axis=-1, keepdims=True)
    if sink is not None:
        den = den + jnp.exp2(sink - m)
    acc = _dot(probs[0].astype(BF16), values[0])
    for p, v in zip(probs[1:], values[1:]):
        acc = acc + _dot(p.astype(BF16), v)
    return acc / den


def _ctx_attn_kernel(sink_ref, p_ref, *refs):
    u_ref, ak_ref, av_ref, bk_ref, bv_ref = refs[-5:]
    def col(c):
        return p_ref[:, c:c + HEAD_DIM]

    def head(cq, k, v, cg, sink, cu):
        q = (col(cq) * Q_SCALE_LOG2).astype(BF16)
        o = _softmax_pv([_dot_nt(q, k)], [v], None if sink is None else sink * LOG2_E)
        u_ref[:, cu:cu + HEAD_DIM] = (o * _silu(col(cg))).astype(BF16)

    for n in range(A_KV_HEADS):
        k32 = col(COL_KA + n * HEAD_DIM)
        v32 = col(COL_VA + n * HEAD_DIM)
        ak_ref[n] = k32
        av_ref[n] = v32
        k = k32.astype(BF16)
        v = v32.astype(BF16)
        for g in range(A_GROUP):
            h = n * A_GROUP + g
            head(COL_QA + h * HEAD_DIM, k, v, COL_GATE + h * HEAD_DIM, sink_ref[h], h * HEAD_DIM)
    for h in range(B_HEADS):
        k32 = col(COL_KB + h * HEAD_DIM)
        v32 = col(COL_VB + h * HEAD_DIM)
        bk_ref[h] = k32
        bv_ref[h] = v32
        head(COL_QB + h * HEAD_DIM, k32.astype(BF16), v32.astype(BF16),
             COL_GATE + A_Q + h * HEAD_DIM, None, A_Q + h * HEAD_DIM)


def _ctx_attn(proj, sink, idx, prev_caches):
    blocks = (SEQ * EVEN_IN * 4 + SEQ * EVEN_WIDTH * 2
              + 2 * (A_KV_HEADS + B_HEADS) * SEQ * HEAD_DIM * 4)
    heads = (A_KV_HEADS, A_KV_HEADS, B_HEADS, B_HEADS)
    cache = lambda nh: jax.ShapeDtypeStruct((BATCH, N_EVEN, nh, SEQ, HEAD_DIM), F32)
    cache_spec = lambda nh: pl.BlockSpec((None, None, nh, SEQ, HEAD_DIM), lambda b: (b, idx, 0, 0, 0))
    n_prev = len(prev_caches)
    return pl.pallas_call(
        _ctx_attn_kernel,
        out_shape=(jax.ShapeDtypeStruct((NP, EVEN_WIDTH), BF16),) + tuple(cache(nh) for nh in heads),
        grid=(BATCH,),
        in_specs=[
            pl.BlockSpec(memory_space=pltpu.SMEM),
            pl.BlockSpec((SEQ, EVEN_IN), lambda b: (b, 0)),
        ] + [_any_spec() for _ in prev_caches],
        out_specs=(pl.BlockSpec((SEQ, EVEN_WIDTH), lambda b: (b, 0)),) + tuple(cache_spec(nh) for nh in heads),
        input_output_aliases={2 + k: 1 + k for k in range(n_prev)},
        compiler_params=_params(_vmem_limit(blocks, temp_bytes=8 * SEQ * SEQ * 4), 1),
        name=f"ctx_attn_{idx}",
    )(sink, proj, *prev_caches)


WIN_PAD = DEC_SEQ + 2 * A_BLOCK


def _rope_tables():
    t = jnp.arange(DEC_SEQ)
    half = HEAD_DIM // 2
    nf = half // 2
    inv = ROPE_THETA ** (-jnp.arange(nf, dtype=F32) / nf)
    ang_r = (t // GRID_W).astype(F32)[:, None] * inv[None]
    ang_c = (t % GRID_W).astype(F32)[:, None] * inv[None]
    cos = jnp.concatenate([jnp.cos(ang_r)] * 2 + [jnp.cos(ang_c)] * 2, axis=-1)
    sin = jnp.concatenate([-jnp.sin(ang_r), jnp.sin(ang_r), -jnp.sin(ang_c), jnp.sin(ang_c)], axis=-1)
    return cos, sin


def _rope(x, cos, sin):
    quarter = HEAD_DIM // 4
    lane = lax.broadcasted_iota(jnp.int32, x.shape, 1)
    first = (lane & (2 * quarter - 1)) < quarter
    partner = jnp.where(first, pltpu.roll(x, HEAD_DIM - quarter, 1), pltpu.roll(x, quarter, 1))
    return x * cos + partner * sin


def _win_attn_kernel(sink_ref, q_ref, k_ref, v_ref, ck_ref, cv_ref, gate_ref, cos_ref, sin_ref,
                     u_ref, kpad_ref, vpad_ref):
    n = pl.program_id(1)
    nqb = DEC_SEQ // A_BLOCK
    rows = A_GROUP * A_BLOCK

    zeros = jnp.zeros((A_BLOCK, HEAD_DIM), BF16)
    for ref in (kpad_ref, vpad_ref):
        ref[0:A_BLOCK, :] = zeros
        ref[A_BLOCK + DEC_SEQ:WIN_PAD, :] = zeros
    kpad_ref[A_BLOCK:A_BLOCK + DEC_SEQ, :] = _rope(k_ref[...], cos_ref[...], sin_ref[...]).astype(BF16)
    vpad_ref[A_BLOCK:A_BLOCK + DEC_SEQ, :] = v_ref[...].astype(BF16)
    ck = ck_ref[...].astype(BF16)
    cv = cv_ref[...].astype(BF16)

    row = lax.broadcasted_iota(jnp.int32, (rows, 1), 0)
    qi = row & (A_BLOCK - 1)
    kk = lax.broadcasted_iota(jnp.int32, (rows, 3 * A_BLOCK), 1)
    head = lax.shift_right_logical(row, A_BLOCK.bit_length() - 1)
    sink = jnp.zeros((rows, 1), F32)
    for g in range(A_GROUP):
        sink = jnp.where(head == g, sink_ref[n * A_GROUP + g] * LOG2_E, sink)

    for j in range(nqb):
        blk = slice(j * A_BLOCK, (j + 1) * A_BLOCK)
        q = jnp.concatenate(
            [(_rope(q_ref[blk, g * HEAD_DIM:(g + 1) * HEAD_DIM], cos_ref[blk, :], sin_ref[blk, :])
              * Q_SCALE_LOG2).astype(BF16) for g in range(A_GROUP)],
            axis=0)
        band = slice(j * A_BLOCK, (j + 3) * A_BLOCK)
        s_c = _dot_nt(q, ck)
        s_w = _dot_nt(q, kpad_ref[band, :])
        lower = jnp.maximum(qi, A_BLOCK if j == 0 else 0)
        upper = jnp.minimum(qi + 2 * A_WINDOW, (2 if j == nqb - 1 else 3) * A_BLOCK - 1)
        s_w = jnp.where((kk >= lower) & (kk <= upper), s_w, NEG_INF)
        o = _softmax_pv([s_c, s_w], [cv, vpad_ref[band, :]], sink)
        for g in range(A_GROUP):
            cols = slice(g * HEAD_DIM, (g + 1) * HEAD_DIM)
            u_ref[blk, cols] = (o[g * A_BLOCK:(g + 1) * A_BLOCK] * _silu(gate_ref[blk, cols])).astype(BF16)


def _win_attn(proj, sink, cache_k, cache_v, idx, cos, sin):
    gw = A_GROUP * HEAD_DIM
    blocks = (2 * DEC_SEQ * gw * 4 + 2 * DEC_SEQ * HEAD_DIM * 4 + 2 * PAST_LEN * HEAD_DIM * 4
              + 2 * DEC_SEQ * HEAD_DIM * 4 + DEC_SEQ * gw * 2)
    cache_spec = pl.BlockSpec((None, None, None, PAST_LEN, HEAD_DIM), lambda b, n: (b, idx, n, 0, 0))
    table_spec = pl.BlockSpec((DEC_SEQ, HEAD_DIM), lambda b, n: (0, 0))
    return pl.pallas_call(
        _win_attn_kernel,
        out_shape=jax.ShapeDtypeStruct((NS, A_Q), BF16),
        grid=(DEC_BATCH, A_KV_HEADS),
        in_specs=[
            pl.BlockSpec(memory_space=pltpu.SMEM),
            pl.BlockSpec((DEC_SEQ, gw), lambda b, n: (b, COL_QA // gw + n)),
            pl.BlockSpec((DEC_SEQ, HEAD_DIM), lambda b, n: (b, COL_KA // HEAD_DIM + n)),
            pl.BlockSpec((DEC_SEQ, HEAD_DIM), lambda b, n: (b, COL_VA // HEAD_DIM + n)),
            cache_spec,
            cache_spec,
            pl.BlockSpec((DEC_SEQ, gw), lambda b, n: (b, COL_GATE // gw + n)),
            table_spec,
            table_spec,
        ],
        out_specs=pl.BlockSpec((DEC_SEQ, gw), lambda b, n: (b, n)),
        scratch_shapes=[pltpu.VMEM((WIN_PAD, HEAD_DIM), BF16), pltpu.VMEM((WIN_PAD, HEAD_DIM), BF16)],
        compiler_params=_params(
            _vmem_limit(blocks, 2 * WIN_PAD * HEAD_DIM * 2,
                        (DEC_SEQ // A_BLOCK) * 3 * A_GROUP * A_BLOCK * (PAST_LEN + 3 * A_BLOCK) * 4), 2),
        name=f"win_attn_{idx}",
    )(sink, proj, proj, proj, cache_k, cache_v, proj, cos, sin)


RPB_PAD = (16, 128)


def _na_row_offsets(g):
    offsets = []
    for rl in range(NA_QROWS):
        r = g * NA_QROWS + rl
        r0 = min(max(r - NA_ROWS // 2, 0), GRID_ROWS - NA_ROWS)
        row = []
        for kl in range(NA_KROWS):
            kr = g * (GRID_ROWS - NA_KROWS) + kl
            row.append(kr - r + NA_ROWS - 1 if r0 <= kr < r0 + NA_ROWS else None)
        offsets.append(row)
    return offsets


def _fill_na_bias(rpb_ref, bias_ref, g):
    shape = (GRID_W, 2 * GRID_W)
    c = lax.broadcasted_iota(jnp.int32, shape, 0)
    lane = lax.broadcasted_iota(jnp.int32, shape, 1)
    kc = lane & (GRID_W - 1)
    c0 = jnp.clip(c - NA_COLS // 2, 0, GRID_W - NA_COLS)
    col_ok = (kc >= c0) & (kc < c0 + NA_COLS)
    low = lane < GRID_W
    offsets = _na_row_offsets(g)
    used = sorted({d for row in offsets for d in row if d is not None})
    lo, hi = {}, {}
    for d in used:
        row = jnp.broadcast_to(rpb_ref[d:d + 1, :] * LOG2_E, shape)
        lo[d] = pltpu.roll(row, 2 * GRID_W - (NA_COLS - 1), 1, stride=1, stride_axis=0)
        hi[d] = pltpu.roll(row, GRID_W - (NA_COLS - 1), 1, stride=1, stride_axis=0)
    neg = jnp.full(shape, NEG_INF, F32)
    for rl in range(NA_QROWS):
        for p in range(NA_KROWS // 2):
            da, db = offsets[rl][2 * p], offsets[rl][2 * p + 1]
            a = neg if da is None else lo[da]
            b = neg if db is None else hi[db]
            piece = jnp.where(col_ok, jnp.where(low, a, b), NEG_INF)
            bias_ref[rl * GRID_W:(rl + 1) * GRID_W, 2 * p * GRID_W:2 * (p + 1) * GRID_W] = piece


def _na_attn_kernel(q_ref, k_ref, v_ref, ck_ref, cv_ref, gate_ref, rpb_ref, u_ref, bias_ref):
    g = pl.program_id(1)

    @pl.when(pl.program_id(2) == 0)
    def _():
        for group in range(DEC_SEQ // NA_Q):
            @pl.when(g == group)
            def _():
                _fill_na_bias(rpb_ref, bias_ref, group)

    k0 = pl.multiple_of(g * NA_KSHIFT, NA_KSHIFT)
    for i in range(NA_REQS):
        q = (q_ref[i] * Q_SCALE_LOG2).astype(BF16)
        s_c = _dot_nt(q, ck_ref[i].astype(BF16))
        s_n = _dot_nt(q, k_ref[i, pl.ds(k0, NA_K), :].astype(BF16)) + bias_ref[...]
        o = _softmax_pv([s_c, s_n], [cv_ref[i].astype(BF16), v_ref[i, pl.ds(k0, NA_K), :].astype(BF16)])
        u_ref[i] = (o * _silu(gate_ref[i])).astype(BF16)


def _na_attn(proj, rpb, cache_k, cache_v, idx):
    per_seq = DEC_SEQ // NA_Q
    proj = proj.reshape(DEC_BATCH, DEC_SEQ, EVEN_IN)
    blocks = NA_REQS * (2 * NA_Q * HEAD_DIM * 4 + 2 * DEC_SEQ * HEAD_DIM * 4 + 2 * PAST_LEN * HEAD_DIM * 4
                        + NA_Q * HEAD_DIM * 2) + RPB_PAD[0] * RPB_PAD[1] * 4
    tok_spec = lambda rows, col0: pl.BlockSpec(
        (NA_REQS, rows, HEAD_DIM), lambda h, g, b: (b, g if rows == NA_Q else 0, col0 // HEAD_DIM + h))
    cache_spec = pl.BlockSpec((NA_REQS, None, None, PAST_LEN, HEAD_DIM), lambda h, g, b: (b, idx, h, 0, 0))
    out = pl.pallas_call(
        _na_attn_kernel,
        out_shape=jax.ShapeDtypeStruct((DEC_BATCH, DEC_SEQ, B_W), BF16),
        grid=(B_HEADS, per_seq, DEC_BATCH // NA_REQS),
        in_specs=[
            tok_spec(NA_Q, COL_QB),
            tok_spec(DEC_SEQ, COL_KB),
            tok_spec(DEC_SEQ, COL_VB),
            cache_spec,
            cache_spec,
            tok_spec(NA_Q, COL_GATE + A_Q),
            pl.BlockSpec((None, None) + RPB_PAD, lambda h, g, b: (idx, h, 0, 0)),
        ],
        out_specs=pl.BlockSpec((NA_REQS, NA_Q, HEAD_DIM), lambda h, g, b: (b, g, h)),
        scratch_shapes=[pltpu.VMEM((NA_Q, NA_K), F32)],
        compiler_params=_params(
            _vmem_limit(blocks, NA_Q * NA_K * 4, NA_REQS * 3 * NA_Q * (NA_K + PAST_LEN) * 4), 3),
        name=f"na_attn_{idx}",
    )(proj, proj, proj, cache_k, cache_v, proj, rpb)
    return out.reshape(NS, B_W)


RET_CHUNK = 256


def _ret_kernel(*refs, seq, heads, has_state, emit_state, n_prev):
    dec_ref, q_ref, k_ref, v_ref, gate_ref, gn_ref = refs[:6]
    pos = 6
    if has_state:
        s0f_ref, s0b_ref = refs[pos:pos + 2]
        pos += 2
    pos += n_prev
    u_ref = refs[pos]
    pos += 1
    if emit_state:
        sf_ref, sb_ref = refs[pos:pos + 2]
        pos += 2
    stf_ref, stb_ref, o_ref, dmat_ref = refs[pos:pos + 4]

    ch = RET_CHUNK
    nc = seq // ch
    kscale = C_DK ** -0.5
    ii = lax.broadcasted_iota(jnp.int32, (ch, ch), 0).astype(F32)
    jj = lax.broadcasted_iota(jnp.int32, (ch, ch), 1).astype(F32)
    icol = lax.broadcasted_iota(jnp.int32, (ch, 1), 0).astype(F32)

    def decay_matrix(direction, head):
        log_g = -jnp.exp(jnp.full((ch, ch), dec_ref[direction, head], F32))
        diff = (ii - jj) if direction == 0 else (jj - ii)
        return jnp.where(diff >= 0, jnp.exp(log_g * jnp.maximum(diff, 0.0)), 0.0)

    def decay_vectors(direction, head):
        dec = dec_ref[direction, head]
        log_g_col = -jnp.exp(jnp.full((ch, 1), dec, F32))
        log_g_row = -jnp.exp(jnp.full((1, C_DV), dec, F32))
        if direction == 0:
            q_dec = jnp.exp(log_g_col * (icol + 1.0))
            k_dec = jnp.exp(log_g_col * (ch - 1.0 - icol)) * kscale
        else:
            q_dec = jnp.exp(log_g_col * (ch - icol))
            k_dec = jnp.exp(log_g_col * icol) * kscale
        return q_dec, k_dec, jnp.exp(log_g_row * float(ch))

    def one_head(hh):
        cols = slice(hh * C_DK, (hh + 1) * C_DK)
        head = pl.program_id(1) * heads + hh
        dmat = dmat_ref[head]
        q_dec_f, k_dec_f, chunk_dec_f = decay_vectors(0, head)
        q_dec_b, k_dec_b, chunk_dec_b = decay_vectors(1, head)
        gn = gn_ref[:, cols]

        def rows_of(c):
            return slice(c * ch, (c + 1) * ch)

        def finalize(c, o):
            mu = jnp.mean(o, axis=-1, keepdims=True)
            d = o - mu
            var = jnp.mean(d * d, axis=-1, keepdims=True)
            y = (d * lax.rsqrt(var + EPS)) * gn
            u_ref[rows_of(c), cols] = (y * _silu(gate_ref[rows_of(c), cols])).astype(BF16)

        parked = {}

        def visit(c, value):
            if c not in parked:
                parked[c] = value is not None
                if value is not None:
                    o_ref[rows_of(c), cols] = value
            elif parked[c]:
                first = o_ref[rows_of(c), cols]
                finalize(c, first if value is None else first + value)
            else:
                finalize(c, value)

        def scan_step(state_ref, s0_ref, t, q, k32, v, q_dec, k_dec, chunk_dec):
            update = _dot_tn((k32 * k_dec).astype(BF16), v)
            if t == 0 and not has_state:
                state_ref[hh] = update
                return None
            state = s0_ref[hh] if t == 0 else state_ref[hh]
            state_ref[hh] = state * chunk_dec + update
            return _dot(q, state.astype(BF16)) * q_dec

        for t in range(nc):
            cf, cb = t, nc - 1 - t
            q = q_ref[rows_of(cf), cols].astype(BF16)
            k32 = k_ref[rows_of(cf), cols]
            v = v_ref[rows_of(cf), cols].astype(BF16)
            s = _dot_nt(q, k32.astype(BF16)) * dmat
            val_f = _dot(s.astype(BF16), v)
            cross = scan_step(stf_ref, s0f_ref if has_state else None, t, q, k32, v,
                              q_dec_f, k_dec_f, chunk_dec_f)
            if cross is not None:
                val_f = val_f + cross
            if cb != cf:
                q = q_ref[rows_of(cb), cols].astype(BF16)
                k32 = k_ref[rows_of(cb), cols]
                v = v_ref[rows_of(cb), cols].astype(BF16)
            val_b = scan_step(stb_ref, s0b_ref if has_state else None, t, q, k32, v,
                              q_dec_b, k_dec_b, chunk_dec_b)
            if cb == cf:
                finalize(cf, val_f if val_b is None else val_f + val_b)
            else:
                visit(cf, val_f)
                visit(cb, val_b)

        if emit_state:
            sf_ref[hh] = stf_ref[hh]
            sb_ref[hh] = stb_ref[hh]

    @pl.when(pl.program_id(0) == 0)
    def _():
        for hh in range(heads):
            head = pl.program_id(1) * heads + hh
            dmat_ref[head] = (decay_matrix(0, head) + decay_matrix(1, head)) * kscale

    for hh in range(heads):
        one_head(hh)


RET_HEADS_PROMPT = 8
RET_HEADS_LATENT = 4


def _retention(proj, decays, gn, idx, *, latent, state_f=None, state_b=None, prev_states=()):
    seq = DEC_SEQ if latent else SEQ
    nb = DEC_BATCH if latent else BATCH
    heads = RET_HEADS_LATENT if latent else RET_HEADS_PROMPT
    width = heads * C_DK
    groups = C_HEADS // heads
    has_state = latent
    emit_state = not latent
    tok = lambda kind: pl.BlockSpec((seq, width), lambda b, h: (b, kind * groups + h))
    in_specs = [pl.BlockSpec(memory_space=pltpu.SMEM), tok(0), tok(1), tok(2), tok(3),
                pl.BlockSpec((None, 1, width), lambda b, h: (idx, 0, h))]
    args = [decays, proj, proj, proj, proj, gn]
    state_spec = pl.BlockSpec((None, None, heads, C_DK, C_DV), lambda b, h: (b, idx, h, 0, 0))
    if has_state:
        in_specs += [state_spec, state_spec]
        args += [state_f, state_b]
    aliases = {len(args) + k: 1 + k for k in range(len(prev_states))}
    in_specs += [_any_spec() for _ in prev_states]
    args += list(prev_states)
    out_shape = [jax.ShapeDtypeStruct((nb * seq, D_MODEL), BF16)]
    out_specs = [pl.BlockSpec((seq, width), lambda b, h: (b, h))]
    if emit_state:
        st = jax.ShapeDtypeStruct((nb, N_ODD, C_HEADS, C_DK, C_DV), F32)
        out_shape += [st, st]
        out_specs += [state_spec, state_spec]
    blocks = 4 * seq * width * 4 + width * 4 + 2 * heads * C_DK * C_DV * 4 + seq * width * 2
    scratch = 2 * heads * C_DK * C_DV * 4 + seq * width * 4 + C_HEADS * RET_CHUNK * RET_CHUNK * 4
    return pl.pallas_call(
        functools.partial(_ret_kernel, seq=seq, heads=heads, has_state=has_state, emit_state=emit_state,
                          n_prev=len(prev_states)),
        out_shape=tuple(out_shape),
        grid=(nb, groups),
        in_specs=in_specs,
        out_specs=tuple(out_specs),
        scratch_shapes=[pltpu.VMEM((heads, C_DK, C_DV), F32), pltpu.VMEM((heads, C_DK, C_DV), F32),
                        pltpu.VMEM((seq, width), F32), pltpu.VMEM((C_HEADS, RET_CHUNK, RET_CHUNK), F32)],
        input_output_aliases=aliases,
        compiler_params=_params(_vmem_limit(blocks, scratch, 24 * RET_CHUNK * C_DV * 4), 2),
        name=f"retention_{'latent' if latent else 'prompt'}_{idx}",
    )(*args)


def kernel(x_prompt, x_sample, c, cache_a_k, cache_a_v, cache_b_k, cache_b_v, state_ret_f, state_ret_b,
           c_ctx, w_ada, b_ada, norm_pre, norm_post, w_in_even, w_out_even, a_sink, na_rpb,
           w_in_odd, w_out_odd, ret_decay_f, ret_decay_b, ret_gn):
    xp = x_prompt.reshape(NP, D_MODEL)
    xs = x_sample.reshape(NS, D_MODEL)
    cvec = jnp.concatenate(
        [c_ctx[None, :], c, jnp.zeros((MOD_ROWS - 1 - DEC_BATCH, D_MODEL), F32)], axis=0)
    mods = _adaln(cvec, w_ada, b_ada).reshape(DEPTH, MOD_ROWS, 3, 1, D_MODEL)
    gain_pre = norm_pre.reshape(DEPTH, 1, D_MODEL)
    gain_post = norm_post.reshape(DEPTH, 1, D_MODEL)
    gn = ret_gn.reshape(N_ODD, 1, D_MODEL)
    cos, sin = _rope_tables()
    rpb = jnp.pad(na_rpb, ((0, 0), (0, 0), (0, RPB_PAD[0] - na_rpb.shape[2]),
                           (0, RPB_PAD[1] - na_rpb.shape[3])))
    caches = ()
    states = ()
    hp = _prenorm(xp, gain_pre, mods, 0, False)
    hs = _prenorm(xs, gain_pre, mods, 0, True)
    for layer in range(DEPTH):
        idx = layer // 2
        if layer % 2 == 0:
            proj_p = _inproj(hp, layer, w_in_even, False)
            proj_s = _inproj(hs, layer, w_in_even, True)
            u_p, *caches = _ctx_attn(proj_p, a_sink[idx], idx, caches)
            u_a = _win_attn(proj_s, a_sink[idx], cache_a_k, cache_a_v, idx, cos, sin)
            u_b = _na_attn(proj_s, rpb, cache_b_k, cache_b_v, idx)
            us_p, us_s, w_out = [u_p], [u_a, u_b], w_out_even
        else:
            proj_p = _inproj(hp, layer, w_in_odd, False)
            proj_s = _inproj(hs, layer, w_in_odd, True)
            decays = jnp.stack([ret_decay_f[idx], ret_decay_b[idx]], axis=0)
            u_p, *states = _retention(proj_p, decays, gn, idx, latent=False, prev_states=states)
            (u_s,) = _retention(proj_s, decays, gn, idx, latent=True,
                                state_f=state_ret_f, state_b=state_ret_b)
            us_p, us_s = [u_p], [u_s]
            w_out = w_out_odd
        xp, *hp = _outproj(us_p, w_out, xp, gain_post, gain_pre, mods, layer, False)
        xs, *hs = _outproj(us_s, w_out, xs, gain_post, gain_pre, mods, layer, True)
        hp, hs = (hp[0], hs[0]) if hp else (None, None)
    return (xp.reshape(BATCH, SEQ, D_MODEL), xs.reshape(DEC_BATCH, DEC_SEQ, D_MODEL), *caches, *states)
```

```python
import functools

import numpy as np
import jax
import jax.numpy as jnp
from jax import lax
from jax.experimental import pallas as pl
from jax.experimental.pallas import tpu as pltpu

D_MODEL = 2048
BATCH = 16
SEQ = 256
DEPTH = 4
DEC_BATCH = 4
DEC_SEQ = 1024
PAST_LEN = 256
GRID_W = 64
HEAD_DIM = 128
A_HEADS = 8
A_KV_HEADS = 2
A_GROUP = A_HEADS // A_KV_HEADS
A_WINDOW = 128
A_BLOCK = 128
B_HEADS = 8
NA_ROWS = 8
NA_COLS = 16
C_HEADS = 8
C_DK = D_MODEL // C_HEADS
C_DV = D_MODEL // C_HEADS
C_CHUNK = 128
ROPE_THETA = 10000.0
EPS = 1e-6

N_EVEN = (DEPTH + 1) // 2
N_ODD = DEPTH // 2
A_Q = A_HEADS * HEAD_DIM
A_KV = A_KV_HEADS * HEAD_DIM
B_W = B_HEADS * HEAD_DIM
EVEN_WIDTH = A_Q + B_W
EVEN_IN = A_Q + 2 * A_KV + 3 * B_W + EVEN_WIDTH
ODD_IN = 4 * D_MODEL
GRID_ROWS = DEC_SEQ // GRID_W

NP = BATCH * SEQ
NS = DEC_BATCH * DEC_SEQ
MOD_ROWS = 8

COL_QA = 0
COL_KA = A_Q
COL_VA = A_Q + A_KV
COL_QB = A_Q + 2 * A_KV
COL_KB = COL_QB + B_W
COL_VB = COL_KB + B_W
COL_GATE = COL_VB + B_W

NA_QROWS = 8
NA_KROWS = 12
NA_Q = NA_QROWS * GRID_W
NA_K = NA_KROWS * GRID_W
NA_KSHIFT = (GRID_ROWS - NA_KROWS) * GRID_W
NA_REQS = 4

LANES = 128
V7X_VMEM_BYTES = 64 * 1024 * 1024
VMEM_HEADROOM = 4 * 1024 * 1024

F32 = jnp.float32
BF16 = jnp.bfloat16
NEG_INF = float("-inf")


def _vmem_limit(block_bytes, scratch_bytes=0, temp_bytes=0):
    usable = V7X_VMEM_BYTES - 4 * 1024 * 1024
    assert 2 * block_bytes + scratch_bytes + temp_bytes + VMEM_HEADROOM <= usable
    return usable


def _params(vmem_bytes, ndims):
    return pltpu.CompilerParams(dimension_semantics=("arbitrary",) * ndims, vmem_limit_bytes=vmem_bytes)


def _silu(x):
    half = 0.5 * x
    return half + half * jnp.tanh(half)


def _dot(a, b):
    return jnp.dot(a, b, preferred_element_type=F32)


def _dot_nt(a, b):
    return lax.dot_general(a, b, (((1,), (1,)), ((), ())), preferred_element_type=F32)


def _dot_tn(a, b):
    return lax.dot_general(a, b, (((0,), (0,)), ((), ())), preferred_element_type=F32)


def _any_spec():
    return pl.BlockSpec(memory_space=pl.ANY)


ADA_TN = 1024


def _adaln_kernel(c_ref, w_ref, b_ref, o_ref):
    a = _silu(c_ref[...]).astype(BF16)
    o_ref[...] = _dot(a, w_ref[...].astype(BF16)) + b_ref[...]


def _adaln(cvec, w_ada, b_ada):
    n = 3 * D_MODEL
    blocks = MOD_ROWS * D_MODEL * 4 + D_MODEL * ADA_TN * 4 + ADA_TN * 4 + MOD_ROWS * ADA_TN * 4
    return pl.pallas_call(
        _adaln_kernel,
        out_shape=jax.ShapeDtypeStruct((DEPTH, MOD_ROWS, n), F32),
        grid=(DEPTH, n // ADA_TN),
        in_specs=[
            pl.BlockSpec((MOD_ROWS, D_MODEL), lambda l, j: (0, 0)),
            pl.BlockSpec((None, D_MODEL, ADA_TN), lambda l, j: (l, 0, j)),
            pl.BlockSpec((None, 1, ADA_TN), lambda l, j: (l, 0, j)),
        ],
        out_specs=pl.BlockSpec((None, MOD_ROWS, ADA_TN), lambda l, j: (l, 0, j)),
        compiler_params=_params(_vmem_limit(blocks, temp_bytes=D_MODEL * ADA_TN * 2), 2),
        name="adaln",
    )(cvec, w_ada, b_ada.reshape(DEPTH, 1, n))


def _mod_row(tile, tm, latent):
    return 1 + tile // (DEC_SEQ // tm) if latent else 0


NORM_TM = 512
NORM_ROWS = 32
IN_TN = 512


def _prenorm_kernel(x_ref, g_ref, sh_ref, sc_ref, h_ref):
    gain = g_ref[...]
    one_sc = 1.0 + sc_ref[...]
    sh = sh_ref[...]

    def body(r, carry):
        sl = pl.ds(pl.multiple_of(r * NORM_ROWS, NORM_ROWS), NORM_ROWS)
        x = x_ref[sl, :]
        ms = jnp.mean(x * x, axis=-1, keepdims=True)
        y = (x * lax.rsqrt(ms + EPS)) * gain
        h_ref[sl, :] = (y * one_sc + sh).astype(BF16)
        return carry

    lax.fori_loop(0, NORM_TM // NORM_ROWS, body, 0, unroll=True)


def _prenorm(x, gain, mods, layer, latent):
    ntok = x.shape[0]
    blocks = NORM_TM * D_MODEL * 4 + 3 * D_MODEL * 4 + NORM_TM * D_MODEL * 2
    mod_spec = lambda which: pl.BlockSpec(
        (None, None, None, 1, D_MODEL), lambda i: (layer, _mod_row(i, NORM_TM, latent), which, 0, 0))
    return pl.pallas_call(
        _prenorm_kernel,
        out_shape=jax.ShapeDtypeStruct((ntok, D_MODEL), BF16),
        grid=(ntok // NORM_TM,),
        in_specs=[
            pl.BlockSpec((NORM_TM, D_MODEL), lambda i: (i, 0)),
            pl.BlockSpec((None, 1, D_MODEL), lambda i: (layer, 0, 0)),
            mod_spec(0),
            mod_spec(1),
        ],
        out_specs=pl.BlockSpec((NORM_TM, D_MODEL), lambda i: (i, 0)),
        compiler_params=_params(_vmem_limit(blocks, temp_bytes=4 * NORM_ROWS * D_MODEL * 4), 1),
        name=f"prenorm_l{layer}_{'latent' if latent else 'prompt'}",
    )(x, gain, mods, mods)


def _inproj_kernel(h_ref, w_ref, o_ref):
    o_ref[...] = _dot(h_ref[...], w_ref[...].astype(BF16))


def _inproj(h, layer, w, latent):
    ntok = h.shape[0]
    n = w.shape[2]
    blocks = D_MODEL * IN_TN * 4 + ntok * IN_TN * 4
    return pl.pallas_call(
        _inproj_kernel,
        out_shape=jax.ShapeDtypeStruct((ntok, n), F32),
        grid=(n // IN_TN,),
        in_specs=[
            pl.BlockSpec((ntok, D_MODEL), lambda j: (0, 0), pipeline_mode=pl.Buffered(1)),
            pl.BlockSpec((None, D_MODEL, IN_TN), lambda j: (layer // 2, 0, j)),
        ],
        out_specs=pl.BlockSpec((ntok, IN_TN), lambda j: (0, j)),
        compiler_params=_params(_vmem_limit(blocks, ntok * D_MODEL * 2, D_MODEL * IN_TN * 2), 1),
        name=f"inproj_l{layer}_{'latent' if latent else 'prompt'}",
    )(h, w)


def _inproj_split_kernel(h_ref, w_ref, lo_ref, hi_ref, *, n_lo):
    @pl.when(pl.program_id(0) < n_lo)
    def _():
        lo_ref[...] = _dot(h_ref[...], w_ref[...].astype(BF16)).astype(lo_ref.dtype)

    @pl.when(pl.program_id(0) >= n_lo)
    def _():
        hi_ref[...] = _dot(h_ref[...], w_ref[...].astype(BF16))


def _inproj_split(h, layer, w, latent, lo_width):
    ntok = h.shape[0]
    n = w.shape[2]
    n_lo = lo_width // IN_TN
    assert lo_width % IN_TN == 0 and 0 < n_lo < n // IN_TN
    blocks = D_MODEL * IN_TN * 4 + ntok * IN_TN * (2 + 4)
    return pl.pallas_call(
        functools.partial(_inproj_split_kernel, n_lo=n_lo),
        out_shape=(jax.ShapeDtypeStruct((ntok, lo_width), BF16),
                   jax.ShapeDtypeStruct((ntok, n - lo_width), F32)),
        grid=(n // IN_TN,),
        in_specs=[
            pl.BlockSpec((ntok, D_MODEL), lambda j: (0, 0), pipeline_mode=pl.Buffered(1)),
            pl.BlockSpec((None, D_MODEL, IN_TN), lambda j: (layer // 2, 0, j)),
        ],
        out_specs=(pl.BlockSpec((ntok, IN_TN), lambda j: (0, jnp.minimum(j, n_lo - 1))),
                   pl.BlockSpec((ntok, IN_TN), lambda j: (0, jnp.maximum(j - n_lo, 0)))),
        compiler_params=_params(_vmem_limit(blocks, ntok * D_MODEL * 2, D_MODEL * IN_TN * 2), 1),
        name=f"inproj_l{layer}_{'latent' if latent else 'prompt'}",
    )(h, w)


OUT_TM = 512
OUT_SUB = 256


def _outproj_kernel(*refs, n_pieces, emit_next):
    u_refs = refs[:n_pieces]
    w_refs = refs[n_pieces:2 * n_pieces]
    pos = 2 * n_pieces
    x_ref, gain_ref, gate_ref = refs[pos:pos + 3]
    pos += 3
    if emit_next:
        ngain_ref, nshift_ref, nscale_ref = refs[pos:pos + 3]
        pos += 3
    o_ref = refs[pos]
    pos += 1
    if emit_next:
        h_ref = refs[pos]
        pos += 1
    wbf_ref = refs[pos]

    @pl.when(pl.program_id(0) == 0)
    def _():
        for p, w_ref in enumerate(w_refs):
            wbf_ref[p] = w_ref[...].astype(BF16)

    gated_gain = gate_ref[...] * gain_ref[...]
    if emit_next:
        next_gain = ngain_ref[...] * (1.0 + nscale_ref[...])
        next_shift = nshift_ref[...]
    for r in range(OUT_TM // OUT_SUB):
        rows = slice(r * OUT_SUB, (r + 1) * OUT_SUB)
        out = _dot(u_refs[0][rows, :], wbf_ref[0])
        for p in range(1, n_pieces):
            out = out + _dot(u_refs[p][rows, :], wbf_ref[p])
        ms = jnp.mean(out * out, axis=-1, keepdims=True)
        x_new = x_ref[rows, :] + (out * lax.rsqrt(ms + EPS)) * gated_gain
        o_ref[rows, :] = x_new
        if emit_next:
            ms = jnp.mean(x_new * x_new, axis=-1, keepdims=True)
            h_ref[rows, :] = ((x_new * lax.rsqrt(ms + EPS)) * next_gain + next_shift).astype(BF16)


def _outproj(us, w, x, gain_post, gain_pre, mods, layer, latent):
    ntok = x.shape[0]
    widths = [u.shape[1] for u in us]
    kp = widths[0]
    k = w.shape[1]
    assert all(kw == kp for kw in widths) and kp * len(us) == k
    emit_next = layer + 1 < DEPTH
    blocks = OUT_TM * k * 2 + 2 * OUT_TM * D_MODEL * 4 + 5 * D_MODEL * 4 + emit_next * OUT_TM * D_MODEL * 2
    tile_spec = pl.BlockSpec((OUT_TM, D_MODEL), lambda i: (i, 0))
    gain_spec = lambda l: pl.BlockSpec((None, 1, D_MODEL), lambda i: (l, 0, 0))
    mod_spec = lambda l, which: pl.BlockSpec(
        (None, None, None, 1, D_MODEL), lambda i: (l, _mod_row(i, OUT_TM, latent), which, 0, 0))
    u_specs = [pl.BlockSpec((OUT_TM, kp), lambda i: (i, 0)) for _ in us]
    w_specs = [pl.BlockSpec((None, kp, D_MODEL), lambda i, p=p: (layer // 2, p, 0),
                            pipeline_mode=pl.Buffered(1)) for p in range(len(us))]
    in_specs = u_specs + w_specs + [tile_spec, gain_spec(layer), mod_spec(layer, 2)]
    args = [*us, *([w] * len(us)), x, gain_post, mods]
    out_shape = [jax.ShapeDtypeStruct((ntok, D_MODEL), F32)]
    out_specs = [tile_spec]
    if emit_next:
        in_specs += [gain_spec(layer + 1), mod_spec(layer + 1, 0), mod_spec(layer + 1, 1)]
        args += [gain_pre, mods, mods]
        out_shape.append(jax.ShapeDtypeStruct((ntok, D_MODEL), BF16))
        out_specs.append(tile_spec)
    return pl.pallas_call(
        functools.partial(_outproj_kernel, n_pieces=len(us), emit_next=emit_next),
        out_shape=tuple(out_shape),
        grid=(ntok // OUT_TM,),
        in_specs=in_specs,
        out_specs=tuple(out_specs),
        scratch_shapes=[pltpu.VMEM((len(us), kp, D_MODEL), BF16)],
        compiler_params=_params(
            _vmem_limit(blocks, k * D_MODEL * (4 + 2), 3 * OUT_TM * D_MODEL * 2), 1),
        name=f"outproj_l{layer}_{'latent' if latent else 'prompt'}",
    )(*args)


LOG2_E = 1.4426950408889634
Q_SCALE_LOG2 = HEAD_DIM ** -0.5 * LOG2_E


def _softmax_pv(scores, values, sink=None):
    def fold(blocks, op):
        tiles = [b[:, t:t + LANES] for b in blocks for t in range(0, b.shape[-1], LANES)]
        out = tiles[0]
        for t in tiles[1:]:
            out = op(out, t)
        return out

    m = jnp.max(fold(scores, jnp.maximum), axis=-1, keepdims=True)
    if sink is not None:
        m = jnp.maximum(m, sink)
    probs = [jnp.exp2(s - m) for s in scores]
    den = jnp.sum(fold(probs, jnp.add), axis=-1, keepdims=True)
    if sink is not None:
        den = den + jnp.exp2(sink - m)
    acc = _dot(probs[0].astype(BF16), values[0])
    for p, v in zip(probs[1:], values[1:]):
        acc = acc + _dot(p.astype(BF16), v)
    return acc / den


def _ctx_attn_kernel(sink_ref, p_ref, *refs):
    u_ref, ak_ref, av_ref, bk_ref, bv_ref = refs[-5:]
    def col(c):
        return p_ref[:, c:c + HEAD_DIM]

    def head(cq, k, v, cg, sink, cu):
        q = (col(cq) * Q_SCALE_LOG2).astype(BF16)
        o = _softmax_pv([_dot_nt(q, k)], [v], None if sink is None else sink * LOG2_E)
        u_ref[:, cu:cu + HEAD_DIM] = (o * _silu(col(cg))).astype(BF16)

    for n in range(A_KV_HEADS):
        k32 = col(COL_KA + n * HEAD_DIM)
        v32 = col(COL_VA + n * HEAD_DIM)
        ak_ref[n] = k32
        av_ref[n] = v32
        k = k32.astype(BF16)
        v = v32.astype(BF16)
        for g in range(A_GROUP):
            h = n * A_GROUP + g
            head(COL_QA + h * HEAD_DIM, k, v, COL_GATE + h * HEAD_DIM, sink_ref[h], h * HEAD_DIM)
    for h in range(B_HEADS):
        k32 = col(COL_KB + h * HEAD_DIM)
        v32 = col(COL_VB + h * HEAD_DIM)
        bk_ref[h] = k32
        bv_ref[h] = v32
        head(COL_QB + h * HEAD_DIM, k32.astype(BF16), v32.astype(BF16),
             COL_GATE + A_Q + h * HEAD_DIM, None, A_Q + h * HEAD_DIM)


def _ctx_attn(proj, sink, idx, prev_caches):
    blocks = (SEQ * EVEN_IN * 4 + SEQ * EVEN_WIDTH * 2
              + 2 * (A_KV_HEADS + B_HEADS) * SEQ * HEAD_DIM * 4)
    heads = (A_KV_HEADS, A_KV_HEADS, B_HEADS, B_HEADS)
    cache = lambda nh: jax.ShapeDtypeStruct((BATCH, N_EVEN, nh, SEQ, HEAD_DIM), F32)
    cache_spec = lambda nh: pl.BlockSpec((None, None, nh, SEQ, HEAD_DIM), lambda b: (b, idx, 0, 0, 0))
    n_prev = len(prev_caches)
    return pl.pallas_call(
        _ctx_attn_kernel,
        out_shape=(jax.ShapeDtypeStruct((NP, EVEN_WIDTH), BF16),) + tuple(cache(nh) for nh in heads),
        grid=(BATCH,),
        in_specs=[
            pl.BlockSpec(memory_space=pltpu.SMEM),
            pl.BlockSpec((SEQ, EVEN_IN), lambda b: (b, 0)),
        ] + [_any_spec() for _ in prev_caches],
        out_specs=(pl.BlockSpec((SEQ, EVEN_WIDTH), lambda b: (b, 0)),) + tuple(cache_spec(nh) for nh in heads),
        input_output_aliases={2 + k: 1 + k for k in range(n_prev)},
        compiler_params=_params(_vmem_limit(blocks, temp_bytes=8 * SEQ * SEQ * 4), 1),
        name=f"ctx_attn_{idx}",
    )(sink, proj, *prev_caches)


WIN_PAD = DEC_SEQ + 2 * A_BLOCK


def _rope_tables():
    t = jnp.arange(DEC_SEQ)
    half = HEAD_DIM // 2
    nf = half // 2
    inv = ROPE_THETA ** (-jnp.arange(nf, dtype=F32) / nf)
    ang_r = (t // GRID_W).astype(F32)[:, None] * inv[None]
    ang_c = (t % GRID_W).astype(F32)[:, None] * inv[None]
    cos = jnp.concatenate([jnp.cos(ang_r)] * 2 + [jnp.cos(ang_c)] * 2, axis=-1)
    sin = jnp.concatenate([-jnp.sin(ang_r), jnp.sin(ang_r), -jnp.sin(ang_c), jnp.sin(ang_c)], axis=-1)
    return cos, sin


def _rope(x, cos, sin):
    quarter = HEAD_DIM // 4
    lane = lax.broadcasted_iota(jnp.int32, x.shape, 1)
    first = (lane & (2 * quarter - 1)) < quarter
    partner = jnp.where(first, pltpu.roll(x, HEAD_DIM - quarter, 1), pltpu.roll(x, quarter, 1))
    return x * cos + partner * sin


def _win_attn_kernel(sink_ref, q_ref, k_ref, v_ref, ck_ref, cv_ref, gate_ref, cos_ref, sin_ref,
                     u_ref, kpad_ref, vpad_ref):
    n = pl.program_id(1)
    nqb = DEC_SEQ // A_BLOCK
    rows = A_GROUP * A_BLOCK

    zeros = jnp.zeros((A_BLOCK, HEAD_DIM), BF16)
    for ref in (kpad_ref, vpad_ref):
        ref[0:A_BLOCK, :] = zeros
        ref[A_BLOCK + DEC_SEQ:WIN_PAD, :] = zeros
    kpad_ref[A_BLOCK:A_BLOCK + DEC_SEQ, :] = _rope(k_ref[...], cos_ref[...], sin_ref[...]).astype(BF16)
    vpad_ref[A_BLOCK:A_BLOCK + DEC_SEQ, :] = v_ref[...].astype(BF16)
    ck = ck_ref[...].astype(BF16)
    cv = cv_ref[...].astype(BF16)

    row = lax.broadcasted_iota(jnp.int32, (rows, 1), 0)
    qi = row & (A_BLOCK - 1)
    kk = lax.broadcasted_iota(jnp.int32, (rows, 3 * A_BLOCK), 1)
    head = lax.shift_right_logical(row, A_BLOCK.bit_length() - 1)
    sink = jnp.zeros((rows, 1), F32)
    for g in range(A_GROUP):
        sink = jnp.where(head == g, sink_ref[n * A_GROUP + g] * LOG2_E, sink)

    for j in range(nqb):
        blk = slice(j * A_BLOCK, (j + 1) * A_BLOCK)
        q = jnp.concatenate(
            [(_rope(q_ref[blk, g * HEAD_DIM:(g + 1) * HEAD_DIM], cos_ref[blk, :], sin_ref[blk, :])
              * Q_SCALE_LOG2).astype(BF16) for g in range(A_GROUP)],
            axis=0)
        band = slice(j * A_BLOCK, (j + 3) * A_BLOCK)
        s_c = _dot_nt(q, ck)
        s_w = _dot_nt(q, kpad_ref[band, :])
        lower = jnp.maximum(qi, A_BLOCK if j == 0 else 0)
        upper = jnp.minimum(qi + 2 * A_WINDOW, (2 if j == nqb - 1 else 3) * A_BLOCK - 1)
        s_w = jnp.where((kk >= lower) & (kk <= upper), s_w, NEG_INF)
        o = _softmax_pv([s_c, s_w], [cv, vpad_ref[band, :]], sink)
        for g in range(A_GROUP):
            cols = slice(g * HEAD_DIM, (g + 1) * HEAD_DIM)
            u_ref[blk, cols] = (o[g * A_BLOCK:(g + 1) * A_BLOCK] * _silu(gate_ref[blk, cols])).astype(BF16)


def _win_attn(proj, sink, cache_k, cache_v, idx, cos, sin):
    gw = A_GROUP * HEAD_DIM
    blocks = (2 * DEC_SEQ * gw * 4 + 2 * DEC_SEQ * HEAD_DIM * 4 + 2 * PAST_LEN * HEAD_DIM * 4
              + 2 * DEC_SEQ * HEAD_DIM * 4 + DEC_SEQ * gw * 2)
    cache_spec = pl.BlockSpec((None, None, None, PAST_LEN, HEAD_DIM), lambda b, n: (b, idx, n, 0, 0))
    table_spec = pl.BlockSpec((DEC_SEQ, HEAD_DIM), lambda b, n: (0, 0))
    return pl.pallas_call(
        _win_attn_kernel,
        out_shape=jax.ShapeDtypeStruct((NS, A_Q), BF16),
        grid=(DEC_BATCH, A_KV_HEADS),
        in_specs=[
            pl.BlockSpec(memory_space=pltpu.SMEM),
            pl.BlockSpec((DEC_SEQ, gw), lambda b, n: (b, COL_QA // gw + n)),
            pl.BlockSpec((DEC_SEQ, HEAD_DIM), lambda b, n: (b, COL_KA // HEAD_DIM + n)),
            pl.BlockSpec((DEC_SEQ, HEAD_DIM), lambda b, n: (b, COL_VA // HEAD_DIM + n)),
            cache_spec,
            cache_spec,
            pl.BlockSpec((DEC_SEQ, gw), lambda b, n: (b, COL_GATE // gw + n)),
            table_spec,
            table_spec,
        ],
        out_specs=pl.BlockSpec((DEC_SEQ, gw), lambda b, n: (b, n)),
        scratch_shapes=[pltpu.VMEM((WIN_PAD, HEAD_DIM), BF16), pltpu.VMEM((WIN_PAD, HEAD_DIM), BF16)],
        compiler_params=_params(
            _vmem_limit(blocks, 2 * WIN_PAD * HEAD_DIM * 2,
                        (DEC_SEQ // A_BLOCK) * 3 * A_GROUP * A_BLOCK * (PAST_LEN + 3 * A_BLOCK) * 4), 2),
        name=f"win_attn_{idx}",
    )(sink, proj, proj, proj, cache_k, cache_v, proj, cos, sin)


RPB_PAD = (16, 128)


def _na_row_offsets(g):
    offsets = []
    for rl in range(NA_QROWS):
        r = g * NA_QROWS + rl
        r0 = min(max(r - NA_ROWS // 2, 0), GRID_ROWS - NA_ROWS)
        row = []
        for kl in range(NA_KROWS):
            kr = g * (GRID_ROWS - NA_KROWS) + kl
            row.append(kr - r + NA_ROWS - 1 if r0 <= kr < r0 + NA_ROWS else None)
        offsets.append(row)
    return offsets


def _fill_na_bias(rpb_ref, bias_ref, g):
    shape = (GRID_W, 2 * GRID_W)
    c = lax.broadcasted_iota(jnp.int32, shape, 0)
    lane = lax.broadcasted_iota(jnp.int32, shape, 1)
    kc = lane & (GRID_W - 1)
    c0 = jnp.clip(c - NA_COLS // 2, 0, GRID_W - NA_COLS)
    col_ok = (kc >= c0) & (kc < c0 + NA_COLS)
    low = lane < GRID_W
    offsets = _na_row_offsets(g)
    used = sorted({d for row in offsets for d in row if d is not None})
    lo, hi = {}, {}
    for d in used:
        row = jnp.broadcast_to(rpb_ref[d:d + 1, :] * LOG2_E, shape)
        lo[d] = pltpu.roll(row, 2 * GRID_W - (NA_COLS - 1), 1, stride=1, stride_axis=0)
        hi[d] = pltpu.roll(row, GRID_W - (NA_COLS - 1), 1, stride=1, stride_axis=0)
    neg = jnp.full(shape, NEG_INF, F32)
    for rl in range(NA_QROWS):
        for p in range(NA_KROWS // 2):
            da, db = offsets[rl][2 * p], offsets[rl][2 * p + 1]
            a = neg if da is None else lo[da]
            b = neg if db is None else hi[db]
            piece = jnp.where(col_ok, jnp.where(low, a, b), NEG_INF)
            bias_ref[rl * GRID_W:(rl + 1) * GRID_W, 2 * p * GRID_W:2 * (p + 1) * GRID_W] = piece


def _na_attn_kernel(q_ref, k_ref, v_ref, ck_ref, cv_ref, gate_ref, rpb_ref, u_ref, bias_ref):
    g = pl.program_id(1)

    @pl.when(pl.program_id(2) == 0)
    def _():
        for group in range(DEC_SEQ // NA_Q):
            @pl.when(g == group)
            def _():
                _fill_na_bias(rpb_ref, bias_ref, group)

    k0 = pl.multiple_of(g * NA_KSHIFT, NA_KSHIFT)
    for i in range(NA_REQS):
        q = (q_ref[i] * Q_SCALE_LOG2).astype(BF16)
        s_c = _dot_nt(q, ck_ref[i].astype(BF16))
        s_n = _dot_nt(q, k_ref[i, pl.ds(k0, NA_K), :].astype(BF16)) + bias_ref[...]
        o = _softmax_pv([s_c, s_n], [cv_ref[i].astype(BF16), v_ref[i, pl.ds(k0, NA_K), :].astype(BF16)])
        u_ref[i] = (o * _silu(gate_ref[i])).astype(BF16)


def _na_attn(proj, rpb, cache_k, cache_v, idx):
    per_seq = DEC_SEQ // NA_Q
    proj = proj.reshape(DEC_BATCH, DEC_SEQ, EVEN_IN)
    blocks = NA_REQS * (2 * NA_Q * HEAD_DIM * 4 + 2 * DEC_SEQ * HEAD_DIM * 4 + 2 * PAST_LEN * HEAD_DIM * 4
                        + NA_Q * HEAD_DIM * 2) + RPB_PAD[0] * RPB_PAD[1] * 4
    tok_spec = lambda rows, col0: pl.BlockSpec(
        (NA_REQS, rows, HEAD_DIM), lambda h, g, b: (b, g if rows == NA_Q else 0, col0 // HEAD_DIM + h))
    cache_spec = pl.BlockSpec((NA_REQS, None, None, PAST_LEN, HEAD_DIM), lambda h, g, b: (b, idx, h, 0, 0))
    out = pl.pallas_call(
        _na_attn_kernel,
        out_shape=jax.ShapeDtypeStruct((DEC_BATCH, DEC_SEQ, B_W), BF16),
        grid=(B_HEADS, per_seq, DEC_BATCH // NA_REQS),
        in_specs=[
            tok_spec(NA_Q, COL_QB),
            tok_spec(DEC_SEQ, COL_KB),
            tok_spec(DEC_SEQ, COL_VB),
            cache_spec,
            cache_spec,
            tok_spec(NA_Q, COL_GATE + A_Q),
            pl.BlockSpec((None, None) + RPB_PAD, lambda h, g, b: (idx, h, 0, 0)),
        ],
        out_specs=pl.BlockSpec((NA_REQS, NA_Q, HEAD_DIM), lambda h, g, b: (b, g, h)),
        scratch_shapes=[pltpu.VMEM((NA_Q, NA_K), F32)],
        compiler_params=_params(
            _vmem_limit(blocks, NA_Q * NA_K * 4, NA_REQS * 3 * NA_Q * (NA_K + PAST_LEN) * 4), 3),
        name=f"na_attn_{idx}",
    )(proj, proj, proj, cache_k, cache_v, proj, rpb)
    return out.reshape(NS, B_W)


RET_CHUNK = 256


def _ret_kernel(*refs, seq, heads, has_state, emit_state, n_prev):
    dec_ref, q_ref, k_ref, v_ref, gate_ref, gn_ref = refs[:6]
    pos = 6
    if has_state:
        s0f_ref, s0b_ref = refs[pos:pos + 2]
        pos += 2
    pos += n_prev
    u_ref = refs[pos]
    pos += 1
    if emit_state:
        sf_ref, sb_ref = refs[pos:pos + 2]
        pos += 2
    stf_ref, stb_ref, o_ref, dmat_ref = refs[pos:pos + 4]

    ch = RET_CHUNK
    nc = seq // ch
    kscale = C_DK ** -0.5
    ii = lax.broadcasted_iota(jnp.int32, (ch, ch), 0).astype(F32)
    jj = lax.broadcasted_iota(jnp.int32, (ch, ch), 1).astype(F32)
    icol = lax.broadcasted_iota(jnp.int32, (ch, 1), 0).astype(F32)

    def decay_matrix(direction, head):
        log_g = -jnp.exp(jnp.full((ch, ch), dec_ref[direction, head], F32))
        diff = (ii - jj) if direction == 0 else (jj - ii)
        return jnp.where(diff >= 0, jnp.exp(log_g * jnp.maximum(diff, 0.0)), 0.0)

    def decay_vectors(direction, head):
        dec = dec_ref[direction, head]
        log_g_col = -jnp.exp(jnp.full((ch, 1), dec, F32))
        log_g_row = -jnp.exp(jnp.full((1, C_DV), dec, F32))
        if direction == 0:
            q_dec = jnp.exp(log_g_col * (icol + 1.0))
            k_dec = jnp.exp(log_g_col * (ch - 1.0 - icol)) * kscale
        else:
            q_dec = jnp.exp(log_g_col * (ch - icol))
            k_dec = jnp.exp(log_g_col * icol) * kscale
        return q_dec, k_dec, jnp.exp(log_g_row * float(ch))

    def one_head(hh):
        cols = slice(hh * C_DK, (hh + 1) * C_DK)
        head = pl.program_id(1) * heads + hh
        dmat = dmat_ref[head]
        q_dec_f, k_dec_f, chunk_dec_f = decay_vectors(0, head)
        q_dec_b, k_dec_b, chunk_dec_b = decay_vectors(1, head)
        gn = gn_ref[:, cols]

        def rows_of(c):
            return slice(c * ch, (c + 1) * ch)

        def finalize(c, o):
            mu = jnp.mean(o, axis=-1, keepdims=True)
            d = o - mu
            var = jnp.mean(d * d, axis=-1, keepdims=True)
            y = (d * lax.rsqrt(var + EPS)) * gn
            u_ref[rows_of(c), cols] = (y * _silu(gate_ref[rows_of(c), cols])).astype(BF16)

        parked = {}

        def visit(c, value):
            if c not in parked:
                parked[c] = value is not None
                if value is not None:
                    o_ref[rows_of(c), cols] = value
            elif parked[c]:
                first = o_ref[rows_of(c), cols]
                finalize(c, first if value is None else first + value)
            else:
                finalize(c, value)

        def scan_step(state_ref, s0_ref, t, q, k32, v, q_dec, k_dec, chunk_dec):
            update = _dot_tn((k32 * k_dec).astype(BF16), v)
            if t == 0 and not has_state:
                state_ref[hh] = update
                return None
            state = s0_ref[hh] if t == 0 else state_ref[hh]
            state_ref[hh] = state * chunk_dec + update
            return _dot(q, state.astype(BF16)) * q_dec

        for t in range(nc):
            cf, cb = t, nc - 1 - t
            q = q_ref[rows_of(cf), cols].astype(BF16)
            k32 = k_ref[rows_of(cf), cols]
            v = v_ref[rows_of(cf), cols].astype(BF16)
            s = _dot_nt(q, k32.astype(BF16)) * dmat
            val_f = _dot(s.astype(BF16), v)
            cross = scan_step(stf_ref, s0f_ref if has_state else None, t, q, k32, v,
                              q_dec_f, k_dec_f, chunk_dec_f)
            if cross is not None:
                val_f = val_f + cross
            if cb != cf:
                q = q_ref[rows_of(cb), cols].astype(BF16)
                k32 = k_ref[rows_of(cb), cols]
                v = v_ref[rows_of(cb), cols].astype(BF16)
            val_b = scan_step(stb_ref, s0b_ref if has_state else None, t, q, k32, v,
                              q_dec_b, k_dec_b, chunk_dec_b)
            if cb == cf:
                finalize(cf, val_f if val_b is None else val_f + val_b)
            else:
                visit(cf, val_f)
                visit(cb, val_b)

        if emit_state:
            sf_ref[hh] = stf_ref[hh]
            sb_ref[hh] = stb_ref[hh]

    @pl.when(pl.program_id(0) == 0)
    def _():
        for hh in range(heads):
            head = pl.program_id(1) * heads + hh
            dmat_ref[head] = (decay_matrix(0, head) + decay_matrix(1, head)) * kscale

    for hh in range(heads):
        one_head(hh)


RET_HEADS_PROMPT = 8
RET_HEADS_LATENT = 4


def _retention(proj, decays, gn, idx, *, latent, state_f=None, state_b=None, prev_states=()):
    qkv, gate = proj
    seq = DEC_SEQ if latent else SEQ
    nb = DEC_BATCH if latent else BATCH
    heads = RET_HEADS_LATENT if latent else RET_HEADS_PROMPT
    width = heads * C_DK
    groups = C_HEADS // heads
    has_state = latent
    emit_state = not latent
    tok = lambda kind: pl.BlockSpec((seq, width), lambda b, h: (b, kind * groups + h))
    in_specs = [pl.BlockSpec(memory_space=pltpu.SMEM), tok(0), tok(1), tok(2), tok(0),
                pl.BlockSpec((None, 1, width), lambda b, h: (idx, 0, h))]
    args = [decays, qkv, qkv, qkv, gate, gn]
    state_spec = pl.BlockSpec((None, None, heads, C_DK, C_DV), lambda b, h: (b, idx, h, 0, 0))
    if has_state:
        in_specs += [state_spec, state_spec]
        args += [state_f, state_b]
    aliases = {len(args) + k: 1 + k for k in range(len(prev_states))}
    in_specs += [_any_spec() for _ in prev_states]
    args += list(prev_states)
    out_shape = [jax.ShapeDtypeStruct((nb * seq, D_MODEL), BF16)]
    out_specs = [pl.BlockSpec((seq, width), lambda b, h: (b, h))]
    if emit_state:
        st = jax.ShapeDtypeStruct((nb, N_ODD, C_HEADS, C_DK, C_DV), F32)
        out_shape += [st, st]
        out_specs += [state_spec, state_spec]
    blocks = seq * width * (3 * 2 + 4) + width * 4 + 2 * heads * C_DK * C_DV * 4 + seq * width * 2
    scratch = 2 * heads * C_DK * C_DV * 4 + seq * width * 4 + C_HEADS * RET_CHUNK * RET_CHUNK * 4
    return pl.pallas_call(
        functools.partial(_ret_kernel, seq=seq, heads=heads, has_state=has_state, emit_state=emit_state,
                          n_prev=len(prev_states)),
        out_shape=tuple(out_shape),
        grid=(nb, groups),
        in_specs=in_specs,
        out_specs=tuple(out_specs),
        scratch_shapes=[pltpu.VMEM((heads, C_DK, C_DV), F32), pltpu.VMEM((heads, C_DK, C_DV), F32),
                        pltpu.VMEM((seq, width), F32), pltpu.VMEM((C_HEADS, RET_CHUNK, RET_CHUNK), F32)],
        input_output_aliases=aliases,
        compiler_params=_params(_vmem_limit(blocks, scratch, 24 * RET_CHUNK * C_DV * 4), 2),
        name=f"retention_{'latent' if latent else 'prompt'}_{idx}",
    )(*args)


def kernel(x_prompt, x_sample, c, cache_a_k, cache_a_v, cache_b_k, cache_b_v, state_ret_f, state_ret_b,
           c_ctx, w_ada, b_ada, norm_pre, norm_post, w_in_even, w_out_even, a_sink, na_rpb,
           w_in_odd, w_out_odd, ret_decay_f, ret_decay_b, ret_gn):
    xp = x_prompt.reshape(NP, D_MODEL)
    xs = x_sample.reshape(NS, D_MODEL)
    cvec = jnp.concatenate(
        [c_ctx[None, :], c, jnp.zeros((MOD_ROWS - 1 - DEC_BATCH, D_MODEL), F32)], axis=0)
    mods = _adaln(cvec, w_ada, b_ada).reshape(DEPTH, MOD_ROWS, 3, 1, D_MODEL)
    gain_pre = norm_pre.reshape(DEPTH, 1, D_MODEL)
    gain_post = norm_post.reshape(DEPTH, 1, D_MODEL)
    gn = ret_gn.reshape(N_ODD, 1, D_MODEL)
    cos, sin = _rope_tables()
    rpb = jnp.pad(na_rpb, ((0, 0), (0, 0), (0, RPB_PAD[0] - na_rpb.shape[2]),
                           (0, RPB_PAD[1] - na_rpb.shape[3])))
    caches = ()
    states = ()
    hp = _prenorm(xp, gain_pre, mods, 0, False)
    hs = _prenorm(xs, gain_pre, mods, 0, True)
    for layer in range(DEPTH):
        idx = layer // 2
        if layer % 2 == 0:
            proj_p = _inproj(hp, layer, w_in_even, False)
            proj_s = _inproj(hs, layer, w_in_even, True)
            u_p, *caches = _ctx_attn(proj_p, a_sink[idx], idx, caches)
            u_a = _win_attn(proj_s, a_sink[idx], cache_a_k, cache_a_v, idx, cos, sin)
            u_b = _na_attn(proj_s, rpb, cache_b_k, cache_b_v, idx)
            us_p, us_s, w_out = [u_p], [u_a, u_b], w_out_even
        else:
            proj_p = _inproj_split(hp, layer, w_in_odd, False, 3 * D_MODEL)
            proj_s = _inproj_split(hs, layer, w_in_odd, True, 3 * D_MODEL)
            decays = jnp.stack([ret_decay_f[idx], ret_decay_b[idx]], axis=0)
            u_p, *states = _retention(proj_p, decays, gn, idx, latent=False, prev_states=states)
            (u_s,) = _retention(proj_s, decays, gn, idx, latent=True,
                                state_f=state_ret_f, state_b=state_ret_b)
            us_p, us_s = [u_p], [u_s]
            w_out = w_out_odd
        xp, *hp = _outproj(us_p, w_out, xp, gain_post, gain_pre, mods, layer, False)
        xs, *hs = _outproj(us_s, w_out, xs, gain_post, gain_pre, mods, layer, True)
        hp, hs = (hp[0], hs[0]) if hp else (None, None)
    return (xp.reshape(BATCH, SEQ, D_MODEL), xs.reshape(DEC_BATCH, DEC_SEQ, D_MODEL), *caches, *states)
```

```python
import functools

import numpy as np
import jax
import jax.numpy as jnp
from jax import lax
from jax.experimental import pallas as pl
from jax.experimental.pallas import tpu as pltpu

D_MODEL = 2048
BATCH = 16
SEQ = 256
DEPTH = 4
DEC_BATCH = 4
DEC_SEQ = 1024
PAST_LEN = 256
GRID_W = 64
HEAD_DIM = 128
A_HEADS = 8
A_KV_HEADS = 2
A_GROUP = A_HEADS // A_KV_HEADS
A_WINDOW = 128
A_BLOCK = 128
B_HEADS = 8
NA_ROWS = 8
NA_COLS = 16
C_HEADS = 8
C_DK = D_MODEL // C_HEADS
C_DV = D_MODEL // C_HEADS
C_CHUNK = 128
ROPE_THETA = 10000.0
EPS = 1e-6

N_EVEN = (DEPTH + 1) // 2
N_ODD = DEPTH // 2
A_Q = A_HEADS * HEAD_DIM
A_KV = A_KV_HEADS * HEAD_DIM
B_W = B_HEADS * HEAD_DIM
EVEN_WIDTH = A_Q + B_W
EVEN_IN = A_Q + 2 * A_KV + 3 * B_W + EVEN_WIDTH
ODD_IN = 4 * D_MODEL
GRID_ROWS = DEC_SEQ // GRID_W

NP = BATCH * SEQ
NS = DEC_BATCH * DEC_SEQ
MOD_ROWS = 8

COL_QA = 0
COL_KA = A_Q
COL_VA = A_Q + A_KV
COL_QB = A_Q + 2 * A_KV
COL_KB = COL_QB + B_W
COL_VB = COL_KB + B_W
COL_GATE = COL_VB + B_W

NA_QROWS = 8
NA_KROWS = 12
NA_Q = NA_QROWS * GRID_W
NA_K = NA_KROWS * GRID_W
NA_KSHIFT = (GRID_ROWS - NA_KROWS) * GRID_W
NA_REQS = 4

LANES = 128
V7X_VMEM_BYTES = 64 * 1024 * 1024
VMEM_HEADROOM = 4 * 1024 * 1024

F32 = jnp.float32
BF16 = jnp.bfloat16
NEG_INF = float("-inf")


def _vmem_limit(block_bytes, scratch_bytes=0, temp_bytes=0):
    usable = V7X_VMEM_BYTES - 4 * 1024 * 1024
    assert 2 * block_bytes + scratch_bytes + temp_bytes + VMEM_HEADROOM <= usable
    return usable


def _params(vmem_bytes, ndims):
    return pltpu.CompilerParams(dimension_semantics=("arbitrary",) * ndims, vmem_limit_bytes=vmem_bytes)


def _silu(x):
    half = 0.5 * x
    return half + half * jnp.tanh(half)


def _dot(a, b):
    return jnp.dot(a, b, preferred_element_type=F32)


def _dot_nt(a, b):
    return lax.dot_general(a, b, (((1,), (1,)), ((), ())), preferred_element_type=F32)


def _dot_tn(a, b):
    return lax.dot_general(a, b, (((0,), (0,)), ((), ())), preferred_element_type=F32)


def _any_spec():
    return pl.BlockSpec(memory_space=pl.ANY)


ADA_TN = 1024


def _adaln_kernel(c_ref, w_ref, b_ref, o_ref):
    a = _silu(c_ref[...]).astype(BF16)
    o_ref[...] = _dot(a, w_ref[...].astype(BF16)) + b_ref[...]


def _adaln(cvec, w_ada, b_ada):
    n = 3 * D_MODEL
    blocks = MOD_ROWS * D_MODEL * 4 + D_MODEL * ADA_TN * 4 + ADA_TN * 4 + MOD_ROWS * ADA_TN * 4
    return pl.pallas_call(
        _adaln_kernel,
        out_shape=jax.ShapeDtypeStruct((DEPTH, MOD_ROWS, n), F32),
        grid=(DEPTH, n // ADA_TN),
        in_specs=[
            pl.BlockSpec((MOD_ROWS, D_MODEL), lambda l, j: (0, 0)),
            pl.BlockSpec((None, D_MODEL, ADA_TN), lambda l, j: (l, 0, j)),
            pl.BlockSpec((None, 1, ADA_TN), lambda l, j: (l, 0, j)),
        ],
        out_specs=pl.BlockSpec((None, MOD_ROWS, ADA_TN), lambda l, j: (l, 0, j)),
        compiler_params=_params(_vmem_limit(blocks, temp_bytes=D_MODEL * ADA_TN * 2), 2),
        name="adaln",
    )(cvec, w_ada, b_ada.reshape(DEPTH, 1, n))


def _mod_row(tile, tm, latent):
    return 1 + tile // (DEC_SEQ // tm) if latent else 0


NORM_TM = 1024
NORM_ROWS = 32
IN_TN = 512


def _prenorm_kernel(x_ref, g_ref, sh_ref, sc_ref, h_ref):
    gain = g_ref[...]
    one_sc = 1.0 + sc_ref[...]
    sh = sh_ref[...]

    def body(r, carry):
        sl = pl.ds(pl.multiple_of(r * NORM_ROWS, NORM_ROWS), NORM_ROWS)
        x = x_ref[sl, :]
        ms = jnp.mean(x * x, axis=-1, keepdims=True)
        y = (x * lax.rsqrt(ms + EPS)) * gain
        h_ref[sl, :] = (y * one_sc + sh).astype(BF16)
        return carry

    lax.fori_loop(0, NORM_TM // NORM_ROWS, body, 0, unroll=True)


def _prenorm(x, gain, mods, layer, latent):
    ntok = x.shape[0]
    blocks = NORM_TM * D_MODEL * 4 + 3 * D_MODEL * 4 + NORM_TM * D_MODEL * 2
    mod_spec = lambda which: pl.BlockSpec(
        (None, None, None, 1, D_MODEL), lambda i: (layer, _mod_row(i, NORM_TM, latent), which, 0, 0))
    return pl.pallas_call(
        _prenorm_kernel,
        out_shape=jax.ShapeDtypeStruct((ntok, D_MODEL), BF16),
        grid=(ntok // NORM_TM,),
        in_specs=[
            pl.BlockSpec((NORM_TM, D_MODEL), lambda i: (i, 0)),
            pl.BlockSpec((None, 1, D_MODEL), lambda i: (layer, 0, 0)),
            mod_spec(0),
            mod_spec(1),
        ],
        out_specs=pl.BlockSpec((NORM_TM, D_MODEL), lambda i: (i, 0)),
        compiler_params=_params(_vmem_limit(blocks, temp_bytes=4 * NORM_ROWS * D_MODEL * 4), 1),
        name=f"prenorm_l{layer}_{'latent' if latent else 'prompt'}",
    )(x, gain, mods, mods)


def _inproj_kernel(h_ref, w_ref, o_ref):
    o_ref[...] = _dot(h_ref[...], w_ref[...].astype(BF16))


def _inproj(h, layer, w, latent):
    ntok = h.shape[0]
    n = w.shape[2]
    blocks = D_MODEL * IN_TN * 4 + ntok * IN_TN * 4
    return pl.pallas_call(
        _inproj_kernel,
        out_shape=jax.ShapeDtypeStruct((ntok, n), F32),
        grid=(n // IN_TN,),
        in_specs=[
            pl.BlockSpec((ntok, D_MODEL), lambda j: (0, 0), pipeline_mode=pl.Buffered(1)),
            pl.BlockSpec((None, D_MODEL, IN_TN), lambda j: (layer // 2, 0, j)),
        ],
        out_specs=pl.BlockSpec((ntok, IN_TN), lambda j: (0, j)),
        compiler_params=_params(_vmem_limit(blocks, ntok * D_MODEL * 2, D_MODEL * IN_TN * 2), 1),
        name=f"inproj_l{layer}_{'latent' if latent else 'prompt'}",
    )(h, w)


def _inproj_split_kernel(h_ref, w_ref, lo_ref, hi_ref, *, n_lo):
    @pl.when(pl.program_id(0) < n_lo)
    def _():
        lo_ref[...] = _dot(h_ref[...], w_ref[...].astype(BF16)).astype(lo_ref.dtype)

    @pl.when(pl.program_id(0) >= n_lo)
    def _():
        hi_ref[...] = _dot(h_ref[...], w_ref[...].astype(BF16))


def _inproj_split(h, layer, w, latent, lo_width):
    ntok = h.shape[0]
    n = w.shape[2]
    n_lo = lo_width // IN_TN
    assert lo_width % IN_TN == 0 and 0 < n_lo < n // IN_TN
    blocks = D_MODEL * IN_TN * 4 + ntok * IN_TN * (2 + 4)
    return pl.pallas_call(
        functools.partial(_inproj_split_kernel, n_lo=n_lo),
        out_shape=(jax.ShapeDtypeStruct((ntok, lo_width), BF16),
                   jax.ShapeDtypeStruct((ntok, n - lo_width), F32)),
        grid=(n // IN_TN,),
        in_specs=[
            pl.BlockSpec((ntok, D_MODEL), lambda j: (0, 0), pipeline_mode=pl.Buffered(1)),
            pl.BlockSpec((None, D_MODEL, IN_TN), lambda j: (layer // 2, 0, j)),
        ],
        out_specs=(pl.BlockSpec((ntok, IN_TN), lambda j: (0, jnp.minimum(j, n_lo - 1))),
                   pl.BlockSpec((ntok, IN_TN), lambda j: (0, jnp.maximum(j - n_lo, 0)))),
        compiler_params=_params(_vmem_limit(blocks, ntok * D_MODEL * 2, D_MODEL * IN_TN * 2), 1),
        name=f"inproj_l{layer}_{'latent' if latent else 'prompt'}",
    )(h, w)


OUT_TM = 512
OUT_SUB = 128


def _outproj_kernel(*refs, n_pieces, emit_next):
    u_refs = refs[:n_pieces]
    w_refs = refs[n_pieces:2 * n_pieces]
    pos = 2 * n_pieces
    x_ref, gain_ref, gate_ref = refs[pos:pos + 3]
    pos += 3
    if emit_next:
        ngain_ref, nshift_ref, nscale_ref = refs[pos:pos + 3]
        pos += 3
    o_ref = refs[pos]
    pos += 1
    if emit_next:
        h_ref = refs[pos]
        pos += 1
    wbf_ref = refs[pos]

    @pl.when(pl.program_id(0) == 0)
    def _():
        for p, w_ref in enumerate(w_refs):
            wbf_ref[p] = w_ref[...].astype(BF16)

    gated_gain = gate_ref[...] * gain_ref[...]
    if emit_next:
        next_gain = ngain_ref[...] * (1.0 + nscale_ref[...])
        next_shift = nshift_ref[...]
    for r in range(OUT_TM // OUT_SUB):
        rows = slice(r * OUT_SUB, (r + 1) * OUT_SUB)
        out = _dot(u_refs[0][rows, :], wbf_ref[0])
        for p in range(1, n_pieces):
            out = out + _dot(u_refs[p][rows, :], wbf_ref[p])
        ms = jnp.mean(out * out, axis=-1, keepdims=True)
        x_new = x_ref[rows, :] + (out * lax.rsqrt(ms + EPS)) * gated_gain
        o_ref[rows, :] = x_new
        if emit_next:
            ms = jnp.mean(x_new * x_new, axis=-1, keepdims=True)
            h_ref[rows, :] = ((x_new * lax.rsqrt(ms + EPS)) * next_gain + next_shift).astype(BF16)


def _outproj(us, w, x, gain_post, gain_pre, mods, layer, latent):
    ntok = x.shape[0]
    widths = [u.shape[1] for u in us]
    kp = widths[0]
    k = w.shape[1]
    assert all(kw == kp for kw in widths) and kp * len(us) == k
    emit_next = layer + 1 < DEPTH
    blocks = OUT_TM * k * 2 + 2 * OUT_TM * D_MODEL * 4 + 5 * D_MODEL * 4 + emit_next * OUT_TM * D_MODEL * 2
    tile_spec = pl.BlockSpec((OUT_TM, D_MODEL), lambda i: (i, 0))
    gain_spec = lambda l: pl.BlockSpec((None, 1, D_MODEL), lambda i: (l, 0, 0))
    mod_spec = lambda l, which: pl.BlockSpec(
        (None, None, None, 1, D_MODEL), lambda i: (l, _mod_row(i, OUT_TM, latent), which, 0, 0))
    u_specs = [pl.BlockSpec((OUT_TM, kp), lambda i: (i, 0)) for _ in us]
    w_specs = [pl.BlockSpec((None, kp, D_MODEL), lambda i, p=p: (layer // 2, p, 0),
                            pipeline_mode=pl.Buffered(1)) for p in range(len(us))]
    in_specs = u_specs + w_specs + [tile_spec, gain_spec(layer), mod_spec(layer, 2)]
    args = [*us, *([w] * len(us)), x, gain_post, mods]
    out_shape = [jax.ShapeDtypeStruct((ntok, D_MODEL), F32)]
    out_specs = [tile_spec]
    if emit_next:
        in_specs += [gain_spec(layer + 1), mod_spec(layer + 1, 0), mod_spec(layer + 1, 1)]
        args += [gain_pre, mods, mods]
        out_shape.append(jax.ShapeDtypeStruct((ntok, D_MODEL), BF16))
        out_specs.append(tile_spec)
    return pl.pallas_call(
        functools.partial(_outproj_kernel, n_pieces=len(us), emit_next=emit_next),
        out_shape=tuple(out_shape),
        grid=(ntok // OUT_TM,),
        in_specs=in_specs,
        out_specs=tuple(out_specs),
        scratch_shapes=[pltpu.VMEM((len(us), kp, D_MODEL), BF16)],
        compiler_params=_params(
            _vmem_limit(blocks, k * D_MODEL * (4 + 2), 3 * OUT_TM * D_MODEL * 2), 1),
        name=f"outproj_l{layer}_{'latent' if latent else 'prompt'}",
    )(*args)


LOG2_E = 1.4426950408889634
Q_SCALE_LOG2 = HEAD_DIM ** -0.5 * LOG2_E


def _softmax_pv(scores, values, sink=None):
    def fold(blocks, op):
        tiles = [b[:, t:t + LANES] for b in blocks for t in range(0, b.shape[-1], LANES)]
        out = tiles[0]
        for t in tiles[1:]:
            out = op(out, t)
        return out

    m = jnp.max(fold(scores, jnp.maximum), axis=-1, keepdims=True)
    if sink is not None:
        m = jnp.maximum(m, sink)
    probs = [jnp.exp2(s - m) for s in scores]
    den = jnp.sum(fold(probs, jnp.add), axis=-1, keepdims=True)
    if sink is not None:
        den = den + jnp.exp2(sink - m)
    acc = _dot(probs[0].astype(BF16), values[0])
    for p, v in zip(probs[1:], values[1:]):
        acc = acc + _dot(p.astype(BF16), v)
    return acc / den


def _ctx_attn_kernel(sink_ref, p_ref, *refs):
    u_ref, ak_ref, av_ref, bk_ref, bv_ref = refs[-5:]
    def col(c):
        return p_ref[:, c:c + HEAD_DIM]

    def head(cq, k, v, cg, sink, cu):
        q = (col(cq) * Q_SCALE_LOG2).astype(BF16)
        o = _softmax_pv([_dot_nt(q, k)], [v], None if sink is None else sink * LOG2_E)
        u_ref[:, cu:cu + HEAD_DIM] = (o * _silu(col(cg))).astype(BF16)

    for n in range(A_KV_HEADS):
        k32 = col(COL_KA + n * HEAD_DIM)
        v32 = col(COL_VA + n * HEAD_DIM)
        ak_ref[n] = k32
        av_ref[n] = v32
        k = k32.astype(BF16)
        v = v32.astype(BF16)
        for g in range(A_GROUP):
            h = n * A_GROUP + g
            head(COL_QA + h * HEAD_DIM, k, v, COL_GATE + h * HEAD_DIM, sink_ref[h], h * HEAD_DIM)
    for h in range(B_HEADS):
        k32 = col(COL_KB + h * HEAD_DIM)
        v32 = col(COL_VB + h * HEAD_DIM)
        bk_ref[h] = k32
        bv_ref[h] = v32
        head(COL_QB + h * HEAD_DIM, k32.astype(BF16), v32.astype(BF16),
             COL_GATE + A_Q + h * HEAD_DIM, None, A_Q + h * HEAD_DIM)


def _ctx_attn(proj, sink, idx, prev_caches):
    blocks = (SEQ * EVEN_IN * 4 + SEQ * EVEN_WIDTH * 2
              + 2 * (A_KV_HEADS + B_HEADS) * SEQ * HEAD_DIM * 4)
    heads = (A_KV_HEADS, A_KV_HEADS, B_HEADS, B_HEADS)
    cache = lambda nh: jax.ShapeDtypeStruct((BATCH, N_EVEN, nh, SEQ, HEAD_DIM), F32)
    cache_spec = lambda nh: pl.BlockSpec((None, None, nh, SEQ, HEAD_DIM), lambda b: (b, idx, 0, 0, 0))
    n_prev = len(prev_caches)
    return pl.pallas_call(
        _ctx_attn_kernel,
        out_shape=(jax.ShapeDtypeStruct((NP, EVEN_WIDTH), BF16),) + tuple(cache(nh) for nh in heads),
        grid=(BATCH,),
        in_specs=[
            pl.BlockSpec(memory_space=pltpu.SMEM),
            pl.BlockSpec((SEQ, EVEN_IN), lambda b: (b, 0)),
        ] + [_any_spec() for _ in prev_caches],
        out_specs=(pl.BlockSpec((SEQ, EVEN_WIDTH), lambda b: (b, 0)),) + tuple(cache_spec(nh) for nh in heads),
        input_output_aliases={2 + k: 1 + k for k in range(n_prev)},
        compiler_params=_params(_vmem_limit(blocks, temp_bytes=8 * SEQ * SEQ * 4), 1),
        name=f"ctx_attn_{idx}",
    )(sink, proj, *prev_caches)


WIN_PAD = DEC_SEQ + 2 * A_BLOCK


def _rope_tables():
    t = jnp.arange(DEC_SEQ)
    half = HEAD_DIM // 2
    nf = half // 2
    inv = ROPE_THETA ** (-jnp.arange(nf, dtype=F32) / nf)
    ang_r = (t // GRID_W).astype(F32)[:, None] * inv[None]
    ang_c = (t % GRID_W).astype(F32)[:, None] * inv[None]
    cos = jnp.concatenate([jnp.cos(ang_r)] * 2 + [jnp.cos(ang_c)] * 2, axis=-1)
    sin = jnp.concatenate([-jnp.sin(ang_r), jnp.sin(ang_r), -jnp.sin(ang_c), jnp.sin(ang_c)], axis=-1)
    return cos, sin


def _rope(x, cos, sin):
    quarter = HEAD_DIM // 4
    lane = lax.broadcasted_iota(jnp.int32, x.shape, 1)
    first = (lane & (2 * quarter - 1)) < quarter
    partner = jnp.where(first, pltpu.roll(x, HEAD_DIM - quarter, 1), pltpu.roll(x, quarter, 1))
    return x * cos + partner * sin


def _win_attn_kernel(sink_ref, q_ref, k_ref, v_ref, ck_ref, cv_ref, gate_ref, cos_ref, sin_ref,
                     u_ref, kpad_ref, vpad_ref):
    n = pl.program_id(1)
    nqb = DEC_SEQ // A_BLOCK
    rows = A_GROUP * A_BLOCK

    zeros = jnp.zeros((A_BLOCK, HEAD_DIM), BF16)
    for ref in (kpad_ref, vpad_ref):
        ref[0:A_BLOCK, :] = zeros
        ref[A_BLOCK + DEC_SEQ:WIN_PAD, :] = zeros
    kpad_ref[A_BLOCK:A_BLOCK + DEC_SEQ, :] = _rope(k_ref[...], cos_ref[...], sin_ref[...]).astype(BF16)
    vpad_ref[A_BLOCK:A_BLOCK + DEC_SEQ, :] = v_ref[...].astype(BF16)
    ck = ck_ref[...].astype(BF16)
    cv = cv_ref[...].astype(BF16)

    row = lax.broadcasted_iota(jnp.int32, (rows, 1), 0)
    qi = row & (A_BLOCK - 1)
    kk = lax.broadcasted_iota(jnp.int32, (rows, 3 * A_BLOCK), 1)
    head = lax.shift_right_logical(row, A_BLOCK.bit_length() - 1)
    sink = jnp.zeros((rows, 1), F32)
    for g in range(A_GROUP):
        sink = jnp.where(head == g, sink_ref[n * A_GROUP + g] * LOG2_E, sink)

    for j in range(nqb):
        blk = slice(j * A_BLOCK, (j + 1) * A_BLOCK)
        q = jnp.concatenate(
            [(_rope(q_ref[blk, g * HEAD_DIM:(g + 1) * HEAD_DIM], cos_ref[blk, :], sin_ref[blk, :])
              * Q_SCALE_LOG2).astype(BF16) for g in range(A_GROUP)],
            axis=0)
        band = slice(j * A_BLOCK, (j + 3) * A_BLOCK)
        s_c = _dot_nt(q, ck)
        s_w = _dot_nt(q, kpad_ref[band, :])
        lower = jnp.maximum(qi, A_BLOCK if j == 0 else 0)
        upper = jnp.minimum(qi + 2 * A_WINDOW, (2 if j == nqb - 1 else 3) * A_BLOCK - 1)
        s_w = jnp.where((kk >= lower) & (kk <= upper), s_w, NEG_INF)
        o = _softmax_pv([s_c, s_w], [cv, vpad_ref[band, :]], sink)
        for g in range(A_GROUP):
            cols = slice(g * HEAD_DIM, (g + 1) * HEAD_DIM)
            u_ref[blk, cols] = (o[g * A_BLOCK:(g + 1) * A_BLOCK] * _silu(gate_ref[blk, cols])).astype(BF16)


def _win_attn(proj, sink, cache_k, cache_v, idx, cos, sin):
    gw = A_GROUP * HEAD_DIM
    blocks = (2 * DEC_SEQ * gw * 4 + 2 * DEC_SEQ * HEAD_DIM * 4 + 2 * PAST_LEN * HEAD_DIM * 4
              + 2 * DEC_SEQ * HEAD_DIM * 4 + DEC_SEQ * gw * 2)
    cache_spec = pl.BlockSpec((None, None, None, PAST_LEN, HEAD_DIM), lambda b, n: (b, idx, n, 0, 0))
    table_spec = pl.BlockSpec((DEC_SEQ, HEAD_DIM), lambda b, n: (0, 0))
    return pl.pallas_call(
        _win_attn_kernel,
        out_shape=jax.ShapeDtypeStruct((NS, A_Q), BF16),
        grid=(DEC_BATCH, A_KV_HEADS),
        in_specs=[
            pl.BlockSpec(memory_space=pltpu.SMEM),
            pl.BlockSpec((DEC_SEQ, gw), lambda b, n: (b, COL_QA // gw + n)),
            pl.BlockSpec((DEC_SEQ, HEAD_DIM), lambda b, n: (b, COL_KA // HEAD_DIM + n)),
            pl.BlockSpec((DEC_SEQ, HEAD_DIM), lambda b, n: (b, COL_VA // HEAD_DIM + n)),
            cache_spec,
            cache_spec,
            pl.BlockSpec((DEC_SEQ, gw), lambda b, n: (b, COL_GATE // gw + n)),
            table_spec,
            table_spec,
        ],
        out_specs=pl.BlockSpec((DEC_SEQ, gw), lambda b, n: (b, n)),
        scratch_shapes=[pltpu.VMEM((WIN_PAD, HEAD_DIM), BF16), pltpu.VMEM((WIN_PAD, HEAD_DIM), BF16)],
        compiler_params=_params(
            _vmem_limit(blocks, 2 * WIN_PAD * HEAD_DIM * 2,
                        (DEC_SEQ // A_BLOCK) * 3 * A_GROUP * A_BLOCK * (PAST_LEN + 3 * A_BLOCK) * 4), 2),
        name=f"win_attn_{idx}",
    )(sink, proj, proj, proj, cache_k, cache_v, proj, cos, sin)


RPB_PAD = (16, 128)


def _na_row_offsets(g):
    offsets = []
    for rl in range(NA_QROWS):
        r = g * NA_QROWS + rl
        r0 = min(max(r - NA_ROWS // 2, 0), GRID_ROWS - NA_ROWS)
        row = []
        for kl in range(NA_KROWS):
            kr = g * (GRID_ROWS - NA_KROWS) + kl
            row.append(kr - r + NA_ROWS - 1 if r0 <= kr < r0 + NA_ROWS else None)
        offsets.append(row)
    return offsets


def _fill_na_bias(rpb_ref, bias_ref, g):
    shape = (GRID_W, 2 * GRID_W)
    c = lax.broadcasted_iota(jnp.int32, shape, 0)
    lane = lax.broadcasted_iota(jnp.int32, shape, 1)
    kc = lane & (GRID_W - 1)
    c0 = jnp.clip(c - NA_COLS // 2, 0, GRID_W - NA_COLS)
    col_ok = (kc >= c0) & (kc < c0 + NA_COLS)
    low = lane < GRID_W
    offsets = _na_row_offsets(g)
    used = sorted({d for row in offsets for d in row if d is not None})
    lo, hi = {}, {}
    for d in used:
        row = jnp.broadcast_to(rpb_ref[d:d + 1, :] * LOG2_E, shape)
        lo[d] = pltpu.roll(row, 2 * GRID_W - (NA_COLS - 1), 1, stride=1, stride_axis=0)
        hi[d] = pltpu.roll(row, GRID_W - (NA_COLS - 1), 1, stride=1, stride_axis=0)
    neg = jnp.full(shape, NEG_INF, F32)
    for rl in range(NA_QROWS):
        for p in range(NA_KROWS // 2):
            da, db = offsets[rl][2 * p], offsets[rl][2 * p + 1]
            a = neg if da is None else lo[da]
            b = neg if db is None else hi[db]
            piece = jnp.where(col_ok, jnp.where(low, a, b), NEG_INF)
            bias_ref[rl * GRID_W:(rl + 1) * GRID_W, 2 * p * GRID_W:2 * (p + 1) * GRID_W] = piece


def _na_attn_kernel(q_ref, k_ref, v_ref, ck_ref, cv_ref, gate_ref, rpb_ref, u_ref, bias_ref):
    g = pl.program_id(1)

    @pl.when(pl.program_id(2) == 0)
    def _():
        for group in range(DEC_SEQ // NA_Q):
            @pl.when(g == group)
            def _():
                _fill_na_bias(rpb_ref, bias_ref, group)

    k0 = pl.multiple_of(g * NA_KSHIFT, NA_KSHIFT)
    for i in range(NA_REQS):
        q = (q_ref[i] * Q_SCALE_LOG2).astype(BF16)
        s_c = _dot_nt(q, ck_ref[i].astype(BF16))
        s_n = _dot_nt(q, k_ref[i, pl.ds(k0, NA_K), :].astype(BF16)) + bias_ref[...]
        o = _softmax_pv([s_c, s_n], [cv_ref[i].astype(BF16), v_ref[i, pl.ds(k0, NA_K), :].astype(BF16)])
        u_ref[i] = (o * _silu(gate_ref[i])).astype(BF16)


def _na_attn(proj, rpb, cache_k, cache_v, idx):
    per_seq = DEC_SEQ // NA_Q
    proj = proj.reshape(DEC_BATCH, DEC_SEQ, EVEN_IN)
    blocks = NA_REQS * (2 * NA_Q * HEAD_DIM * 4 + 2 * DEC_SEQ * HEAD_DIM * 4 + 2 * PAST_LEN * HEAD_DIM * 4
                        + NA_Q * HEAD_DIM * 2) + RPB_PAD[0] * RPB_PAD[1] * 4
    tok_spec = lambda rows, col0: pl.BlockSpec(
        (NA_REQS, rows, HEAD_DIM), lambda h, g, b: (b, g if rows == NA_Q else 0, col0 // HEAD_DIM + h))
    cache_spec = pl.BlockSpec((NA_REQS, None, None, PAST_LEN, HEAD_DIM), lambda h, g, b: (b, idx, h, 0, 0))
    out = pl.pallas_call(
        _na_attn_kernel,
        out_shape=jax.ShapeDtypeStruct((DEC_BATCH, DEC_SEQ, B_W), BF16),
        grid=(B_HEADS, per_seq, DEC_BATCH // NA_REQS),
        in_specs=[
            tok_spec(NA_Q, COL_QB),
            tok_spec(DEC_SEQ, COL_KB),
            tok_spec(DEC_SEQ, COL_VB),
            cache_spec,
            cache_spec,
            tok_spec(NA_Q, COL_GATE + A_Q),
            pl.BlockSpec((None, None) + RPB_PAD, lambda h, g, b: (idx, h, 0, 0)),
        ],
        out_specs=pl.BlockSpec((NA_REQS, NA_Q, HEAD_DIM), lambda h, g, b: (b, g, h)),
        scratch_shapes=[pltpu.VMEM((NA_Q, NA_K), F32)],
        compiler_params=_params(
            _vmem_limit(blocks, NA_Q * NA_K * 4, NA_REQS * 3 * NA_Q * (NA_K + PAST_LEN) * 4), 3),
        name=f"na_attn_{idx}",
    )(proj, proj, proj, cache_k, cache_v, proj, rpb)
    return out.reshape(NS, B_W)


RET_CHUNK = 256


def _ret_kernel(*refs, seq, heads, has_state, emit_state, n_prev):
    dec_ref, q_ref, k_ref, v_ref, gate_ref, gn_ref = refs[:6]
    pos = 6
    if has_state:
        s0f_ref, s0b_ref = refs[pos:pos + 2]
        pos += 2
    pos += n_prev
    u_ref = refs[pos]
    pos += 1
    if emit_state:
        sf_ref, sb_ref = refs[pos:pos + 2]
        pos += 2
    stf_ref, stb_ref, o_ref, dmat_ref = refs[pos:pos + 4]

    ch = RET_CHUNK
    nc = seq // ch
    kscale = C_DK ** -0.5
    ii = lax.broadcasted_iota(jnp.int32, (ch, ch), 0).astype(F32)
    jj = lax.broadcasted_iota(jnp.int32, (ch, ch), 1).astype(F32)
    icol = lax.broadcasted_iota(jnp.int32, (ch, 1), 0).astype(F32)

    def decay_matrix(direction, head):
        log_g = -jnp.exp(jnp.full((ch, ch), dec_ref[direction, head], F32))
        diff = (ii - jj) if direction == 0 else (jj - ii)
        return jnp.where(diff >= 0, jnp.exp(log_g * jnp.maximum(diff, 0.0)), 0.0)

    def decay_vectors(direction, head):
        dec = dec_ref[direction, head]
        log_g_col = -jnp.exp(jnp.full((ch, 1), dec, F32))
        log_g_row = -jnp.exp(jnp.full((1, C_DV), dec, F32))
        if direction == 0:
            q_dec = jnp.exp(log_g_col * (icol + 1.0))
            k_dec = jnp.exp(log_g_col * (ch - 1.0 - icol)) * kscale
        else:
            q_dec = jnp.exp(log_g_col * (ch - icol))
            k_dec = jnp.exp(log_g_col * icol) * kscale
        return q_dec, k_dec, jnp.exp(log_g_row * float(ch))

    def one_head(hh):
        cols = slice(hh * C_DK, (hh + 1) * C_DK)
        head = pl.program_id(1) * heads + hh
        dmat = dmat_ref[head]
        q_dec_f, k_dec_f, chunk_dec_f = decay_vectors(0, head)
        q_dec_b, k_dec_b, chunk_dec_b = decay_vectors(1, head)
        gn = gn_ref[:, cols]

        def rows_of(c):
            return slice(c * ch, (c + 1) * ch)

        def finalize(c, o):
            mu = jnp.mean(o, axis=-1, keepdims=True)
            d = o - mu
            var = jnp.mean(d * d, axis=-1, keepdims=True)
            y = (d * lax.rsqrt(var + EPS)) * gn
            u_ref[rows_of(c), cols] = (y * _silu(gate_ref[rows_of(c), cols])).astype(BF16)

        parked = {}

        def visit(c, value):
            if c not in parked:
                parked[c] = value is not None
                if value is not None:
                    o_ref[rows_of(c), cols] = value
            elif parked[c]:
                first = o_ref[rows_of(c), cols]
                finalize(c, first if value is None else first + value)
            else:
                finalize(c, value)

        def scan_step(state_ref, s0_ref, t, q, k32, v, q_dec, k_dec, chunk_dec):
            update = _dot_tn((k32 * k_dec).astype(BF16), v)
            if t == 0 and not has_state:
                state_ref[hh] = update
                return None
            state = s0_ref[hh] if t == 0 else state_ref[hh]
            state_ref[hh] = state * chunk_dec + update
            return _dot(q, state.astype(BF16)) * q_dec

        for t in range(nc):
            cf, cb = t, nc - 1 - t
            q = q_ref[rows_of(cf), cols].astype(BF16)
            k32 = k_ref[rows_of(cf), cols]
            v = v_ref[rows_of(cf), cols].astype(BF16)
            s = _dot_nt(q, k32.astype(BF16)) * dmat
            val_f = _dot(s.astype(BF16), v)
            cross = scan_step(stf_ref, s0f_ref if has_state else None, t, q, k32, v,
                              q_dec_f, k_dec_f, chunk_dec_f)
            if cross is not None:
                val_f = val_f + cross
            if cb != cf:
                q = q_ref[rows_of(cb), cols].astype(BF16)
                k32 = k_ref[rows_of(cb), cols]
                v = v_ref[rows_of(cb), cols].astype(BF16)
            val_b = scan_step(stb_ref, s0b_ref if has_state else None, t, q, k32, v,
                              q_dec_b, k_dec_b, chunk_dec_b)
            if cb == cf:
                finalize(cf, val_f if val_b is None else val_f + val_b)
            else:
                visit(cf, val_f)
                visit(cb, val_b)

        if emit_state:
            sf_ref[hh] = stf_ref[hh]
            sb_ref[hh] = stb_ref[hh]

    @pl.when(pl.program_id(0) == 0)
    def _():
        for hh in range(heads):
            head = pl.program_id(1) * heads + hh
            dmat_ref[head] = (decay_matrix(0, head) + decay_matrix(1, head)) * kscale

    for hh in range(heads):
        one_head(hh)


RET_HEADS_PROMPT = 8
RET_HEADS_LATENT = 4


def _retention(proj, decays, gn, idx, *, latent, state_f=None, state_b=None, prev_states=()):
    qkv, gate = proj
    seq = DEC_SEQ if latent else SEQ
    nb = DEC_BATCH if latent else BATCH
    heads = RET_HEADS_LATENT if latent else RET_HEADS_PROMPT
    width = heads * C_DK
    groups = C_HEADS // heads
    has_state = latent
    emit_state = not latent
    tok = lambda kind: pl.BlockSpec((seq, width), lambda b, h: (b, kind * groups + h))
    in_specs = [pl.BlockSpec(memory_space=pltpu.SMEM), tok(0), tok(1), tok(2), tok(0),
                pl.BlockSpec((None, 1, width), lambda b, h: (idx, 0, h))]
    args = [decays, qkv, qkv, qkv, gate, gn]
    state_spec = pl.BlockSpec((None, None, heads, C_DK, C_DV), lambda b, h: (b, idx, h, 0, 0))
    if has_state:
        in_specs += [state_spec, state_spec]
        args += [state_f, state_b]
    aliases = {len(args) + k: 1 + k for k in range(len(prev_states))}
    in_specs += [_any_spec() for _ in prev_states]
    args += list(prev_states)
    out_shape = [jax.ShapeDtypeStruct((nb * seq, D_MODEL), BF16)]
    out_specs = [pl.BlockSpec((seq, width), lambda b, h: (b, h))]
    if emit_state:
        st = jax.ShapeDtypeStruct((nb, N_ODD, C_HEADS, C_DK, C_DV), F32)
        out_shape += [st, st]
        out_specs += [state_spec, state_spec]
    blocks = seq * width * (3 * 2 + 4) + width * 4 + 2 * heads * C_DK * C_DV * 4 + seq * width * 2
    scratch = 2 * heads * C_DK * C_DV * 4 + seq * width * 4 + C_HEADS * RET_CHUNK * RET_CHUNK * 4
    return pl.pallas_call(
        functools.partial(_ret_kernel, seq=seq, heads=heads, has_state=has_state, emit_state=emit_state,
                          n_prev=len(prev_states)),
        out_shape=tuple(out_shape),
        grid=(nb, groups),
        in_specs=in_specs,
        out_specs=tuple(out_specs),
        scratch_shapes=[pltpu.VMEM((heads, C_DK, C_DV), F32), pltpu.VMEM((heads, C_DK, C_DV), F32),
                        pltpu.VMEM((seq, width), F32), pltpu.VMEM((C_HEADS, RET_CHUNK, RET_CHUNK), F32)],
        input_output_aliases=aliases,
        compiler_params=_params(_vmem_limit(blocks, scratch, 24 * RET_CHUNK * C_DV * 4), 2),
        name=f"retention_{'latent' if latent else 'prompt'}_{idx}",
    )(*args)


def kernel(x_prompt, x_sample, c, cache_a_k, cache_a_v, cache_b_k, cache_b_v, state_ret_f, state_ret_b,
           c_ctx, w_ada, b_ada, norm_pre, norm_post, w_in_even, w_out_even, a_sink, na_rpb,
           w_in_odd, w_out_odd, ret_decay_f, ret_decay_b, ret_gn):
    xp = x_prompt.reshape(NP, D_MODEL)
    xs = x_sample.reshape(NS, D_MODEL)
    cvec = jnp.concatenate(
        [c_ctx[None, :], c, jnp.zeros((MOD_ROWS - 1 - DEC_BATCH, D_MODEL), F32)], axis=0)
    mods = _adaln(cvec, w_ada, b_ada).reshape(DEPTH, MOD_ROWS, 3, 1, D_MODEL)
    gain_pre = norm_pre.reshape(DEPTH, 1, D_MODEL)
    gain_post = norm_post.reshape(DEPTH, 1, D_MODEL)
    gn = ret_gn.reshape(N_ODD, 1, D_MODEL)
    cos, sin = _rope_tables()
    rpb = jnp.pad(na_rpb, ((0, 0), (0, 0), (0, RPB_PAD[0] - na_rpb.shape[2]),
                           (0, RPB_PAD[1] - na_rpb.shape[3])))
    caches = ()
    states = ()
    hp = _prenorm(xp, gain_pre, mods, 0, False)
    hs = _prenorm(xs, gain_pre, mods, 0, True)
    for layer in range(DEPTH):
        idx = layer // 2
        if layer % 2 == 0:
            proj_p = _inproj(hp, layer, w_in_even, False)
            proj_s = _inproj(hs, layer, w_in_even, True)
            u_p, *caches = _ctx_attn(proj_p, a_sink[idx], idx, caches)
            u_a = _win_attn(proj_s, a_sink[idx], cache_a_k, cache_a_v, idx, cos, sin)
            u_b = _na_attn(proj_s, rpb, cache_b_k, cache_b_v, idx)
            us_p, us_s, w_out = [u_p], [u_a, u_b], w_out_even
        else:
            proj_p = _inproj_split(hp, layer, w_in_odd, False, 3 * D_MODEL)
            proj_s = _inproj_split(hs, layer, w_in_odd, True, 3 * D_MODEL)
            decays = jnp.stack([ret_decay_f[idx], ret_decay_b[idx]], axis=0)
            u_p, *states = _retention(proj_p, decays, gn, idx, latent=False, prev_states=states)
            (u_s,) = _retention(proj_s, decays, gn, idx, latent=True,
                                state_f=state_ret_f, state_b=state_ret_b)
            us_p, us_s = [u_p], [u_s]
            w_out = w_out_odd
        xp, *hp = _outproj(us_p, w_out, xp, gain_post, gain_pre, mods, layer, False)
        xs, *hs = _outproj(us_s, w_out, xs, gain_post, gain_pre, mods, layer, True)
        hp, hs = (hp[0], hs[0]) if hp else (None, None)
    return (xp.reshape(BATCH, SEQ, D_MODEL), xs.reshape(DEC_BATCH, DEC_SEQ, D_MODEL), *caches, *states)
```

```python
import functools

import jax
import jax.numpy as jnp
from jax import lax
from jax.experimental import pallas as pl
from jax.experimental.pallas import tpu as pltpu

D_MODEL = 2048
BATCH = 16
SEQ = 256
DEPTH = 4
DEC_BATCH = 4
DEC_SEQ = 1024
PAST_LEN = 256
GRID_W = 64
HEAD_DIM = 128
A_HEADS = 8
A_KV_HEADS = 2
A_GROUP = A_HEADS // A_KV_HEADS
A_WINDOW = 128
A_BLOCK = 128
B_HEADS = 8
NA_ROWS = 8
NA_COLS = 16
C_HEADS = 8
C_DK = D_MODEL // C_HEADS
C_DV = D_MODEL // C_HEADS
ROPE_THETA = 10000.0
EPS = 1e-6

N_EVEN = (DEPTH + 1) // 2
N_ODD = DEPTH // 2
A_Q = A_HEADS * HEAD_DIM
A_KV = A_KV_HEADS * HEAD_DIM
B_W = B_HEADS * HEAD_DIM
EVEN_WIDTH = A_Q + B_W
EVEN_IN = A_Q + 2 * A_KV + 3 * B_W + EVEN_WIDTH
GRID_ROWS = DEC_SEQ // GRID_W

SUBLANES = 8
LANES = 128
V7X_VMEM_BYTES = 64 * 1024 * 1024
VMEM_LEFT_TO_COMPILER = 4 * 1024 * 1024
VMEM_HEADROOM = 4 * 1024 * 1024

NP = BATCH * SEQ
NS = DEC_BATCH * DEC_SEQ
MOD_ROWS = SUBLANES
assert 1 + DEC_BATCH <= MOD_ROWS

COL_QA = 0
COL_KA = A_Q
COL_VA = A_Q + A_KV
COL_QB = A_Q + 2 * A_KV
COL_KB = COL_QB + B_W
COL_VB = COL_KB + B_W
COL_GATE = COL_VB + B_W

NA_QROWS = 8
NA_KROWS = 12
NA_Q = NA_QROWS * GRID_W
NA_K = NA_KROWS * GRID_W
NA_KSHIFT = (GRID_ROWS - NA_KROWS) * GRID_W

F32 = jnp.float32
BF16 = jnp.bfloat16
NEG_INF = float("-inf")


def _vmem_limit(block_bytes, scratch_bytes=0, temp_bytes=0):
    usable = V7X_VMEM_BYTES - VMEM_LEFT_TO_COMPILER
    assert 2 * block_bytes + scratch_bytes + temp_bytes + VMEM_HEADROOM <= usable
    return usable


def _params(vmem_bytes, ndims):
    return pltpu.CompilerParams(dimension_semantics=("arbitrary",) * ndims, vmem_limit_bytes=vmem_bytes)


def _silu(x):
    half = 0.5 * x
    return half + half * jnp.tanh(half)


def _dot(a, b):
    return jnp.dot(a, b, preferred_element_type=F32)


def _dot_nt(a, b):
    return lax.dot_general(a, b, (((1,), (1,)), ((), ())), preferred_element_type=F32)


def _dot_tn(a, b):
    return lax.dot_general(a, b, (((0,), (0,)), ((), ())), preferred_element_type=F32)


def _any_spec():
    return pl.BlockSpec(memory_space=pl.ANY)


ADA_TN = 1024


def _adaln_kernel(c_ref, w_ref, b_ref, o_ref):
    a = _silu(c_ref[...]).astype(BF16)
    o_ref[...] = _dot(a, w_ref[...].astype(BF16)) + b_ref[...]


def _adaln(cvec, w_ada, b_ada):
    n = 3 * D_MODEL
    blocks = MOD_ROWS * D_MODEL * 4 + D_MODEL * ADA_TN * 4 + ADA_TN * 4 + MOD_ROWS * ADA_TN * 4
    return pl.pallas_call(
        _adaln_kernel,
        out_shape=jax.ShapeDtypeStruct((DEPTH, MOD_ROWS, n), F32),
        grid=(DEPTH, n // ADA_TN),
        in_specs=[
            pl.BlockSpec((MOD_ROWS, D_MODEL), lambda l, j: (0, 0)),
            pl.BlockSpec((None, D_MODEL, ADA_TN), lambda l, j: (l, 0, j)),
            pl.BlockSpec((None, 1, ADA_TN), lambda l, j: (l, 0, j)),
        ],
        out_specs=pl.BlockSpec((None, MOD_ROWS, ADA_TN), lambda l, j: (l, 0, j)),
        compiler_params=_params(_vmem_limit(blocks, temp_bytes=D_MODEL * ADA_TN * 2), 2),
        name="adaln",
    )(cvec, w_ada, b_ada.reshape(DEPTH, 1, n))


def _mod_row(tile, tm, latent):
    return 1 + tile // (DEC_SEQ // tm) if latent else 0


NORM_TM = 512
NORM_ROWS = 32
IN_TN = 512


def _prenorm_kernel(x_ref, g_ref, sh_ref, sc_ref, h_ref):
    gain = g_ref[...]
    one_sc = 1.0 + sc_ref[...]
    sh = sh_ref[...]

    def body(r, carry):
        sl = pl.ds(pl.multiple_of(r * NORM_ROWS, NORM_ROWS), NORM_ROWS)
        x = x_ref[sl, :]
        ms = jnp.mean(x * x, axis=-1, keepdims=True)
        y = (x * lax.rsqrt(ms + EPS)) * gain
        h_ref[sl, :] = (y * one_sc + sh).astype(BF16)
        return carry

    lax.fori_loop(0, NORM_TM // NORM_ROWS, body, 0, unroll=True)


def _prenorm(x, gain, mods, layer, latent):
    ntok = x.shape[0]
    blocks = NORM_TM * D_MODEL * 4 + 3 * D_MODEL * 4 + NORM_TM * D_MODEL * 2
    mod_spec = lambda which: pl.BlockSpec(
        (None, None, None, 1, D_MODEL), lambda i: (layer, _mod_row(i, NORM_TM, latent), which, 0, 0))
    return pl.pallas_call(
        _prenorm_kernel,
        out_shape=jax.ShapeDtypeStruct((ntok, D_MODEL), BF16),
        grid=(ntok // NORM_TM,),
        in_specs=[
            pl.BlockSpec((NORM_TM, D_MODEL), lambda i: (i, 0)),
            pl.BlockSpec((None, 1, D_MODEL), lambda i: (layer, 0, 0)),
            mod_spec(0),
            mod_spec(1),
        ],
        out_specs=pl.BlockSpec((NORM_TM, D_MODEL), lambda i: (i, 0)),
        compiler_params=_params(_vmem_limit(blocks, temp_bytes=4 * NORM_ROWS * D_MODEL * 4), 1),
        name=f"prenorm_l{layer}_{'latent' if latent else 'prompt'}",
    )(x, gain, mods, mods)


def _inproj_kernel(h_ref, w_ref, o_ref):
    o_ref[...] = _dot(h_ref[...], w_ref[...].astype(BF16))


def _inproj(h, layer, w, latent):
    ntok = h.shape[0]
    n = w.shape[2]
    blocks = D_MODEL * IN_TN * 4 + ntok * IN_TN * 4
    return pl.pallas_call(
        _inproj_kernel,
        out_shape=jax.ShapeDtypeStruct((ntok, n), F32),
        grid=(n // IN_TN,),
        in_specs=[
            pl.BlockSpec((ntok, D_MODEL), lambda j: (0, 0), pipeline_mode=pl.Buffered(1)),
            pl.BlockSpec((None, D_MODEL, IN_TN), lambda j: (layer // 2, 0, j)),
        ],
        out_specs=pl.BlockSpec((ntok, IN_TN), lambda j: (0, j)),
        compiler_params=_params(_vmem_limit(blocks, ntok * D_MODEL * 2, D_MODEL * IN_TN * 2), 1),
        name=f"inproj_l{layer}_{'latent' if latent else 'prompt'}",
    )(h, w)


def _inproj_split_kernel(h_ref, w_ref, lo_ref, hi_ref, *, n_lo):
    @pl.when(pl.program_id(0) < n_lo)
    def _():
        lo_ref[...] = _dot(h_ref[...], w_ref[...].astype(BF16)).astype(lo_ref.dtype)

    @pl.when(pl.program_id(0) >= n_lo)
    def _():
        hi_ref[...] = _dot(h_ref[...], w_ref[...].astype(BF16))


def _inproj_split(h, layer, w, latent, lo_width):
    ntok = h.shape[0]
    n = w.shape[2]
    n_lo = lo_width // IN_TN
    assert lo_width % IN_TN == 0 and 0 < n_lo < n // IN_TN
    blocks = D_MODEL * IN_TN * 4 + ntok * IN_TN * (2 + 4)
    return pl.pallas_call(
        functools.partial(_inproj_split_kernel, n_lo=n_lo),
        out_shape=(jax.ShapeDtypeStruct((ntok, lo_width), BF16),
                   jax.ShapeDtypeStruct((ntok, n - lo_width), F32)),
        grid=(n // IN_TN,),
        in_specs=[
            pl.BlockSpec((ntok, D_MODEL), lambda j: (0, 0), pipeline_mode=pl.Buffered(1)),
            pl.BlockSpec((None, D_MODEL, IN_TN), lambda j: (layer // 2, 0, j)),
        ],
        out_specs=(pl.BlockSpec((ntok, IN_TN), lambda j: (0, jnp.minimum(j, n_lo - 1))),
                   pl.BlockSpec((ntok, IN_TN), lambda j: (0, jnp.maximum(j - n_lo, 0)))),
        compiler_params=_params(_vmem_limit(blocks, ntok * D_MODEL * 2, D_MODEL * IN_TN * 2), 1),
        name=f"inproj_l{layer}_{'latent' if latent else 'prompt'}",
    )(h, w)


OUT_TM = 512
OUT_SUB = 256


def _outproj_kernel(*refs, n_pieces, emit_next):
    u_refs = refs[:n_pieces]
    w_refs = refs[n_pieces:2 * n_pieces]
    pos = 2 * n_pieces
    x_ref, gain_ref, gate_ref = refs[pos:pos + 3]
    pos += 3
    if emit_next:
        ngain_ref, nshift_ref, nscale_ref = refs[pos:pos + 3]
        pos += 3
    o_ref = refs[pos]
    pos += 1
    if emit_next:
        h_ref = refs[pos]
        pos += 1
    wbf_ref = refs[pos]

    @pl.when(pl.program_id(0) == 0)
    def _():
        for p, w_ref in enumerate(w_refs):
            wbf_ref[p] = w_ref[...].astype(BF16)

    gated_gain = gate_ref[...] * gain_ref[...]
    if emit_next:
        next_gain = ngain_ref[...] * (1.0 + nscale_ref[...])
        next_shift = nshift_ref[...]
    for r in range(OUT_TM // OUT_SUB):
        rows = slice(r * OUT_SUB, (r + 1) * OUT_SUB)
        out = _dot(u_refs[0][rows, :], wbf_ref[0])
        for p in range(1, n_pieces):
            out = out + _dot(u_refs[p][rows, :], wbf_ref[p])
        ms = jnp.mean(out * out, axis=-1, keepdims=True)
        x_new = x_ref[rows, :] + (out * lax.rsqrt(ms + EPS)) * gated_gain
        o_ref[rows, :] = x_new
        if emit_next:
            ms = jnp.mean(x_new * x_new, axis=-1, keepdims=True)
            h_ref[rows, :] = ((x_new * lax.rsqrt(ms + EPS)) * next_gain + next_shift).astype(BF16)


def _outproj(us, w, x, gain_post, gain_pre, mods, layer, latent):
    ntok = x.shape[0]
    widths = [u.shape[1] for u in us]
    kp = widths[0]
    k = w.shape[1]
    assert all(kw == kp for kw in widths) and kp * len(us) == k
    emit_next = layer + 1 < DEPTH
    blocks = OUT_TM * k * 2 + 2 * OUT_TM * D_MODEL * 4 + 5 * D_MODEL * 4 + emit_next * OUT_TM * D_MODEL * 2
    tile_spec = pl.BlockSpec((OUT_TM, D_MODEL), lambda i: (i, 0))
    gain_spec = lambda l: pl.BlockSpec((None, 1, D_MODEL), lambda i: (l, 0, 0))
    mod_spec = lambda l, which: pl.BlockSpec(
        (None, None, None, 1, D_MODEL), lambda i: (l, _mod_row(i, OUT_TM, latent), which, 0, 0))
    u_specs = [pl.BlockSpec((OUT_TM, kp), lambda i: (i, 0)) for _ in us]
    w_specs = [pl.BlockSpec((None, kp, D_MODEL), lambda i, p=p: (layer // 2, p, 0),
                            pipeline_mode=pl.Buffered(1)) for p in range(len(us))]
    in_specs = u_specs + w_specs + [tile_spec, gain_spec(layer), mod_spec(layer, 2)]
    args = [*us, *([w] * len(us)), x, gain_post, mods]
    out_shape = [jax.ShapeDtypeStruct((ntok, D_MODEL), F32)]
    out_specs = [tile_spec]
    if emit_next:
        in_specs += [gain_spec(layer + 1), mod_spec(layer + 1, 0), mod_spec(layer + 1, 1)]
        args += [gain_pre, mods, mods]
        out_shape.append(jax.ShapeDtypeStruct((ntok, D_MODEL), BF16))
        out_specs.append(tile_spec)
    return pl.pallas_call(
        functools.partial(_outproj_kernel, n_pieces=len(us), emit_next=emit_next),
        out_shape=tuple(out_shape),
        grid=(ntok // OUT_TM,),
        in_specs=in_specs,
        out_specs=tuple(out_specs),
        scratch_shapes=[pltpu.VMEM((len(us), kp, D_MODEL), BF16)],
        compiler_params=_params(
            _vmem_limit(blocks, k * D_MODEL * (4 + 2), 3 * OUT_TM * D_MODEL * 2), 1),
        name=f"outproj_l{layer}_{'latent' if latent else 'prompt'}",
    )(*args)


LOG2_E = 1.4426950408889634
Q_SCALE_LOG2 = HEAD_DIM ** -0.5 * LOG2_E


def _softmax_pv(scores, values, sink=None):
    def fold(blocks, op):
        tiles = [b[:, t:t + LANES] for b in blocks for t in range(0, b.shape[-1], LANES)]
        out = tiles[0]
        for t in tiles[1:]:
            out = op(out, t)
        return out

    m = jnp.max(fold(scores, jnp.maximum), axis=-1, keepdims=True)
    if sink is not None:
        m = jnp.maximum(m, sink)
    probs = [jnp.exp2(s - m) for s in scores]
    den = jnp.sum(fold(probs, jnp.add), axis=-1, keepdims=True)
    if sink is not None:
        den = den + jnp.exp2(sink - m)
    acc = _dot(probs[0].astype(BF16), values[0])
    for p, v in zip(probs[1:], values[1:]):
        acc = acc + _dot(p.astype(BF16), v)
    return acc / den


def _ctx_attn_kernel(sink_ref, p_ref, *refs):
    u_ref, ak_ref, av_ref, bk_ref, bv_ref = refs[-5:]

    def col(c):
        return p_ref[:, c:c + HEAD_DIM]

    def head(cq, k, v, cg, sink, cu):
        q = (col(cq) * Q_SCALE_LOG2).astype(BF16)
        o = _softmax_pv([_dot_nt(q, k)], [v], None if sink is None else sink * LOG2_E)
        u_ref[:, cu:cu + HEAD_DIM] = (o * _silu(col(cg))).astype(BF16)

    for n in range(A_KV_HEADS):
        k32 = col(COL_KA + n * HEAD_DIM)
        v32 = col(COL_VA + n * HEAD_DIM)
        ak_ref[n] = k32
        av_ref[n] = v32
        k = k32.astype(BF16)
        v = v32.astype(BF16)
        for g in range(A_GROUP):
            h = n * A_GROUP + g
            head(COL_QA + h * HEAD_DIM, k, v, COL_GATE + h * HEAD_DIM, sink_ref[h], h * HEAD_DIM)
    for h in range(B_HEADS):
        k32 = col(COL_KB + h * HEAD_DIM)
        v32 = col(COL_VB + h * HEAD_DIM)
        bk_ref[h] = k32
        bv_ref[h] = v32
        head(COL_QB + h * HEAD_DIM, k32.astype(BF16), v32.astype(BF16),
             COL_GATE + A_Q + h * HEAD_DIM, None, A_Q + h * HEAD_DIM)


def _ctx_attn(proj, sink, idx, prev_caches):
    blocks = (SEQ * EVEN_IN * 4 + SEQ * EVEN_WIDTH * 2
              + 2 * (A_KV_HEADS + B_HEADS) * SEQ * HEAD_DIM * 4)
    heads = (A_KV_HEADS, A_KV_HEADS, B_HEADS, B_HEADS)
    cache = lambda nh: jax.ShapeDtypeStruct((BATCH, N_EVEN, nh, SEQ, HEAD_DIM), F32)
    cache_spec = lambda nh: pl.BlockSpec((None, None, nh, SEQ, HEAD_DIM), lambda b: (b, idx, 0, 0, 0))
    n_prev = len(prev_caches)
    return pl.pallas_call(
        _ctx_attn_kernel,
        out_shape=(jax.ShapeDtypeStruct((NP, EVEN_WIDTH), BF16),) + tuple(cache(nh) for nh in heads),
        grid=(BATCH,),
        in_specs=[
            pl.BlockSpec(memory_space=pltpu.SMEM),
            pl.BlockSpec((SEQ, EVEN_IN), lambda b: (b, 0)),
        ] + [_any_spec() for _ in prev_caches],
        out_specs=(pl.BlockSpec((SEQ, EVEN_WIDTH), lambda b: (b, 0)),) + tuple(cache_spec(nh) for nh in heads),
        input_output_aliases={2 + k: 1 + k for k in range(n_prev)},
        compiler_params=_params(_vmem_limit(blocks, temp_bytes=8 * SEQ * SEQ * 4), 1),
        name=f"ctx_attn_{idx}",
    )(sink, proj, *prev_caches)


WIN_PAD = DEC_SEQ + 2 * A_BLOCK


def _rope_tables():
    t = jnp.arange(DEC_SEQ)
    half = HEAD_DIM // 2
    nf = half // 2
    inv = ROPE_THETA ** (-jnp.arange(nf, dtype=F32) / nf)
    ang_r = (t // GRID_W).astype(F32)[:, None] * inv[None]
    ang_c = (t % GRID_W).astype(F32)[:, None] * inv[None]
    cos = jnp.concatenate([jnp.cos(ang_r)] * 2 + [jnp.cos(ang_c)] * 2, axis=-1)
    sin = jnp.concatenate([-jnp.sin(ang_r), jnp.sin(ang_r), -jnp.sin(ang_c), jnp.sin(ang_c)], axis=-1)
    return cos, sin


def _rope(x, cos, sin):
    quarter = HEAD_DIM // 4
    lane = lax.broadcasted_iota(jnp.int32, x.shape, 1)
    first = (lane & (2 * quarter - 1)) < quarter
    partner = jnp.where(first, pltpu.roll(x, HEAD_DIM - quarter, 1), pltpu.roll(x, quarter, 1))
    return x * cos + partner * sin


def _win_attn_kernel(sink_ref, q_ref, k_ref, v_ref, ck_ref, cv_ref, gate_ref, cos_ref, sin_ref,
                     u_ref, kpad_ref, vpad_ref):
    n = pl.program_id(1)
    nqb = DEC_SEQ // A_BLOCK
    rows = A_GROUP * A_BLOCK

    zeros = jnp.zeros((A_BLOCK, HEAD_DIM), BF16)
    for ref in (kpad_ref, vpad_ref):
        ref[0:A_BLOCK, :] = zeros
        ref[A_BLOCK + DEC_SEQ:WIN_PAD, :] = zeros
    kpad_ref[A_BLOCK:A_BLOCK + DEC_SEQ, :] = _rope(k_ref[...], cos_ref[...], sin_ref[...]).astype(BF16)
    vpad_ref[A_BLOCK:A_BLOCK + DEC_SEQ, :] = v_ref[...].astype(BF16)
    ck = ck_ref[...].astype(BF16)
    cv = cv_ref[...].astype(BF16)

    row = lax.broadcasted_iota(jnp.int32, (rows, 1), 0)
    qi = row & (A_BLOCK - 1)
    kk = lax.broadcasted_iota(jnp.int32, (rows, 3 * A_BLOCK), 1)
    head = lax.shift_right_logical(row, A_BLOCK.bit_length() - 1)
    sink = jnp.zeros((rows, 1), F32)
    for g in range(A_GROUP):
        sink = jnp.where(head == g, sink_ref[n * A_GROUP + g] * LOG2_E, sink)

    for j in range(nqb):
        blk = slice(j * A_BLOCK, (j + 1) * A_BLOCK)
        q = jnp.concatenate(
            [(_rope(q_ref[blk, g * HEAD_DIM:(g + 1) * HEAD_DIM], cos_ref[blk, :], sin_ref[blk, :])
              * Q_SCALE_LOG2).astype(BF16) for g in range(A_GROUP)],
            axis=0)
        band = slice(j * A_BLOCK, (j + 3) * A_BLOCK)
        s_c = _dot_nt(q, ck)
        s_w = _dot_nt(q, kpad_ref[band, :])
        lower = jnp.maximum(qi, A_BLOCK if j == 0 else 0)
        upper = jnp.minimum(qi + 2 * A_WINDOW, (2 if j == nqb - 1 else 3) * A_BLOCK - 1)
        s_w = jnp.where((kk >= lower) & (kk <= upper), s_w, NEG_INF)
        o = _softmax_pv([s_c, s_w], [cv, vpad_ref[band, :]], sink)
        for g in range(A_GROUP):
            cols = slice(g * HEAD_DIM, (g + 1) * HEAD_DIM)
            u_ref[blk, cols] = (o[g * A_BLOCK:(g + 1) * A_BLOCK] * _silu(gate_ref[blk, cols])).astype(BF16)


def _win_attn(proj, sink, cache_k, cache_v, idx, cos, sin):
    gw = A_GROUP * HEAD_DIM
    blocks = (2 * DEC_SEQ * gw * 4 + 2 * DEC_SEQ * HEAD_DIM * 4 + 2 * PAST_LEN * HEAD_DIM * 4
              + 2 * DEC_SEQ * HEAD_DIM * 4 + DEC_SEQ * gw * 2)
    cache_spec = pl.BlockSpec((None, None, None, PAST_LEN, HEAD_DIM), lambda b, n: (b, idx, n, 0, 0))
    table_spec = pl.BlockSpec((DEC_SEQ, HEAD_DIM), lambda b, n: (0, 0))
    return pl.pallas_call(
        _win_attn_kernel,
        out_shape=jax.ShapeDtypeStruct((NS, A_Q), BF16),
        grid=(DEC_BATCH, A_KV_HEADS),
        in_specs=[
            pl.BlockSpec(memory_space=pltpu.SMEM),
            pl.BlockSpec((DEC_SEQ, gw), lambda b, n: (b, COL_QA // gw + n)),
            pl.BlockSpec((DEC_SEQ, HEAD_DIM), lambda b, n: (b, COL_KA // HEAD_DIM + n)),
            pl.BlockSpec((DEC_SEQ, HEAD_DIM), lambda b, n: (b, COL_VA // HEAD_DIM + n)),
            cache_spec,
            cache_spec,
            pl.BlockSpec((DEC_SEQ, gw), lambda b, n: (b, COL_GATE // gw + n)),
            table_spec,
            table_spec,
        ],
        out_specs=pl.BlockSpec((DEC_SEQ, gw), lambda b, n: (b, n)),
        scratch_shapes=[pltpu.VMEM((WIN_PAD, HEAD_DIM), BF16), pltpu.VMEM((WIN_PAD, HEAD_DIM), BF16)],
        compiler_params=_params(
            _vmem_limit(blocks, 2 * WIN_PAD * HEAD_DIM * 2,
                        (DEC_SEQ // A_BLOCK) * 3 * A_GROUP * A_BLOCK * (PAST_LEN + 3 * A_BLOCK) * 4), 2),
        name=f"win_attn_{idx}",
    )(sink, proj, proj, proj, cache_k, cache_v, proj, cos, sin)


RPB_PAD = (-(-(2 * NA_ROWS - 1) // SUBLANES) * SUBLANES, -(-(2 * NA_COLS - 1) // LANES) * LANES)


def _na_row_offsets(g):
    offsets = []
    for rl in range(NA_QROWS):
        r = g * NA_QROWS + rl
        r0 = min(max(r - NA_ROWS // 2, 0), GRID_ROWS - NA_ROWS)
        row = []
        for kl in range(NA_KROWS):
            kr = g * (GRID_ROWS - NA_KROWS) + kl
            row.append(kr - r + NA_ROWS - 1 if r0 <= kr < r0 + NA_ROWS else None)
        offsets.append(row)
    return offsets


def _fill_na_bias(rpb_ref, bias_ref, g):
    shape = (GRID_W, 2 * GRID_W)
    c = lax.broadcasted_iota(jnp.int32, shape, 0)
    lane = lax.broadcasted_iota(jnp.int32, shape, 1)
    kc = lane & (GRID_W - 1)
    c0 = jnp.clip(c - NA_COLS // 2, 0, GRID_W - NA_COLS)
    col_ok = (kc >= c0) & (kc < c0 + NA_COLS)
    low = lane < GRID_W
    offsets = _na_row_offsets(g)
    used = sorted({d for row in offsets for d in row if d is not None})
    lo, hi = {}, {}
    for d in used:
        row = jnp.broadcast_to(rpb_ref[d:d + 1, :] * LOG2_E, shape)
        lo[d] = pltpu.roll(row, 2 * GRID_W - (NA_COLS - 1), 1, stride=1, stride_axis=0)
        hi[d] = pltpu.roll(row, GRID_W - (NA_COLS - 1), 1, stride=1, stride_axis=0)
    neg = jnp.full(shape, NEG_INF, F32)
    for rl in range(NA_QROWS):
        for p in range(NA_KROWS // 2):
            da, db = offsets[rl][2 * p], offsets[rl][2 * p + 1]
            a = neg if da is None else lo[da]
            b = neg if db is None else hi[db]
            piece = jnp.where(col_ok, jnp.where(low, a, b), NEG_INF)
            bias_ref[rl * GRID_W:(rl + 1) * GRID_W, 2 * p * GRID_W:2 * (p + 1) * GRID_W] = piece


def _na_attn_kernel(q_ref, k_ref, v_ref, ck_ref, cv_ref, gate_ref, rpb_ref, u_ref, bias_ref):
    n_groups = DEC_SEQ // NA_Q
    for g in range(n_groups):
        _fill_na_bias(rpb_ref, bias_ref.at[g], g)
    for i in range(DEC_BATCH):
        k = k_ref[i].astype(BF16)
        v = v_ref[i].astype(BF16)
        ck = ck_ref[i].astype(BF16)
        cv = cv_ref[i].astype(BF16)
        for g in range(n_groups):
            rows = slice(g * NA_Q, (g + 1) * NA_Q)
            window = slice(g * NA_KSHIFT, g * NA_KSHIFT + NA_K)
            q = (q_ref[i, rows, :] * Q_SCALE_LOG2).astype(BF16)
            s_c = _dot_nt(q, ck)
            s_n = _dot_nt(q, k[window]) + bias_ref[g]
            o = _softmax_pv([s_c, s_n], [cv, v[window]])
            u_ref[i, rows, :] = (o * _silu(gate_ref[i, rows, :])).astype(BF16)


def _na_attn(proj, rpb, cache_k, cache_v, idx):
    n_groups = DEC_SEQ // NA_Q
    proj = proj.reshape(DEC_BATCH, DEC_SEQ, EVEN_IN)
    blocks = DEC_BATCH * (4 * DEC_SEQ * HEAD_DIM * 4 + 2 * PAST_LEN * HEAD_DIM * 4
                          + DEC_SEQ * HEAD_DIM * 2) + RPB_PAD[0] * RPB_PAD[1] * 4
    tok_spec = lambda col0: pl.BlockSpec((DEC_BATCH, DEC_SEQ, HEAD_DIM), lambda h: (0, 0, col0 // HEAD_DIM + h))
    cache_spec = pl.BlockSpec((DEC_BATCH, None, None, PAST_LEN, HEAD_DIM), lambda h: (0, idx, h, 0, 0))
    out = pl.pallas_call(
        _na_attn_kernel,
        out_shape=jax.ShapeDtypeStruct((DEC_BATCH, DEC_SEQ, B_W), BF16),
        grid=(B_HEADS,),
        in_specs=[
            tok_spec(COL_QB),
            tok_spec(COL_KB),
            tok_spec(COL_VB),
            cache_spec,
            cache_spec,
            tok_spec(COL_GATE + A_Q),
            pl.BlockSpec((None, None) + RPB_PAD, lambda h: (idx, h, 0, 0)),
        ],
        out_specs=pl.BlockSpec((DEC_BATCH, DEC_SEQ, HEAD_DIM), lambda h: (0, 0, h)),
        scratch_shapes=[pltpu.VMEM((n_groups, NA_Q, NA_K), F32)],
        compiler_params=_params(
            _vmem_limit(blocks, n_groups * NA_Q * NA_K * 4,
                        DEC_BATCH * n_groups * 2 * NA_Q * (NA_K + PAST_LEN) * 4), 1),
        name=f"na_attn_{idx}",
    )(proj, proj, proj, cache_k, cache_v, proj, rpb)
    return out.reshape(NS, B_W)


RET_CHUNK = 256
assert SEQ % RET_CHUNK == 0 and DEC_SEQ % RET_CHUNK == 0


def _ret_kernel(*refs, seq, heads, has_state, emit_state, n_prev):
    dec_ref, q_ref, k_ref, v_ref, gate_ref, gn_ref = refs[:6]
    pos = 6
    if has_state:
        s0f_ref, s0b_ref = refs[pos:pos + 2]
        pos += 2
    pos += n_prev
    u_ref = refs[pos]
    pos += 1
    if emit_state:
        sf_ref, sb_ref = refs[pos:pos + 2]
        pos += 2
    stf_ref, stb_ref, o_ref, dmat_ref = refs[pos:pos + 4]

    ch = RET_CHUNK
    nc = seq // ch
    kscale = C_DK ** -0.5
    ii = lax.broadcasted_iota(jnp.int32, (ch, ch), 0).astype(F32)
    jj = lax.broadcasted_iota(jnp.int32, (ch, ch), 1).astype(F32)
    icol = lax.broadcasted_iota(jnp.int32, (ch, 1), 0).astype(F32)

    def decay_matrix(direction, head):
        log_g = -jnp.exp(jnp.full((ch, ch), dec_ref[direction, head], F32))
        diff = (ii - jj) if direction == 0 else (jj - ii)
        return jnp.where(diff >= 0, jnp.exp(log_g * jnp.maximum(diff, 0.0)), 0.0)

    def decay_vectors(direction, head):
        dec = dec_ref[direction, head]
        log_g_col = -jnp.exp(jnp.full((ch, 1), dec, F32))
        log_g_row = -jnp.exp(jnp.full((1, C_DV), dec, F32))
        if direction == 0:
            q_dec = jnp.exp(log_g_col * (icol + 1.0))
            k_dec = jnp.exp(log_g_col * (ch - 1.0 - icol)) * kscale
        else:
            q_dec = jnp.exp(log_g_col * (ch - icol))
            k_dec = jnp.exp(log_g_col * icol) * kscale
        return q_dec, k_dec, jnp.exp(log_g_row * float(ch))

    def one_head(hh):
        cols = slice(hh * C_DK, (hh + 1) * C_DK)
        head = pl.program_id(1) * heads + hh
        dmat = dmat_ref[head]
        q_dec_f, k_dec_f, chunk_dec_f = decay_vectors(0, head)
        q_dec_b, k_dec_b, chunk_dec_b = decay_vectors(1, head)
        gn = gn_ref[:, cols]

        def rows_of(c):
            return slice(c * ch, (c + 1) * ch)

        def finalize(c, o):
            mu = jnp.mean(o, axis=-1, keepdims=True)
            d = o - mu
            var = jnp.mean(d * d, axis=-1, keepdims=True)
            y = (d * lax.rsqrt(var + EPS)) * gn
            u_ref[rows_of(c), cols] = (y * _silu(gate_ref[rows_of(c), cols])).astype(BF16)

        parked = {}

        def visit(c, value):
            if c not in parked:
                parked[c] = value is not None
                if value is not None:
                    o_ref[rows_of(c), cols] = value
            elif parked[c]:
                first = o_ref[rows_of(c), cols]
                finalize(c, first if value is None else first + value)
            else:
                finalize(c, value)

        def scan_step(state_ref, s0_ref, t, q, k32, v, q_dec, k_dec, chunk_dec):
            update = _dot_tn((k32 * k_dec).astype(BF16), v)
            if t == 0 and not has_state:
                state_ref[hh] = update
                return None
            state = s0_ref[hh] if t == 0 else state_ref[hh]
            state_ref[hh] = state * chunk_dec + update
            return _dot(q, state.astype(BF16)) * q_dec

        for t in range(nc):
            cf, cb = t, nc - 1 - t
            q = q_ref[rows_of(cf), cols].astype(BF16)
            k32 = k_ref[rows_of(cf), cols]
            v = v_ref[rows_of(cf), cols].astype(BF16)
            s = _dot_nt(q, k32.astype(BF16)) * dmat
            val_f = _dot(s.astype(BF16), v)
            cross = scan_step(stf_ref, s0f_ref if has_state else None, t, q, k32, v,
                              q_dec_f, k_dec_f, chunk_dec_f)
            if cross is not None:
                val_f = val_f + cross
            if cb != cf:
                q = q_ref[rows_of(cb), cols].astype(BF16)
                k32 = k_ref[rows_of(cb), cols]
                v = v_ref[rows_of(cb), cols].astype(BF16)
            val_b = scan_step(stb_ref, s0b_ref if has_state else None, t, q, k32, v,
                              q_dec_b, k_dec_b, chunk_dec_b)
            if cb == cf:
                finalize(cf, val_f if val_b is None else val_f + val_b)
            else:
                visit(cf, val_f)
                visit(cb, val_b)

        if emit_state:
            sf_ref[hh] = stf_ref[hh]
            sb_ref[hh] = stb_ref[hh]

    @pl.when(pl.program_id(0) == 0)
    def _():
        for hh in range(heads):
            head = pl.program_id(1) * heads + hh
            dmat_ref[head] = (decay_matrix(0, head) + decay_matrix(1, head)) * kscale

    for hh in range(heads):
        one_head(hh)


RET_HEADS_PROMPT = 8
RET_HEADS_LATENT = 4


def _retention(proj, decays, gn, idx, *, latent, state_f=None, state_b=None, prev_states=()):
    qkv, gate = proj
    seq = DEC_SEQ if latent else SEQ
    nb = DEC_BATCH if latent else BATCH
    heads = RET_HEADS_LATENT if latent else RET_HEADS_PROMPT
    width = heads * C_DK
    groups = C_HEADS // heads
    has_state = latent
    emit_state = not latent
    tok = lambda kind: pl.BlockSpec((seq, width), lambda b, h: (b, kind * groups + h))
    in_specs = [pl.BlockSpec(memory_space=pltpu.SMEM), tok(0), tok(1), tok(2), tok(0),
                pl.BlockSpec((None, 1, width), lambda b, h: (idx, 0, h))]
    args = [decays, qkv, qkv, qkv, gate, gn]
    state_spec = pl.BlockSpec((None, None, heads, C_DK, C_DV), lambda b, h: (b, idx, h, 0, 0))
    if has_state:
        in_specs += [state_spec, state_spec]
        args += [state_f, state_b]
    aliases = {len(args) + k: 1 + k for k in range(len(prev_states))}
    in_specs += [_any_spec() for _ in prev_states]
    args += list(prev_states)
    out_shape = [jax.ShapeDtypeStruct((nb * seq, D_MODEL), BF16)]
    out_specs = [pl.BlockSpec((seq, width), lambda b, h: (b, h))]
    if emit_state:
        st = jax.ShapeDtypeStruct((nb, N_ODD, C_HEADS, C_DK, C_DV), F32)
        out_shape += [st, st]
        out_specs += [state_spec, state_spec]
    blocks = seq * width * (3 * 2 + 4) + width * 4 + 2 * heads * C_DK * C_DV * 4 + seq * width * 2
    scratch = 2 * heads * C_DK * C_DV * 4 + seq * width * 4 + C_HEADS * RET_CHUNK * RET_CHUNK * 4
    return pl.pallas_call(
        functools.partial(_ret_kernel, seq=seq, heads=heads, has_state=has_state, emit_state=emit_state,
                          n_prev=len(prev_states)),
        out_shape=tuple(out_shape),
        grid=(nb, groups),
        in_specs=in_specs,
        out_specs=tuple(out_specs),
        scratch_shapes=[pltpu.VMEM((heads, C_DK, C_DV), F32), pltpu.VMEM((heads, C_DK, C_DV), F32),
                        pltpu.VMEM((seq, width), F32), pltpu.VMEM((C_HEADS, RET_CHUNK, RET_CHUNK), F32)],
        input_output_aliases=aliases,
        compiler_params=_params(_vmem_limit(blocks, scratch, 24 * RET_CHUNK * C_DV * 4), 2),
        name=f"retention_{'latent' if latent else 'prompt'}_{idx}",
    )(*args)


def kernel(x_prompt, x_sample, c, cache_a_k, cache_a_v, cache_b_k, cache_b_v, state_ret_f, state_ret_b,
           c_ctx, w_ada, b_ada, norm_pre, norm_post, w_in_even, w_out_even, a_sink, na_rpb,
           w_in_odd, w_out_odd, ret_decay_f, ret_decay_b, ret_gn):
    xp = x_prompt.reshape(NP, D_MODEL)
    xs = x_sample.reshape(NS, D_MODEL)
    cvec = jnp.concatenate(
        [c_ctx[None, :], c, jnp.zeros((MOD_ROWS - 1 - DEC_BATCH, D_MODEL), F32)], axis=0)
    mods = _adaln(cvec, w_ada, b_ada).reshape(DEPTH, MOD_ROWS, 3, 1, D_MODEL)
    gain_pre = norm_pre.reshape(DEPTH, 1, D_MODEL)
    gain_post = norm_post.reshape(DEPTH, 1, D_MODEL)
    gn = ret_gn.reshape(N_ODD, 1, D_MODEL)
    cos, sin = _rope_tables()
    rpb = jnp.pad(na_rpb, ((0, 0), (0, 0), (0, RPB_PAD[0] - na_rpb.shape[2]),
                           (0, RPB_PAD[1] - na_rpb.shape[3])))
    caches = ()
    states = ()
    hp = _prenorm(xp, gain_pre, mods, 0, False)
    hs = _prenorm(xs, gain_pre, mods, 0, True)
    for layer in range(DEPTH):
        idx = layer // 2
        if layer % 2 == 0:
            proj_p = _inproj(hp, layer, w_in_even, False)
            proj_s = _inproj(hs, layer, w_in_even, True)
            u_p, *caches = _ctx_attn(proj_p, a_sink[idx], idx, caches)
            u_a = _win_attn(proj_s, a_sink[idx], cache_a_k, cache_a_v, idx, cos, sin)
            u_b = _na_attn(proj_s, rpb, cache_b_k, cache_b_v, idx)
            us_p, us_s, w_out = [u_p], [u_a, u_b], w_out_even
        else:
            proj_p = _inproj_split(hp, layer, w_in_odd, False, 3 * D_MODEL)
            proj_s = _inproj_split(hs, layer, w_in_odd, True, 3 * D_MODEL)
            decays = jnp.stack([ret_decay_f[idx], ret_decay_b[idx]], axis=0)
            u_p, *states = _retention(proj_p, decays, gn, idx, latent=False, prev_states=states)
            (u_s,) = _retention(proj_s, decays, gn, idx, latent=True,
                                state_f=state_ret_f, state_b=state_ret_b)
            us_p, us_s = [u_p], [u_s]
            w_out = w_out_odd
        xp, *hp = _outproj(us_p, w_out, xp, gain_post, gain_pre, mods, layer, False)
        xs, *hs = _outproj(us_s, w_out, xs, gain_post, gain_pre, mods, layer, True)
        hp, hs = (hp[0], hs[0]) if hp else (None, None)
    return (xp.reshape(BATCH, SEQ, D_MODEL), xs.reshape(DEC_BATCH, DEC_SEQ, D_MODEL), *caches, *states)
```

```python
import functools

import jax
import jax.numpy as jnp
from jax import lax
from jax.experimental import pallas as pl
from jax.experimental.pallas import tpu as pltpu

D_MODEL = 2048
BATCH = 16
SEQ = 256
DEPTH = 4
DEC_BATCH = 4
DEC_SEQ = 1024
PAST_LEN = 256
GRID_W = 64
HEAD_DIM = 128
A_HEADS = 8
A_KV_HEADS = 2
A_GROUP = A_HEADS // A_KV_HEADS
A_WINDOW = 128
A_BLOCK = 128
B_HEADS = 8
NA_ROWS = 8
NA_COLS = 16
C_HEADS = 8
C_DK = D_MODEL // C_HEADS
C_DV = D_MODEL // C_HEADS
ROPE_THETA = 10000.0
EPS = 1e-6

N_EVEN = (DEPTH + 1) // 2
N_ODD = DEPTH // 2
A_Q = A_HEADS * HEAD_DIM
A_KV = A_KV_HEADS * HEAD_DIM
B_W = B_HEADS * HEAD_DIM
EVEN_WIDTH = A_Q + B_W
EVEN_IN = A_Q + 2 * A_KV + 3 * B_W + EVEN_WIDTH
GRID_ROWS = DEC_SEQ // GRID_W

SUBLANES = 8
LANES = 128
V7X_VMEM_BYTES = 64 * 1024 * 1024
VMEM_LEFT_TO_COMPILER = 4 * 1024 * 1024
VMEM_HEADROOM = 4 * 1024 * 1024

NP = BATCH * SEQ
NS = DEC_BATCH * DEC_SEQ
MOD_ROWS = SUBLANES
assert 1 + DEC_BATCH <= MOD_ROWS

COL_QA = 0
COL_KA = A_Q
COL_VA = A_Q + A_KV
COL_QB = A_Q + 2 * A_KV
COL_KB = COL_QB + B_W
COL_VB = COL_KB + B_W
COL_GATE = COL_VB + B_W

NA_QROWS = 8
NA_KROWS = 12
NA_Q = NA_QROWS * GRID_W
NA_K = NA_KROWS * GRID_W
NA_KSHIFT = (GRID_ROWS - NA_KROWS) * GRID_W

F32 = jnp.float32
BF16 = jnp.bfloat16
NEG_INF = float("-inf")


def _vmem_limit(block_bytes, scratch_bytes=0, temp_bytes=0):
    usable = V7X_VMEM_BYTES - VMEM_LEFT_TO_COMPILER
    assert 2 * block_bytes + scratch_bytes + temp_bytes + VMEM_HEADROOM <= usable
    return usable


def _params(vmem_bytes, ndims):
    return pltpu.CompilerParams(dimension_semantics=("arbitrary",) * ndims, vmem_limit_bytes=vmem_bytes)


def _silu(x):
    half = 0.5 * x
    return half + half * jnp.tanh(half)


def _dot(a, b):
    return jnp.dot(a, b, preferred_element_type=F32)


def _dot_nt(a, b):
    return lax.dot_general(a, b, (((1,), (1,)), ((), ())), preferred_element_type=F32)


def _dot_tn(a, b):
    return lax.dot_general(a, b, (((0,), (0,)), ((), ())), preferred_element_type=F32)


def _any_spec():
    return pl.BlockSpec(memory_space=pl.ANY)


ADA_TN = 1024


def _adaln_kernel(c_ref, w_ref, b_ref, o_ref):
    a = _silu(c_ref[...]).astype(BF16)
    o_ref[...] = _dot(a, w_ref[...].astype(BF16)) + b_ref[...]


def _adaln(cvec, w_ada, b_ada):
    n = 3 * D_MODEL
    blocks = MOD_ROWS * D_MODEL * 4 + D_MODEL * ADA_TN * 4 + ADA_TN * 4 + MOD_ROWS * ADA_TN * 4
    return pl.pallas_call(
        _adaln_kernel,
        out_shape=jax.ShapeDtypeStruct((DEPTH, MOD_ROWS, n), F32),
        grid=(DEPTH, n // ADA_TN),
        in_specs=[
            pl.BlockSpec((MOD_ROWS, D_MODEL), lambda l, j: (0, 0)),
            pl.BlockSpec((None, D_MODEL, ADA_TN), lambda l, j: (l, 0, j)),
            pl.BlockSpec((None, 1, ADA_TN), lambda l, j: (l, 0, j)),
        ],
        out_specs=pl.BlockSpec((None, MOD_ROWS, ADA_TN), lambda l, j: (l, 0, j)),
        compiler_params=_params(_vmem_limit(blocks, temp_bytes=D_MODEL * ADA_TN * 2), 2),
        name="adaln",
    )(cvec, w_ada, b_ada.reshape(DEPTH, 1, n))


def _mod_row(tile, tm, latent):
    return 1 + tile // (DEC_SEQ // tm) if latent else 0


NORM_TM = 512
NORM_ROWS = 32
IN_TN = 512


def _prenorm_kernel(x_ref, g_ref, sh_ref, sc_ref, h_ref):
    gain = g_ref[...]
    one_sc = 1.0 + sc_ref[...]
    sh = sh_ref[...]

    def body(r, carry):
        sl = pl.ds(pl.multiple_of(r * NORM_ROWS, NORM_ROWS), NORM_ROWS)
        x = x_ref[sl, :]
        ms = jnp.mean(x * x, axis=-1, keepdims=True)
        y = (x * lax.rsqrt(ms + EPS)) * gain
        h_ref[sl, :] = (y * one_sc + sh).astype(BF16)
        return carry

    lax.fori_loop(0, NORM_TM // NORM_ROWS, body, 0, unroll=True)


def _prenorm(x, gain, mods, layer, latent):
    ntok = x.shape[0]
    blocks = NORM_TM * D_MODEL * 4 + 3 * D_MODEL * 4 + NORM_TM * D_MODEL * 2
    mod_spec = lambda which: pl.BlockSpec(
        (None, None, None, 1, D_MODEL), lambda i: (layer, _mod_row(i, NORM_TM, latent), which, 0, 0))
    return pl.pallas_call(
        _prenorm_kernel,
        out_shape=jax.ShapeDtypeStruct((ntok, D_MODEL), BF16),
        grid=(ntok // NORM_TM,),
        in_specs=[
            pl.BlockSpec((NORM_TM, D_MODEL), lambda i: (i, 0)),
            pl.BlockSpec((None, 1, D_MODEL), lambda i: (layer, 0, 0)),
            mod_spec(0),
            mod_spec(1),
        ],
        out_specs=pl.BlockSpec((NORM_TM, D_MODEL), lambda i: (i, 0)),
        compiler_params=_params(_vmem_limit(blocks, temp_bytes=4 * NORM_ROWS * D_MODEL * 4), 1),
        name=f"prenorm_l{layer}_{'latent' if latent else 'prompt'}",
    )(x, gain, mods, mods)


def _inproj_kernel(h_ref, w_ref, o_ref):
    o_ref[...] = _dot(h_ref[...], w_ref[...].astype(BF16))


def _inproj(h, layer, w, latent):
    ntok = h.shape[0]
    n = w.shape[2]
    blocks = D_MODEL * IN_TN * 4 + ntok * IN_TN * 4
    return pl.pallas_call(
        _inproj_kernel,
        out_shape=jax.ShapeDtypeStruct((ntok, n), F32),
        grid=(n // IN_TN,),
        in_specs=[
            pl.BlockSpec((ntok, D_MODEL), lambda j: (0, 0), pipeline_mode=pl.Buffered(1)),
            pl.BlockSpec((None, D_MODEL, IN_TN), lambda j: (layer // 2, 0, j)),
        ],
        out_specs=pl.BlockSpec((ntok, IN_TN), lambda j: (0, j)),
        compiler_params=_params(_vmem_limit(blocks, ntok * D_MODEL * 2, D_MODEL * IN_TN * 2), 1),
        name=f"inproj_l{layer}_{'latent' if latent else 'prompt'}",
    )(h, w)


def _inproj_split_kernel(h_ref, w_ref, lo_ref, hi_ref, *, n_lo):
    @pl.when(pl.program_id(0) < n_lo)
    def _():
        lo_ref[...] = _dot(h_ref[...], w_ref[...].astype(BF16)).astype(lo_ref.dtype)

    @pl.when(pl.program_id(0) >= n_lo)
    def _():
        hi_ref[...] = _dot(h_ref[...], w_ref[...].astype(BF16))


def _inproj_split(h, layer, w, latent, lo_width):
    ntok = h.shape[0]
    n = w.shape[2]
    n_lo = lo_width // IN_TN
    assert lo_width % IN_TN == 0 and 0 < n_lo < n // IN_TN
    blocks = D_MODEL * IN_TN * 4 + ntok * IN_TN * (2 + 4)
    return pl.pallas_call(
        functools.partial(_inproj_split_kernel, n_lo=n_lo),
        out_shape=(jax.ShapeDtypeStruct((ntok, lo_width), BF16),
                   jax.ShapeDtypeStruct((ntok, n - lo_width), F32)),
        grid=(n // IN_TN,),
        in_specs=[
            pl.BlockSpec((ntok, D_MODEL), lambda j: (0, 0), pipeline_mode=pl.Buffered(1)),
            pl.BlockSpec((None, D_MODEL, IN_TN), lambda j: (layer // 2, 0, j)),
        ],
        out_specs=(pl.BlockSpec((ntok, IN_TN), lambda j: (0, jnp.minimum(j, n_lo - 1))),
                   pl.BlockSpec((ntok, IN_TN), lambda j: (0, jnp.maximum(j - n_lo, 0)))),
        compiler_params=_params(_vmem_limit(blocks, ntok * D_MODEL * 2, D_MODEL * IN_TN * 2), 1),
        name=f"inproj_l{layer}_{'latent' if latent else 'prompt'}",
    )(h, w)


OUT_TM = 512
OUT_SUB = 256


def _outproj_kernel(*refs, n_pieces, emit_next):
    u_refs = refs[:n_pieces]
    w_refs = refs[n_pieces:2 * n_pieces]
    pos = 2 * n_pieces
    x_ref, gain_ref, gate_ref = refs[pos:pos + 3]
    pos += 3
    if emit_next:
        ngain_ref, nshift_ref, nscale_ref = refs[pos:pos + 3]
        pos += 3
    o_ref = refs[pos]
    pos += 1
    if emit_next:
        h_ref = refs[pos]
        pos += 1
    wbf_ref = refs[pos]

    @pl.when(pl.program_id(0) == 0)
    def _():
        for p, w_ref in enumerate(w_refs):
            wbf_ref[p] = w_ref[...].astype(BF16)

    gated_gain = gate_ref[...] * gain_ref[...]
    if emit_next:
        next_gain = ngain_ref[...] * (1.0 + nscale_ref[...])
        next_shift = nshift_ref[...]
    for r in range(OUT_TM // OUT_SUB):
        rows = slice(r * OUT_SUB, (r + 1) * OUT_SUB)
        out = _dot(u_refs[0][rows, :], wbf_ref[0])
        for p in range(1, n_pieces):
            out = out + _dot(u_refs[p][rows, :], wbf_ref[p])
        ms = jnp.mean(out * out, axis=-1, keepdims=True)
        x_new = x_ref[rows, :] + (out * lax.rsqrt(ms + EPS)) * gated_gain
        o_ref[rows, :] = x_new
        if emit_next:
            ms = jnp.mean(x_new * x_new, axis=-1, keepdims=True)
            h_ref[rows, :] = ((x_new * lax.rsqrt(ms + EPS)) * next_gain + next_shift).astype(BF16)


def _outproj(us, w, x, gain_post, gain_pre, mods, layer, latent):
    ntok = x.shape[0]
    widths = [u.shape[1] for u in us]
    kp = widths[0]
    k = w.shape[1]
    assert all(kw == kp for kw in widths) and kp * len(us) == k
    emit_next = layer + 1 < DEPTH
    blocks = OUT_TM * k * 2 + 2 * OUT_TM * D_MODEL * 4 + 5 * D_MODEL * 4 + emit_next * OUT_TM * D_MODEL * 2
    tile_spec = pl.BlockSpec((OUT_TM, D_MODEL), lambda i: (i, 0))
    gain_spec = lambda l: pl.BlockSpec((None, 1, D_MODEL), lambda i: (l, 0, 0))
    mod_spec = lambda l, which: pl.BlockSpec(
        (None, None, None, 1, D_MODEL), lambda i: (l, _mod_row(i, OUT_TM, latent), which, 0, 0))
    u_specs = [pl.BlockSpec((OUT_TM, kp), lambda i: (i, 0)) for _ in us]
    w_specs = [pl.BlockSpec((None, kp, D_MODEL), lambda i, p=p: (layer // 2, p, 0),
                            pipeline_mode=pl.Buffered(1)) for p in range(len(us))]
    in_specs = u_specs + w_specs + [tile_spec, gain_spec(layer), mod_spec(layer, 2)]
    args = [*us, *([w] * len(us)), x, gain_post, mods]
    out_shape = [jax.ShapeDtypeStruct((ntok, D_MODEL), F32)]
    out_specs = [tile_spec]
    if emit_next:
        in_specs += [gain_spec(layer + 1), mod_spec(layer + 1, 0), mod_spec(layer + 1, 1)]
        args += [gain_pre, mods, mods]
        out_shape.append(jax.ShapeDtypeStruct((ntok, D_MODEL), BF16))
        out_specs.append(tile_spec)
    return pl.pallas_call(
        functools.partial(_outproj_kernel, n_pieces=len(us), emit_next=emit_next),
        out_shape=tuple(out_shape),
        grid=(ntok // OUT_TM,),
        in_specs=in_specs,
        out_specs=tuple(out_specs),
        scratch_shapes=[pltpu.VMEM((len(us), kp, D_MODEL), BF16)],
        compiler_params=_params(
            _vmem_limit(blocks, k * D_MODEL * (4 + 2), 3 * OUT_TM * D_MODEL * 2), 1),
        name=f"outproj_l{layer}_{'latent' if latent else 'prompt'}",
    )(*args)


LOG2_E = 1.4426950408889634
Q_SCALE_LOG2 = HEAD_DIM ** -0.5 * LOG2_E


def _softmax_pv(scores, values, sink=None):
    def fold(blocks, op):
        tiles = [b[:, t:t + LANES] for b in blocks for t in range(0, b.shape[-1], LANES)]
        out = tiles[0]
        for t in tiles[1:]:
            out = op(out, t)
        return out

    m = jnp.max(fold(scores, jnp.maximum), axis=-1, keepdims=True)
    if sink is not None:
        m = jnp.maximum(m, sink)
    probs = [jnp.exp2(s - m) for s in scores]
    den = jnp.sum(fold(probs, jnp.add), axis=-1, keepdims=True)
    if sink is not None:
        den = den + jnp.exp2(sink - m)
    acc = _dot(probs[0].astype(BF16), values[0])
    for p, v in zip(probs[1:], values[1:]):
        acc = acc + _dot(p.astype(BF16), v)
    return acc / den


CTX_RING = 3


def _ctx_attn_kernel(sink_ref, proj_ref, *refs):
    u_ref, ak_ref, av_ref, bk_ref, bv_ref, ring_ref, sem_ref = refs[-7:]
    b = pl.program_id(0)

    def fetch(req):
        slot = req % CTX_RING
        rows = pl.ds(pl.multiple_of(req * SEQ, SEQ), SEQ)
        return pltpu.make_async_copy(proj_ref.at[rows, :], ring_ref.at[slot], sem_ref.at[slot])

    @pl.when(b == 0)
    def _():
        for req in range(CTX_RING - 1):
            fetch(req).start()

    @pl.when(b + CTX_RING - 1 < BATCH)
    def _():
        fetch(b + CTX_RING - 1).start()

    fetch(b).wait()
    slot = b % CTX_RING

    def col(c):
        return ring_ref[slot, :, c:c + HEAD_DIM]

    def head(cq, k, v, cg, sink, cu):
        q = (col(cq) * Q_SCALE_LOG2).astype(BF16)
        o = _softmax_pv([_dot_nt(q, k)], [v], None if sink is None else sink * LOG2_E)
        u_ref[:, cu:cu + HEAD_DIM] = (o * _silu(col(cg))).astype(BF16)

    for n in range(A_KV_HEADS):
        k32 = col(COL_KA + n * HEAD_DIM)
        v32 = col(COL_VA + n * HEAD_DIM)
        ak_ref[n] = k32
        av_ref[n] = v32
        k = k32.astype(BF16)
        v = v32.astype(BF16)
        for g in range(A_GROUP):
            h = n * A_GROUP + g
            head(COL_QA + h * HEAD_DIM, k, v, COL_GATE + h * HEAD_DIM, sink_ref[h], h * HEAD_DIM)
    for h in range(B_HEADS):
        k32 = col(COL_KB + h * HEAD_DIM)
        v32 = col(COL_VB + h * HEAD_DIM)
        bk_ref[h] = k32
        bv_ref[h] = v32
        head(COL_QB + h * HEAD_DIM, k32.astype(BF16), v32.astype(BF16),
             COL_GATE + A_Q + h * HEAD_DIM, None, A_Q + h * HEAD_DIM)


def _ctx_attn(proj, sink, idx, prev_caches):
    blocks = SEQ * EVEN_WIDTH * 2 + 2 * (A_KV_HEADS + B_HEADS) * SEQ * HEAD_DIM * 4
    ring_bytes = CTX_RING * SEQ * EVEN_IN * 4
    heads = (A_KV_HEADS, A_KV_HEADS, B_HEADS, B_HEADS)
    cache = lambda nh: jax.ShapeDtypeStruct((BATCH, N_EVEN, nh, SEQ, HEAD_DIM), F32)
    cache_spec = lambda nh: pl.BlockSpec((None, None, nh, SEQ, HEAD_DIM), lambda b: (b, idx, 0, 0, 0))
    n_prev = len(prev_caches)
    return pl.pallas_call(
        _ctx_attn_kernel,
        out_shape=(jax.ShapeDtypeStruct((NP, EVEN_WIDTH), BF16),) + tuple(cache(nh) for nh in heads),
        grid=(BATCH,),
        in_specs=[
            pl.BlockSpec(memory_space=pltpu.SMEM),
            _any_spec(),
        ] + [_any_spec() for _ in prev_caches],
        out_specs=(pl.BlockSpec((SEQ, EVEN_WIDTH), lambda b: (b, 0)),) + tuple(cache_spec(nh) for nh in heads),
        scratch_shapes=[pltpu.VMEM((CTX_RING, SEQ, EVEN_IN), F32), pltpu.SemaphoreType.DMA((CTX_RING,))],
        input_output_aliases={2 + k: 1 + k for k in range(n_prev)},
        compiler_params=_params(_vmem_limit(blocks, ring_bytes, 8 * SEQ * SEQ * 4), 1),
        name=f"ctx_attn_{idx}",
    )(sink, proj, *prev_caches)


WIN_PAD = DEC_SEQ + 2 * A_BLOCK


def _rope_tables():
    t = jnp.arange(DEC_SEQ)
    half = HEAD_DIM // 2
    nf = half // 2
    inv = ROPE_THETA ** (-jnp.arange(nf, dtype=F32) / nf)
    ang_r = (t // GRID_W).astype(F32)[:, None] * inv[None]
    ang_c = (t % GRID_W).astype(F32)[:, None] * inv[None]
    cos = jnp.concatenate([jnp.cos(ang_r)] * 2 + [jnp.cos(ang_c)] * 2, axis=-1)
    sin = jnp.concatenate([-jnp.sin(ang_r), jnp.sin(ang_r), -jnp.sin(ang_c), jnp.sin(ang_c)], axis=-1)
    return cos, sin


def _rope(x, cos, sin):
    quarter = HEAD_DIM // 4
    lane = lax.broadcasted_iota(jnp.int32, x.shape, 1)
    first = (lane & (2 * quarter - 1)) < quarter
    partner = jnp.where(first, pltpu.roll(x, HEAD_DIM - quarter, 1), pltpu.roll(x, quarter, 1))
    return x * cos + partner * sin


def _win_attn_kernel(sink_ref, q_ref, k_ref, v_ref, ck_ref, cv_ref, gate_ref, cos_ref, sin_ref,
                     u_ref, kpad_ref, vpad_ref):
    n = pl.program_id(1)
    nqb = DEC_SEQ // A_BLOCK
    rows = A_GROUP * A_BLOCK

    zeros = jnp.zeros((A_BLOCK, HEAD_DIM), BF16)
    for ref in (kpad_ref, vpad_ref):
        ref[0:A_BLOCK, :] = zeros
        ref[A_BLOCK + DEC_SEQ:WIN_PAD, :] = zeros
    kpad_ref[A_BLOCK:A_BLOCK + DEC_SEQ, :] = _rope(k_ref[...], cos_ref[...], sin_ref[...]).astype(BF16)
    vpad_ref[A_BLOCK:A_BLOCK + DEC_SEQ, :] = v_ref[...].astype(BF16)
    ck = ck_ref[...].astype(BF16)
    cv = cv_ref[...].astype(BF16)

    row = lax.broadcasted_iota(jnp.int32, (rows, 1), 0)
    qi = row & (A_BLOCK - 1)
    kk = lax.broadcasted_iota(jnp.int32, (rows, 3 * A_BLOCK), 1)
    head = lax.shift_right_logical(row, A_BLOCK.bit_length() - 1)
    sink = jnp.zeros((rows, 1), F32)
    for g in range(A_GROUP):
        sink = jnp.where(head == g, sink_ref[n * A_GROUP + g] * LOG2_E, sink)

    for j in range(nqb):
        blk = slice(j * A_BLOCK, (j + 1) * A_BLOCK)
        q = jnp.concatenate(
            [(_rope(q_ref[blk, g * HEAD_DIM:(g + 1) * HEAD_DIM], cos_ref[blk, :], sin_ref[blk, :])
              * Q_SCALE_LOG2).astype(BF16) for g in range(A_GROUP)],
            axis=0)
        band = slice(j * A_BLOCK, (j + 3) * A_BLOCK)
        s_c = _dot_nt(q, ck)
        s_w = _dot_nt(q, kpad_ref[band, :])
        lower = jnp.maximum(qi, A_BLOCK if j == 0 else 0)
        upper = jnp.minimum(qi + 2 * A_WINDOW, (2 if j == nqb - 1 else 3) * A_BLOCK - 1)
        s_w = jnp.where((kk >= lower) & (kk <= upper), s_w, NEG_INF)
        o = _softmax_pv([s_c, s_w], [cv, vpad_ref[band, :]], sink)
        for g in range(A_GROUP):
            cols = slice(g * HEAD_DIM, (g + 1) * HEAD_DIM)
            u_ref[blk, cols] = (o[g * A_BLOCK:(g + 1) * A_BLOCK] * _silu(gate_ref[blk, cols])).astype(BF16)


def _win_attn(proj, sink, cache_k, cache_v, idx, cos, sin):
    gw = A_GROUP * HEAD_DIM
    blocks = (2 * DEC_SEQ * gw * 4 + 2 * DEC_SEQ * HEAD_DIM * 4 + 2 * PAST_LEN * HEAD_DIM * 4
              + 2 * DEC_SEQ * HEAD_DIM * 4 + DEC_SEQ * gw * 2)
    cache_spec = pl.BlockSpec((None, None, None, PAST_LEN, HEAD_DIM), lambda b, n: (b, idx, n, 0, 0))
    table_spec = pl.BlockSpec((DEC_SEQ, HEAD_DIM), lambda b, n: (0, 0))
    return pl.pallas_call(
        _win_attn_kernel,
        out_shape=jax.ShapeDtypeStruct((NS, A_Q), BF16),
        grid=(DEC_BATCH, A_KV_HEADS),
        in_specs=[
            pl.BlockSpec(memory_space=pltpu.SMEM),
            pl.BlockSpec((DEC_SEQ, gw), lambda b, n: (b, COL_QA // gw + n)),
            pl.BlockSpec((DEC_SEQ, HEAD_DIM), lambda b, n: (b, COL_KA // HEAD_DIM + n)),
            pl.BlockSpec((DEC_SEQ, HEAD_DIM), lambda b, n: (b, COL_VA // HEAD_DIM + n)),
            cache_spec,
            cache_spec,
            pl.BlockSpec((DEC_SEQ, gw), lambda b, n: (b, COL_GATE // gw + n)),
            table_spec,
            table_spec,
        ],
        out_specs=pl.BlockSpec((DEC_SEQ, gw), lambda b, n: (b, n)),
        scratch_shapes=[pltpu.VMEM((WIN_PAD, HEAD_DIM), BF16), pltpu.VMEM((WIN_PAD, HEAD_DIM), BF16)],
        compiler_params=_params(
            _vmem_limit(blocks, 2 * WIN_PAD * HEAD_DIM * 2,
                        (DEC_SEQ // A_BLOCK) * 3 * A_GROUP * A_BLOCK * (PAST_LEN + 3 * A_BLOCK) * 4), 2),
        name=f"win_attn_{idx}",
    )(sink, proj, proj, proj, cache_k, cache_v, proj, cos, sin)


RPB_PAD = (-(-(2 * NA_ROWS - 1) // SUBLANES) * SUBLANES, -(-(2 * NA_COLS - 1) // LANES) * LANES)


def _na_row_offsets(g):
    offsets = []
    for rl in range(NA_QROWS):
        r = g * NA_QROWS + rl
        r0 = min(max(r - NA_ROWS // 2, 0), GRID_ROWS - NA_ROWS)
        row = []
        for kl in range(NA_KROWS):
            kr = g * (GRID_ROWS - NA_KROWS) + kl
            row.append(kr - r + NA_ROWS - 1 if r0 <= kr < r0 + NA_ROWS else None)
        offsets.append(row)
    return offsets


def _fill_na_bias(rpb_ref, bias_ref, g):
    shape = (GRID_W, 2 * GRID_W)
    c = lax.broadcasted_iota(jnp.int32, shape, 0)
    lane = lax.broadcasted_iota(jnp.int32, shape, 1)
    kc = lane & (GRID_W - 1)
    c0 = jnp.clip(c - NA_COLS // 2, 0, GRID_W - NA_COLS)
    col_ok = (kc >= c0) & (kc < c0 + NA_COLS)
    low = lane < GRID_W
    offsets = _na_row_offsets(g)
    used = sorted({d for row in offsets for d in row if d is not None})
    lo, hi = {}, {}
    for d in used:
        row = jnp.broadcast_to(rpb_ref[d:d + 1, :] * LOG2_E, shape)
        lo[d] = pltpu.roll(row, 2 * GRID_W - (NA_COLS - 1), 1, stride=1, stride_axis=0)
        hi[d] = pltpu.roll(row, GRID_W - (NA_COLS - 1), 1, stride=1, stride_axis=0)
    neg = jnp.full(shape, NEG_INF, F32)
    for rl in range(NA_QROWS):
        for p in range(NA_KROWS // 2):
            da, db = offsets[rl][2 * p], offsets[rl][2 * p + 1]
            a = neg if da is None else lo[da]
            b = neg if db is None else hi[db]
            piece = jnp.where(col_ok, jnp.where(low, a, b), NEG_INF)
            bias_ref[rl * GRID_W:(rl + 1) * GRID_W, 2 * p * GRID_W:2 * (p + 1) * GRID_W] = piece


def _na_attn_kernel(q_ref, k_ref, v_ref, ck_ref, cv_ref, gate_ref, rpb_ref, u_ref, bias_ref):
    n_groups = DEC_SEQ // NA_Q
    for g in range(n_groups):
        _fill_na_bias(rpb_ref, bias_ref.at[g], g)
    for i in range(DEC_BATCH):
        k = k_ref[i].astype(BF16)
        v = v_ref[i].astype(BF16)
        ck = ck_ref[i].astype(BF16)
        cv = cv_ref[i].astype(BF16)
        for g in range(n_groups):
            rows = slice(g * NA_Q, (g + 1) * NA_Q)
            window = slice(g * NA_KSHIFT, g * NA_KSHIFT + NA_K)
            q = (q_ref[i, rows, :] * Q_SCALE_LOG2).astype(BF16)
            s_c = _dot_nt(q, ck)
            s_n = _dot_nt(q, k[window]) + bias_ref[g]
            o = _softmax_pv([s_c, s_n], [cv, v[window]])
            u_ref[i, rows, :] = (o * _silu(gate_ref[i, rows, :])).astype(BF16)


def _na_attn(proj, rpb, cache_k, cache_v, idx):
    n_groups = DEC_SEQ // NA_Q
    proj = proj.reshape(DEC_BATCH, DEC_SEQ, EVEN_IN)
    blocks = DEC_BATCH * (4 * DEC_SEQ * HEAD_DIM * 4 + 2 * PAST_LEN * HEAD_DIM * 4
                          + DEC_SEQ * HEAD_DIM * 2) + RPB_PAD[0] * RPB_PAD[1] * 4
    tok_spec = lambda col0: pl.BlockSpec((DEC_BATCH, DEC_SEQ, HEAD_DIM), lambda h: (0, 0, col0 // HEAD_DIM + h))
    cache_spec = pl.BlockSpec((DEC_BATCH, None, None, PAST_LEN, HEAD_DIM), lambda h: (0, idx, h, 0, 0))
    out = pl.pallas_call(
        _na_attn_kernel,
        out_shape=jax.ShapeDtypeStruct((DEC_BATCH, DEC_SEQ, B_W), BF16),
        grid=(B_HEADS,),
        in_specs=[
            tok_spec(COL_QB),
            tok_spec(COL_KB),
            tok_spec(COL_VB),
            cache_spec,
            cache_spec,
            tok_spec(COL_GATE + A_Q),
            pl.BlockSpec((None, None) + RPB_PAD, lambda h: (idx, h, 0, 0)),
        ],
        out_specs=pl.BlockSpec((DEC_BATCH, DEC_SEQ, HEAD_DIM), lambda h: (0, 0, h)),
        scratch_shapes=[pltpu.VMEM((n_groups, NA_Q, NA_K), F32)],
        compiler_params=_params(
            _vmem_limit(blocks, n_groups * NA_Q * NA_K * 4,
                        DEC_BATCH * n_groups * 2 * NA_Q * (NA_K + PAST_LEN) * 4), 1),
        name=f"na_attn_{idx}",
    )(proj, proj, proj, cache_k, cache_v, proj, rpb)
    return out.reshape(NS, B_W)


RET_CHUNK = 256
assert SEQ % RET_CHUNK == 0 and DEC_SEQ % RET_CHUNK == 0


def _ret_kernel(*refs, seq, heads, has_state, emit_state, n_prev):
    dec_ref, q_ref, k_ref, v_ref, gate_ref, gn_ref = refs[:6]
    pos = 6
    if has_state:
        s0f_ref, s0b_ref = refs[pos:pos + 2]
        pos += 2
    pos += n_prev
    u_ref = refs[pos]
    pos += 1
    if emit_state:
        sf_ref, sb_ref = refs[pos:pos + 2]
        pos += 2
    stf_ref, stb_ref, o_ref, dmat_ref = refs[pos:pos + 4]

    ch = RET_CHUNK
    nc = seq // ch
    kscale = C_DK ** -0.5
    ii = lax.broadcasted_iota(jnp.int32, (ch, ch), 0).astype(F32)
    jj = lax.broadcasted_iota(jnp.int32, (ch, ch), 1).astype(F32)
    icol = lax.broadcasted_iota(jnp.int32, (ch, 1), 0).astype(F32)

    def decay_matrix(direction, head):
        log_g = -jnp.exp(jnp.full((ch, ch), dec_ref[direction, head], F32))
        diff = (ii - jj) if direction == 0 else (jj - ii)
        return jnp.where(diff >= 0, jnp.exp(log_g * jnp.maximum(diff, 0.0)), 0.0)

    def decay_vectors(direction, head):
        dec = dec_ref[direction, head]
        log_g_col = -jnp.exp(jnp.full((ch, 1), dec, F32))
        log_g_row = -jnp.exp(jnp.full((1, C_DV), dec, F32))
        if direction == 0:
            q_dec = jnp.exp(log_g_col * (icol + 1.0))
            k_dec = jnp.exp(log_g_col * (ch - 1.0 - icol)) * kscale
        else:
            q_dec = jnp.exp(log_g_col * (ch - icol))
            k_dec = jnp.exp(log_g_col * icol) * kscale
        return q_dec, k_dec, jnp.exp(log_g_row * float(ch))

    def one_head(hh):
        cols = slice(hh * C_DK, (hh + 1) * C_DK)
        head = pl.program_id(1) * heads + hh
        dmat = dmat_ref[head]
        q_dec_f, k_dec_f, chunk_dec_f = decay_vectors(0, head)
        q_dec_b, k_dec_b, chunk_dec_b = decay_vectors(1, head)
        gn = gn_ref[:, cols]

        def rows_of(c):
            return slice(c * ch, (c + 1) * ch)

        def finalize(c, o):
            mu = jnp.mean(o, axis=-1, keepdims=True)
            d = o - mu
            var = jnp.mean(d * d, axis=-1, keepdims=True)
            y = (d * lax.rsqrt(var + EPS)) * gn
            u_ref[rows_of(c), cols] = (y * _silu(gate_ref[rows_of(c), cols])).astype(BF16)

        parked = {}

        def visit(c, value):
            if c not in parked:
                parked[c] = value is not None
                if value is not None:
                    o_ref[rows_of(c), cols] = value
            elif parked[c]:
                first = o_ref[rows_of(c), cols]
                finalize(c, first if value is None else first + value)
            else:
                finalize(c, value)

        def scan_step(state_ref, s0_ref, t, q, k32, v, q_dec, k_dec, chunk_dec):
            update = _dot_tn((k32 * k_dec).astype(BF16), v)
            if t == 0 and not has_state:
                state_ref[hh] = update
                return None
            state = s0_ref[hh] if t == 0 else state_ref[hh]
            state_ref[hh] = state * chunk_dec + update
            return _dot(q, state.astype(BF16)) * q_dec

        for t in range(nc):
            cf, cb = t, nc - 1 - t
            q = q_ref[rows_of(cf), cols].astype(BF16)
            k32 = k_ref[rows_of(cf), cols]
            v = v_ref[rows_of(cf), cols].astype(BF16)
            s = _dot_nt(q, k32.astype(BF16)) * dmat
            val_f = _dot(s.astype(BF16), v)
            cross = scan_step(stf_ref, s0f_ref if has_state else None, t, q, k32, v,
                              q_dec_f, k_dec_f, chunk_dec_f)
            if cross is not None:
                val_f = val_f + cross
            if cb != cf:
                q = q_ref[rows_of(cb), cols].astype(BF16)
                k32 = k_ref[rows_of(cb), cols]
                v = v_ref[rows_of(cb), cols].astype(BF16)
            val_b = scan_step(stb_ref, s0b_ref if has_state else None, t, q, k32, v,
                              q_dec_b, k_dec_b, chunk_dec_b)
            if cb == cf:
                finalize(cf, val_f if val_b is None else val_f + val_b)
            else:
                visit(cf, val_f)
                visit(cb, val_b)

        if emit_state:
            sf_ref[hh] = stf_ref[hh]
            sb_ref[hh] = stb_ref[hh]

    @pl.when(pl.program_id(0) == 0)
    def _():
        for hh in range(heads):
            head = pl.program_id(1) * heads + hh
            dmat_ref[head] = (decay_matrix(0, head) + decay_matrix(1, head)) * kscale

    for hh in range(heads):
        one_head(hh)


RET_HEADS_PROMPT = 8
RET_HEADS_LATENT = 4


def _retention(proj, decays, gn, idx, *, latent, state_f=None, state_b=None, prev_states=()):
    qkv, gate = proj
    seq = DEC_SEQ if latent else SEQ
    nb = DEC_BATCH if latent else BATCH
    heads = RET_HEADS_LATENT if latent else RET_HEADS_PROMPT
    width = heads * C_DK
    groups = C_HEADS // heads
    has_state = latent
    emit_state = not latent
    tok = lambda kind: pl.BlockSpec((seq, width), lambda b, h: (b, kind * groups + h))
    in_specs = [pl.BlockSpec(memory_space=pltpu.SMEM), tok(0), tok(1), tok(2), tok(0),
                pl.BlockSpec((None, 1, width), lambda b, h: (idx, 0, h))]
    args = [decays, qkv, qkv, qkv, gate, gn]
    state_spec = pl.BlockSpec((None, None, heads, C_DK, C_DV), lambda b, h: (b, idx, h, 0, 0))
    if has_state:
        in_specs += [state_spec, state_spec]
        args += [state_f, state_b]
    aliases = {len(args) + k: 1 + k for k in range(len(prev_states))}
    in_specs += [_any_spec() for _ in prev_states]
    args += list(prev_states)
    out_shape = [jax.ShapeDtypeStruct((nb * seq, D_MODEL), BF16)]
    out_specs = [pl.BlockSpec((seq, width), lambda b, h: (b, h))]
    if emit_state:
        st = jax.ShapeDtypeStruct((nb, N_ODD, C_HEADS, C_DK, C_DV), F32)
        out_shape += [st, st]
        out_specs += [state_spec, state_spec]
    blocks = seq * width * (3 * 2 + 4) + width * 4 + 2 * heads * C_DK * C_DV * 4 + seq * width * 2
    scratch = 2 * heads * C_DK * C_DV * 4 + seq * width * 4 + C_HEADS * RET_CHUNK * RET_CHUNK * 4
    return pl.pallas_call(
        functools.partial(_ret_kernel, seq=seq, heads=heads, has_state=has_state, emit_state=emit_state,
                          n_prev=len(prev_states)),
        out_shape=tuple(out_shape),
        grid=(nb, groups),
        in_specs=in_specs,
        out_specs=tuple(out_specs),
        scratch_shapes=[pltpu.VMEM((heads, C_DK, C_DV), F32), pltpu.VMEM((heads, C_DK, C_DV), F32),
                        pltpu.VMEM((seq, width), F32), pltpu.VMEM((C_HEADS, RET_CHUNK, RET_CHUNK), F32)],
        input_output_aliases=aliases,
        compiler_params=_params(_vmem_limit(blocks, scratch, 24 * RET_CHUNK * C_DV * 4), 2),
        name=f"retention_{'latent' if latent else 'prompt'}_{idx}",
    )(*args)


def kernel(x_prompt, x_sample, c, cache_a_k, cache_a_v, cache_b_k, cache_b_v, state_ret_f, state_ret_b,
           c_ctx, w_ada, b_ada, norm_pre, norm_post, w_in_even, w_out_even, a_sink, na_rpb,
           w_in_odd, w_out_odd, ret_decay_f, ret_decay_b, ret_gn):
    xp = x_prompt.reshape(NP, D_MODEL)
    xs = x_sample.reshape(NS, D_MODEL)
    cvec = jnp.concatenate(
        [c_ctx[None, :], c, jnp.zeros((MOD_ROWS - 1 - DEC_BATCH, D_MODEL), F32)], axis=0)
    mods = _adaln(cvec, w_ada, b_ada).reshape(DEPTH, MOD_ROWS, 3, 1, D_MODEL)
    gain_pre = norm_pre.reshape(DEPTH, 1, D_MODEL)
    gain_post = norm_post.reshape(DEPTH, 1, D_MODEL)
    gn = ret_gn.reshape(N_ODD, 1, D_MODEL)
    cos, sin = _rope_tables()
    rpb = jnp.pad(na_rpb, ((0, 0), (0, 0), (0, RPB_PAD[0] - na_rpb.shape[2]),
                           (0, RPB_PAD[1] - na_rpb.shape[3])))
    caches = ()
    states = ()
    hp = _prenorm(xp, gain_pre, mods, 0, False)
    hs = _prenorm(xs, gain_pre, mods, 0, True)
    for layer in range(DEPTH):
        idx = layer // 2
        if layer % 2 == 0:
            proj_p = _inproj(hp, layer, w_in_even, False)
            proj_s = _inproj(hs, layer, w_in_even, True)
            u_p, *caches = _ctx_attn(proj_p, a_sink[idx], idx, caches)
            u_a = _win_attn(proj_s, a_sink[idx], cache_a_k, cache_a_v, idx, cos, sin)
            u_b = _na_attn(proj_s, rpb, cache_b_k, cache_b_v, idx)
            us_p, us_s, w_out = [u_p], [u_a, u_b], w_out_even
        else:
            proj_p = _inproj_split(hp, layer, w_in_odd, False, 3 * D_MODEL)
            proj_s = _inproj_split(hs, layer, w_in_odd, True, 3 * D_MODEL)
            decays = jnp.stack([ret_decay_f[idx], ret_decay_b[idx]], axis=0)
            u_p, *states = _retention(proj_p, decays, gn, idx, latent=False, prev_states=states)
            (u_s,) = _retention(proj_s, decays, gn, idx, latent=True,
                                state_f=state_ret_f, state_b=state_ret_b)
            us_p, us_s = [u_p], [u_s]
            w_out = w_out_odd
        xp, *hp = _outproj(us_p, w_out, xp, gain_post, gain_pre, mods, layer, False)
        xs, *hs = _outproj(us_s, w_out, xs, gain_post, gain_pre, mods, layer, True)
        hp, hs = (hp[0], hs[0]) if hp else (None, None)
    return (xp.reshape(BATCH, SEQ, D_MODEL), xs.reshape(DEC_BATCH, DEC_SEQ, D_MODEL), *caches, *states)
```

```python
import functools

import jax
import jax.numpy as jnp
from jax import lax
from jax.experimental import pallas as pl
from jax.experimental.pallas import tpu as pltpu

D_MODEL = 2048
BATCH = 16
SEQ = 256
DEPTH = 4
DEC_BATCH = 4
DEC_SEQ = 1024
PAST_LEN = 256
GRID_W = 64
HEAD_DIM = 128
A_HEADS = 8
A_KV_HEADS = 2
A_GROUP = A_HEADS // A_KV_HEADS
A_WINDOW = 128
A_BLOCK = 128
B_HEADS = 8
NA_ROWS = 8
NA_COLS = 16
C_HEADS = 8
C_DK = D_MODEL // C_HEADS
C_DV = D_MODEL // C_HEADS
ROPE_THETA = 10000.0
EPS = 1e-6

N_EVEN = (DEPTH + 1) // 2
N_ODD = DEPTH // 2
A_Q = A_HEADS * HEAD_DIM
A_KV = A_KV_HEADS * HEAD_DIM
B_W = B_HEADS * HEAD_DIM
EVEN_WIDTH = A_Q + B_W
EVEN_IN = A_Q + 2 * A_KV + 3 * B_W + EVEN_WIDTH
GRID_ROWS = DEC_SEQ // GRID_W

SUBLANES = 8
LANES = 128
V7X_VMEM_BYTES = 64 * 1024 * 1024
VMEM_LEFT_TO_COMPILER = 4 * 1024 * 1024
VMEM_HEADROOM = 4 * 1024 * 1024

NP = BATCH * SEQ
NS = DEC_BATCH * DEC_SEQ
MOD_ROWS = SUBLANES
assert 1 + DEC_BATCH <= MOD_ROWS

COL_QA = 0
COL_KA = A_Q
COL_VA = A_Q + A_KV
COL_QB = A_Q + 2 * A_KV
COL_KB = COL_QB + B_W
COL_VB = COL_KB + B_W
COL_GATE = COL_VB + B_W

NA_QROWS = 8
NA_KROWS = 12
NA_Q = NA_QROWS * GRID_W
NA_K = NA_KROWS * GRID_W
NA_KSHIFT = (GRID_ROWS - NA_KROWS) * GRID_W

F32 = jnp.float32
BF16 = jnp.bfloat16
NEG_INF = float("-inf")


def _vmem_limit(block_bytes, scratch_bytes=0, temp_bytes=0):
    usable = V7X_VMEM_BYTES - VMEM_LEFT_TO_COMPILER
    assert 2 * block_bytes + scratch_bytes + temp_bytes + VMEM_HEADROOM <= usable
    return usable


def _params(vmem_bytes, ndims):
    return pltpu.CompilerParams(dimension_semantics=("arbitrary",) * ndims, vmem_limit_bytes=vmem_bytes)


def _silu(x):
    half = 0.5 * x
    return half + half * jnp.tanh(half)


def _dot(a, b):
    return jnp.dot(a, b, preferred_element_type=F32)


def _dot_nt(a, b):
    return lax.dot_general(a, b, (((1,), (1,)), ((), ())), preferred_element_type=F32)


def _dot_tn(a, b):
    return lax.dot_general(a, b, (((0,), (0,)), ((), ())), preferred_element_type=F32)


def _any_spec():
    return pl.BlockSpec(memory_space=pl.ANY)


ADA_TN = 2048


def _adaln_kernel(c_ref, w_ref, b_ref, o_ref):
    a = _silu(c_ref[...]).astype(BF16)
    o_ref[...] = _dot(a, w_ref[...].astype(BF16)) + b_ref[...]


def _adaln(cvec, w_ada, b_ada):
    n = 3 * D_MODEL
    blocks = MOD_ROWS * D_MODEL * 4 + D_MODEL * ADA_TN * 4 + ADA_TN * 4 + MOD_ROWS * ADA_TN * 4
    return pl.pallas_call(
        _adaln_kernel,
        out_shape=jax.ShapeDtypeStruct((DEPTH, MOD_ROWS, n), F32),
        grid=(DEPTH, n // ADA_TN),
        in_specs=[
            pl.BlockSpec((MOD_ROWS, D_MODEL), lambda l, j: (0, 0)),
            pl.BlockSpec((None, D_MODEL, ADA_TN), lambda l, j: (l, 0, j)),
            pl.BlockSpec((None, 1, ADA_TN), lambda l, j: (l, 0, j)),
        ],
        out_specs=pl.BlockSpec((None, MOD_ROWS, ADA_TN), lambda l, j: (l, 0, j)),
        compiler_params=_params(_vmem_limit(blocks, temp_bytes=D_MODEL * ADA_TN * 2), 2),
        name="adaln",
    )(cvec, w_ada, b_ada.reshape(DEPTH, 1, n))


def _mod_row(tile, tm, latent):
    return 1 + tile // (DEC_SEQ // tm) if latent else 0


NORM_TM = 1024
NORM_ROWS = 32
IN_TN = 512


def _prenorm_kernel(x_ref, g_ref, sh_ref, sc_ref, h_ref):
    gain = g_ref[...]
    one_sc = 1.0 + sc_ref[...]
    sh = sh_ref[...]

    def body(r, carry):
        sl = pl.ds(pl.multiple_of(r * NORM_ROWS, NORM_ROWS), NORM_ROWS)
        x = x_ref[sl, :]
        ms = jnp.mean(x * x, axis=-1, keepdims=True)
        y = (x * lax.rsqrt(ms + EPS)) * gain
        h_ref[sl, :] = (y * one_sc + sh).astype(BF16)
        return carry

    lax.fori_loop(0, NORM_TM // NORM_ROWS, body, 0, unroll=True)


def _prenorm(x, gain, mods, layer, latent):
    ntok = x.shape[0]
    blocks = NORM_TM * D_MODEL * 4 + 3 * D_MODEL * 4 + NORM_TM * D_MODEL * 2
    mod_spec = lambda which: pl.BlockSpec(
        (None, None, None, 1, D_MODEL), lambda i: (layer, _mod_row(i, NORM_TM, latent), which, 0, 0))
    return pl.pallas_call(
        _prenorm_kernel,
        out_shape=jax.ShapeDtypeStruct((ntok, D_MODEL), BF16),
        grid=(ntok // NORM_TM,),
        in_specs=[
            pl.BlockSpec((NORM_TM, D_MODEL), lambda i: (i, 0)),
            pl.BlockSpec((None, 1, D_MODEL), lambda i: (layer, 0, 0)),
            mod_spec(0),
            mod_spec(1),
        ],
        out_specs=pl.BlockSpec((NORM_TM, D_MODEL), lambda i: (i, 0)),
        compiler_params=_params(_vmem_limit(blocks, temp_bytes=4 * NORM_ROWS * D_MODEL * 4), 1),
        name=f"prenorm_l{layer}_{'latent' if latent else 'prompt'}",
    )(x, gain, mods, mods)


def _inproj_kernel(h_ref, w_ref, o_ref):
    o_ref[...] = _dot(h_ref[...], w_ref[...].astype(BF16))


def _inproj(h, layer, w, latent):
    ntok = h.shape[0]
    n = w.shape[2]
    blocks = D_MODEL * IN_TN * 4 + ntok * IN_TN * 4
    return pl.pallas_call(
        _inproj_kernel,
        out_shape=jax.ShapeDtypeStruct((ntok, n), F32),
        grid=(n // IN_TN,),
        in_specs=[
            pl.BlockSpec((ntok, D_MODEL), lambda j: (0, 0), pipeline_mode=pl.Buffered(1)),
            pl.BlockSpec((None, D_MODEL, IN_TN), lambda j: (layer // 2, 0, j)),
        ],
        out_specs=pl.BlockSpec((ntok, IN_TN), lambda j: (0, j)),
        compiler_params=_params(_vmem_limit(blocks, ntok * D_MODEL * 2, D_MODEL * IN_TN * 2), 1),
        name=f"inproj_l{layer}_{'latent' if latent else 'prompt'}",
    )(h, w)


def _inproj_split_kernel(h_ref, w_ref, lo_ref, hi_ref, *, n_lo):
    @pl.when(pl.program_id(0) < n_lo)
    def _():
        lo_ref[...] = _dot(h_ref[...], w_ref[...].astype(BF16)).astype(lo_ref.dtype)

    @pl.when(pl.program_id(0) >= n_lo)
    def _():
        hi_ref[...] = _dot(h_ref[...], w_ref[...].astype(BF16))


def _inproj_split(h, layer, w, latent, lo_width):
    ntok = h.shape[0]
    n = w.shape[2]
    n_lo = lo_width // IN_TN
    assert lo_width % IN_TN == 0 and 0 < n_lo < n // IN_TN
    blocks = D_MODEL * IN_TN * 4 + ntok * IN_TN * (2 + 4)
    return pl.pallas_call(
        functools.partial(_inproj_split_kernel, n_lo=n_lo),
        out_shape=(jax.ShapeDtypeStruct((ntok, lo_width), BF16),
                   jax.ShapeDtypeStruct((ntok, n - lo_width), F32)),
        grid=(n // IN_TN,),
        in_specs=[
            pl.BlockSpec((ntok, D_MODEL), lambda j: (0, 0), pipeline_mode=pl.Buffered(1)),
            pl.BlockSpec((None, D_MODEL, IN_TN), lambda j: (layer // 2, 0, j)),
        ],
        out_specs=(pl.BlockSpec((ntok, IN_TN), lambda j: (0, jnp.minimum(j, n_lo - 1))),
                   pl.BlockSpec((ntok, IN_TN), lambda j: (0, jnp.maximum(j - n_lo, 0)))),
        compiler_params=_params(_vmem_limit(blocks, ntok * D_MODEL * 2, D_MODEL * IN_TN * 2), 1),
        name=f"inproj_l{layer}_{'latent' if latent else 'prompt'}",
    )(h, w)


OUT_TM = 512
OUT_SUB = 256


def _outproj_kernel(*refs, n_pieces, emit_next):
    u_refs = refs[:n_pieces]
    w_refs = refs[n_pieces:2 * n_pieces]
    pos = 2 * n_pieces
    x_ref, gain_ref, gate_ref = refs[pos:pos + 3]
    pos += 3
    if emit_next:
        ngain_ref, nshift_ref, nscale_ref = refs[pos:pos + 3]
        pos += 3
    o_ref = refs[pos]
    pos += 1
    if emit_next:
        h_ref = refs[pos]
        pos += 1
    wbf_ref = refs[pos]

    @pl.when(pl.program_id(0) == 0)
    def _():
        for p, w_ref in enumerate(w_refs):
            wbf_ref[p] = w_ref[...].astype(BF16)

    gated_gain = gate_ref[...] * gain_ref[...]
    if emit_next:
        next_gain = ngain_ref[...] * (1.0 + nscale_ref[...])
        next_shift = nshift_ref[...]
    for r in range(OUT_TM // OUT_SUB):
        rows = slice(r * OUT_SUB, (r + 1) * OUT_SUB)
        out = _dot(u_refs[0][rows, :], wbf_ref[0])
        for p in range(1, n_pieces):
            out = out + _dot(u_refs[p][rows, :], wbf_ref[p])
        ms = jnp.mean(out * out, axis=-1, keepdims=True)
        x_new = x_ref[rows, :] + (out * lax.rsqrt(ms + EPS)) * gated_gain
        o_ref[rows, :] = x_new
        if emit_next:
            ms = jnp.mean(x_new * x_new, axis=-1, keepdims=True)
            h_ref[rows, :] = ((x_new * lax.rsqrt(ms + EPS)) * next_gain + next_shift).astype(BF16)


def _outproj(us, w, x, gain_post, gain_pre, mods, layer, latent):
    ntok = x.shape[0]
    widths = [u.shape[1] for u in us]
    kp = widths[0]
    k = w.shape[1]
    assert all(kw == kp for kw in widths) and kp * len(us) == k
    emit_next = layer + 1 < DEPTH
    blocks = OUT_TM * k * 2 + 2 * OUT_TM * D_MODEL * 4 + 5 * D_MODEL * 4 + emit_next * OUT_TM * D_MODEL * 2
    tile_spec = pl.BlockSpec((OUT_TM, D_MODEL), lambda i: (i, 0))
    gain_spec = lambda l: pl.BlockSpec((None, 1, D_MODEL), lambda i: (l, 0, 0))
    mod_spec = lambda l, which: pl.BlockSpec(
        (None, None, None, 1, D_MODEL), lambda i: (l, _mod_row(i, OUT_TM, latent), which, 0, 0))
    u_specs = [pl.BlockSpec((OUT_TM, kp), lambda i: (i, 0)) for _ in us]
    w_specs = [pl.BlockSpec((None, kp, D_MODEL), lambda i, p=p: (layer // 2, p, 0),
                            pipeline_mode=pl.Buffered(1)) for p in range(len(us))]
    in_specs = u_specs + w_specs + [tile_spec, gain_spec(layer), mod_spec(layer, 2)]
    args = [*us, *([w] * len(us)), x, gain_post, mods]
    out_shape = [jax.ShapeDtypeStruct((ntok, D_MODEL), F32)]
    out_specs = [tile_spec]
    if emit_next:
        in_specs += [gain_spec(layer + 1), mod_spec(layer + 1, 0), mod_spec(layer + 1, 1)]
        args += [gain_pre, mods, mods]
        out_shape.append(jax.ShapeDtypeStruct((ntok, D_MODEL), BF16))
        out_specs.append(tile_spec)
    return pl.pallas_call(
        functools.partial(_outproj_kernel, n_pieces=len(us), emit_next=emit_next),
        out_shape=tuple(out_shape),
        grid=(ntok // OUT_TM,),
        in_specs=in_specs,
        out_specs=tuple(out_specs),
        scratch_shapes=[pltpu.VMEM((len(us), kp, D_MODEL), BF16)],
        compiler_params=_params(
            _vmem_limit(blocks, k * D_MODEL * (4 + 2), 3 * OUT_TM * D_MODEL * 2), 1),
        name=f"outproj_l{layer}_{'latent' if latent else 'prompt'}",
    )(*args)


LOG2_E = 1.4426950408889634
Q_SCALE_LOG2 = HEAD_DIM ** -0.5 * LOG2_E


def _softmax_pv(scores, values, sink=None):
    def fold(blocks, op):
        tiles = [b[:, t:t + LANES] for b in blocks for t in range(0, b.shape[-1], LANES)]
        out = tiles[0]
        for t in tiles[1:]:
            out = op(out, t)
        return out

    m = jnp.max(fold(scores, jnp.maximum), axis=-1, keepdims=True)
    if sink is not None:
        m = jnp.maximum(m, sink)
    probs = [jnp.exp2(s - m) for s in scores]
    den = jnp.sum(fold(probs, jnp.add), axis=-1, keepdims=True)
    if sink is not None:
        den = den + jnp.exp2(sink - m)
    acc = _dot(probs[0].astype(BF16), values[0])
    for p, v in zip(probs[1:], values[1:]):
        acc = acc + _dot(p.astype(BF16), v)
    return acc / den


CTX_RING = 3


def _ctx_attn_kernel(sink_ref, proj_ref, *refs):
    u_ref, ak_ref, av_ref, bk_ref, bv_ref, ring_ref, sem_ref = refs[-7:]
    b = pl.program_id(0)

    def fetch(req):
        slot = req % CTX_RING
        rows = pl.ds(pl.multiple_of(req * SEQ, SEQ), SEQ)
        return pltpu.make_async_copy(proj_ref.at[rows, :], ring_ref.at[slot], sem_ref.at[slot])

    @pl.when(b == 0)
    def _():
        for req in range(CTX_RING - 1):
            fetch(req).start()

    @pl.when(b + CTX_RING - 1 < BATCH)
    def _():
        fetch(b + CTX_RING - 1).start()

    fetch(b).wait()
    slot = b % CTX_RING

    def col(c):
        return ring_ref[slot, :, c:c + HEAD_DIM]

    def head(cq, k, v, cg, sink, cu):
        q = (col(cq) * Q_SCALE_LOG2).astype(BF16)
        o = _softmax_pv([_dot_nt(q, k)], [v], None if sink is None else sink * LOG2_E)
        u_ref[:, cu:cu + HEAD_DIM] = (o * _silu(col(cg))).astype(BF16)

    for n in range(A_KV_HEADS):
        k32 = col(COL_KA + n * HEAD_DIM)
        v32 = col(COL_VA + n * HEAD_DIM)
        ak_ref[n] = k32
        av_ref[n] = v32
        k = k32.astype(BF16)
        v = v32.astype(BF16)
        for g in range(A_GROUP):
            h = n * A_GROUP + g
            head(COL_QA + h * HEAD_DIM, k, v, COL_GATE + h * HEAD_DIM, sink_ref[h], h * HEAD_DIM)
    for h in range(B_HEADS):
        k32 = col(COL_KB + h * HEAD_DIM)
        v32 = col(COL_VB + h * HEAD_DIM)
        bk_ref[h] = k32
        bv_ref[h] = v32
        head(COL_QB + h * HEAD_DIM, k32.astype(BF16), v32.astype(BF16),
             COL_GATE + A_Q + h * HEAD_DIM, None, A_Q + h * HEAD_DIM)


def _ctx_attn(proj, sink, idx, prev_caches):
    blocks = SEQ * EVEN_WIDTH * 2 + 2 * (A_KV_HEADS + B_HEADS) * SEQ * HEAD_DIM * 4
    ring_bytes = CTX_RING * SEQ * EVEN_IN * 4
    heads = (A_KV_HEADS, A_KV_HEADS, B_HEADS, B_HEADS)
    cache = lambda nh: jax.ShapeDtypeStruct((BATCH, N_EVEN, nh, SEQ, HEAD_DIM), F32)
    cache_spec = lambda nh: pl.BlockSpec((None, None, nh, SEQ, HEAD_DIM), lambda b: (b, idx, 0, 0, 0))
    n_prev = len(prev_caches)
    return pl.pallas_call(
        _ctx_attn_kernel,
        out_shape=(jax.ShapeDtypeStruct((NP, EVEN_WIDTH), BF16),) + tuple(cache(nh) for nh in heads),
        grid=(BATCH,),
        in_specs=[
            pl.BlockSpec(memory_space=pltpu.SMEM),
            _any_spec(),
        ] + [_any_spec() for _ in prev_caches],
        out_specs=(pl.BlockSpec((SEQ, EVEN_WIDTH), lambda b: (b, 0)),) + tuple(cache_spec(nh) for nh in heads),
        scratch_shapes=[pltpu.VMEM((CTX_RING, SEQ, EVEN_IN), F32), pltpu.SemaphoreType.DMA((CTX_RING,))],
        input_output_aliases={2 + k: 1 + k for k in range(n_prev)},
        compiler_params=_params(_vmem_limit(blocks, ring_bytes, 8 * SEQ * SEQ * 4), 1),
        name=f"ctx_attn_{idx}",
    )(sink, proj, *prev_caches)


WIN_PAD = DEC_SEQ + 2 * A_BLOCK


def _rope_tables():
    t = jnp.arange(DEC_SEQ)
    half = HEAD_DIM // 2
    nf = half // 2
    inv = ROPE_THETA ** (-jnp.arange(nf, dtype=F32) / nf)
    ang_r = (t // GRID_W).astype(F32)[:, None] * inv[None]
    ang_c = (t % GRID_W).astype(F32)[:, None] * inv[None]
    cos = jnp.concatenate([jnp.cos(ang_r)] * 2 + [jnp.cos(ang_c)] * 2, axis=-1)
    sin = jnp.concatenate([-jnp.sin(ang_r), jnp.sin(ang_r), -jnp.sin(ang_c), jnp.sin(ang_c)], axis=-1)
    return cos, sin


def _rope(x, cos, sin):
    quarter = HEAD_DIM // 4
    lane = lax.broadcasted_iota(jnp.int32, x.shape, 1)
    first = (lane & (2 * quarter - 1)) < quarter
    partner = jnp.where(first, pltpu.roll(x, HEAD_DIM - quarter, 1), pltpu.roll(x, quarter, 1))
    return x * cos + partner * sin


def _win_attn_kernel(sink_ref, q_ref, k_ref, v_ref, ck_ref, cv_ref, gate_ref, cos_ref, sin_ref,
                     u_ref, kpad_ref, vpad_ref):
    n = pl.program_id(1)
    nqb = DEC_SEQ // A_BLOCK
    rows = A_GROUP * A_BLOCK

    zeros = jnp.zeros((A_BLOCK, HEAD_DIM), BF16)
    for ref in (kpad_ref, vpad_ref):
        ref[0:A_BLOCK, :] = zeros
        ref[A_BLOCK + DEC_SEQ:WIN_PAD, :] = zeros
    kpad_ref[A_BLOCK:A_BLOCK + DEC_SEQ, :] = _rope(k_ref[...], cos_ref[...], sin_ref[...]).astype(BF16)
    vpad_ref[A_BLOCK:A_BLOCK + DEC_SEQ, :] = v_ref[...].astype(BF16)
    ck = ck_ref[...].astype(BF16)
    cv = cv_ref[...].astype(BF16)

    row = lax.broadcasted_iota(jnp.int32, (rows, 1), 0)
    qi = row & (A_BLOCK - 1)
    kk = lax.broadcasted_iota(jnp.int32, (rows, 3 * A_BLOCK), 1)
    head = lax.shift_right_logical(row, A_BLOCK.bit_length() - 1)
    sink = jnp.zeros((rows, 1), F32)
    for g in range(A_GROUP):
        sink = jnp.where(head == g, sink_ref[n * A_GROUP + g] * LOG2_E, sink)

    for j in range(nqb):
        blk = slice(j * A_BLOCK, (j + 1) * A_BLOCK)
        q = jnp.concatenate(
            [(_rope(q_ref[blk, g * HEAD_DIM:(g + 1) * HEAD_DIM], cos_ref[blk, :], sin_ref[blk, :])
              * Q_SCALE_LOG2).astype(BF16) for g in range(A_GROUP)],
            axis=0)
        band = slice(j * A_BLOCK, (j + 3) * A_BLOCK)
        s_c = _dot_nt(q, ck)
        s_w = _dot_nt(q, kpad_ref[band, :])
        lower = jnp.maximum(qi, A_BLOCK if j == 0 else 0)
        upper = jnp.minimum(qi + 2 * A_WINDOW, (2 if j == nqb - 1 else 3) * A_BLOCK - 1)
        s_w = jnp.where((kk >= lower) & (kk <= upper), s_w, NEG_INF)
        o = _softmax_pv([s_c, s_w], [cv, vpad_ref[band, :]], sink)
        for g in range(A_GROUP):
            cols = slice(g * HEAD_DIM, (g + 1) * HEAD_DIM)
            u_ref[blk, cols] = (o[g * A_BLOCK:(g + 1) * A_BLOCK] * _silu(gate_ref[blk, cols])).astype(BF16)


def _win_attn(proj, sink, cache_k, cache_v, idx, cos, sin):
    gw = A_GROUP * HEAD_DIM
    blocks = (2 * DEC_SEQ * gw * 4 + 2 * DEC_SEQ * HEAD_DIM * 4 + 2 * PAST_LEN * HEAD_DIM * 4
              + 2 * DEC_SEQ * HEAD_DIM * 4 + DEC_SEQ * gw * 2)
    cache_spec = pl.BlockSpec((None, None, None, PAST_LEN, HEAD_DIM), lambda b, n: (b, idx, n, 0, 0))
    table_spec = pl.BlockSpec((DEC_SEQ, HEAD_DIM), lambda b, n: (0, 0))
    return pl.pallas_call(
        _win_attn_kernel,
        out_shape=jax.ShapeDtypeStruct((NS, A_Q), BF16),
        grid=(DEC_BATCH, A_KV_HEADS),
        in_specs=[
            pl.BlockSpec(memory_space=pltpu.SMEM),
            pl.BlockSpec((DEC_SEQ, gw), lambda b, n: (b, COL_QA // gw + n)),
            pl.BlockSpec((DEC_SEQ, HEAD_DIM), lambda b, n: (b, COL_KA // HEAD_DIM + n)),
            pl.BlockSpec((DEC_SEQ, HEAD_DIM), lambda b, n: (b, COL_VA // HEAD_DIM + n)),
            cache_spec,
            cache_spec,
            pl.BlockSpec((DEC_SEQ, gw), lambda b, n: (b, COL_GATE // gw + n)),
            table_spec,
            table_spec,
        ],
        out_specs=pl.BlockSpec((DEC_SEQ, gw), lambda b, n: (b, n)),
        scratch_shapes=[pltpu.VMEM((WIN_PAD, HEAD_DIM), BF16), pltpu.VMEM((WIN_PAD, HEAD_DIM), BF16)],
        compiler_params=_params(
            _vmem_limit(blocks, 2 * WIN_PAD * HEAD_DIM * 2,
                        (DEC_SEQ // A_BLOCK) * 3 * A_GROUP * A_BLOCK * (PAST_LEN + 3 * A_BLOCK) * 4), 2),
        name=f"win_attn_{idx}",
    )(sink, proj, proj, proj, cache_k, cache_v, proj, cos, sin)


RPB_PAD = (-(-(2 * NA_ROWS - 1) // SUBLANES) * SUBLANES, -(-(2 * NA_COLS - 1) // LANES) * LANES)


def _na_row_offsets(g):
    offsets = []
    for rl in range(NA_QROWS):
        r = g * NA_QROWS + rl
        r0 = min(max(r - NA_ROWS // 2, 0), GRID_ROWS - NA_ROWS)
        row = []
        for kl in range(NA_KROWS):
            kr = g * (GRID_ROWS - NA_KROWS) + kl
            row.append(kr - r + NA_ROWS - 1 if r0 <= kr < r0 + NA_ROWS else None)
        offsets.append(row)
    return offsets


def _fill_na_bias(rpb_ref, bias_ref, g):
    shape = (GRID_W, 2 * GRID_W)
    c = lax.broadcasted_iota(jnp.int32, shape, 0)
    lane = lax.broadcasted_iota(jnp.int32, shape, 1)
    kc = lane & (GRID_W - 1)
    c0 = jnp.clip(c - NA_COLS // 2, 0, GRID_W - NA_COLS)
    col_ok = (kc >= c0) & (kc < c0 + NA_COLS)
    low = lane < GRID_W
    offsets = _na_row_offsets(g)
    used = sorted({d for row in offsets for d in row if d is not None})
    lo, hi = {}, {}
    for d in used:
        row = jnp.broadcast_to(rpb_ref[d:d + 1, :] * LOG2_E, shape)
        lo[d] = pltpu.roll(row, 2 * GRID_W - (NA_COLS - 1), 1, stride=1, stride_axis=0)
        hi[d] = pltpu.roll(row, GRID_W - (NA_COLS - 1), 1, stride=1, stride_axis=0)
    neg = jnp.full(shape, NEG_INF, F32)
    for rl in range(NA_QROWS):
        for p in range(NA_KROWS // 2):
            da, db = offsets[rl][2 * p], offsets[rl][2 * p + 1]
            a = neg if da is None else lo[da]
            b = neg if db is None else hi[db]
            piece = jnp.where(col_ok, jnp.where(low, a, b), NEG_INF)
            bias_ref[rl * GRID_W:(rl + 1) * GRID_W, 2 * p * GRID_W:2 * (p + 1) * GRID_W] = piece


def _na_attn_kernel(q_ref, k_ref, v_ref, ck_ref, cv_ref, gate_ref, rpb_ref, u_ref, bias_ref):
    n_groups = DEC_SEQ // NA_Q
    for g in range(n_groups):
        _fill_na_bias(rpb_ref, bias_ref.at[g], g)
    for i in range(DEC_BATCH):
        k = k_ref[i].astype(BF16)
        v = v_ref[i].astype(BF16)
        ck = ck_ref[i].astype(BF16)
        cv = cv_ref[i].astype(BF16)
        for g in range(n_groups):
            rows = slice(g * NA_Q, (g + 1) * NA_Q)
            window = slice(g * NA_KSHIFT, g * NA_KSHIFT + NA_K)
            q = (q_ref[i, rows, :] * Q_SCALE_LOG2).astype(BF16)
            s_c = _dot_nt(q, ck)
            s_n = _dot_nt(q, k[window]) + bias_ref[g]
            o = _softmax_pv([s_c, s_n], [cv, v[window]])
            u_ref[i, rows, :] = (o * _silu(gate_ref[i, rows, :])).astype(BF16)


def _na_attn(proj, rpb, cache_k, cache_v, idx):
    n_groups = DEC_SEQ // NA_Q
    proj = proj.reshape(DEC_BATCH, DEC_SEQ, EVEN_IN)
    blocks = DEC_BATCH * (4 * DEC_SEQ * HEAD_DIM * 4 + 2 * PAST_LEN * HEAD_DIM * 4
                          + DEC_SEQ * HEAD_DIM * 2) + RPB_PAD[0] * RPB_PAD[1] * 4
    tok_spec = lambda col0: pl.BlockSpec((DEC_BATCH, DEC_SEQ, HEAD_DIM), lambda h: (0, 0, col0 // HEAD_DIM + h))
    cache_spec = pl.BlockSpec((DEC_BATCH, None, None, PAST_LEN, HEAD_DIM), lambda h: (0, idx, h, 0, 0))
    out = pl.pallas_call(
        _na_attn_kernel,
        out_shape=jax.ShapeDtypeStruct((DEC_BATCH, DEC_SEQ, B_W), BF16),
        grid=(B_HEADS,),
        in_specs=[
            tok_spec(COL_QB),
            tok_spec(COL_KB),
            tok_spec(COL_VB),
            cache_spec,
            cache_spec,
            tok_spec(COL_GATE + A_Q),
            pl.BlockSpec((None, None) + RPB_PAD, lambda h: (idx, h, 0, 0)),
        ],
        out_specs=pl.BlockSpec((DEC_BATCH, DEC_SEQ, HEAD_DIM), lambda h: (0, 0, h)),
        scratch_shapes=[pltpu.VMEM((n_groups, NA_Q, NA_K), F32)],
        compiler_params=_params(
            _vmem_limit(blocks, n_groups * NA_Q * NA_K * 4,
                        DEC_BATCH * n_groups * 2 * NA_Q * (NA_K + PAST_LEN) * 4), 1),
        name=f"na_attn_{idx}",
    )(proj, proj, proj, cache_k, cache_v, proj, rpb)
    return out.reshape(NS, B_W)


RET_CHUNK = 256
assert SEQ % RET_CHUNK == 0 and DEC_SEQ % RET_CHUNK == 0


def _ret_kernel(*refs, seq, heads, has_state, emit_state, n_prev):
    dec_ref, q_ref, k_ref, v_ref, gate_ref, gn_ref = refs[:6]
    pos = 6
    if has_state:
        s0f_ref, s0b_ref = refs[pos:pos + 2]
        pos += 2
    pos += n_prev
    u_ref = refs[pos]
    pos += 1
    if emit_state:
        sf_ref, sb_ref = refs[pos:pos + 2]
        pos += 2
    stf_ref, stb_ref, o_ref, dmat_ref = refs[pos:pos + 4]

    ch = RET_CHUNK
    nc = seq // ch
    kscale = C_DK ** -0.5
    ii = lax.broadcasted_iota(jnp.int32, (ch, ch), 0).astype(F32)
    jj = lax.broadcasted_iota(jnp.int32, (ch, ch), 1).astype(F32)
    icol = lax.broadcasted_iota(jnp.int32, (ch, 1), 0).astype(F32)

    def decay_matrix(direction, head):
        log_g = -jnp.exp(jnp.full((ch, ch), dec_ref[direction, head], F32))
        diff = (ii - jj) if direction == 0 else (jj - ii)
        return jnp.where(diff >= 0, jnp.exp(log_g * jnp.maximum(diff, 0.0)), 0.0)

    def decay_vectors(direction, head):
        dec = dec_ref[direction, head]
        log_g_col = -jnp.exp(jnp.full((ch, 1), dec, F32))
        log_g_row = -jnp.exp(jnp.full((1, C_DV), dec, F32))
        if direction == 0:
            q_dec = jnp.exp(log_g_col * (icol + 1.0))
            k_dec = jnp.exp(log_g_col * (ch - 1.0 - icol)) * kscale
        else:
            q_dec = jnp.exp(log_g_col * (ch - icol))
            k_dec = jnp.exp(log_g_col * icol) * kscale
        return q_dec, k_dec, jnp.exp(log_g_row * float(ch))

    def one_head(hh):
        cols = slice(hh * C_DK, (hh + 1) * C_DK)
        head = pl.program_id(1) * heads + hh
        dmat = dmat_ref[head]
        q_dec_f, k_dec_f, chunk_dec_f = decay_vectors(0, head)
        q_dec_b, k_dec_b, chunk_dec_b = decay_vectors(1, head)
        gn = gn_ref[:, cols]

        def rows_of(c):
            return slice(c * ch, (c + 1) * ch)

        def finalize(c, o):
            mu = jnp.mean(o, axis=-1, keepdims=True)
            d = o - mu
            var = jnp.mean(d * d, axis=-1, keepdims=True)
            y = (d * lax.rsqrt(var + EPS)) * gn
            u_ref[rows_of(c), cols] = (y * _silu(gate_ref[rows_of(c), cols])).astype(BF16)

        parked = {}

        def visit(c, value):
            if c not in parked:
                parked[c] = value is not None
                if value is not None:
                    o_ref[rows_of(c), cols] = value
            elif parked[c]:
                first = o_ref[rows_of(c), cols]
                finalize(c, first if value is None else first + value)
            else:
                finalize(c, value)

        def scan_step(state_ref, s0_ref, t, q, k32, v, q_dec, k_dec, chunk_dec):
            update = _dot_tn((k32 * k_dec).astype(BF16), v)
            if t == 0 and not has_state:
                state_ref[hh] = update
                return None
            state = s0_ref[hh] if t == 0 else state_ref[hh]
            state_ref[hh] = state * chunk_dec + update
            return _dot(q, state.astype(BF16)) * q_dec

        for t in range(nc):
            cf, cb = t, nc - 1 - t
            q = q_ref[rows_of(cf), cols].astype(BF16)
            k32 = k_ref[rows_of(cf), cols]
            v = v_ref[rows_of(cf), cols].astype(BF16)
            s = _dot_nt(q, k32.astype(BF16)) * dmat
            val_f = _dot(s.astype(BF16), v)
            cross = scan_step(stf_ref, s0f_ref if has_state else None, t, q, k32, v,
                              q_dec_f, k_dec_f, chunk_dec_f)
            if cross is not None:
                val_f = val_f + cross
            if cb != cf:
                q = q_ref[rows_of(cb), cols].astype(BF16)
                k32 = k_ref[rows_of(cb), cols]
                v = v_ref[rows_of(cb), cols].astype(BF16)
            val_b = scan_step(stb_ref, s0b_ref if has_state else None, t, q, k32, v,
                              q_dec_b, k_dec_b, chunk_dec_b)
            if cb == cf:
                finalize(cf, val_f if val_b is None else val_f + val_b)
            else:
                visit(cf, val_f)
                visit(cb, val_b)

        if emit_state:
            sf_ref[hh] = stf_ref[hh]
            sb_ref[hh] = stb_ref[hh]

    @pl.when(pl.program_id(0) == 0)
    def _():
        for hh in range(heads):
            head = pl.program_id(1) * heads + hh
            dmat_ref[head] = (decay_matrix(0, head) + decay_matrix(1, head)) * kscale

    for hh in range(heads):
        one_head(hh)


RET_HEADS_PROMPT = 8
RET_HEADS_LATENT = 4


def _retention(proj, decays, gn, idx, *, latent, state_f=None, state_b=None, prev_states=()):
    qkv, gate = proj
    seq = DEC_SEQ if latent else SEQ
    nb = DEC_BATCH if latent else BATCH
    heads = RET_HEADS_LATENT if latent else RET_HEADS_PROMPT
    width = heads * C_DK
    groups = C_HEADS // heads
    has_state = latent
    emit_state = not latent
    tok = lambda kind: pl.BlockSpec((seq, width), lambda b, h: (b, kind * groups + h))
    in_specs = [pl.BlockSpec(memory_space=pltpu.SMEM), tok(0), tok(1), tok(2), tok(0),
                pl.BlockSpec((None, 1, width), lambda b, h: (idx, 0, h))]
    args = [decays, qkv, qkv, qkv, gate, gn]
    state_spec = pl.BlockSpec((None, None, heads, C_DK, C_DV), lambda b, h: (b, idx, h, 0, 0))
    if has_state:
        in_specs += [state_spec, state_spec]
        args += [state_f, state_b]
    aliases = {len(args) + k: 1 + k for k in range(len(prev_states))}
    in_specs += [_any_spec() for _ in prev_states]
    args += list(prev_states)
    out_shape = [jax.ShapeDtypeStruct((nb * seq, D_MODEL), BF16)]
    out_specs = [pl.BlockSpec((seq, width), lambda b, h: (b, h))]
    if emit_state:
        st = jax.ShapeDtypeStruct((nb, N_ODD, C_HEADS, C_DK, C_DV), F32)
        out_shape += [st, st]
        out_specs += [state_spec, state_spec]
    blocks = seq * width * (3 * 2 + 4) + width * 4 + 2 * heads * C_DK * C_DV * 4 + seq * width * 2
    scratch = 2 * heads * C_DK * C_DV * 4 + seq * width * 4 + C_HEADS * RET_CHUNK * RET_CHUNK * 4
    return pl.pallas_call(
        functools.partial(_ret_kernel, seq=seq, heads=heads, has_state=has_state, emit_state=emit_state,
                          n_prev=len(prev_states)),
        out_shape=tuple(out_shape),
        grid=(nb, groups),
        in_specs=in_specs,
        out_specs=tuple(out_specs),
        scratch_shapes=[pltpu.VMEM((heads, C_DK, C_DV), F32), pltpu.VMEM((heads, C_DK, C_DV), F32),
                        pltpu.VMEM((seq, width), F32), pltpu.VMEM((C_HEADS, RET_CHUNK, RET_CHUNK), F32)],
        input_output_aliases=aliases,
        compiler_params=_params(_vmem_limit(blocks, scratch, 24 * RET_CHUNK * C_DV * 4), 2),
        name=f"retention_{'latent' if latent else 'prompt'}_{idx}",
    )(*args)


def kernel(x_prompt, x_sample, c, cache_a_k, cache_a_v, cache_b_k, cache_b_v, state_ret_f, state_ret_b,
           c_ctx, w_ada, b_ada, norm_pre, norm_post, w_in_even, w_out_even, a_sink, na_rpb,
           w_in_odd, w_out_odd, ret_decay_f, ret_decay_b, ret_gn):
    xp = x_prompt.reshape(NP, D_MODEL)
    xs = x_sample.reshape(NS, D_MODEL)
    cvec = jnp.concatenate(
        [c_ctx[None, :], c, jnp.zeros((MOD_ROWS - 1 - DEC_BATCH, D_MODEL), F32)], axis=0)
    mods = _adaln(cvec, w_ada, b_ada).reshape(DEPTH, MOD_ROWS, 3, 1, D_MODEL)
    gain_pre = norm_pre.reshape(DEPTH, 1, D_MODEL)
    gain_post = norm_post.reshape(DEPTH, 1, D_MODEL)
    gn = ret_gn.reshape(N_ODD, 1, D_MODEL)
    cos, sin = _rope_tables()
    rpb = jnp.pad(na_rpb, ((0, 0), (0, 0), (0, RPB_PAD[0] - na_rpb.shape[2]),
                           (0, RPB_PAD[1] - na_rpb.shape[3])))
    caches = ()
    states = ()
    hp = _prenorm(xp, gain_pre, mods, 0, False)
    hs = _prenorm(xs, gain_pre, mods, 0, True)
    for layer in range(DEPTH):
        idx = layer // 2
        if layer % 2 == 0:
            proj_p = _inproj(hp, layer, w_in_even, False)
            proj_s = _inproj(hs, layer, w_in_even, True)
            u_p, *caches = _ctx_attn(proj_p, a_sink[idx], idx, caches)
            u_a = _win_attn(proj_s, a_sink[idx], cache_a_k, cache_a_v, idx, cos, sin)
            u_b = _na_attn(proj_s, rpb, cache_b_k, cache_b_v, idx)
            us_p, us_s, w_out = [u_p], [u_a, u_b], w_out_even
        else:
            proj_p = _inproj_split(hp, layer, w_in_odd, False, 3 * D_MODEL)
            proj_s = _inproj_split(hs, layer, w_in_odd, True, 3 * D_MODEL)
            decays = jnp.stack([ret_decay_f[idx], ret_decay_b[idx]], axis=0)
            u_p, *states = _retention(proj_p, decays, gn, idx, latent=False, prev_states=states)
            (u_s,) = _retention(proj_s, decays, gn, idx, latent=True,
                                state_f=state_ret_f, state_b=state_ret_b)
            us_p, us_s = [u_p], [u_s]
            w_out = w_out_odd
        xp, *hp = _outproj(us_p, w_out, xp, gain_post, gain_pre, mods, layer, False)
        xs, *hs = _outproj(us_s, w_out, xs, gain_post, gain_pre, mods, layer, True)
        hp, hs = (hp[0], hs[0]) if hp else (None, None)
    return (xp.reshape(BATCH, SEQ, D_MODEL), xs.reshape(DEC_BATCH, DEC_SEQ, D_MODEL), *caches, *states)
```

```python
import functools

import jax
import jax.numpy as jnp
from jax import lax
from jax.experimental import pallas as pl
from jax.experimental.pallas import tpu as pltpu

D_MODEL = 2048
BATCH = 16
SEQ = 256
DEPTH = 4
DEC_BATCH = 4
DEC_SEQ = 1024
PAST_LEN = 256
GRID_W = 64
HEAD_DIM = 128
A_HEADS = 8
A_KV_HEADS = 2
A_GROUP = A_HEADS // A_KV_HEADS
A_WINDOW = 128
A_BLOCK = 128
B_HEADS = 8
NA_ROWS = 8
NA_COLS = 16
C_HEADS = 8
C_DK = D_MODEL // C_HEADS
C_DV = D_MODEL // C_HEADS
ROPE_THETA = 10000.0
EPS = 1e-6

N_EVEN = (DEPTH + 1) // 2
N_ODD = DEPTH // 2
A_Q = A_HEADS * HEAD_DIM
A_KV = A_KV_HEADS * HEAD_DIM
B_W = B_HEADS * HEAD_DIM
EVEN_WIDTH = A_Q + B_W
EVEN_IN = A_Q + 2 * A_KV + 3 * B_W + EVEN_WIDTH
GRID_ROWS = DEC_SEQ // GRID_W

SUBLANES = 8
LANES = 128
V7X_VMEM_BYTES = 64 * 1024 * 1024
VMEM_LEFT_TO_COMPILER = 4 * 1024 * 1024
VMEM_HEADROOM = 4 * 1024 * 1024

NP = BATCH * SEQ
NS = DEC_BATCH * DEC_SEQ
MOD_ROWS = SUBLANES
assert 1 + DEC_BATCH <= MOD_ROWS

COL_QA = 0
COL_KA = A_Q
COL_VA = A_Q + A_KV
COL_QB = A_Q + 2 * A_KV
COL_KB = COL_QB + B_W
COL_VB = COL_KB + B_W
COL_GATE = COL_VB + B_W

NA_QROWS = 8
NA_KROWS = 12
NA_Q = NA_QROWS * GRID_W
NA_K = NA_KROWS * GRID_W
NA_KSHIFT = (GRID_ROWS - NA_KROWS) * GRID_W

F32 = jnp.float32
BF16 = jnp.bfloat16
NEG_INF = float("-inf")


def _vmem_limit(block_bytes, scratch_bytes=0, temp_bytes=0):
    usable = V7X_VMEM_BYTES - VMEM_LEFT_TO_COMPILER
    assert 2 * block_bytes + scratch_bytes + temp_bytes + VMEM_HEADROOM <= usable
    return usable


def _params(vmem_bytes, ndims):
    return pltpu.CompilerParams(dimension_semantics=("arbitrary",) * ndims, vmem_limit_bytes=vmem_bytes)


def _silu(x):
    half = 0.5 * x
    return half + half * jnp.tanh(half)


def _dot(a, b):
    return jnp.dot(a, b, preferred_element_type=F32)


def _dot_nt(a, b):
    return lax.dot_general(a, b, (((1,), (1,)), ((), ())), preferred_element_type=F32)


def _dot_tn(a, b):
    return lax.dot_general(a, b, (((0,), (0,)), ((), ())), preferred_element_type=F32)


def _any_spec():
    return pl.BlockSpec(memory_space=pl.ANY)


ADA_TN = 1024


def _adaln_kernel(c_ref, w_ref, b_ref, o_ref):
    a = _silu(c_ref[...]).astype(BF16)
    o_ref[...] = _dot(a, w_ref[...].astype(BF16)) + b_ref[...]


def _adaln(cvec, w_ada, b_ada):
    n = 3 * D_MODEL
    blocks = MOD_ROWS * D_MODEL * 4 + D_MODEL * ADA_TN * 4 + ADA_TN * 4 + MOD_ROWS * ADA_TN * 4
    return pl.pallas_call(
        _adaln_kernel,
        out_shape=jax.ShapeDtypeStruct((DEPTH, MOD_ROWS, n), F32),
        grid=(DEPTH, n // ADA_TN),
        in_specs=[
            pl.BlockSpec((MOD_ROWS, D_MODEL), lambda l, j: (0, 0)),
            pl.BlockSpec((None, D_MODEL, ADA_TN), lambda l, j: (l, 0, j)),
            pl.BlockSpec((None, 1, ADA_TN), lambda l, j: (l, 0, j)),
        ],
        out_specs=pl.BlockSpec((None, MOD_ROWS, ADA_TN), lambda l, j: (l, 0, j)),
        compiler_params=_params(_vmem_limit(blocks, temp_bytes=D_MODEL * ADA_TN * 2), 2),
        name="adaln",
    )(cvec, w_ada, b_ada.reshape(DEPTH, 1, n))


def _mod_row(tile, tm, latent):
    return 1 + tile // (DEC_SEQ // tm) if latent else 0


NORM_TM = 512
NORM_ROWS = 32
IN_TN = 512


def _prenorm_kernel(x_ref, g_ref, sh_ref, sc_ref, h_ref):
    gain = g_ref[...]
    one_sc = 1.0 + sc_ref[...]
    sh = sh_ref[...]

    def body(r, carry):
        sl = pl.ds(pl.multiple_of(r * NORM_ROWS, NORM_ROWS), NORM_ROWS)
        x = x_ref[sl, :]
        ms = jnp.mean(x * x, axis=-1, keepdims=True)
        y = (x * lax.rsqrt(ms + EPS)) * gain
        h_ref[sl, :] = (y * one_sc + sh).astype(BF16)
        return carry

    lax.fori_loop(0, NORM_TM // NORM_ROWS, body, 0, unroll=True)


def _prenorm(x, gain, mods, layer, latent):
    ntok = x.shape[0]
    blocks = NORM_TM * D_MODEL * 4 + 3 * D_MODEL * 4 + NORM_TM * D_MODEL * 2
    mod_spec = lambda which: pl.BlockSpec(
        (None, None, None, 1, D_MODEL), lambda i: (layer, _mod_row(i, NORM_TM, latent), which, 0, 0))
    return pl.pallas_call(
        _prenorm_kernel,
        out_shape=jax.ShapeDtypeStruct((ntok, D_MODEL), BF16),
        grid=(ntok // NORM_TM,),
        in_specs=[
            pl.BlockSpec((NORM_TM, D_MODEL), lambda i: (i, 0)),
            pl.BlockSpec((None, 1, D_MODEL), lambda i: (layer, 0, 0)),
            mod_spec(0),
            mod_spec(1),
        ],
        out_specs=pl.BlockSpec((NORM_TM, D_MODEL), lambda i: (i, 0)),
        compiler_params=_params(_vmem_limit(blocks, temp_bytes=4 * NORM_ROWS * D_MODEL * 4), 1),
        name=f"prenorm_l{layer}_{'latent' if latent else 'prompt'}",
    )(x, gain, mods, mods)


def _inproj_kernel(h_ref, w_ref, o_ref):
    o_ref[...] = _dot(h_ref[...], w_ref[...].astype(BF16))


def _inproj(h, layer, w, latent):
    ntok = h.shape[0]
    n = w.shape[2]
    blocks = D_MODEL * IN_TN * 4 + ntok * IN_TN * 4
    return pl.pallas_call(
        _inproj_kernel,
        out_shape=jax.ShapeDtypeStruct((ntok, n), F32),
        grid=(n // IN_TN,),
        in_specs=[
            pl.BlockSpec((ntok, D_MODEL), lambda j: (0, 0), pipeline_mode=pl.Buffered(1)),
            pl.BlockSpec((None, D_MODEL, IN_TN), lambda j: (layer // 2, 0, j)),
        ],
        out_specs=pl.BlockSpec((ntok, IN_TN), lambda j: (0, j)),
        compiler_params=_params(_vmem_limit(blocks, ntok * D_MODEL * 2, D_MODEL * IN_TN * 2), 1),
        name=f"inproj_l{layer}_{'latent' if latent else 'prompt'}",
    )(h, w)


def _inproj_split_kernel(h_ref, w_ref, lo_ref, hi_ref, *, n_lo):
    @pl.when(pl.program_id(0) < n_lo)
    def _():
        lo_ref[...] = _dot(h_ref[...], w_ref[...].astype(BF16)).astype(lo_ref.dtype)

    @pl.when(pl.program_id(0) >= n_lo)
    def _():
        hi_ref[...] = _dot(h_ref[...], w_ref[...].astype(BF16))


def _inproj_split(h, layer, w, latent, lo_width):
    ntok = h.shape[0]
    n = w.shape[2]
    n_lo = lo_width // IN_TN
    assert lo_width % IN_TN == 0 and 0 < n_lo < n // IN_TN
    blocks = D_MODEL * IN_TN * 4 + ntok * IN_TN * (2 + 4)
    return pl.pallas_call(
        functools.partial(_inproj_split_kernel, n_lo=n_lo),
        out_shape=(jax.ShapeDtypeStruct((ntok, lo_width), BF16),
                   jax.ShapeDtypeStruct((ntok, n - lo_width), F32)),
        grid=(n // IN_TN,),
        in_specs=[
            pl.BlockSpec((ntok, D_MODEL), lambda j: (0, 0), pipeline_mode=pl.Buffered(1)),
            pl.BlockSpec((None, D_MODEL, IN_TN), lambda j: (layer // 2, 0, j)),
        ],
        out_specs=(pl.BlockSpec((ntok, IN_TN), lambda j: (0, jnp.minimum(j, n_lo - 1))),
                   pl.BlockSpec((ntok, IN_TN), lambda j: (0, jnp.maximum(j - n_lo, 0)))),
        compiler_params=_params(_vmem_limit(blocks, ntok * D_MODEL * 2, D_MODEL * IN_TN * 2), 1),
        name=f"inproj_l{layer}_{'latent' if latent else 'prompt'}",
    )(h, w)


OUT_TM = 512
OUT_SUB = 256


def _outproj_kernel(*refs, n_pieces, emit_next):
    u_refs = refs[:n_pieces]
    w_refs = refs[n_pieces:2 * n_pieces]
    pos = 2 * n_pieces
    x_ref, gain_ref, gate_ref = refs[pos:pos + 3]
    pos += 3
    if emit_next:
        ngain_ref, nshift_ref, nscale_ref = refs[pos:pos + 3]
        pos += 3
    o_ref = refs[pos]
    pos += 1
    if emit_next:
        h_ref = refs[pos]
        pos += 1
    wbf_ref = refs[pos]

    @pl.when(pl.program_id(0) == 0)
    def _():
        for p, w_ref in enumerate(w_refs):
            wbf_ref[p] = w_ref[...].astype(BF16)

    gated_gain = gate_ref[...] * gain_ref[...]
    if emit_next:
        next_gain = ngain_ref[...] * (1.0 + nscale_ref[...])
        next_shift = nshift_ref[...]
    for r in range(OUT_TM // OUT_SUB):
        rows = slice(r * OUT_SUB, (r + 1) * OUT_SUB)
        out = _dot(u_refs[0][rows, :], wbf_ref[0])
        for p in range(1, n_pieces):
            out = out + _dot(u_refs[p][rows, :], wbf_ref[p])
        ms = jnp.mean(out * out, axis=-1, keepdims=True)
        x_new = x_ref[rows, :] + (out * lax.rsqrt(ms + EPS)) * gated_gain
        o_ref[rows, :] = x_new
        if emit_next:
            ms = jnp.mean(x_new * x_new, axis=-1, keepdims=True)
            h_ref[rows, :] = ((x_new * lax.rsqrt(ms + EPS)) * next_gain + next_shift).astype(BF16)


def _outproj(us, w, x, gain_post, gain_pre, mods, layer, latent):
    ntok = x.shape[0]
    widths = [u.shape[1] for u in us]
    kp = widths[0]
    k = w.shape[1]
    assert all(kw == kp for kw in widths) and kp * len(us) == k
    emit_next = layer + 1 < DEPTH
    blocks = OUT_TM * k * 2 + 2 * OUT_TM * D_MODEL * 4 + 5 * D_MODEL * 4 + emit_next * OUT_TM * D_MODEL * 2
    tile_spec = pl.BlockSpec((OUT_TM, D_MODEL), lambda i: (i, 0))
    gain_spec = lambda l: pl.BlockSpec((None, 1, D_MODEL), lambda i: (l, 0, 0))
    mod_spec = lambda l, which: pl.BlockSpec(
        (None, None, None, 1, D_MODEL), lambda i: (l, _mod_row(i, OUT_TM, latent), which, 0, 0))
    u_specs = [pl.BlockSpec((OUT_TM, kp), lambda i: (i, 0)) for _ in us]
    w_specs = [pl.BlockSpec((None, kp, D_MODEL), lambda i, p=p: (layer // 2, p, 0),
                            pipeline_mode=pl.Buffered(1)) for p in range(len(us))]
    in_specs = u_specs + w_specs + [tile_spec, gain_spec(layer), mod_spec(layer, 2)]
    args = [*us, *([w] * len(us)), x, gain_post, mods]
    out_shape = [jax.ShapeDtypeStruct((ntok, D_MODEL), F32)]
    out_specs = [tile_spec]
    if emit_next:
        in_specs += [gain_spec(layer + 1), mod_spec(layer + 1, 0), mod_spec(layer + 1, 1)]
        args += [gain_pre, mods, mods]
        out_shape.append(jax.ShapeDtypeStruct((ntok, D_MODEL), BF16))
        out_specs.append(tile_spec)
    return pl.pallas_call(
        functools.partial(_outproj_kernel, n_pieces=len(us), emit_next=emit_next),
        out_shape=tuple(out_shape),
        grid=(ntok // OUT_TM,),
        in_specs=in_specs,
        out_specs=tuple(out_specs),
        scratch_shapes=[pltpu.VMEM((len(us), kp, D_MODEL), BF16)],
        compiler_params=_params(
            _vmem_limit(blocks, k * D_MODEL * (4 + 2), 3 * OUT_TM * D_MODEL * 2), 1),
        name=f"outproj_l{layer}_{'latent' if latent else 'prompt'}",
    )(*args)


LOG2_E = 1.4426950408889634
Q_SCALE_LOG2 = HEAD_DIM ** -0.5 * LOG2_E


def _softmax_pv(scores, values, sink=None):
    def fold(blocks, op):
        tiles = [b[:, t:t + LANES] for b in blocks for t in range(0, b.shape[-1], LANES)]
        out = tiles[0]
        for t in tiles[1:]:
            out = op(out, t)
        return out

    m = jnp.max(fold(scores, jnp.maximum), axis=-1, keepdims=True)
    if sink is not None:
        m = jnp.maximum(m, sink)
    probs = [jnp.exp2(s - m) for s in scores]
    den = jnp.sum(fold(probs, jnp.add), axis=-1, keepdims=True)
    if sink is not None:
        den = den + jnp.exp2(sink - m)
    acc = _dot(probs[0].astype(BF16), values[0])
    for p, v in zip(probs[1:], values[1:]):
        acc = acc + _dot(p.astype(BF16), v)
    return acc / den


CTX_REQS = 2


def _ctx_attn_kernel(sink_ref, p_ref, *refs):
    u_ref, ak_ref, av_ref, bk_ref, bv_ref = refs[-5:]

    for r in range(CTX_REQS):
        rows = slice(r * SEQ, (r + 1) * SEQ)

        def col(c):
            return p_ref[rows, c:c + HEAD_DIM]

        def head(cq, k, v, cg, sink, cu):
            q = (col(cq) * Q_SCALE_LOG2).astype(BF16)
            o = _softmax_pv([_dot_nt(q, k)], [v], None if sink is None else sink * LOG2_E)
            u_ref[rows, cu:cu + HEAD_DIM] = (o * _silu(col(cg))).astype(BF16)

        for n in range(A_KV_HEADS):
            k32 = col(COL_KA + n * HEAD_DIM)
            v32 = col(COL_VA + n * HEAD_DIM)
            ak_ref[r, n] = k32
            av_ref[r, n] = v32
            k = k32.astype(BF16)
            v = v32.astype(BF16)
            for g in range(A_GROUP):
                h = n * A_GROUP + g
                head(COL_QA + h * HEAD_DIM, k, v, COL_GATE + h * HEAD_DIM, sink_ref[h], h * HEAD_DIM)
        for h in range(B_HEADS):
            k32 = col(COL_KB + h * HEAD_DIM)
            v32 = col(COL_VB + h * HEAD_DIM)
            bk_ref[r, h] = k32
            bv_ref[r, h] = v32
            head(COL_QB + h * HEAD_DIM, k32.astype(BF16), v32.astype(BF16),
                 COL_GATE + A_Q + h * HEAD_DIM, None, A_Q + h * HEAD_DIM)


def _ctx_attn(proj, sink, idx, prev_caches):
    rows = CTX_REQS * SEQ
    blocks = (rows * EVEN_IN * 4 + rows * EVEN_WIDTH * 2
              + 2 * CTX_REQS * (A_KV_HEADS + B_HEADS) * SEQ * HEAD_DIM * 4)
    heads = (A_KV_HEADS, A_KV_HEADS, B_HEADS, B_HEADS)
    cache = lambda nh: jax.ShapeDtypeStruct((BATCH, N_EVEN, nh, SEQ, HEAD_DIM), F32)
    cache_spec = lambda nh: pl.BlockSpec((CTX_REQS, None, nh, SEQ, HEAD_DIM), lambda b: (b, idx, 0, 0, 0))
    n_prev = len(prev_caches)
    return pl.pallas_call(
        _ctx_attn_kernel,
        out_shape=(jax.ShapeDtypeStruct((NP, EVEN_WIDTH), BF16),) + tuple(cache(nh) for nh in heads),
        grid=(BATCH // CTX_REQS,),
        in_specs=[
            pl.BlockSpec(memory_space=pltpu.SMEM),
            pl.BlockSpec((rows, EVEN_IN), lambda b: (b, 0)),
        ] + [_any_spec() for _ in prev_caches],
        out_specs=(pl.BlockSpec((rows, EVEN_WIDTH), lambda b: (b, 0)),) + tuple(cache_spec(nh) for nh in heads),
        input_output_aliases={2 + k: 1 + k for k in range(n_prev)},
        compiler_params=_params(_vmem_limit(blocks, temp_bytes=CTX_REQS * 8 * SEQ * SEQ * 4), 1),
        name=f"ctx_attn_{idx}",
    )(sink, proj, *prev_caches)


WIN_PAD = DEC_SEQ + 2 * A_BLOCK


def _rope_tables():
    t = jnp.arange(DEC_SEQ)
    half = HEAD_DIM // 2
    nf = half // 2
    inv = ROPE_THETA ** (-jnp.arange(nf, dtype=F32) / nf)
    ang_r = (t // GRID_W).astype(F32)[:, None] * inv[None]
    ang_c = (t % GRID_W).astype(F32)[:, None] * inv[None]
    cos = jnp.concatenate([jnp.cos(ang_r)] * 2 + [jnp.cos(ang_c)] * 2, axis=-1)
    sin = jnp.concatenate([-jnp.sin(ang_r), jnp.sin(ang_r), -jnp.sin(ang_c), jnp.sin(ang_c)], axis=-1)
    return cos, sin


def _rope(x, cos, sin):
    quarter = HEAD_DIM // 4
    lane = lax.broadcasted_iota(jnp.int32, x.shape, 1)
    first = (lane & (2 * quarter - 1)) < quarter
    partner = jnp.where(first, pltpu.roll(x, HEAD_DIM - quarter, 1), pltpu.roll(x, quarter, 1))
    return x * cos + partner * sin


def _win_attn_kernel(sink_ref, q_ref, k_ref, v_ref, ck_ref, cv_ref, gate_ref, cos_ref, sin_ref,
                     u_ref, kpad_ref, vpad_ref):
    n = pl.program_id(1)
    nqb = DEC_SEQ // A_BLOCK
    rows = A_GROUP * A_BLOCK

    zeros = jnp.zeros((A_BLOCK, HEAD_DIM), BF16)
    for ref in (kpad_ref, vpad_ref):
        ref[0:A_BLOCK, :] = zeros
        ref[A_BLOCK + DEC_SEQ:WIN_PAD, :] = zeros
    kpad_ref[A_BLOCK:A_BLOCK + DEC_SEQ, :] = _rope(k_ref[...], cos_ref[...], sin_ref[...]).astype(BF16)
    vpad_ref[A_BLOCK:A_BLOCK + DEC_SEQ, :] = v_ref[...].astype(BF16)
    ck = ck_ref[...].astype(BF16)
    cv = cv_ref[...].astype(BF16)

    row = lax.broadcasted_iota(jnp.int32, (rows, 1), 0)
    qi = row & (A_BLOCK - 1)
    kk = lax.broadcasted_iota(jnp.int32, (rows, 3 * A_BLOCK), 1)
    head = lax.shift_right_logical(row, A_BLOCK.bit_length() - 1)
    sink = jnp.zeros((rows, 1), F32)
    for g in range(A_GROUP):
        sink = jnp.where(head == g, sink_ref[n * A_GROUP + g] * LOG2_E, sink)

    for j in range(nqb):
        blk = slice(j * A_BLOCK, (j + 1) * A_BLOCK)
        q = jnp.concatenate(
            [(_rope(q_ref[blk, g * HEAD_DIM:(g + 1) * HEAD_DIM], cos_ref[blk, :], sin_ref[blk, :])
              * Q_SCALE_LOG2).astype(BF16) for g in range(A_GROUP)],
            axis=0)
        band = slice(j * A_BLOCK, (j + 3) * A_BLOCK)
        s_c = _dot_nt(q, ck)
        s_w = _dot_nt(q, kpad_ref[band, :])
        lower = jnp.maximum(qi, A_BLOCK if j == 0 else 0)
        upper = jnp.minimum(qi + 2 * A_WINDOW, (2 if j == nqb - 1 else 3) * A_BLOCK - 1)
        s_w = jnp.where((kk >= lower) & (kk <= upper), s_w, NEG_INF)
        o = _softmax_pv([s_c, s_w], [cv, vpad_ref[band, :]], sink)
        for g in range(A_GROUP):
            cols = slice(g * HEAD_DIM, (g + 1) * HEAD_DIM)
            u_ref[blk, cols] = (o[g * A_BLOCK:(g + 1) * A_BLOCK] * _silu(gate_ref[blk, cols])).astype(BF16)


def _win_attn(proj, sink, cache_k, cache_v, idx, cos, sin):
    gw = A_GROUP * HEAD_DIM
    blocks = (2 * DEC_SEQ * gw * 4 + 2 * DEC_SEQ * HEAD_DIM * 4 + 2 * PAST_LEN * HEAD_DIM * 4
              + 2 * DEC_SEQ * HEAD_DIM * 4 + DEC_SEQ * gw * 2)
    cache_spec = pl.BlockSpec((None, None, None, PAST_LEN, HEAD_DIM), lambda b, n: (b, idx, n, 0, 0))
    table_spec = pl.BlockSpec((DEC_SEQ, HEAD_DIM), lambda b, n: (0, 0))
    return pl.pallas_call(
        _win_attn_kernel,
        out_shape=jax.ShapeDtypeStruct((NS, A_Q), BF16),
        grid=(DEC_BATCH, A_KV_HEADS),
        in_specs=[
            pl.BlockSpec(memory_space=pltpu.SMEM),
            pl.BlockSpec((DEC_SEQ, gw), lambda b, n: (b, COL_QA // gw + n)),
            pl.BlockSpec((DEC_SEQ, HEAD_DIM), lambda b, n: (b, COL_KA // HEAD_DIM + n)),
            pl.BlockSpec((DEC_SEQ, HEAD_DIM), lambda b, n: (b, COL_VA // HEAD_DIM + n)),
            cache_spec,
            cache_spec,
            pl.BlockSpec((DEC_SEQ, gw), lambda b, n: (b, COL_GATE // gw + n)),
            table_spec,
            table_spec,
        ],
        out_specs=pl.BlockSpec((DEC_SEQ, gw), lambda b, n: (b, n)),
        scratch_shapes=[pltpu.VMEM((WIN_PAD, HEAD_DIM), BF16), pltpu.VMEM((WIN_PAD, HEAD_DIM), BF16)],
        compiler_params=_params(
            _vmem_limit(blocks, 2 * WIN_PAD * HEAD_DIM * 2,
                        (DEC_SEQ // A_BLOCK) * 3 * A_GROUP * A_BLOCK * (PAST_LEN + 3 * A_BLOCK) * 4), 2),
        name=f"win_attn_{idx}",
    )(sink, proj, proj, proj, cache_k, cache_v, proj, cos, sin)


RPB_PAD = (-(-(2 * NA_ROWS - 1) // SUBLANES) * SUBLANES, -(-(2 * NA_COLS - 1) // LANES) * LANES)


def _na_row_offsets(g):
    offsets = []
    for rl in range(NA_QROWS):
        r = g * NA_QROWS + rl
        r0 = min(max(r - NA_ROWS // 2, 0), GRID_ROWS - NA_ROWS)
        row = []
        for kl in range(NA_KROWS):
            kr = g * (GRID_ROWS - NA_KROWS) + kl
            row.append(kr - r + NA_ROWS - 1 if r0 <= kr < r0 + NA_ROWS else None)
        offsets.append(row)
    return offsets


def _fill_na_bias(rpb_ref, bias_ref, g):
    shape = (GRID_W, 2 * GRID_W)
    c = lax.broadcasted_iota(jnp.int32, shape, 0)
    lane = lax.broadcasted_iota(jnp.int32, shape, 1)
    kc = lane & (GRID_W - 1)
    c0 = jnp.clip(c - NA_COLS // 2, 0, GRID_W - NA_COLS)
    col_ok = (kc >= c0) & (kc < c0 + NA_COLS)
    low = lane < GRID_W
    offsets = _na_row_offsets(g)
    used = sorted({d for row in offsets for d in row if d is not None})
    lo, hi = {}, {}
    for d in used:
        row = jnp.broadcast_to(rpb_ref[d:d + 1, :] * LOG2_E, shape)
        lo[d] = pltpu.roll(row, 2 * GRID_W - (NA_COLS - 1), 1, stride=1, stride_axis=0)
        hi[d] = pltpu.roll(row, GRID_W - (NA_COLS - 1), 1, stride=1, stride_axis=0)
    neg = jnp.full(shape, NEG_INF, F32)
    for rl in range(NA_QROWS):
        for p in range(NA_KROWS // 2):
            da, db = offsets[rl][2 * p], offsets[rl][2 * p + 1]
            a = neg if da is None else lo[da]
            b = neg if db is None else hi[db]
            piece = jnp.where(col_ok, jnp.where(low, a, b), NEG_INF)
            bias_ref[rl * GRID_W:(rl + 1) * GRID_W, 2 * p * GRID_W:2 * (p + 1) * GRID_W] = piece


def _na_attn_kernel(q_ref, k_ref, v_ref, ck_ref, cv_ref, gate_ref, rpb_ref, u_ref, bias_ref):
    n_groups = DEC_SEQ // NA_Q
    for g in range(n_groups):
        _fill_na_bias(rpb_ref, bias_ref.at[g], g)
    for i in range(DEC_BATCH):
        k = k_ref[i].astype(BF16)
        v = v_ref[i].astype(BF16)
        ck = ck_ref[i].astype(BF16)
        cv = cv_ref[i].astype(BF16)
        for g in range(n_groups):
            rows = slice(g * NA_Q, (g + 1) * NA_Q)
            window = slice(g * NA_KSHIFT, g * NA_KSHIFT + NA_K)
            q = (q_ref[i, rows, :] * Q_SCALE_LOG2).astype(BF16)
            s_c = _dot_nt(q, ck)
            s_n = _dot_nt(q, k[window]) + bias_ref[g]
            o = _softmax_pv([s_c, s_n], [cv, v[window]])
            u_ref[i, rows, :] = (o * _silu(gate_ref[i, rows, :])).astype(BF16)


def _na_attn(proj, rpb, cache_k, cache_v, idx):
    n_groups = DEC_SEQ // NA_Q
    proj = proj.reshape(DEC_BATCH, DEC_SEQ, EVEN_IN)
    blocks = DEC_BATCH * (4 * DEC_SEQ * HEAD_DIM * 4 + 2 * PAST_LEN * HEAD_DIM * 4
                          + DEC_SEQ * HEAD_DIM * 2) + RPB_PAD[0] * RPB_PAD[1] * 4
    tok_spec = lambda col0: pl.BlockSpec((DEC_BATCH, DEC_SEQ, HEAD_DIM), lambda h: (0, 0, col0 // HEAD_DIM + h))
    cache_spec = pl.BlockSpec((DEC_BATCH, None, None, PAST_LEN, HEAD_DIM), lambda h: (0, idx, h, 0, 0))
    out = pl.pallas_call(
        _na_attn_kernel,
        out_shape=jax.ShapeDtypeStruct((DEC_BATCH, DEC_SEQ, B_W), BF16),
        grid=(B_HEADS,),
        in_specs=[
            tok_spec(COL_QB),
            tok_spec(COL_KB),
            tok_spec(COL_VB),
            cache_spec,
            cache_spec,
            tok_spec(COL_GATE + A_Q),
            pl.BlockSpec((None, None) + RPB_PAD, lambda h: (idx, h, 0, 0)),
        ],
        out_specs=pl.BlockSpec((DEC_BATCH, DEC_SEQ, HEAD_DIM), lambda h: (0, 0, h)),
        scratch_shapes=[pltpu.VMEM((n_groups, NA_Q, NA_K), F32)],
        compiler_params=_params(
            _vmem_limit(blocks, n_groups * NA_Q * NA_K * 4,
                        DEC_BATCH * n_groups * 2 * NA_Q * (NA_K + PAST_LEN) * 4), 1),
        name=f"na_attn_{idx}",
    )(proj, proj, proj, cache_k, cache_v, proj, rpb)
    return out.reshape(NS, B_W)


RET_CHUNK = 256
assert SEQ % RET_CHUNK == 0 and DEC_SEQ % RET_CHUNK == 0


def _ret_kernel(*refs, seq, heads, has_state, emit_state, n_prev):
    dec_ref, q_ref, k_ref, v_ref, gate_ref, gn_ref = refs[:6]
    pos = 6
    if has_state:
        s0f_ref, s0b_ref = refs[pos:pos + 2]
        pos += 2
    pos += n_prev
    u_ref = refs[pos]
    pos += 1
    if emit_state:
        sf_ref, sb_ref = refs[pos:pos + 2]
        pos += 2
    stf_ref, stb_ref, o_ref, dmat_ref = refs[pos:pos + 4]

    ch = RET_CHUNK
    nc = seq // ch
    kscale = C_DK ** -0.5
    ii = lax.broadcasted_iota(jnp.int32, (ch, ch), 0).astype(F32)
    jj = lax.broadcasted_iota(jnp.int32, (ch, ch), 1).astype(F32)
    icol = lax.broadcasted_iota(jnp.int32, (ch, 1), 0).astype(F32)

    def decay_matrix(direction, head):
        log_g = -jnp.exp(jnp.full((ch, ch), dec_ref[direction, head], F32))
        diff = (ii - jj) if direction == 0 else (jj - ii)
        return jnp.where(diff >= 0, jnp.exp(log_g * jnp.maximum(diff, 0.0)), 0.0)

    def decay_vectors(direction, head):
        dec = dec_ref[direction, head]
        log_g_col = -jnp.exp(jnp.full((ch, 1), dec, F32))
        log_g_row = -jnp.exp(jnp.full((1, C_DV), dec, F32))
        if direction == 0:
            q_dec = jnp.exp(log_g_col * (icol + 1.0))
            k_dec = jnp.exp(log_g_col * (ch - 1.0 - icol)) * kscale
        else:
            q_dec = jnp.exp(log_g_col * (ch - icol))
            k_dec = jnp.exp(log_g_col * icol) * kscale
        return q_dec, k_dec, jnp.exp(log_g_row * float(ch))

    def one_head(hh):
        cols = slice(hh * C_DK, (hh + 1) * C_DK)
        head = pl.program_id(1) * heads + hh
        dmat = dmat_ref[head]
        q_dec_f, k_dec_f, chunk_dec_f = decay_vectors(0, head)
        q_dec_b, k_dec_b, chunk_dec_b = decay_vectors(1, head)
        gn = gn_ref[:, cols]

        def rows_of(c):
            return slice(c * ch, (c + 1) * ch)

        def finalize(c, o):
            mu = jnp.mean(o, axis=-1, keepdims=True)
            d = o - mu
            var = jnp.mean(d * d, axis=-1, keepdims=True)
            y = (d * lax.rsqrt(var + EPS)) * gn
            u_ref[rows_of(c), cols] = (y * _silu(gate_ref[rows_of(c), cols])).astype(BF16)

        parked = {}

        def visit(c, value):
            if c not in parked:
                parked[c] = value is not None
                if value is not None:
                    o_ref[rows_of(c), cols] = value
            elif parked[c]:
                first = o_ref[rows_of(c), cols]
                finalize(c, first if value is None else first + value)
            else:
                finalize(c, value)

        def scan_step(state_ref, s0_ref, t, q, k32, v, q_dec, k_dec, chunk_dec):
            update = _dot_tn((k32 * k_dec).astype(BF16), v)
            if t == 0 and not has_state:
                state_ref[hh] = update
                return None
            state = s0_ref[hh] if t == 0 else state_ref[hh]
            state_ref[hh] = state * chunk_dec + update
            return _dot(q, state.astype(BF16)) * q_dec

        for t in range(nc):
            cf, cb = t, nc - 1 - t
            q = q_ref[rows_of(cf), cols].astype(BF16)
            k32 = k_ref[rows_of(cf), cols]
            v = v_ref[rows_of(cf), cols].astype(BF16)
            s = _dot_nt(q, k32.astype(BF16)) * dmat
            val_f = _dot(s.astype(BF16), v)
            cross = scan_step(stf_ref, s0f_ref if has_state else None, t, q, k32, v,
                              q_dec_f, k_dec_f, chunk_dec_f)
            if cross is not None:
                val_f = val_f + cross
            if cb != cf:
                q = q_ref[rows_of(cb), cols].astype(BF16)
                k32 = k_ref[rows_of(cb), cols]
                v = v_ref[rows_of(cb), cols].astype(BF16)
            val_b = scan_step(stb_ref, s0b_ref if has_state else None, t, q, k32, v,
                              q_dec_b, k_dec_b, chunk_dec_b)
            if cb == cf:
                finalize(cf, val_f if val_b is None else val_f + val_b)
            else:
                visit(cf, val_f)
                visit(cb, val_b)

        if emit_state:
            sf_ref[hh] = stf_ref[hh]
            sb_ref[hh] = stb_ref[hh]

    @pl.when(pl.program_id(0) == 0)
    def _():
        for hh in range(heads):
            head = pl.program_id(1) * heads + hh
            dmat_ref[head] = (decay_matrix(0, head) + decay_matrix(1, head)) * kscale

    for hh in range(heads):
        one_head(hh)


RET_HEADS_PROMPT = 8
RET_HEADS_LATENT = 4


def _retention(proj, decays, gn, idx, *, latent, state_f=None, state_b=None, prev_states=()):
    qkv, gate = proj
    seq = DEC_SEQ if latent else SEQ
    nb = DEC_BATCH if latent else BATCH
    heads = RET_HEADS_LATENT if latent else RET_HEADS_PROMPT
    width = heads * C_DK
    groups = C_HEADS // heads
    has_state = latent
    emit_state = not latent
    tok = lambda kind: pl.BlockSpec((seq, width), lambda b, h: (b, kind * groups + h))
    in_specs = [pl.BlockSpec(memory_space=pltpu.SMEM), tok(0), tok(1), tok(2), tok(0),
                pl.BlockSpec((None, 1, width), lambda b, h: (idx, 0, h))]
    args = [decays, qkv, qkv, qkv, gate, gn]
    state_spec = pl.BlockSpec((None, None, heads, C_DK, C_DV), lambda b, h: (b, idx, h, 0, 0))
    if has_state:
        in_specs += [state_spec, state_spec]
        args += [state_f, state_b]
    aliases = {len(args) + k: 1 + k for k in range(len(prev_states))}
    in_specs += [_any_spec() for _ in prev_states]
    args += list(prev_states)
    out_shape = [jax.ShapeDtypeStruct((nb * seq, D_MODEL), BF16)]
    out_specs = [pl.BlockSpec((seq, width), lambda b, h: (b, h))]
    if emit_state:
        st = jax.ShapeDtypeStruct((nb, N_ODD, C_HEADS, C_DK, C_DV), F32)
        out_shape += [st, st]
        out_specs += [state_spec, state_spec]
    blocks = seq * width * (3 * 2 + 4) + width * 4 + 2 * heads * C_DK * C_DV * 4 + seq * width * 2
    scratch = 2 * heads * C_DK * C_DV * 4 + seq * width * 4 + C_HEADS * RET_CHUNK * RET_CHUNK * 4
    return pl.pallas_call(
        functools.partial(_ret_kernel, seq=seq, heads=heads, has_state=has_state, emit_state=emit_state,
                          n_prev=len(prev_states)),
        out_shape=tuple(out_shape),
        grid=(nb, groups),
        in_specs=in_specs,
        out_specs=tuple(out_specs),
        scratch_shapes=[pltpu.VMEM((heads, C_DK, C_DV), F32), pltpu.VMEM((heads, C_DK, C_DV), F32),
                        pltpu.VMEM((seq, width), F32), pltpu.VMEM((C_HEADS, RET_CHUNK, RET_CHUNK), F32)],
        input_output_aliases=aliases,
        compiler_params=_params(_vmem_limit(blocks, scratch, 24 * RET_CHUNK * C_DV * 4), 2),
        name=f"retention_{'latent' if latent else 'prompt'}_{idx}",
    )(*args)


def kernel(x_prompt, x_sample, c, cache_a_k, cache_a_v, cache_b_k, cache_b_v, state_ret_f, state_ret_b,
           c_ctx, w_ada, b_ada, norm_pre, norm_post, w_in_even, w_out_even, a_sink, na_rpb,
           w_in_odd, w_out_odd, ret_decay_f, ret_decay_b, ret_gn):
    xp = x_prompt.reshape(NP, D_MODEL)
    xs = x_sample.reshape(NS, D_MODEL)
    cvec = jnp.concatenate(
        [c_ctx[None, :], c, jnp.zeros((MOD_ROWS - 1 - DEC_BATCH, D_MODEL), F32)], axis=0)
    mods = _adaln(cvec, w_ada, b_ada).reshape(DEPTH, MOD_ROWS, 3, 1, D_MODEL)
    gain_pre = norm_pre.reshape(DEPTH, 1, D_MODEL)
    gain_post = norm_post.reshape(DEPTH, 1, D_MODEL)
    gn = ret_gn.reshape(N_ODD, 1, D_MODEL)
    cos, sin = _rope_tables()
    rpb = jnp.pad(na_rpb, ((0, 0), (0, 0), (0, RPB_PAD[0] - na_rpb.shape[2]),
                           (0, RPB_PAD[1] - na_rpb.shape[3])))
    caches = ()
    states = ()
    hp = _prenorm(xp, gain_pre, mods, 0, False)
    hs = _prenorm(xs, gain_pre, mods, 0, True)
    for layer in range(DEPTH):
        idx = layer // 2
        if layer % 2 == 0:
            proj_p = _inproj(hp, layer, w_in_even, False)
            proj_s = _inproj(hs, layer, w_in_even, True)
            u_p, *caches = _ctx_attn(proj_p, a_sink[idx], idx, caches)
            u_a = _win_attn(proj_s, a_sink[idx], cache_a_k, cache_a_v, idx, cos, sin)
            u_b = _na_attn(proj_s, rpb, cache_b_k, cache_b_v, idx)
            us_p, us_s, w_out = [u_p], [u_a, u_b], w_out_even
        else:
            proj_p = _inproj_split(hp, layer, w_in_odd, False, 3 * D_MODEL)
            proj_s = _inproj_split(hs, layer, w_in_odd, True, 3 * D_MODEL)
            decays = jnp.stack([ret_decay_f[idx], ret_decay_b[idx]], axis=0)
            u_p, *states = _retention(proj_p, decays, gn, idx, latent=False, prev_states=states)
            (u_s,) = _retention(proj_s, decays, gn, idx, latent=True,
                                state_f=state_ret_f, state_b=state_ret_b)
            us_p, us_s = [u_p], [u_s]
            w_out = w_out_odd
        xp, *hp = _outproj(us_p, w_out, xp, gain_post, gain_pre, mods, layer, False)
        xs, *hs = _outproj(us_s, w_out, xs, gain_post, gain_pre, mods, layer, True)
        hp, hs = (hp[0], hs[0]) if hp else (None, None)
    return (xp.reshape(BATCH, SEQ, D_MODEL), xs.reshape(DEC_BATCH, DEC_SEQ, D_MODEL), *caches, *states)
```

```python
import functools

import jax
import jax.numpy as jnp
from jax import lax
from jax.experimental import pallas as pl
from jax.experimental.pallas import tpu as pltpu

D_MODEL = 2048
BATCH = 16
SEQ = 256
DEPTH = 4
DEC_BATCH = 4
DEC_SEQ = 1024
PAST_LEN = 256
GRID_W = 64
HEAD_DIM = 128
A_HEADS = 8
A_KV_HEADS = 2
A_GROUP = A_HEADS // A_KV_HEADS
A_WINDOW = 128
A_BLOCK = 128
B_HEADS = 8
NA_ROWS = 8
NA_COLS = 16
C_HEADS = 8
C_DK = D_MODEL // C_HEADS
C_DV = D_MODEL // C_HEADS
ROPE_THETA = 10000.0
EPS = 1e-6

N_EVEN = (DEPTH + 1) // 2
N_ODD = DEPTH // 2
A_Q = A_HEADS * HEAD_DIM
A_KV = A_KV_HEADS * HEAD_DIM
B_W = B_HEADS * HEAD_DIM
EVEN_WIDTH = A_Q + B_W
EVEN_IN = A_Q + 2 * A_KV + 3 * B_W + EVEN_WIDTH
GRID_ROWS = DEC_SEQ // GRID_W

SUBLANES = 8
LANES = 128
V7X_VMEM_BYTES = 64 * 1024 * 1024
VMEM_LEFT_TO_COMPILER = 4 * 1024 * 1024
VMEM_HEADROOM = 4 * 1024 * 1024

NP = BATCH * SEQ
NS = DEC_BATCH * DEC_SEQ
MOD_ROWS = SUBLANES
assert 1 + DEC_BATCH <= MOD_ROWS

COL_QA = 0
COL_KA = A_Q
COL_VA = A_Q + A_KV
COL_QB = A_Q + 2 * A_KV
COL_KB = COL_QB + B_W
COL_VB = COL_KB + B_W
COL_GATE = COL_VB + B_W

NA_QROWS = 8
NA_KROWS = 12
NA_Q = NA_QROWS * GRID_W
NA_K = NA_KROWS * GRID_W
NA_KSHIFT = (GRID_ROWS - NA_KROWS) * GRID_W

F32 = jnp.float32
BF16 = jnp.bfloat16
NEG_INF = float("-inf")


def _vmem_limit(block_bytes, scratch_bytes=0, temp_bytes=0):
    usable = V7X_VMEM_BYTES - VMEM_LEFT_TO_COMPILER
    assert 2 * block_bytes + scratch_bytes + temp_bytes + VMEM_HEADROOM <= usable
    return usable


def _params(vmem_bytes, ndims):
    return pltpu.CompilerParams(dimension_semantics=("arbitrary",) * ndims, vmem_limit_bytes=vmem_bytes)


def _silu(x):
    half = 0.5 * x
    return half + half * jnp.tanh(half)


def _dot(a, b):
    return jnp.dot(a, b, preferred_element_type=F32)


def _dot_nt(a, b):
    return lax.dot_general(a, b, (((1,), (1,)), ((), ())), preferred_element_type=F32)


def _dot_tn(a, b):
    return lax.dot_general(a, b, (((0,), (0,)), ((), ())), preferred_element_type=F32)


def _any_spec():
    return pl.BlockSpec(memory_space=pl.ANY)


ADA_TN = 1024


def _adaln_kernel(c_ref, w_ref, b_ref, o_ref):
    a = _silu(c_ref[...]).astype(BF16)
    o_ref[...] = _dot(a, w_ref[...].astype(BF16)) + b_ref[...]


def _adaln(cvec, w_ada, b_ada):
    n = 3 * D_MODEL
    blocks = MOD_ROWS * D_MODEL * 4 + D_MODEL * ADA_TN * 4 + ADA_TN * 4 + MOD_ROWS * ADA_TN * 4
    return pl.pallas_call(
        _adaln_kernel,
        out_shape=jax.ShapeDtypeStruct((DEPTH, MOD_ROWS, n), F32),
        grid=(DEPTH, n // ADA_TN),
        in_specs=[
            pl.BlockSpec((MOD_ROWS, D_MODEL), lambda l, j: (0, 0)),
            pl.BlockSpec((None, D_MODEL, ADA_TN), lambda l, j: (l, 0, j)),
            pl.BlockSpec((None, 1, ADA_TN), lambda l, j: (l, 0, j)),
        ],
        out_specs=pl.BlockSpec((None, MOD_ROWS, ADA_TN), lambda l, j: (l, 0, j)),
        compiler_params=_params(_vmem_limit(blocks, temp_bytes=D_MODEL * ADA_TN * 2), 2),
        name="adaln",
    )(cvec, w_ada, b_ada.reshape(DEPTH, 1, n))


def _mod_row(tile, tm, latent):
    return 1 + tile // (DEC_SEQ // tm) if latent else 0


NORM_TM = 512
NORM_ROWS = 32
IN_TN = 512


def _prenorm_kernel(x_ref, g_ref, sh_ref, sc_ref, h_ref):
    gain = g_ref[...]
    one_sc = 1.0 + sc_ref[...]
    sh = sh_ref[...]

    def body(r, carry):
        sl = pl.ds(pl.multiple_of(r * NORM_ROWS, NORM_ROWS), NORM_ROWS)
        x = x_ref[sl, :]
        ms = jnp.mean(x * x, axis=-1, keepdims=True)
        y = (x * lax.rsqrt(ms + EPS)) * gain
        h_ref[sl, :] = (y * one_sc + sh).astype(BF16)
        return carry

    lax.fori_loop(0, NORM_TM // NORM_ROWS, body, 0, unroll=True)


def _prenorm(x, gain, mods, layer, latent):
    ntok = x.shape[0]
    blocks = NORM_TM * D_MODEL * 4 + 3 * D_MODEL * 4 + NORM_TM * D_MODEL * 2
    mod_spec = lambda which: pl.BlockSpec(
        (None, None, None, 1, D_MODEL), lambda i: (layer, _mod_row(i, NORM_TM, latent), which, 0, 0))
    return pl.pallas_call(
        _prenorm_kernel,
        out_shape=jax.ShapeDtypeStruct((ntok, D_MODEL), BF16),
        grid=(ntok // NORM_TM,),
        in_specs=[
            pl.BlockSpec((NORM_TM, D_MODEL), lambda i: (i, 0)),
            pl.BlockSpec((None, 1, D_MODEL), lambda i: (layer, 0, 0)),
            mod_spec(0),
            mod_spec(1),
        ],
        out_specs=pl.BlockSpec((NORM_TM, D_MODEL), lambda i: (i, 0)),
        compiler_params=_params(_vmem_limit(blocks, temp_bytes=4 * NORM_ROWS * D_MODEL * 4), 1),
        name=f"prenorm_l{layer}_{'latent' if latent else 'prompt'}",
    )(x, gain, mods, mods)


def _inproj_kernel(h_ref, w_ref, o_ref):
    o_ref[...] = _dot(h_ref[...], w_ref[...].astype(BF16))


def _inproj(h, layer, w, latent):
    ntok = h.shape[0]
    n = w.shape[2]
    blocks = D_MODEL * IN_TN * 4 + ntok * IN_TN * 4
    return pl.pallas_call(
        _inproj_kernel,
        out_shape=jax.ShapeDtypeStruct((ntok, n), F32),
        grid=(n // IN_TN,),
        in_specs=[
            pl.BlockSpec((ntok, D_MODEL), lambda j: (0, 0), pipeline_mode=pl.Buffered(1)),
            pl.BlockSpec((None, D_MODEL, IN_TN), lambda j: (layer // 2, 0, j)),
        ],
        out_specs=pl.BlockSpec((ntok, IN_TN), lambda j: (0, j)),
        compiler_params=_params(_vmem_limit(blocks, ntok * D_MODEL * 2, D_MODEL * IN_TN * 2), 1),
        name=f"inproj_l{layer}_{'latent' if latent else 'prompt'}",
    )(h, w)


def _inproj_split_kernel(h_ref, w_ref, lo_ref, hi_ref, *, n_lo):
    @pl.when(pl.program_id(0) < n_lo)
    def _():
        lo_ref[...] = _dot(h_ref[...], w_ref[...].astype(BF16)).astype(lo_ref.dtype)

    @pl.when(pl.program_id(0) >= n_lo)
    def _():
        hi_ref[...] = _dot(h_ref[...], w_ref[...].astype(BF16))


def _inproj_split(h, layer, w, latent, lo_width):
    ntok = h.shape[0]
    n = w.shape[2]
    n_lo = lo_width // IN_TN
    assert lo_width % IN_TN == 0 and 0 < n_lo < n // IN_TN
    blocks = D_MODEL * IN_TN * 4 + ntok * IN_TN * (2 + 4)
    return pl.pallas_call(
        functools.partial(_inproj_split_kernel, n_lo=n_lo),
        out_shape=(jax.ShapeDtypeStruct((ntok, lo_width), BF16),
                   jax.ShapeDtypeStruct((ntok, n - lo_width), F32)),
        grid=(n // IN_TN,),
        in_specs=[
            pl.BlockSpec((ntok, D_MODEL), lambda j: (0, 0), pipeline_mode=pl.Buffered(1)),
            pl.BlockSpec((None, D_MODEL, IN_TN), lambda j: (layer // 2, 0, j)),
        ],
        out_specs=(pl.BlockSpec((ntok, IN_TN), lambda j: (0, jnp.minimum(j, n_lo - 1))),
                   pl.BlockSpec((ntok, IN_TN), lambda j: (0, jnp.maximum(j - n_lo, 0)))),
        compiler_params=_params(_vmem_limit(blocks, ntok * D_MODEL * 2, D_MODEL * IN_TN * 2), 1),
        name=f"inproj_l{layer}_{'latent' if latent else 'prompt'}",
    )(h, w)


OUT_TM = 512
OUT_SUB = 256


def _outproj_kernel(*refs, n_pieces, emit_next):
    u_refs = refs[:n_pieces]
    w_refs = refs[n_pieces:2 * n_pieces]
    pos = 2 * n_pieces
    x_ref, gain_ref, gate_ref = refs[pos:pos + 3]
    pos += 3
    if emit_next:
        ngain_ref, nshift_ref, nscale_ref = refs[pos:pos + 3]
        pos += 3
    o_ref = refs[pos]
    pos += 1
    if emit_next:
        h_ref = refs[pos]
        pos += 1
    wbf_ref = refs[pos]

    @pl.when(pl.program_id(0) == 0)
    def _():
        for p, w_ref in enumerate(w_refs):
            wbf_ref[p] = w_ref[...].astype(BF16)

    gated_gain = gate_ref[...] * gain_ref[...]
    if emit_next:
        next_gain = ngain_ref[...] * (1.0 + nscale_ref[...])
        next_shift = nshift_ref[...]
    for r in range(OUT_TM // OUT_SUB):
        rows = slice(r * OUT_SUB, (r + 1) * OUT_SUB)
        out = _dot(u_refs[0][rows, :], wbf_ref[0])
        for p in range(1, n_pieces):
            out = out + _dot(u_refs[p][rows, :], wbf_ref[p])
        ms = jnp.mean(out * out, axis=-1, keepdims=True)
        x_new = x_ref[rows, :] + (out * lax.rsqrt(ms + EPS)) * gated_gain
        o_ref[rows, :] = x_new
        if emit_next:
            ms = jnp.mean(x_new * x_new, axis=-1, keepdims=True)
            h_ref[rows, :] = ((x_new * lax.rsqrt(ms + EPS)) * next_gain + next_shift).astype(BF16)


def _outproj(us, w, x, gain_post, gain_pre, mods, layer, latent):
    ntok = x.shape[0]
    widths = [u.shape[1] for u in us]
    kp = widths[0]
    k = w.shape[1]
    assert all(kw == kp for kw in widths) and kp * len(us) == k
    emit_next = layer + 1 < DEPTH
    blocks = OUT_TM * k * 2 + 2 * OUT_TM * D_MODEL * 4 + 5 * D_MODEL * 4 + emit_next * OUT_TM * D_MODEL * 2
    tile_spec = pl.BlockSpec((OUT_TM, D_MODEL), lambda i: (i, 0))
    gain_spec = lambda l: pl.BlockSpec((None, 1, D_MODEL), lambda i: (l, 0, 0))
    mod_spec = lambda l, which: pl.BlockSpec(
        (None, None, None, 1, D_MODEL), lambda i: (l, _mod_row(i, OUT_TM, latent), which, 0, 0))
    u_specs = [pl.BlockSpec((OUT_TM, kp), lambda i: (i, 0)) for _ in us]
    w_specs = [pl.BlockSpec((None, kp, D_MODEL), lambda i, p=p: (layer // 2, p, 0),
                            pipeline_mode=pl.Buffered(1)) for p in range(len(us))]
    in_specs = u_specs + w_specs + [tile_spec, gain_spec(layer), mod_spec(layer, 2)]
    args = [*us, *([w] * len(us)), x, gain_post, mods]
    out_shape = [jax.ShapeDtypeStruct((ntok, D_MODEL), F32)]
    out_specs = [tile_spec]
    if emit_next:
        in_specs += [gain_spec(layer + 1), mod_spec(layer + 1, 0), mod_spec(layer + 1, 1)]
        args += [gain_pre, mods, mods]
        out_shape.append(jax.ShapeDtypeStruct((ntok, D_MODEL), BF16))
        out_specs.append(tile_spec)
    return pl.pallas_call(
        functools.partial(_outproj_kernel, n_pieces=len(us), emit_next=emit_next),
        out_shape=tuple(out_shape),
        grid=(ntok // OUT_TM,),
        in_specs=in_specs,
        out_specs=tuple(out_specs),
        scratch_shapes=[pltpu.VMEM((len(us), kp, D_MODEL), BF16)],
        compiler_params=_params(
            _vmem_limit(blocks, k * D_MODEL * (4 + 2), 3 * OUT_TM * D_MODEL * 2), 1),
        name=f"outproj_l{layer}_{'latent' if latent else 'prompt'}",
    )(*args)


LOG2_E = 1.4426950408889634
Q_SCALE_LOG2 = HEAD_DIM ** -0.5 * LOG2_E


def _softmax_pv(scores, values, sink=None):
    def fold(blocks, op):
        tiles = [b[:, t:t + LANES] for b in blocks for t in range(0, b.shape[-1], LANES)]
        out = tiles[0]
        for t in tiles[1:]:
            out = op(out, t)
        return out

    m = jnp.max(fold(scores, jnp.maximum), axis=-1, keepdims=True)
    if sink is not None:
        m = jnp.maximum(m, sink)
    width = values[0].shape[-1]
    acc = None
    for s, v in zip(scores, values):
        p = jnp.exp2(s - m).astype(BF16)
        pv = _dot(p, jnp.concatenate([v, jnp.ones_like(v)], axis=1))
        acc = pv if acc is None else acc + pv
    den = acc[:, width:]
    if sink is not None:
        den = den + jnp.exp2(sink - m)
    return acc[:, :width] / den


def _ctx_attn_kernel(sink_ref, p_ref, *refs):
    u_ref, ak_ref, av_ref, bk_ref, bv_ref = refs[-5:]

    def col(c):
        return p_ref[:, c:c + HEAD_DIM]

    def head(cq, k, v, cg, sink, cu):
        q = (col(cq) * Q_SCALE_LOG2).astype(BF16)
        o = _softmax_pv([_dot_nt(q, k)], [v], None if sink is None else sink * LOG2_E)
        u_ref[:, cu:cu + HEAD_DIM] = (o * _silu(col(cg))).astype(BF16)

    for n in range(A_KV_HEADS):
        k32 = col(COL_KA + n * HEAD_DIM)
        v32 = col(COL_VA + n * HEAD_DIM)
        ak_ref[n] = k32
        av_ref[n] = v32
        k = k32.astype(BF16)
        v = v32.astype(BF16)
        for g in range(A_GROUP):
            h = n * A_GROUP + g
            head(COL_QA + h * HEAD_DIM, k, v, COL_GATE + h * HEAD_DIM, sink_ref[h], h * HEAD_DIM)
    for h in range(B_HEADS):
        k32 = col(COL_KB + h * HEAD_DIM)
        v32 = col(COL_VB + h * HEAD_DIM)
        bk_ref[h] = k32
        bv_ref[h] = v32
        head(COL_QB + h * HEAD_DIM, k32.astype(BF16), v32.astype(BF16),
             COL_GATE + A_Q + h * HEAD_DIM, None, A_Q + h * HEAD_DIM)


def _ctx_attn(proj, sink, idx, prev_caches):
    blocks = (SEQ * EVEN_IN * 4 + SEQ * EVEN_WIDTH * 2
              + 2 * (A_KV_HEADS + B_HEADS) * SEQ * HEAD_DIM * 4)
    heads = (A_KV_HEADS, A_KV_HEADS, B_HEADS, B_HEADS)
    cache = lambda nh: jax.ShapeDtypeStruct((BATCH, N_EVEN, nh, SEQ, HEAD_DIM), F32)
    cache_spec = lambda nh: pl.BlockSpec((None, None, nh, SEQ, HEAD_DIM), lambda b: (b, idx, 0, 0, 0))
    n_prev = len(prev_caches)
    return pl.pallas_call(
        _ctx_attn_kernel,
        out_shape=(jax.ShapeDtypeStruct((NP, EVEN_WIDTH), BF16),) + tuple(cache(nh) for nh in heads),
        grid=(BATCH,),
        in_specs=[
            pl.BlockSpec(memory_space=pltpu.SMEM),
            pl.BlockSpec((SEQ, EVEN_IN), lambda b: (b, 0)),
        ] + [_any_spec() for _ in prev_caches],
        out_specs=(pl.BlockSpec((SEQ, EVEN_WIDTH), lambda b: (b, 0)),) + tuple(cache_spec(nh) for nh in heads),
        input_output_aliases={2 + k: 1 + k for k in range(n_prev)},
        compiler_params=_params(_vmem_limit(blocks, temp_bytes=8 * SEQ * SEQ * 4), 1),
        name=f"ctx_attn_{idx}",
    )(sink, proj, *prev_caches)


WIN_PAD = DEC_SEQ + 2 * A_BLOCK


def _rope_tables():
    t = jnp.arange(DEC_SEQ)
    half = HEAD_DIM // 2
    nf = half // 2
    inv = ROPE_THETA ** (-jnp.arange(nf, dtype=F32) / nf)
    ang_r = (t // GRID_W).astype(F32)[:, None] * inv[None]
    ang_c = (t % GRID_W).astype(F32)[:, None] * inv[None]
    cos = jnp.concatenate([jnp.cos(ang_r)] * 2 + [jnp.cos(ang_c)] * 2, axis=-1)
    sin = jnp.concatenate([-jnp.sin(ang_r), jnp.sin(ang_r), -jnp.sin(ang_c), jnp.sin(ang_c)], axis=-1)
    return cos, sin


def _rope(x, cos, sin):
    quarter = HEAD_DIM // 4
    lane = lax.broadcasted_iota(jnp.int32, x.shape, 1)
    first = (lane & (2 * quarter - 1)) < quarter
    partner = jnp.where(first, pltpu.roll(x, HEAD_DIM - quarter, 1), pltpu.roll(x, quarter, 1))
    return x * cos + partner * sin


def _win_attn_kernel(sink_ref, q_ref, k_ref, v_ref, ck_ref, cv_ref, gate_ref, cos_ref, sin_ref,
                     u_ref, kpad_ref, vpad_ref):
    n = pl.program_id(1)
    nqb = DEC_SEQ // A_BLOCK
    rows = A_GROUP * A_BLOCK

    zeros = jnp.zeros((A_BLOCK, HEAD_DIM), BF16)
    for ref in (kpad_ref, vpad_ref):
        ref[0:A_BLOCK, :] = zeros
        ref[A_BLOCK + DEC_SEQ:WIN_PAD, :] = zeros
    kpad_ref[A_BLOCK:A_BLOCK + DEC_SEQ, :] = _rope(k_ref[...], cos_ref[...], sin_ref[...]).astype(BF16)
    vpad_ref[A_BLOCK:A_BLOCK + DEC_SEQ, :] = v_ref[...].astype(BF16)
    ck = ck_ref[...].astype(BF16)
    cv = cv_ref[...].astype(BF16)

    row = lax.broadcasted_iota(jnp.int32, (rows, 1), 0)
    qi = row & (A_BLOCK - 1)
    kk = lax.broadcasted_iota(jnp.int32, (rows, 3 * A_BLOCK), 1)
    head = lax.shift_right_logical(row, A_BLOCK.bit_length() - 1)
    sink = jnp.zeros((rows, 1), F32)
    for g in range(A_GROUP):
        sink = jnp.where(head == g, sink_ref[n * A_GROUP + g] * LOG2_E, sink)

    for j in range(nqb):
        blk = slice(j * A_BLOCK, (j + 1) * A_BLOCK)
        q = jnp.concatenate(
            [(_rope(q_ref[blk, g * HEAD_DIM:(g + 1) * HEAD_DIM], cos_ref[blk, :], sin_ref[blk, :])
              * Q_SCALE_LOG2).astype(BF16) for g in range(A_GROUP)],
            axis=0)
        band = slice(j * A_BLOCK, (j + 3) * A_BLOCK)
        s_c = _dot_nt(q, ck)
        s_w = _dot_nt(q, kpad_ref[band, :])
        lower = jnp.maximum(qi, A_BLOCK if j == 0 else 0)
        upper = jnp.minimum(qi + 2 * A_WINDOW, (2 if j == nqb - 1 else 3) * A_BLOCK - 1)
        s_w = jnp.where((kk >= lower) & (kk <= upper), s_w, NEG_INF)
        o = _softmax_pv([s_c, s_w], [cv, vpad_ref[band, :]], sink)
        for g in range(A_GROUP):
            cols = slice(g * HEAD_DIM, (g + 1) * HEAD_DIM)
            u_ref[blk, cols] = (o[g * A_BLOCK:(g + 1) * A_BLOCK] * _silu(gate_ref[blk, cols])).astype(BF16)


def _win_attn(proj, sink, cache_k, cache_v, idx, cos, sin):
    gw = A_GROUP * HEAD_DIM
    blocks = (2 * DEC_SEQ * gw * 4 + 2 * DEC_SEQ * HEAD_DIM * 4 + 2 * PAST_LEN * HEAD_DIM * 4
              + 2 * DEC_SEQ * HEAD_DIM * 4 + DEC_SEQ * gw * 2)
    cache_spec = pl.BlockSpec((None, None, None, PAST_LEN, HEAD_DIM), lambda b, n: (b, idx, n, 0, 0))
    table_spec = pl.BlockSpec((DEC_SEQ, HEAD_DIM), lambda b, n: (0, 0))
    return pl.pallas_call(
        _win_attn_kernel,
        out_shape=jax.ShapeDtypeStruct((NS, A_Q), BF16),
        grid=(DEC_BATCH, A_KV_HEADS),
        in_specs=[
            pl.BlockSpec(memory_space=pltpu.SMEM),
            pl.BlockSpec((DEC_SEQ, gw), lambda b, n: (b, COL_QA // gw + n)),
            pl.BlockSpec((DEC_SEQ, HEAD_DIM), lambda b, n: (b, COL_KA // HEAD_DIM + n)),
            pl.BlockSpec((DEC_SEQ, HEAD_DIM), lambda b, n: (b, COL_VA // HEAD_DIM + n)),
            cache_spec,
            cache_spec,
            pl.BlockSpec((DEC_SEQ, gw), lambda b, n: (b, COL_GATE // gw + n)),
            table_spec,
            table_spec,
        ],
        out_specs=pl.BlockSpec((DEC_SEQ, gw), lambda b, n: (b, n)),
        scratch_shapes=[pltpu.VMEM((WIN_PAD, HEAD_DIM), BF16), pltpu.VMEM((WIN_PAD, HEAD_DIM), BF16)],
        compiler_params=_params(
            _vmem_limit(blocks, 2 * WIN_PAD * HEAD_DIM * 2,
                        (DEC_SEQ // A_BLOCK) * 3 * A_GROUP * A_BLOCK * (PAST_LEN + 3 * A_BLOCK) * 4), 2),
        name=f"win_attn_{idx}",
    )(sink, proj, proj, proj, cache_k, cache_v, proj, cos, sin)


RPB_PAD = (-(-(2 * NA_ROWS - 1) // SUBLANES) * SUBLANES, -(-(2 * NA_COLS - 1) // LANES) * LANES)


def _na_row_offsets(g):
    offsets = []
    for rl in range(NA_QROWS):
        r = g * NA_QROWS + rl
        r0 = min(max(r - NA_ROWS // 2, 0), GRID_ROWS - NA_ROWS)
        row = []
        for kl in range(NA_KROWS):
            kr = g * (GRID_ROWS - NA_KROWS) + kl
            row.append(kr - r + NA_ROWS - 1 if r0 <= kr < r0 + NA_ROWS else None)
        offsets.append(row)
    return offsets


def _fill_na_bias(rpb_ref, bias_ref, g):
    shape = (GRID_W, 2 * GRID_W)
    c = lax.broadcasted_iota(jnp.int32, shape, 0)
    lane = lax.broadcasted_iota(jnp.int32, shape, 1)
    kc = lane & (GRID_W - 1)
    c0 = jnp.clip(c - NA_COLS // 2, 0, GRID_W - NA_COLS)
    col_ok = (kc >= c0) & (kc < c0 + NA_COLS)
    low = lane < GRID_W
    offsets = _na_row_offsets(g)
    used = sorted({d for row in offsets for d in row if d is not None})
    lo, hi = {}, {}
    for d in used:
        row = jnp.broadcast_to(rpb_ref[d:d + 1, :] * LOG2_E, shape)
        lo[d] = pltpu.roll(row, 2 * GRID_W - (NA_COLS - 1), 1, stride=1, stride_axis=0)
        hi[d] = pltpu.roll(row, GRID_W - (NA_COLS - 1), 1, stride=1, stride_axis=0)
    neg = jnp.full(shape, NEG_INF, F32)
    for rl in range(NA_QROWS):
        for p in range(NA_KROWS // 2):
            da, db = offsets[rl][2 * p], offsets[rl][2 * p + 1]
            a = neg if da is None else lo[da]
            b = neg if db is None else hi[db]
            piece = jnp.where(col_ok, jnp.where(low, a, b), NEG_INF)
            bias_ref[rl * GRID_W:(rl + 1) * GRID_W, 2 * p * GRID_W:2 * (p + 1) * GRID_W] = piece


def _na_attn_kernel(q_ref, k_ref, v_ref, ck_ref, cv_ref, gate_ref, rpb_ref, u_ref, bias_ref):
    n_groups = DEC_SEQ // NA_Q
    for g in range(n_groups):
        _fill_na_bias(rpb_ref, bias_ref.at[g], g)
    for i in range(DEC_BATCH):
        k = k_ref[i].astype(BF16)
        v = v_ref[i].astype(BF16)
        ck = ck_ref[i].astype(BF16)
        cv = cv_ref[i].astype(BF16)
        for g in range(n_groups):
            rows = slice(g * NA_Q, (g + 1) * NA_Q)
            window = slice(g * NA_KSHIFT, g * NA_KSHIFT + NA_K)
            q = (q_ref[i, rows, :] * Q_SCALE_LOG2).astype(BF16)
            s_c = _dot_nt(q, ck)
            s_n = _dot_nt(q, k[window]) + bias_ref[g]
            o = _softmax_pv([s_c, s_n], [cv, v[window]])
            u_ref[i, rows, :] = (o * _silu(gate_ref[i, rows, :])).astype(BF16)


def _na_attn(proj, rpb, cache_k, cache_v, idx):
    n_groups = DEC_SEQ // NA_Q
    proj = proj.reshape(DEC_BATCH, DEC_SEQ, EVEN_IN)
    blocks = DEC_BATCH * (4 * DEC_SEQ * HEAD_DIM * 4 + 2 * PAST_LEN * HEAD_DIM * 4
                          + DEC_SEQ * HEAD_DIM * 2) + RPB_PAD[0] * RPB_PAD[1] * 4
    tok_spec = lambda col0: pl.BlockSpec((DEC_BATCH, DEC_SEQ, HEAD_DIM), lambda h: (0, 0, col0 // HEAD_DIM + h))
    cache_spec = pl.BlockSpec((DEC_BATCH, None, None, PAST_LEN, HEAD_DIM), lambda h: (0, idx, h, 0, 0))
    out = pl.pallas_call(
        _na_attn_kernel,
        out_shape=jax.ShapeDtypeStruct((DEC_BATCH, DEC_SEQ, B_W), BF16),
        grid=(B_HEADS,),
        in_specs=[
            tok_spec(COL_QB),
            tok_spec(COL_KB),
            tok_spec(COL_VB),
            cache_spec,
            cache_spec,
            tok_spec(COL_GATE + A_Q),
            pl.BlockSpec((None, None) + RPB_PAD, lambda h: (idx, h, 0, 0)),
        ],
        out_specs=pl.BlockSpec((DEC_BATCH, DEC_SEQ, HEAD_DIM), lambda h: (0, 0, h)),
        scratch_shapes=[pltpu.VMEM((n_groups, NA_Q, NA_K), F32)],
        compiler_params=_params(
            _vmem_limit(blocks, n_groups * NA_Q * NA_K * 4,
                        DEC_BATCH * n_groups * 2 * NA_Q * (NA_K + PAST_LEN) * 4), 1),
        name=f"na_attn_{idx}",
    )(proj, proj, proj, cache_k, cache_v, proj, rpb)
    return out.reshape(NS, B_W)


RET_CHUNK = 256
assert SEQ % RET_CHUNK == 0 and DEC_SEQ % RET_CHUNK == 0


def _ret_kernel(*refs, seq, heads, has_state, emit_state, n_prev):
    dec_ref, q_ref, k_ref, v_ref, gate_ref, gn_ref = refs[:6]
    pos = 6
    if has_state:
        s0f_ref, s0b_ref = refs[pos:pos + 2]
        pos += 2
    pos += n_prev
    u_ref = refs[pos]
    pos += 1
    if emit_state:
        sf_ref, sb_ref = refs[pos:pos + 2]
        pos += 2
    stf_ref, stb_ref, o_ref, dmat_ref = refs[pos:pos + 4]

    ch = RET_CHUNK
    nc = seq // ch
    kscale = C_DK ** -0.5
    ii = lax.broadcasted_iota(jnp.int32, (ch, ch), 0).astype(F32)
    jj = lax.broadcasted_iota(jnp.int32, (ch, ch), 1).astype(F32)
    icol = lax.broadcasted_iota(jnp.int32, (ch, 1), 0).astype(F32)

    def decay_matrix(direction, head):
        log_g = -jnp.exp(jnp.full((ch, ch), dec_ref[direction, head], F32))
        diff = (ii - jj) if direction == 0 else (jj - ii)
        return jnp.where(diff >= 0, jnp.exp(log_g * jnp.maximum(diff, 0.0)), 0.0)

    def decay_vectors(direction, head):
        dec = dec_ref[direction, head]
        log_g_col = -jnp.exp(jnp.full((ch, 1), dec, F32))
        log_g_row = -jnp.exp(jnp.full((1, C_DV), dec, F32))
        if direction == 0:
            q_dec = jnp.exp(log_g_col * (icol + 1.0))
            k_dec = jnp.exp(log_g_col * (ch - 1.0 - icol)) * kscale
        else:
            q_dec = jnp.exp(log_g_col * (ch - icol))
            k_dec = jnp.exp(log_g_col * icol) * kscale
        return q_dec, k_dec, jnp.exp(log_g_row * float(ch))

    def one_head(hh):
        cols = slice(hh * C_DK, (hh + 1) * C_DK)
        head = pl.program_id(1) * heads + hh
        dmat = dmat_ref[head]
        q_dec_f, k_dec_f, chunk_dec_f = decay_vectors(0, head)
        q_dec_b, k_dec_b, chunk_dec_b = decay_vectors(1, head)
        gn = gn_ref[:, cols]

        def rows_of(c):
            return slice(c * ch, (c + 1) * ch)

        def finalize(c, o):
            mu = jnp.mean(o, axis=-1, keepdims=True)
            d = o - mu
            var = jnp.mean(d * d, axis=-1, keepdims=True)
            y = (d * lax.rsqrt(var + EPS)) * gn
            u_ref[rows_of(c), cols] = (y * _silu(gate_ref[rows_of(c), cols])).astype(BF16)

        parked = {}

        def visit(c, value):
            if c not in parked:
                parked[c] = value is not None
                if value is not None:
                    o_ref[rows_of(c), cols] = value
            elif parked[c]:
                first = o_ref[rows_of(c), cols]
                finalize(c, first if value is None else first + value)
            else:
                finalize(c, value)

        def scan_step(state_ref, s0_ref, t, q, k32, v, q_dec, k_dec, chunk_dec):
            update = _dot_tn((k32 * k_dec).astype(BF16), v)
            if t == 0 and not has_state:
                state_ref[hh] = update
                return None
            state = s0_ref[hh] if t == 0 else state_ref[hh]
            state_ref[hh] = state * chunk_dec + update
            return _dot(q, state.astype(BF16)) * q_dec

        for t in range(nc):
            cf, cb = t, nc - 1 - t
            q = q_ref[rows_of(cf), cols].astype(BF16)
            k32 = k_ref[rows_of(cf), cols]
            v = v_ref[rows_of(cf), cols].astype(BF16)
            s = _dot_nt(q, k32.astype(BF16)) * dmat
            val_f = _dot(s.astype(BF16), v)
            cross = scan_step(stf_ref, s0f_ref if has_state else None, t, q, k32, v,
                              q_dec_f, k_dec_f, chunk_dec_f)
            if cross is not None:
                val_f = val_f + cross
            if cb != cf:
                q = q_ref[rows_of(cb), cols].astype(BF16)
                k32 = k_ref[rows_of(cb), cols]
                v = v_ref[rows_of(cb), cols].astype(BF16)
            val_b = scan_step(stb_ref, s0b_ref if has_state else None, t, q, k32, v,
                              q_dec_b, k_dec_b, chunk_dec_b)
            if cb == cf:
                finalize(cf, val_f if val_b is None else val_f + val_b)
            else:
                visit(cf, val_f)
                visit(cb, val_b)

        if emit_state:
            sf_ref[hh] = stf_ref[hh]
            sb_ref[hh] = stb_ref[hh]

    @pl.when(pl.program_id(0) == 0)
    def _():
        for hh in range(heads):
            head = pl.program_id(1) * heads + hh
            dmat_ref[head] = (decay_matrix(0, head) + decay_matrix(1, head)) * kscale

    for hh in range(heads):
        one_head(hh)


RET_HEADS_PROMPT = 8
RET_HEADS_LATENT = 4


def _retention(proj, decays, gn, idx, *, latent, state_f=None, state_b=None, prev_states=()):
    qkv, gate = proj
    seq = DEC_SEQ if latent else SEQ
    nb = DEC_BATCH if latent else BATCH
    heads = RET_HEADS_LATENT if latent else RET_HEADS_PROMPT
    width = heads * C_DK
    groups = C_HEADS // heads
    has_state = latent
    emit_state = not latent
    tok = lambda kind: pl.BlockSpec((seq, width), lambda b, h: (b, kind * groups + h))
    in_specs = [pl.BlockSpec(memory_space=pltpu.SMEM), tok(0), tok(1), tok(2), tok(0),
                pl.BlockSpec((None, 1, width), lambda b, h: (idx, 0, h))]
    args = [decays, qkv, qkv, qkv, gate, gn]
    state_spec = pl.BlockSpec((None, None, heads, C_DK, C_DV), lambda b, h: (b, idx, h, 0, 0))
    if has_state:
        in_specs += [state_spec, state_spec]
        args += [state_f, state_b]
    aliases = {len(args) + k: 1 + k for k in range(len(prev_states))}
    in_specs += [_any_spec() for _ in prev_states]
    args += list(prev_states)
    out_shape = [jax.ShapeDtypeStruct((nb * seq, D_MODEL), BF16)]
    out_specs = [pl.BlockSpec((seq, width), lambda b, h: (b, h))]
    if emit_state:
        st = jax.ShapeDtypeStruct((nb, N_ODD, C_HEADS, C_DK, C_DV), F32)
        out_shape += [st, st]
        out_specs += [state_spec, state_spec]
    blocks = seq * width * (3 * 2 + 4) + width * 4 + 2 * heads * C_DK * C_DV * 4 + seq * width * 2
    scratch = 2 * heads * C_DK * C_DV * 4 + seq * width * 4 + C_HEADS * RET_CHUNK * RET_CHUNK * 4
    return pl.pallas_call(
        functools.partial(_ret_kernel, seq=seq, heads=heads, has_state=has_state, emit_state=emit_state,
                          n_prev=len(prev_states)),
        out_shape=tuple(out_shape),
        grid=(nb, groups),
        in_specs=in_specs,
        out_specs=tuple(out_specs),
        scratch_shapes=[pltpu.VMEM((heads, C_DK, C_DV), F32), pltpu.VMEM((heads, C_DK, C_DV), F32),
                        pltpu.VMEM((seq, width), F32), pltpu.VMEM((C_HEADS, RET_CHUNK, RET_CHUNK), F32)],
        input_output_aliases=aliases,
        compiler_params=_params(_vmem_limit(blocks, scratch, 24 * RET_CHUNK * C_DV * 4), 2),
        name=f"retention_{'latent' if latent else 'prompt'}_{idx}",
    )(*args)


def kernel(x_prompt, x_sample, c, cache_a_k, cache_a_v, cache_b_k, cache_b_v, state_ret_f, state_ret_b,
           c_ctx, w_ada, b_ada, norm_pre, norm_post, w_in_even, w_out_even, a_sink, na_rpb,
           w_in_odd, w_out_odd, ret_decay_f, ret_decay_b, ret_gn):
    xp = x_prompt.reshape(NP, D_MODEL)
    xs = x_sample.reshape(NS, D_MODEL)
    cvec = jnp.concatenate(
        [c_ctx[None, :], c, jnp.zeros((MOD_ROWS - 1 - DEC_BATCH, D_MODEL), F32)], axis=0)
    mods = _adaln(cvec, w_ada, b_ada).reshape(DEPTH, MOD_ROWS, 3, 1, D_MODEL)
    gain_pre = norm_pre.reshape(DEPTH, 1, D_MODEL)
    gain_post = norm_post.reshape(DEPTH, 1, D_MODEL)
    gn = ret_gn.reshape(N_ODD, 1, D_MODEL)
    cos, sin = _rope_tables()
    rpb = jnp.pad(na_rpb, ((0, 0), (0, 0), (0, RPB_PAD[0] - na_rpb.shape[2]),
                           (0, RPB_PAD[1] - na_rpb.shape[3])))
    caches = ()
    states = ()
    hp = _prenorm(xp, gain_pre, mods, 0, False)
    hs = _prenorm(xs, gain_pre, mods, 0, True)
    for layer in range(DEPTH):
        idx = layer // 2
        if layer % 2 == 0:
            proj_p = _inproj(hp, layer, w_in_even, False)
            proj_s = _inproj(hs, layer, w_in_even, True)
            u_p, *caches = _ctx_attn(proj_p, a_sink[idx], idx, caches)
            u_a = _win_attn(proj_s, a_sink[idx], cache_a_k, cache_a_v, idx, cos, sin)
            u_b = _na_attn(proj_s, rpb, cache_b_k, cache_b_v, idx)
            us_p, us_s, w_out = [u_p], [u_a, u_b], w_out_even
        else:
            proj_p = _inproj_split(hp, layer, w_in_odd, False, 3 * D_MODEL)
            proj_s = _inproj_split(hs, layer, w_in_odd, True, 3 * D_MODEL)
            decays = jnp.stack([ret_decay_f[idx], ret_decay_b[idx]], axis=0)
            u_p, *states = _retention(proj_p, decays, gn, idx, latent=False, prev_states=states)
            (u_s,) = _retention(proj_s, decays, gn, idx, latent=True,
                                state_f=state_ret_f, state_b=state_ret_b)
            us_p, us_s = [u_p], [u_s]
            w_out = w_out_odd
        xp, *hp = _outproj(us_p, w_out, xp, gain_post, gain_pre, mods, layer, False)
        xs, *hs = _outproj(us_s, w_out, xs, gain_post, gain_pre, mods, layer, True)
        hp, hs = (hp[0], hs[0]) if hp else (None, None)
    return (xp.reshape(BATCH, SEQ, D_MODEL), xs.reshape(DEC_BATCH, DEC_SEQ, D_MODEL), *caches, *states)
```

```python
import functools

import jax
import jax.numpy as jnp
from jax import lax
from jax.experimental import pallas as pl
from jax.experimental.pallas import tpu as pltpu

D_MODEL = 2048
BATCH = 16
SEQ = 256
DEPTH = 4
DEC_BATCH = 4
DEC_SEQ = 1024
PAST_LEN = 256
GRID_W = 64
HEAD_DIM = 128
A_HEADS = 8
A_KV_HEADS = 2
A_GROUP = A_HEADS // A_KV_HEADS
A_WINDOW = 128
A_BLOCK = 128
B_HEADS = 8
NA_ROWS = 8
NA_COLS = 16
C_HEADS = 8
C_DK = D_MODEL // C_HEADS
C_DV = D_MODEL // C_HEADS
ROPE_THETA = 10000.0
EPS = 1e-6

N_EVEN = (DEPTH + 1) // 2
N_ODD = DEPTH // 2
A_Q = A_HEADS * HEAD_DIM
A_KV = A_KV_HEADS * HEAD_DIM
B_W = B_HEADS * HEAD_DIM
EVEN_WIDTH = A_Q + B_W
EVEN_IN = A_Q + 2 * A_KV + 3 * B_W + EVEN_WIDTH
GRID_ROWS = DEC_SEQ // GRID_W

SUBLANES = 8
LANES = 128
V7X_VMEM_BYTES = 64 * 1024 * 1024
VMEM_LEFT_TO_COMPILER = 4 * 1024 * 1024
VMEM_HEADROOM = 4 * 1024 * 1024

NP = BATCH * SEQ
NS = DEC_BATCH * DEC_SEQ
MOD_ROWS = SUBLANES
assert 1 + DEC_BATCH <= MOD_ROWS

COL_QA = 0
COL_KA = A_Q
COL_VA = A_Q + A_KV
COL_QB = A_Q + 2 * A_KV
COL_KB = COL_QB + B_W
COL_VB = COL_KB + B_W
COL_GATE = COL_VB + B_W

NA_QROWS = 8
NA_KROWS = 12
NA_Q = NA_QROWS * GRID_W
NA_K = NA_KROWS * GRID_W
NA_KSHIFT = (GRID_ROWS - NA_KROWS) * GRID_W

F32 = jnp.float32
BF16 = jnp.bfloat16
NEG_INF = float("-inf")


def _vmem_limit(block_bytes, scratch_bytes=0, temp_bytes=0):
    usable = V7X_VMEM_BYTES - VMEM_LEFT_TO_COMPILER
    assert 2 * block_bytes + scratch_bytes + temp_bytes + VMEM_HEADROOM <= usable
    return usable


def _params(vmem_bytes, ndims):
    return pltpu.CompilerParams(dimension_semantics=("arbitrary",) * ndims, vmem_limit_bytes=vmem_bytes)


def _silu(x):
    half = 0.5 * x
    return half + half * jnp.tanh(half)


def _dot(a, b):
    return jnp.dot(a, b, preferred_element_type=F32)


def _dot_nt(a, b):
    return lax.dot_general(a, b, (((1,), (1,)), ((), ())), preferred_element_type=F32)


def _dot_tn(a, b):
    return lax.dot_general(a, b, (((0,), (0,)), ((), ())), preferred_element_type=F32)


def _any_spec():
    return pl.BlockSpec(memory_space=pl.ANY)


ADA_TN = 1024


def _adaln_kernel(c_ref, w_ref, b_ref, o_ref):
    a = _silu(c_ref[...]).astype(BF16)
    o_ref[...] = _dot(a, w_ref[...].astype(BF16)) + b_ref[...]


def _adaln(cvec, w_ada, b_ada):
    n = 3 * D_MODEL
    blocks = MOD_ROWS * D_MODEL * 4 + D_MODEL * ADA_TN * 4 + ADA_TN * 4 + MOD_ROWS * ADA_TN * 4
    return pl.pallas_call(
        _adaln_kernel,
        out_shape=jax.ShapeDtypeStruct((DEPTH, MOD_ROWS, n), F32),
        grid=(DEPTH, n // ADA_TN),
        in_specs=[
            pl.BlockSpec((MOD_ROWS, D_MODEL), lambda l, j: (0, 0)),
            pl.BlockSpec((None, D_MODEL, ADA_TN), lambda l, j: (l, 0, j)),
            pl.BlockSpec((None, 1, ADA_TN), lambda l, j: (l, 0, j)),
        ],
        out_specs=pl.BlockSpec((None, MOD_ROWS, ADA_TN), lambda l, j: (l, 0, j)),
        compiler_params=_params(_vmem_limit(blocks, temp_bytes=D_MODEL * ADA_TN * 2), 2),
        name="adaln",
    )(cvec, w_ada, b_ada.reshape(DEPTH, 1, n))


def _mod_row(tile, tm, latent):
    return 1 + tile // (DEC_SEQ // tm) if latent else 0


NORM_TM = 512
NORM_ROWS = 32
IN_TN = 512


def _prenorm_kernel(x_ref, g_ref, sh_ref, sc_ref, h_ref):
    gain = g_ref[...]
    one_sc = 1.0 + sc_ref[...]
    sh = sh_ref[...]

    def body(r, carry):
        sl = pl.ds(pl.multiple_of(r * NORM_ROWS, NORM_ROWS), NORM_ROWS)
        x = x_ref[sl, :]
        ms = jnp.mean(x * x, axis=-1, keepdims=True)
        y = (x * lax.rsqrt(ms + EPS)) * gain
        h_ref[sl, :] = (y * one_sc + sh).astype(BF16)
        return carry

    lax.fori_loop(0, NORM_TM // NORM_ROWS, body, 0, unroll=True)


def _prenorm(x, gain, mods, layer, latent):
    ntok = x.shape[0]
    blocks = NORM_TM * D_MODEL * 4 + 3 * D_MODEL * 4 + NORM_TM * D_MODEL * 2
    mod_spec = lambda which: pl.BlockSpec(
        (None, None, None, 1, D_MODEL), lambda i: (layer, _mod_row(i, NORM_TM, latent), which, 0, 0))
    return pl.pallas_call(
        _prenorm_kernel,
        out_shape=jax.ShapeDtypeStruct((ntok, D_MODEL), BF16),
        grid=(ntok // NORM_TM,),
        in_specs=[
            pl.BlockSpec((NORM_TM, D_MODEL), lambda i: (i, 0)),
            pl.BlockSpec((None, 1, D_MODEL), lambda i: (layer, 0, 0)),
            mod_spec(0),
            mod_spec(1),
        ],
        out_specs=pl.BlockSpec((NORM_TM, D_MODEL), lambda i: (i, 0)),
        compiler_params=_params(_vmem_limit(blocks, temp_bytes=4 * NORM_ROWS * D_MODEL * 4), 1),
        name=f"prenorm_l{layer}_{'latent' if latent else 'prompt'}",
    )(x, gain, mods, mods)


def _inproj_kernel(h_ref, w_ref, o_ref):
    o_ref[...] = _dot(h_ref[...], w_ref[...].astype(BF16))


def _inproj(h, layer, w, latent):
    ntok = h.shape[0]
    n = w.shape[2]
    blocks = D_MODEL * IN_TN * 4 + ntok * IN_TN * 4
    return pl.pallas_call(
        _inproj_kernel,
        out_shape=jax.ShapeDtypeStruct((ntok, n), F32),
        grid=(n // IN_TN,),
        in_specs=[
            pl.BlockSpec((ntok, D_MODEL), lambda j: (0, 0), pipeline_mode=pl.Buffered(1)),
            pl.BlockSpec((None, D_MODEL, IN_TN), lambda j: (layer // 2, 0, j)),
        ],
        out_specs=pl.BlockSpec((ntok, IN_TN), lambda j: (0, j)),
        compiler_params=_params(_vmem_limit(blocks, ntok * D_MODEL * 2, D_MODEL * IN_TN * 2), 1),
        name=f"inproj_l{layer}_{'latent' if latent else 'prompt'}",
    )(h, w)


def _inproj_split_kernel(h_ref, w_ref, lo_ref, hi_ref, *, n_lo):
    @pl.when(pl.program_id(0) < n_lo)
    def _():
        lo_ref[...] = _dot(h_ref[...], w_ref[...].astype(BF16)).astype(lo_ref.dtype)

    @pl.when(pl.program_id(0) >= n_lo)
    def _():
        hi_ref[...] = _dot(h_ref[...], w_ref[...].astype(BF16))


def _inproj_split(h, layer, w, latent, lo_width):
    ntok = h.shape[0]
    n = w.shape[2]
    n_lo = lo_width // IN_TN
    assert lo_width % IN_TN == 0 and 0 < n_lo < n // IN_TN
    blocks = D_MODEL * IN_TN * 4 + ntok * IN_TN * (2 + 4)
    return pl.pallas_call(
        functools.partial(_inproj_split_kernel, n_lo=n_lo),
        out_shape=(jax.ShapeDtypeStruct((ntok, lo_width), BF16),
                   jax.ShapeDtypeStruct((ntok, n - lo_width), F32)),
        grid=(n // IN_TN,),
        in_specs=[
            pl.BlockSpec((ntok, D_MODEL), lambda j: (0, 0), pipeline_mode=pl.Buffered(1)),
            pl.BlockSpec((None, D_MODEL, IN_TN), lambda j: (layer // 2, 0, j)),
        ],
        out_specs=(pl.BlockSpec((ntok, IN_TN), lambda j: (0, jnp.minimum(j, n_lo - 1))),
                   pl.BlockSpec((ntok, IN_TN), lambda j: (0, jnp.maximum(j - n_lo, 0)))),
        compiler_params=_params(_vmem_limit(blocks, ntok * D_MODEL * 2, D_MODEL * IN_TN * 2), 1),
        name=f"inproj_l{layer}_{'latent' if latent else 'prompt'}",
    )(h, w)


OUT_TM = 512
OUT_SUB = 256


def _outproj_kernel(*refs, n_pieces, emit_next):
    u_refs = refs[:n_pieces]
    w_refs = refs[n_pieces:2 * n_pieces]
    pos = 2 * n_pieces
    x_ref, gain_ref, gate_ref = refs[pos:pos + 3]
    pos += 3
    if emit_next:
        ngain_ref, nshift_ref, nscale_ref = refs[pos:pos + 3]
        pos += 3
    o_ref = refs[pos]
    pos += 1
    if emit_next:
        h_ref = refs[pos]
        pos += 1
    wbf_ref = refs[pos]

    @pl.when(pl.program_id(0) == 0)
    def _():
        for p, w_ref in enumerate(w_refs):
            wbf_ref[p] = w_ref[...].astype(BF16)

    gated_gain = gate_ref[...] * gain_ref[...]
    if emit_next:
        next_gain = ngain_ref[...] * (1.0 + nscale_ref[...])
        next_shift = nshift_ref[...]
    for r in range(OUT_TM // OUT_SUB):
        rows = slice(r * OUT_SUB, (r + 1) * OUT_SUB)
        out = _dot(u_refs[0][rows, :], wbf_ref[0])
        for p in range(1, n_pieces):
            out = out + _dot(u_refs[p][rows, :], wbf_ref[p])
        ms = jnp.mean(out * out, axis=-1, keepdims=True)
        x_new = x_ref[rows, :] + (out * lax.rsqrt(ms + EPS)) * gated_gain
        o_ref[rows, :] = x_new
        if emit_next:
            ms = jnp.mean(x_new * x_new, axis=-1, keepdims=True)
            h_ref[rows, :] = ((x_new * lax.rsqrt(ms + EPS)) * next_gain + next_shift).astype(BF16)


def _outproj(us, w, x, gain_post, gain_pre, mods, layer, latent):
    ntok = x.shape[0]
    widths = [u.shape[1] for u in us]
    kp = widths[0]
    k = w.shape[1]
    assert all(kw == kp for kw in widths) and kp * len(us) == k
    emit_next = layer + 1 < DEPTH
    blocks = OUT_TM * k * 2 + 2 * OUT_TM * D_MODEL * 4 + 5 * D_MODEL * 4 + emit_next * OUT_TM * D_MODEL * 2
    tile_spec = pl.BlockSpec((OUT_TM, D_MODEL), lambda i: (i, 0))
    gain_spec = lambda l: pl.BlockSpec((None, 1, D_MODEL), lambda i: (l, 0, 0))
    mod_spec = lambda l, which: pl.BlockSpec(
        (None, None, None, 1, D_MODEL), lambda i: (l, _mod_row(i, OUT_TM, latent), which, 0, 0))
    u_specs = [pl.BlockSpec((OUT_TM, kp), lambda i: (i, 0)) for _ in us]
    w_specs = [pl.BlockSpec((None, kp, D_MODEL), lambda i, p=p: (layer // 2, p, 0),
                            pipeline_mode=pl.Buffered(1)) for p in range(len(us))]
    in_specs = u_specs + w_specs + [tile_spec, gain_spec(layer), mod_spec(layer, 2)]
    args = [*us, *([w] * len(us)), x, gain_post, mods]
    out_shape = [jax.ShapeDtypeStruct((ntok, D_MODEL), F32)]
    out_specs = [tile_spec]
    if emit_next:
        in_specs += [gain_spec(layer + 1), mod_spec(layer + 1, 0), mod_spec(layer + 1, 1)]
        args += [gain_pre, mods, mods]
        out_shape.append(jax.ShapeDtypeStruct((ntok, D_MODEL), BF16))
        out_specs.append(tile_spec)
    return pl.pallas_call(
        functools.partial(_outproj_kernel, n_pieces=len(us), emit_next=emit_next),
        out_shape=tuple(out_shape),
        grid=(ntok // OUT_TM,),
        in_specs=in_specs,
        out_specs=tuple(out_specs),
        scratch_shapes=[pltpu.VMEM((len(us), kp, D_MODEL), BF16)],
        compiler_params=_params(
            _vmem_limit(blocks, k * D_MODEL * (4 + 2), 3 * OUT_TM * D_MODEL * 2), 1),
        name=f"outproj_l{layer}_{'latent' if latent else 'prompt'}",
    )(*args)


LOG2_E = 1.4426950408889634
Q_SCALE_LOG2 = HEAD_DIM ** -0.5 * LOG2_E


def _softmax_pv(scores, values, sink=None):
    def fold(blocks, op):
        tiles = [b[:, t:t + LANES] for b in blocks for t in range(0, b.shape[-1], LANES)]
        out = tiles[0]
        for t in tiles[1:]:
            out = op(out, t)
        return out

    m = jnp.max(fold(scores, jnp.maximum), axis=-1, keepdims=True)
    if sink is not None:
        m = jnp.maximum(m, sink)
    width = values[0].shape[-1]
    acc = None
    for s, v in zip(scores, values):
        p = jnp.exp2(s - m).astype(BF16)
        pv = _dot(p, jnp.concatenate([v, jnp.ones_like(v)], axis=1))
        acc = pv if acc is None else acc + pv
    den = acc[:, width:]
    if sink is not None:
        den = den + jnp.exp2(sink - m)
    return acc[:, :width] / den


def _ctx_attn_kernel(sink_ref, p_ref, *refs):
    u_ref, ak_ref, av_ref, bk_ref, bv_ref = refs[-5:]

    def col(c):
        return p_ref[:, c:c + HEAD_DIM]

    def head(cq, k, v, cg, sink, cu):
        q = (col(cq) * Q_SCALE_LOG2).astype(BF16)
        o = _softmax_pv([_dot_nt(q, k)], [v], None if sink is None else sink * LOG2_E)
        u_ref[:, cu:cu + HEAD_DIM] = (o * _silu(col(cg))).astype(BF16)

    for n in range(A_KV_HEADS):
        k32 = col(COL_KA + n * HEAD_DIM)
        v32 = col(COL_VA + n * HEAD_DIM)
        ak_ref[n] = k32
        av_ref[n] = v32
        k = k32.astype(BF16)
        v = v32.astype(BF16)
        heads = [n * A_GROUP + g for g in range(A_GROUP)]
        q = jnp.concatenate(
            [(col(COL_QA + h * HEAD_DIM) * Q_SCALE_LOG2).astype(BF16) for h in heads], axis=0)
        row_head = lax.shift_right_logical(
            lax.broadcasted_iota(jnp.int32, (A_GROUP * SEQ, 1), 0), SEQ.bit_length() - 1)
        sink = jnp.zeros((A_GROUP * SEQ, 1), F32)
        for g, h in enumerate(heads):
            sink = jnp.where(row_head == g, sink_ref[h] * LOG2_E, sink)
        o = _softmax_pv([_dot_nt(q, k)], [v], sink)
        for g, h in enumerate(heads):
            gate = _silu(col(COL_GATE + h * HEAD_DIM))
            u_ref[:, h * HEAD_DIM:(h + 1) * HEAD_DIM] = (o[g * SEQ:(g + 1) * SEQ] * gate).astype(BF16)
    for h in range(B_HEADS):
        k32 = col(COL_KB + h * HEAD_DIM)
        v32 = col(COL_VB + h * HEAD_DIM)
        bk_ref[h] = k32
        bv_ref[h] = v32
        head(COL_QB + h * HEAD_DIM, k32.astype(BF16), v32.astype(BF16),
             COL_GATE + A_Q + h * HEAD_DIM, None, A_Q + h * HEAD_DIM)


def _ctx_attn(proj, sink, idx, prev_caches):
    blocks = (SEQ * EVEN_IN * 4 + SEQ * EVEN_WIDTH * 2
              + 2 * (A_KV_HEADS + B_HEADS) * SEQ * HEAD_DIM * 4)
    heads = (A_KV_HEADS, A_KV_HEADS, B_HEADS, B_HEADS)
    cache = lambda nh: jax.ShapeDtypeStruct((BATCH, N_EVEN, nh, SEQ, HEAD_DIM), F32)
    cache_spec = lambda nh: pl.BlockSpec((None, None, nh, SEQ, HEAD_DIM), lambda b: (b, idx, 0, 0, 0))
    n_prev = len(prev_caches)
    return pl.pallas_call(
        _ctx_attn_kernel,
        out_shape=(jax.ShapeDtypeStruct((NP, EVEN_WIDTH), BF16),) + tuple(cache(nh) for nh in heads),
        grid=(BATCH,),
        in_specs=[
            pl.BlockSpec(memory_space=pltpu.SMEM),
            pl.BlockSpec((SEQ, EVEN_IN), lambda b: (b, 0)),
        ] + [_any_spec() for _ in prev_caches],
        out_specs=(pl.BlockSpec((SEQ, EVEN_WIDTH), lambda b: (b, 0)),) + tuple(cache_spec(nh) for nh in heads),
        input_output_aliases={2 + k: 1 + k for k in range(n_prev)},
        compiler_params=_params(_vmem_limit(blocks, temp_bytes=8 * SEQ * SEQ * 4), 1),
        name=f"ctx_attn_{idx}",
    )(sink, proj, *prev_caches)


WIN_PAD = DEC_SEQ + 2 * A_BLOCK


def _rope_tables():
    t = jnp.arange(DEC_SEQ)
    half = HEAD_DIM // 2
    nf = half // 2
    inv = ROPE_THETA ** (-jnp.arange(nf, dtype=F32) / nf)
    ang_r = (t // GRID_W).astype(F32)[:, None] * inv[None]
    ang_c = (t % GRID_W).astype(F32)[:, None] * inv[None]
    cos = jnp.concatenate([jnp.cos(ang_r)] * 2 + [jnp.cos(ang_c)] * 2, axis=-1)
    sin = jnp.concatenate([-jnp.sin(ang_r), jnp.sin(ang_r), -jnp.sin(ang_c), jnp.sin(ang_c)], axis=-1)
    return cos, sin


def _rope(x, cos, sin):
    quarter = HEAD_DIM // 4
    lane = lax.broadcasted_iota(jnp.int32, x.shape, 1)
    first = (lane & (2 * quarter - 1)) < quarter
    partner = jnp.where(first, pltpu.roll(x, HEAD_DIM - quarter, 1), pltpu.roll(x, quarter, 1))
    return x * cos + partner * sin


def _win_attn_kernel(sink_ref, q_ref, k_ref, v_ref, ck_ref, cv_ref, gate_ref, cos_ref, sin_ref,
                     u_ref, kpad_ref, vpad_ref):
    n = pl.program_id(1)
    nqb = DEC_SEQ // A_BLOCK
    rows = A_GROUP * A_BLOCK

    zeros = jnp.zeros((A_BLOCK, HEAD_DIM), BF16)
    for ref in (kpad_ref, vpad_ref):
        ref[0:A_BLOCK, :] = zeros
        ref[A_BLOCK + DEC_SEQ:WIN_PAD, :] = zeros
    kpad_ref[A_BLOCK:A_BLOCK + DEC_SEQ, :] = _rope(k_ref[...], cos_ref[...], sin_ref[...]).astype(BF16)
    vpad_ref[A_BLOCK:A_BLOCK + DEC_SEQ, :] = v_ref[...].astype(BF16)
    ck = ck_ref[...].astype(BF16)
    cv = cv_ref[...].astype(BF16)

    row = lax.broadcasted_iota(jnp.int32, (rows, 1), 0)
    qi = row & (A_BLOCK - 1)
    kk = lax.broadcasted_iota(jnp.int32, (rows, 3 * A_BLOCK), 1)
    head = lax.shift_right_logical(row, A_BLOCK.bit_length() - 1)
    sink = jnp.zeros((rows, 1), F32)
    for g in range(A_GROUP):
        sink = jnp.where(head == g, sink_ref[n * A_GROUP + g] * LOG2_E, sink)

    for j in range(nqb):
        blk = slice(j * A_BLOCK, (j + 1) * A_BLOCK)
        q = jnp.concatenate(
            [(_rope(q_ref[blk, g * HEAD_DIM:(g + 1) * HEAD_DIM], cos_ref[blk, :], sin_ref[blk, :])
              * Q_SCALE_LOG2).astype(BF16) for g in range(A_GROUP)],
            axis=0)
        band = slice(j * A_BLOCK, (j + 3) * A_BLOCK)
        s_c = _dot_nt(q, ck)
        s_w = _dot_nt(q, kpad_ref[band, :])
        lower = jnp.maximum(qi, A_BLOCK if j == 0 else 0)
        upper = jnp.minimum(qi + 2 * A_WINDOW, (2 if j == nqb - 1 else 3) * A_BLOCK - 1)
        s_w = jnp.where((kk >= lower) & (kk <= upper), s_w, NEG_INF)
        o = _softmax_pv([s_c, s_w], [cv, vpad_ref[band, :]], sink)
        for g in range(A_GROUP):
            cols = slice(g * HEAD_DIM, (g + 1) * HEAD_DIM)
            u_ref[blk, cols] = (o[g * A_BLOCK:(g + 1) * A_BLOCK] * _silu(gate_ref[blk, cols])).astype(BF16)


def _win_attn(proj, sink, cache_k, cache_v, idx, cos, sin):
    gw = A_GROUP * HEAD_DIM
    blocks = (2 * DEC_SEQ * gw * 4 + 2 * DEC_SEQ * HEAD_DIM * 4 + 2 * PAST_LEN * HEAD_DIM * 4
              + 2 * DEC_SEQ * HEAD_DIM * 4 + DEC_SEQ * gw * 2)
    cache_spec = pl.BlockSpec((None, None, None, PAST_LEN, HEAD_DIM), lambda b, n: (b, idx, n, 0, 0))
    table_spec = pl.BlockSpec((DEC_SEQ, HEAD_DIM), lambda b, n: (0, 0))
    return pl.pallas_call(
        _win_attn_kernel,
        out_shape=jax.ShapeDtypeStruct((NS, A_Q), BF16),
        grid=(DEC_BATCH, A_KV_HEADS),
        in_specs=[
            pl.BlockSpec(memory_space=pltpu.SMEM),
            pl.BlockSpec((DEC_SEQ, gw), lambda b, n: (b, COL_QA // gw + n)),
            pl.BlockSpec((DEC_SEQ, HEAD_DIM), lambda b, n: (b, COL_KA // HEAD_DIM + n)),
            pl.BlockSpec((DEC_SEQ, HEAD_DIM), lambda b, n: (b, COL_VA // HEAD_DIM + n)),
            cache_spec,
            cache_spec,
            pl.BlockSpec((DEC_SEQ, gw), lambda b, n: (b, COL_GATE // gw + n)),
            table_spec,
            table_spec,
        ],
        out_specs=pl.BlockSpec((DEC_SEQ, gw), lambda b, n: (b, n)),
        scratch_shapes=[pltpu.VMEM((WIN_PAD, HEAD_DIM), BF16), pltpu.VMEM((WIN_PAD, HEAD_DIM), BF16)],
        compiler_params=_params(
            _vmem_limit(blocks, 2 * WIN_PAD * HEAD_DIM * 2,
                        (DEC_SEQ // A_BLOCK) * 3 * A_GROUP * A_BLOCK * (PAST_LEN + 3 * A_BLOCK) * 4), 2),
        name=f"win_attn_{idx}",
    )(sink, proj, proj, proj, cache_k, cache_v, proj, cos, sin)


RPB_PAD = (-(-(2 * NA_ROWS - 1) // SUBLANES) * SUBLANES, -(-(2 * NA_COLS - 1) // LANES) * LANES)


def _na_row_offsets(g):
    offsets = []
    for rl in range(NA_QROWS):
        r = g * NA_QROWS + rl
        r0 = min(max(r - NA_ROWS // 2, 0), GRID_ROWS - NA_ROWS)
        row = []
        for kl in range(NA_KROWS):
            kr = g * (GRID_ROWS - NA_KROWS) + kl
            row.append(kr - r + NA_ROWS - 1 if r0 <= kr < r0 + NA_ROWS else None)
        offsets.append(row)
    return offsets


def _fill_na_bias(rpb_ref, bias_ref, g):
    shape = (GRID_W, 2 * GRID_W)
    c = lax.broadcasted_iota(jnp.int32, shape, 0)
    lane = lax.broadcasted_iota(jnp.int32, shape, 1)
    kc = lane & (GRID_W - 1)
    c0 = jnp.clip(c - NA_COLS // 2, 0, GRID_W - NA_COLS)
    col_ok = (kc >= c0) & (kc < c0 + NA_COLS)
    low = lane < GRID_W
    offsets = _na_row_offsets(g)
    used = sorted({d for row in offsets for d in row if d is not None})
    lo, hi = {}, {}
    for d in used:
        row = jnp.broadcast_to(rpb_ref[d:d + 1, :] * LOG2_E, shape)
        lo[d] = pltpu.roll(row, 2 * GRID_W - (NA_COLS - 1), 1, stride=1, stride_axis=0)
        hi[d] = pltpu.roll(row, GRID_W - (NA_COLS - 1), 1, stride=1, stride_axis=0)
    neg = jnp.full(shape, NEG_INF, F32)
    for rl in range(NA_QROWS):
        for p in range(NA_KROWS // 2):
            da, db = offsets[rl][2 * p], offsets[rl][2 * p + 1]
            a = neg if da is None else lo[da]
            b = neg if db is None else hi[db]
            piece = jnp.where(col_ok, jnp.where(low, a, b), NEG_INF)
            bias_ref[rl * GRID_W:(rl + 1) * GRID_W, 2 * p * GRID_W:2 * (p + 1) * GRID_W] = piece


def _na_attn_kernel(q_ref, k_ref, v_ref, ck_ref, cv_ref, gate_ref, rpb_ref, u_ref, bias_ref):
    n_groups = DEC_SEQ // NA_Q
    for g in range(n_groups):
        _fill_na_bias(rpb_ref, bias_ref.at[g], g)
    for i in range(DEC_BATCH):
        k = k_ref[i].astype(BF16)
        v = v_ref[i].astype(BF16)
        ck = ck_ref[i].astype(BF16)
        cv = cv_ref[i].astype(BF16)
        for g in range(n_groups):
            rows = slice(g * NA_Q, (g + 1) * NA_Q)
            window = slice(g * NA_KSHIFT, g * NA_KSHIFT + NA_K)
            q = (q_ref[i, rows, :] * Q_SCALE_LOG2).astype(BF16)
            s_c = _dot_nt(q, ck)
            s_n = _dot_nt(q, k[window]) + bias_ref[g]
            o = _softmax_pv([s_c, s_n], [cv, v[window]])
            u_ref[i, rows, :] = (o * _silu(gate_ref[i, rows, :])).astype(BF16)


def _na_attn(proj, rpb, cache_k, cache_v, idx):
    n_groups = DEC_SEQ // NA_Q
    proj = proj.reshape(DEC_BATCH, DEC_SEQ, EVEN_IN)
    blocks = DEC_BATCH * (4 * DEC_SEQ * HEAD_DIM * 4 + 2 * PAST_LEN * HEAD_DIM * 4
                          + DEC_SEQ * HEAD_DIM * 2) + RPB_PAD[0] * RPB_PAD[1] * 4
    tok_spec = lambda col0: pl.BlockSpec((DEC_BATCH, DEC_SEQ, HEAD_DIM), lambda h: (0, 0, col0 // HEAD_DIM + h))
    cache_spec = pl.BlockSpec((DEC_BATCH, None, None, PAST_LEN, HEAD_DIM), lambda h: (0, idx, h, 0, 0))
    out = pl.pallas_call(
        _na_attn_kernel,
        out_shape=jax.ShapeDtypeStruct((DEC_BATCH, DEC_SEQ, B_W), BF16),
        grid=(B_HEADS,),
        in_specs=[
            tok_spec(COL_QB),
            tok_spec(COL_KB),
            tok_spec(COL_VB),
            cache_spec,
            cache_spec,
            tok_spec(COL_GATE + A_Q),
            pl.BlockSpec((None, None) + RPB_PAD, lambda h: (idx, h, 0, 0)),
        ],
        out_specs=pl.BlockSpec((DEC_BATCH, DEC_SEQ, HEAD_DIM), lambda h: (0, 0, h)),
        scratch_shapes=[pltpu.VMEM((n_groups, NA_Q, NA_K), F32)],
        compiler_params=_params(
            _vmem_limit(blocks, n_groups * NA_Q * NA_K * 4,
                        DEC_BATCH * n_groups * 2 * NA_Q * (NA_K + PAST_LEN) * 4), 1),
        name=f"na_attn_{idx}",
    )(proj, proj, proj, cache_k, cache_v, proj, rpb)
    return out.reshape(NS, B_W)


RET_CHUNK = 256
assert SEQ % RET_CHUNK == 0 and DEC_SEQ % RET_CHUNK == 0


def _ret_kernel(*refs, seq, heads, has_state, emit_state, n_prev):
    dec_ref, q_ref, k_ref, v_ref, gate_ref, gn_ref = refs[:6]
    pos = 6
    if has_state:
        s0f_ref, s0b_ref = refs[pos:pos + 2]
        pos += 2
    pos += n_prev
    u_ref = refs[pos]
    pos += 1
    if emit_state:
        sf_ref, sb_ref = refs[pos:pos + 2]
        pos += 2
    stf_ref, stb_ref, o_ref, dmat_ref = refs[pos:pos + 4]

    ch = RET_CHUNK
    nc = seq // ch
    kscale = C_DK ** -0.5
    ii = lax.broadcasted_iota(jnp.int32, (ch, ch), 0).astype(F32)
    jj = lax.broadcasted_iota(jnp.int32, (ch, ch), 1).astype(F32)
    icol = lax.broadcasted_iota(jnp.int32, (ch, 1), 0).astype(F32)

    def decay_matrix(direction, head):
        log_g = -jnp.exp(jnp.full((ch, ch), dec_ref[direction, head], F32))
        diff = (ii - jj) if direction == 0 else (jj - ii)
        return jnp.where(diff >= 0, jnp.exp(log_g * jnp.maximum(diff, 0.0)), 0.0)

    def decay_vectors(direction, head):
        dec = dec_ref[direction, head]
        log_g_col = -jnp.exp(jnp.full((ch, 1), dec, F32))
        log_g_row = -jnp.exp(jnp.full((1, C_DV), dec, F32))
        if direction == 0:
            q_dec = jnp.exp(log_g_col * (icol + 1.0))
            k_dec = jnp.exp(log_g_col * (ch - 1.0 - icol)) * kscale
        else:
            q_dec = jnp.exp(log_g_col * (ch - icol))
            k_dec = jnp.exp(log_g_col * icol) * kscale
        return q_dec, k_dec, jnp.exp(log_g_row * float(ch))

    def one_head(hh):
        cols = slice(hh * C_DK, (hh + 1) * C_DK)
        head = pl.program_id(1) * heads + hh
        dmat = dmat_ref[head]
        q_dec_f, k_dec_f, chunk_dec_f = decay_vectors(0, head)
        q_dec_b, k_dec_b, chunk_dec_b = decay_vectors(1, head)
        gn = gn_ref[:, cols]

        def rows_of(c):
            return slice(c * ch, (c + 1) * ch)

        def finalize(c, o):
            mu = jnp.mean(o, axis=-1, keepdims=True)
            d = o - mu
            var = jnp.mean(d * d, axis=-1, keepdims=True)
            y = (d * lax.rsqrt(var + EPS)) * gn
            u_ref[rows_of(c), cols] = (y * _silu(gate_ref[rows_of(c), cols])).astype(BF16)

        parked = {}

        def visit(c, value):
            if c not in parked:
                parked[c] = value is not None
                if value is not None:
                    o_ref[rows_of(c), cols] = value
            elif parked[c]:
                first = o_ref[rows_of(c), cols]
                finalize(c, first if value is None else first + value)
            else:
                finalize(c, value)

        def scan_step(state_ref, s0_ref, t, q, k32, v, q_dec, k_dec, chunk_dec):
            update = _dot_tn((k32 * k_dec).astype(BF16), v)
            if t == 0 and not has_state:
                state_ref[hh] = update
                return None
            state = s0_ref[hh] if t == 0 else state_ref[hh]
            state_ref[hh] = state * chunk_dec + update
            return _dot(q, state.astype(BF16)) * q_dec

        for t in range(nc):
            cf, cb = t, nc - 1 - t
            q = q_ref[rows_of(cf), cols].astype(BF16)
            k32 = k_ref[rows_of(cf), cols]
            v = v_ref[rows_of(cf), cols].astype(BF16)
            s = _dot_nt(q, k32.astype(BF16)) * dmat
            val_f = _dot(s.astype(BF16), v)
            cross = scan_step(stf_ref, s0f_ref if has_state else None, t, q, k32, v,
                              q_dec_f, k_dec_f, chunk_dec_f)
            if cross is not None:
                val_f = val_f + cross
            if cb != cf:
                q = q_ref[rows_of(cb), cols].astype(BF16)
                k32 = k_ref[rows_of(cb), cols]
                v = v_ref[rows_of(cb), cols].astype(BF16)
            val_b = scan_step(stb_ref, s0b_ref if has_state else None, t, q, k32, v,
                              q_dec_b, k_dec_b, chunk_dec_b)
            if cb == cf:
                finalize(cf, val_f if val_b is None else val_f + val_b)
            else:
                visit(cf, val_f)
                visit(cb, val_b)

        if emit_state:
            sf_ref[hh] = stf_ref[hh]
            sb_ref[hh] = stb_ref[hh]

    @pl.when(pl.program_id(0) == 0)
    def _():
        for hh in range(heads):
            head = pl.program_id(1) * heads + hh
            dmat_ref[head] = (decay_matrix(0, head) + decay_matrix(1, head)) * kscale

    for hh in range(heads):
        one_head(hh)


RET_HEADS_PROMPT = 8
RET_HEADS_LATENT = 4


def _retention(proj, decays, gn, idx, *, latent, state_f=None, state_b=None, prev_states=()):
    qkv, gate = proj
    seq = DEC_SEQ if latent else SEQ
    nb = DEC_BATCH if latent else BATCH
    heads = RET_HEADS_LATENT if latent else RET_HEADS_PROMPT
    width = heads * C_DK
    groups = C_HEADS // heads
    has_state = latent
    emit_state = not latent
    tok = lambda kind: pl.BlockSpec((seq, width), lambda b, h: (b, kind * groups + h))
    in_specs = [pl.BlockSpec(memory_space=pltpu.SMEM), tok(0), tok(1), tok(2), tok(0),
                pl.BlockSpec((None, 1, width), lambda b, h: (idx, 0, h))]
    args = [decays, qkv, qkv, qkv, gate, gn]
    state_spec = pl.BlockSpec((None, None, heads, C_DK, C_DV), lambda b, h: (b, idx, h, 0, 0))
    if has_state:
        in_specs += [state_spec, state_spec]
        args += [state_f, state_b]
    aliases = {len(args) + k: 1 + k for k in range(len(prev_states))}
    in_specs += [_any_spec() for _ in prev_states]
    args += list(prev_states)
    out_shape = [jax.ShapeDtypeStruct((nb * seq, D_MODEL), BF16)]
    out_specs = [pl.BlockSpec((seq, width), lambda b, h: (b, h))]
    if emit_state:
        st = jax.ShapeDtypeStruct((nb, N_ODD, C_HEADS, C_DK, C_DV), F32)
        out_shape += [st, st]
        out_specs += [state_spec, state_spec]
    blocks = seq * width * (3 * 2 + 4) + width * 4 + 2 * heads * C_DK * C_DV * 4 + seq * width * 2
    scratch = 2 * heads * C_DK * C_DV * 4 + seq * width * 4 + C_HEADS * RET_CHUNK * RET_CHUNK * 4
    return pl.pallas_call(
        functools.partial(_ret_kernel, seq=seq, heads=heads, has_state=has_state, emit_state=emit_state,
                          n_prev=len(prev_states)),
        out_shape=tuple(out_shape),
        grid=(nb, groups),
        in_specs=in_specs,
        out_specs=tuple(out_specs),
        scratch_shapes=[pltpu.VMEM((heads, C_DK, C_DV), F32), pltpu.VMEM((heads, C_DK, C_DV), F32),
                        pltpu.VMEM((seq, width), F32), pltpu.VMEM((C_HEADS, RET_CHUNK, RET_CHUNK), F32)],
        input_output_aliases=aliases,
        compiler_params=_params(_vmem_limit(blocks, scratch, 24 * RET_CHUNK * C_DV * 4), 2),
        name=f"retention_{'latent' if latent else 'prompt'}_{idx}",
    )(*args)


def kernel(x_prompt, x_sample, c, cache_a_k, cache_a_v, cache_b_k, cache_b_v, state_ret_f, state_ret_b,
           c_ctx, w_ada, b_ada, norm_pre, norm_post, w_in_even, w_out_even, a_sink, na_rpb,
           w_in_odd, w_out_odd, ret_decay_f, ret_decay_b, ret_gn):
    xp = x_prompt.reshape(NP, D_MODEL)
    xs = x_sample.reshape(NS, D_MODEL)
    cvec = jnp.concatenate(
        [c_ctx[None, :], c, jnp.zeros((MOD_ROWS - 1 - DEC_BATCH, D_MODEL), F32)], axis=0)
    mods = _adaln(cvec, w_ada, b_ada).reshape(DEPTH, MOD_ROWS, 3, 1, D_MODEL)
    gain_pre = norm_pre.reshape(DEPTH, 1, D_MODEL)
    gain_post = norm_post.reshape(DEPTH, 1, D_MODEL)
    gn = ret_gn.reshape(N_ODD, 1, D_MODEL)
    cos, sin = _rope_tables()
    rpb = jnp.pad(na_rpb, ((0, 0), (0, 0), (0, RPB_PAD[0] - na_rpb.shape[2]),
                           (0, RPB_PAD[1] - na_rpb.shape[3])))
    caches = ()
    states = ()
    hp = _prenorm(xp, gain_pre, mods, 0, False)
    hs = _prenorm(xs, gain_pre, mods, 0, True)
    for layer in range(DEPTH):
        idx = layer // 2
        if layer % 2 == 0:
            proj_p = _inproj(hp, layer, w_in_even, False)
            proj_s = _inproj(hs, layer, w_in_even, True)
            u_p, *caches = _ctx_attn(proj_p, a_sink[idx], idx, caches)
            u_a = _win_attn(proj_s, a_sink[idx], cache_a_k, cache_a_v, idx, cos, sin)
            u_b = _na_attn(proj_s, rpb, cache_b_k, cache_b_v, idx)
            us_p, us_s, w_out = [u_p], [u_a, u_b], w_out_even
        else:
            proj_p = _inproj_split(hp, layer, w_in_odd, False, 3 * D_MODEL)
            proj_s = _inproj_split(hs, layer, w_in_odd, True, 3 * D_MODEL)
            decays = jnp.stack([ret_decay_f[idx], ret_decay_b[idx]], axis=0)
            u_p, *states = _retention(proj_p, decays, gn, idx, latent=False, prev_states=states)
            (u_s,) = _retention(proj_s, decays, gn, idx, latent=True,
                                state_f=state_ret_f, state_b=state_ret_b)
            us_p, us_s = [u_p], [u_s]
            w_out = w_out_odd
        xp, *hp = _outproj(us_p, w_out, xp, gain_post, gain_pre, mods, layer, False)
        xs, *hs = _outproj(us_s, w_out, xs, gain_post, gain_pre, mods, layer, True)
        hp, hs = (hp[0], hs[0]) if hp else (None, None)
    return (xp.reshape(BATCH, SEQ, D_MODEL), xs.reshape(DEC_BATCH, DEC_SEQ, D_MODEL), *caches, *states)
```
